```python
import math
import jax, jax.numpy as jnp
from jax import lax
import numpy as np

D_MODEL = 1024
BATCH = 16
SEQ = 2048
DEPTH = 2

PLE_DIM = 256
N_A_LAYERS = DEPTH // 2
N_B_LAYERS = DEPTH - N_A_LAYERS
N_DENSE_LAYERS = (DEPTH + 1) // 2
N_MOE_LAYERS = DEPTH // 2

RWKV_HEAD = 64
RWKV_HEADS = D_MODEL // RWKV_HEAD
DECAY_LORA = 64
ICLR_LORA = 64
GATE_LORA = 128
GN_EPS = 64e-5

N_HEADS = 16
HEAD_DIM = D_MODEL // N_HEADS
N_KV_GROUPS = 4
Q_PER_GROUP = N_HEADS // N_KV_GROUPS
N_BRANCH = 3
CMP_LEN = 32
CMP_STRIDE = 16
CMP_HIDDEN = 256
SEL_BLOCK = 64
N_SEL = 8
WINDOW = 512
Q_CHUNK = 32
ROPE_THETA = 10000.0

D_FF = 2816
N_EXPERTS = 8
TOP_K = 2
D_FF_EXPERT = 1408
MOE_ROWS = 256

LN_EPS = 1e-5
DEEPNORM_ALPHA = (2.0 * DEPTH) ** 0.25
DEEPNORM_BETA = (8.0 * DEPTH) ** -0.25
NEG_INF = -1e30
FORCE_SCORE = 1e4

kernel_name = "yoco_rwkv7_nsa_moe_deepnorm"


def layer_norm(x, g, b):
    xf = x.astype(jnp.float32)
    mu = jnp.mean(xf, -1, keepdims=True)
    var = jnp.mean(jnp.square(xf - mu), -1, keepdims=True)
    return ((xf - mu) * lax.rsqrt(var + LN_EPS) * g + b).astype(x.dtype)


def rope(x, pos):
    half = x.shape[-1] // 2
    inv = ROPE_THETA ** (-jnp.arange(half, dtype=jnp.float32) / half)
    ang = pos.astype(jnp.float32)[:, None] * inv[None, :]
    cos, sin = jnp.cos(ang), jnp.sin(ang)
    xf = x.astype(jnp.float32)
    x1, x2 = xf[..., :half], xf[..., half:]
    return jnp.concatenate([x1 * cos - x2 * sin, x2 * cos + x1 * sin], -1).astype(x.dtype)


def masked_softmax(s, mask):
    s = jnp.where(mask, s.astype(jnp.float32), NEG_INF)
    p = jax.nn.softmax(s, axis=-1)
    return jnp.where(mask, p, 0.0)


def token_shift(x):
    return jnp.pad(x, ((0, 0), (1, 0), (0, 0)))[:, :-1]


def rwkv7_time_mix(x, mu, w_rkv, w0, w1, w2, a0, a1, a2, g1, g2, k_k, k_a, r_k, gn_g, gn_b, w_o):
    B, T, D = x.shape
    H, N = RWKV_HEADS, RWKV_HEAD
    xx = token_shift(x) - x
    mix = lambda i: x + xx * mu[i]
    rkv = jnp.einsum('mbtd,mde->mbte', jnp.stack([mix(0), mix(1), mix(2)]), w_rkv)
    w = -jax.nn.softplus(-(w0 + jnp.tanh(mix(3) @ w1) @ w2)) - 0.5
    a = jax.nn.sigmoid(a0 + (mix(4) @ a1) @ a2)
    g = jax.nn.sigmoid(mix(5) @ g1) @ g2
    heads = lambda z: z.reshape(B, T, H, N).astype(jnp.float32)
    r, k, v, w, a = heads(rkv[0]), heads(rkv[1]), heads(rkv[2]), heads(w), heads(a)
    kk = k * k_k.reshape(H, N).astype(jnp.float32)
    kk = kk / jnp.maximum(jnp.sqrt(jnp.sum(kk * kk, -1, keepdims=True)), 1e-12)
    k = k * (1.0 + (a - 1.0) * k_a.reshape(H, N).astype(jnp.float32))
    decay = jnp.exp(-jnp.exp(w))

    def step(S, inp):
        r_t, k_t, v_t, d_t, kk_t, a_t = inp
        sk = jnp.einsum('bhij,bhj->bhi', S, kk_t)
        S = (S * d_t[:, :, None, :] - sk[..., :, None] * (kk_t * a_t)[:, :, None, :]
             + v_t[..., :, None] * k_t[:, :, None, :])
        return S, jnp.einsum('bhij,bhj->bhi', S, r_t)

    S0 = jnp.zeros((B, H, N, N), jnp.float32)
    seq = tuple(jnp.moveaxis(z, 1, 0) for z in (r, k, v, decay, kk, a))
    _, out = lax.scan(step, S0, seq)
    out = jnp.moveaxis(out, 0, 1)
    mu_o = jnp.mean(out, -1, keepdims=True)
    var_o = jnp.mean(jnp.square(out - mu_o), -1, keepdims=True)
    out = ((out - mu_o) * lax.rsqrt(var_o + GN_EPS)).reshape(B, T, D) * gn_g + gn_b
    bonus = jnp.sum(r * k * r_k.reshape(H, N).astype(jnp.float32), -1, keepdims=True) * v
    out = out + bonus.reshape(B, T, D)
    return (out * g).astype(x.dtype) @ w_o


def nsa_shared_kv(s, w_kv, cmp_pos, phi_w1, phi_b1, phi_w2):
    B, T, _ = s.shape
    G, dh = N_KV_GROUPS, HEAD_DIM
    kv = jnp.transpose((s @ w_kv).reshape(B, T, 6, G, dh), (2, 0, 3, 1, 4))
    pos = jnp.arange(T)
    n_cmp = (T - CMP_LEN) // CMP_STRIDE + 1
    idx = jnp.arange(n_cmp)[:, None] * CMP_STRIDE + jnp.arange(CMP_LEN)[None, :]

    def compress(z, j):
        blk = z[:, :, idx, :] + cmp_pos[j]
        blk = blk.reshape(B, G, n_cmp, CMP_LEN * dh)
        return jax.nn.gelu(blk @ phi_w1[j] + phi_b1[j]) @ phi_w2[j]

    k_cmp = compress(kv[0], 0)
    v_cmp = compress(kv[1], 1)
    k_slc = rope(kv[2], pos)
    k_win = rope(kv[4], pos)
    return (k_cmp, v_cmp, k_slc, kv[3], k_win, kv[5])


def nsa_attention(h, shared, w_in, w_o):
    B, T, D = h.shape
    G, R, dh, H = N_KV_GROUPS, Q_PER_GROUP, HEAD_DIM, N_HEADS
    k_cmp, v_cmp, k_slc, v_slc, k_win, v_win = shared
    proj = h @ w_in
    q = proj[..., :H * dh].reshape(B, T, G, R, dh).transpose(0, 2, 3, 1, 4)
    gates = jax.nn.sigmoid(proj[..., H * dh:].astype(jnp.float32))
    gates = gates.reshape(B, T, G, R, N_BRANCH).transpose(0, 2, 3, 1, 4)
    pos = jnp.arange(T)
    q_rot = rope(q, pos)
    scale = HEAD_DIM ** -0.5

    n_cmp = k_cmp.shape[2]
    n_blk = T // SEL_BLOCK
    n_pick = min(N_SEL, n_blk)
    cmp_start = jnp.arange(n_cmp) * CMP_STRIDE
    cmp_end = cmp_start + CMP_LEN - 1
    blk_start = jnp.arange(n_blk) * SEL_BLOCK
    blk_ids = jnp.arange(n_blk)
    overlap = ((cmp_start[:, None] < blk_start[None, :] + SEL_BLOCK)
               & (cmp_end[:, None] >= blk_start[None, :])).astype(jnp.float32)
    k_blk = k_slc.reshape(B, G, n_blk, SEL_BLOCK, dh)
    v_blk = v_slc.reshape(B, G, n_blk, SEL_BLOCK, dh)
    k_win_pad = jnp.pad(k_win, ((0, 0), (0, 0), (WINDOW, 0), (0, 0)))
    v_win_pad = jnp.pad(v_win, ((0, 0), (0, 0), (WINDOW, 0), (0, 0)))
    bi = jnp.arange(B)[:, None, None, None]
    gi = jnp.arange(G)[None, :, None, None]
    n_chunks = T // Q_CHUNK
    span = WINDOW + Q_CHUNK

    def attend_chunk(args):
        c, q_c, qr_c, g_c = args
        t = c * Q_CHUNK + jnp.arange(Q_CHUNK)
        s = jnp.einsum('bgrqd,bgnd->bgrqn', q_c, k_cmp) * scale
        p_cmp = masked_softmax(s, cmp_end[None, :] <= t[:, None])
        o_cmp = jnp.einsum('bgrqn,bgnd->bgrqd', p_cmp.astype(v_cmp.dtype), v_cmp)
        imp = jnp.einsum('bgrqn,ns->bgqs', p_cmp, overlap)
        cur = t // SEL_BLOCK
        forced = ((blk_ids[None, :] == 0) | (blk_ids[None, :] == cur[:, None])
                  | (blk_ids[None, :] == cur[:, None] - 1))
        causal_blk = blk_start[None, :] <= t[:, None]
        imp = jnp.where(forced, FORCE_SCORE, jnp.where(causal_blk, imp, NEG_INF))
        _, sel = lax.top_k(imp, n_pick)
        k_sel = k_blk[bi, gi, sel].reshape(B, G, Q_CHUNK, n_pick * SEL_BLOCK, dh)
        v_sel = v_blk[bi, gi, sel].reshape(B, G, Q_CHUNK, n_pick * SEL_BLOCK, dh)
        key_pos = (sel[..., None] * SEL_BLOCK + jnp.arange(SEL_BLOCK)).reshape(B, G, Q_CHUNK, -1)
        s = jnp.einsum('bgrqd,bgqkd->bgrqk', qr_c, k_sel) * scale
        p = masked_softmax(s, (key_pos <= t[None, None, :, None])[:, :, None])
        o_slc = jnp.einsum('bgrqk,bgqkd->bgrqd', p.astype(v_sel.dtype), v_sel)
        kw = lax.dynamic_slice_in_dim(k_win_pad, c * Q_CHUNK, span, axis=2)
        vw = lax.dynamic_slice_in_dim(v_win_pad, c * Q_CHUNK, span, axis=2)
        s_pos = c * Q_CHUNK - WINDOW + jnp.arange(span)
        diff = t[:, None] - s_pos[None, :]
        wmask = (diff >= 0) & (diff < WINDOW) & (s_pos[None, :] >= 0)
        s = jnp.einsum('bgrqd,bgkd->bgrqk', qr_c, kw) * scale
        p = masked_softmax(s, wmask)
        o_win = jnp.einsum('bgrqk,bgkd->bgrqd', p.astype(vw.dtype), vw)
        o = g_c[..., 0:1] * o_cmp + g_c[..., 1:2] * o_slc + g_c[..., 2:3] * o_win
        return o.astype(h.dtype)

    chunked = lambda z: jnp.moveaxis(z.reshape(B, G, R, n_chunks, Q_CHUNK, z.shape[-1]), 3, 0)
    out = lax.map(attend_chunk, (jnp.arange(n_chunks), chunked(q), chunked(q_rot), chunked(gates)))
    out = jnp.transpose(out, (1, 0, 4, 2, 3, 5)).reshape(B, T, H * dh)
    return out @ w_o


def swiglu(x, w_gu, w_down):
    gt, up = jnp.split(x @ w_gu, 2, axis=-1)
    return (jax.nn.silu(gt) * up) @ w_down


def moe_swiglu(x, w_router, b_router, w_gu, w_down):
    B, T, D = x.shape
    xt = x.reshape(-1, D)
    n_tok = xt.shape[0]
    n_assign = n_tok * TOP_K
    logits = (xt @ w_router).astype(jnp.float32) + b_router
    top_logit, top_e = lax.top_k(logits, TOP_K)
    weights = jax.nn.softmax(top_logit, axis=-1)
    flat_e = top_e.reshape(-1)
    flat_tok = jnp.arange(n_assign) // TOP_K
    order = jnp.argsort(flat_e)
    e_sorted = flat_e[order]
    tok_sorted = flat_tok[order]
    w_sorted = weights.reshape(-1)[order]
    counts = jnp.bincount(flat_e, length=N_EXPERTS)
    padded = (counts + MOE_ROWS - 1) // MOE_ROWS * MOE_ROWS
    pad_end = jnp.cumsum(padded)
    pad_start = pad_end - padded
    grp_start = jnp.cumsum(counts) - counts
    dest = pad_start[e_sorted] + jnp.arange(n_assign) - grp_start[e_sorted]
    n_blocks = -(-n_assign // MOE_ROWS) + N_EXPERTS
    n_rows = n_blocks * MOE_ROWS
    row_tok = jnp.zeros((n_rows,), jnp.int32).at[dest].set(tok_sorted.astype(jnp.int32))
    block_e = jnp.minimum(jnp.searchsorted(pad_end, jnp.arange(n_blocks) * MOE_ROWS, side='right'),
                          N_EXPERTS - 1)
    xs = xt[row_tok].reshape(n_blocks, MOE_ROWS, D)

    def expert_block(args):
        xb, e = args
        gt, up = jnp.split(xb @ w_gu[e], 2, axis=-1)
        return (jax.nn.silu(gt) * up) @ w_down[e]

    ys = lax.map(expert_block, (xs, block_e)).reshape(n_rows, D)
    y_assign = ys[dest] * w_sorted[:, None].astype(ys.dtype)
    out = jax.ops.segment_sum(y_assign, tok_sorted, num_segments=n_tok)
    return out.reshape(B, T, D)


def setup_inputs(seed: int = 0) -> dict:
    key = jax.random.key(seed)
    ks = iter(jax.random.split(key, 48))
    nrm = lambda shape, sc: jax.random.normal(next(ks), shape, jnp.float32) * sc
    unif = lambda shape, lo, hi: jax.random.uniform(next(ks), shape, jnp.float32, lo, hi)
    D, NA, NB, ND, NM = D_MODEL, N_A_LAYERS, N_B_LAYERS, N_DENSE_LAYERS, N_MOE_LAYERS
    G, dh = N_KV_GROUPS, HEAD_DIM
    return {
        "x": nrm((BATCH, SEQ, D), 1.0),
        "p": nrm((DEPTH, BATCH, SEQ, PLE_DIM), 1.0),
        "a_mu": unif((NA, 6, D), 0.0, 1.0),
        "a_w_rkv": nrm((NA, 3, D, D), D ** -0.5),
        "a_w0": unif((NA, D), -6.0, -0.5),
        "a_w1": nrm((NA, D, DECAY_LORA), D ** -0.5),
        "a_w2": nrm((NA, DECAY_LORA, D), 0.5 * DECAY_LORA ** -0.5),
        "a_a0": nrm((NA, D), 0.1),
        "a_a1": nrm((NA, D, ICLR_LORA), D ** -0.5),
        "a_a2": nrm((NA, ICLR_LORA, D), 0.5 * ICLR_LORA ** -0.5),
        "a_g1": nrm((NA, D, GATE_LORA), D ** -0.5),
        "a_g2": nrm((NA, GATE_LORA, D), GATE_LORA ** -0.5),
        "a_k_k": 0.85 + nrm((NA, D), 0.05),
        "a_k_a": 1.0 + nrm((NA, D), 0.05),
        "a_r_k": nrm((NA, D), 0.1),
        "a_gn_g": 1.0 + nrm((NA, D), 0.05),
        "a_gn_b": nrm((NA, D), 0.02),
        "a_w_o": nrm((NA, D, D), DEEPNORM_BETA * D ** -0.5),
        "b_w_kv": nrm((D, 6 * G * dh), D ** -0.5),
        "b_cmp_pos": nrm((2, CMP_LEN, dh), 0.5),
        "b_phi_w1": nrm((2, CMP_LEN * dh, CMP_HIDDEN), (CMP_LEN * dh) ** -0.5),
        "b_phi_b1": nrm((2, CMP_HIDDEN), 0.02),
        "b_phi_w2": nrm((2, CMP_HIDDEN, dh), CMP_HIDDEN ** -0.5),
        "b_w_in": nrm((NB, D, N_HEADS * dh + N_BRANCH * N_HEADS), D ** -0.5),
        "b_w_o": nrm((NB, N_HEADS * dh, D), DEEPNORM_BETA * D ** -0.5),
        "f_w_gu": nrm((ND, D, 2 * D_FF), D ** -0.5),
        "f_w_down": nrm((ND, D_FF, D), DEEPNORM_BETA * D_FF ** -0.5),
        "m_w_router": nrm((NM, D, N_EXPERTS), D ** -0.5),
        "m_b_router": nrm((NM, N_EXPERTS), 0.01),
        "m_w_gu": nrm((NM, N_EXPERTS, D, 2 * D_FF_EXPERT), D ** -0.5),
        "m_w_down": nrm((NM, N_EXPERTS, D_FF_EXPERT, D), DEEPNORM_BETA * D_FF_EXPERT ** -0.5),
        "ln_g": 1.0 + nrm((DEPTH, 2, D), 0.05),
        "ln_b": nrm((DEPTH, 2, D), 0.02),
        "ple_w": nrm((DEPTH, PLE_DIM, D), PLE_DIM ** -0.5),
        "ple_gate_w": nrm((DEPTH, D, D), D ** -0.5),
        "ple_gate_b": nrm((DEPTH, D), 0.02),
    }


def reference(x, p, a_mu, a_w_rkv, a_w0, a_w1, a_w2, a_a0, a_a1, a_a2, a_g1, a_g2, a_k_k, a_k_a,
              a_r_k, a_gn_g, a_gn_b, a_w_o, b_w_kv, b_cmp_pos, b_phi_w1, b_phi_b1, b_phi_w2,
              b_w_in, b_w_o, f_w_gu, f_w_down, m_w_router, m_b_router, m_w_gu, m_w_down,
              ln_g, ln_b, ple_w, ple_gate_w, ple_gate_b):
    shared = None
    for i in range(DEPTH):
        if i < N_A_LAYERS:
            j = i
            mix = rwkv7_time_mix(x, a_mu[j], a_w_rkv[j], a_w0[j], a_w1[j], a_w2[j], a_a0[j],
                                 a_a1[j], a_a2[j], a_g1[j], a_g2[j], a_k_k[j], a_k_a[j],
                                 a_r_k[j], a_gn_g[j], a_gn_b[j], a_w_o[j])
        else:
            if shared is None:
                shared = nsa_shared_kv(x, b_w_kv, b_cmp_pos, b_phi_w1, b_phi_b1, b_phi_w2)
            j = i - N_A_LAYERS
            mix = nsa_attention(x, shared, b_w_in[j], b_w_o[j])
        x = layer_norm(DEEPNORM_ALPHA * x + mix, ln_g[i, 0], ln_b[i, 0])
        if i % 2 == 0:
            ffn = swiglu(x, f_w_gu[i // 2], f_w_down[i // 2])
        else:
            ffn = moe_swiglu(x, m_w_router[i // 2], m_b_router[i // 2], m_w_gu[i // 2], m_w_down[i // 2])
        x = layer_norm(DEEPNORM_ALPHA * x + ffn, ln_g[i, 1], ln_b[i, 1])
        x = x + (p[i] @ ple_w[i]) * jax.nn.sigmoid(x @ ple_gate_w[i] + ple_gate_b[i])
    return x
```

```python
import functools
import math

import jax
import jax.numpy as jnp
from jax import lax
from jax.experimental import pallas as pl
from jax.experimental.pallas import tpu as pltpu

BF16 = jnp.bfloat16
F32 = jnp.float32

LANES = 128
VMEM_LIMIT_BYTES = 56 * 1024 * 1024

D_MODEL = 1024
PLE_DIM = 256
RWKV_HEAD = 64
GN_EPS = 64e-5
N_HEADS = 16
HEAD_DIM = 64
N_KV_GROUPS = 4
Q_PER_GROUP = 4
N_BRANCH = 3
CMP_LEN = 32
CMP_STRIDE = 16
CMP_HIDDEN = 256
SEL_BLOCK = 64
N_SEL = 8
WINDOW = 512
ROPE_THETA = 10000.0
D_FF = 2816
N_EXPERTS = 8
TOP_K = 2
D_FF_EXPERT = 1408
MOE_ROWS = 256
LN_EPS = 1e-5
DEPTH = 2
DEEPNORM_ALPHA = (2.0 * DEPTH) ** 0.25
NEG_INF = -1e30
FORCE_SCORE = 1e4

SCAN_CHUNK = 64


def _cparams(*sem):
    return pltpu.CompilerParams(dimension_semantics=sem, vmem_limit_bytes=VMEM_LIMIT_BYTES)


def _resident(shape):
    nd = len(shape)
    return pl.BlockSpec(shape, lambda *_: (0,) * nd, pipeline_mode=pl.Buffered(1))


def _dot(a, b):
    return jnp.dot(a, b, preferred_element_type=F32)


def _dot_nt(a, b):
    return lax.dot_general(a, b, (((1,), (1,)), ((), ())), preferred_element_type=F32)


def _split2(x):
    hi = x.astype(BF16)
    lo = (x - hi.astype(F32)).astype(BF16)
    return hi, lo


def _split3(x):
    hi = x.astype(BF16)
    r1 = x - hi.astype(F32)
    mid = r1.astype(BF16)
    lo = (r1 - mid.astype(F32)).astype(BF16)
    return hi, mid, lo


def _layer_norm(y, g, b):
    mu = jnp.mean(y, axis=-1, keepdims=True)
    yc = y - mu
    var = jnp.mean(yc * yc, axis=-1, keepdims=True)
    return yc * lax.rsqrt(var + LN_EPS) * g + b


def _sigmoid(z):
    return 1.0 / (1.0 + jnp.exp(-z))


def _rwkv_proj_kernel(x_ref, xh_ref, mu_ref, wrkv_ref, w0_ref, w1_ref, w2_ref, a0_ref, a1_ref,
                      a2_ref, g1_ref, g2_ref, r_ref, k_ref, v_ref, dl_ref, a_ref, g_ref, *,
                      tiles_per_seq):
    x = x_ref[...]
    tm = x.shape[0]
    first = (pl.program_id(0) % tiles_per_seq) == 0
    prev_row = jnp.where(first, 0.0, xh_ref[0, 7:8, :])
    row = lax.broadcasted_iota(jnp.int32, x.shape, 0)
    x_shift = jnp.where(row == 0, prev_row, pltpu.roll(x, shift=1, axis=0))
    xx = x_shift - x

    def mix(i):
        return (x + xx * mu_ref[i:i + 1, :]).astype(BF16)

    r_ref[...] = _dot(mix(0), wrkv_ref[0])
    k_ref[...] = _dot(mix(1), wrkv_ref[1])
    v_ref[...] = _dot(mix(2), wrkv_ref[2])
    z = w0_ref[...] + _dot(jnp.tanh(_dot(mix(3), w1_ref[...])).astype(BF16), w2_ref[...])
    dl_ref[...] = -math.exp(-0.5) * _sigmoid(z)
    a_ref[...] = _sigmoid(a0_ref[...] + _dot(_dot(mix(4), a1_ref[...]).astype(BF16), a2_ref[...]))
    g_ref[...] = _dot(_sigmoid(_dot(mix(5), g1_ref[...])).astype(BF16), g2_ref[...])


def _rwkv_proj(x2d, seq, mu, w_rkv, w0, w1, w2, a0, a1, a2, g1, g2, tm=256):
    m, d = x2d.shape
    xh = x2d.reshape(m // 8, 8, d)
    row = lambda i: (i, 0)
    out = jax.ShapeDtypeStruct((m, d), F32)
    return pl.pallas_call(
        functools.partial(_rwkv_proj_kernel, tiles_per_seq=seq // tm),
        grid=(m // tm,),
        in_specs=[
            pl.BlockSpec((tm, d), row),
            pl.BlockSpec((1, 8, d), lambda i: (jnp.maximum(i * (tm // 8) - 1, 0), 0, 0)),
            _resident(mu.shape), _resident(w_rkv.shape), _resident(w0.shape), _resident(w1.shape),
            _resident(w2.shape), _resident(a0.shape), _resident(a1.shape), _resident(a2.shape),
            _resident(g1.shape), _resident(g2.shape),
        ],
        out_specs=[pl.BlockSpec((tm, d), row)] * 6,
        out_shape=[out] * 6,
        compiler_params=_cparams("parallel"),
        name="rwkv_proj",
    )(x2d, xh, mu, w_rkv, w0, w1, w2, a0, a1, a2, g1, g2)


def _rwkv_scan_kernel(r_ref, k_ref, v_ref, dl_ref, a_ref, g_ref, kk_ref, ka_ref, rk_ref, gng_ref,
                      gnb_ref, o_ref, state_ref, *, n_pairs, n_chunks):
    C = SCAN_CHUNK
    N = RWKV_HEAD
    W = 2 * N

    @pl.when(pl.program_id(2) == 0)
    def _():
        state_ref[...] = jnp.zeros_like(state_ref)

    tt = lax.broadcasted_iota(jnp.int32, (C, W), 0)
    ln = lax.broadcasted_iota(jnp.int32, (C, W), 1)
    ss = ln & (N - 1)
    strict_lower = ss < tt
    lower = ss <= tt
    eye = jnp.where(ss == tt, 1.0, 0.0)
    level_masks = []
    for lg in range(int(math.log2(C))):
        level_masks.append(((tt >> (lg + 1)) == (ss >> (lg + 1))) & ((tt >> lg) == (ss >> lg) + 1))
    rr = lax.broadcasted_iota(jnp.int32, (W, W), 0)
    cc = lax.broadcasted_iota(jnp.int32, (W, W), 1)
    same_head = (rr < N) == (cc < N)
    ones_bd = jnp.where(same_head, 1.0, 0.0).astype(BF16)
    tri_incl = jnp.where(lax.broadcasted_iota(jnp.int32, (C, C), 1)
                         <= lax.broadcasted_iota(jnp.int32, (C, C), 0), 1.0, 0.0).astype(BF16)
    lane_head0 = ln < N

    def bd(y):
        return jnp.concatenate([jnp.where(lane_head0, y, 0.0), jnp.where(lane_head0, 0.0, y)],
                               axis=0).astype(BF16)

    def pmm(xp, ybd):
        return _dot(xp.astype(BF16), ybd)

    def head_sum(xp):
        hi, lo = _split2(xp)
        return _dot(hi, ones_bd) + _dot(lo, ones_bd)

    def chunk_step(ci, pi):
        rows = pl.ds(pl.multiple_of(ci * C, C), C)
        cols = pl.ds(pi * W, W)
        r = r_ref[rows, cols]
        k = k_ref[rows, cols]
        v = v_ref[rows, cols]
        dl = dl_ref[rows, cols]
        a = a_ref[rows, cols]
        k_k = kk_ref[:, cols]
        k_a = ka_ref[:, cols]
        r_k = rk_ref[:, cols]

        kk = k * k_k
        kk = kk / jnp.maximum(jnp.sqrt(head_sum(kk * kk)), 1e-12)
        k2 = k * (1.0 + (a - 1.0) * k_a)
        b = a * kk

        d_hi, d_mid, d_lo = _split3(dl)
        cum = _dot(tri_incl, d_hi) + _dot(tri_incl, d_mid) + _dot(tri_incl, d_lo)
        p_incl = jnp.exp(cum)
        p_excl = jnp.exp(cum - dl)
        p_inv = jnp.exp(-cum)
        kq = kk * p_excl
        rq = r * p_incl
        bk = b * p_inv
        kkd = k2 * p_inv

        lhs = jnp.concatenate([kq, rq], axis=0).astype(BF16)
        ab = _dot_nt(lhs, bd(bk))
        gb = _dot_nt(lhs, bd(kkd))
        a_m = jnp.where(strict_lower, ab[:C], 0.0)
        aq_m = jnp.where(lower, ab[C:], 0.0)
        g_m = jnp.where(strict_lower, gb[:C], 0.0)
        gq_m = jnp.where(lower, gb[C:], 0.0)

        tinv = eye - jnp.where(level_masks[0], a_m, 0.0)
        for lm in level_masks[1:]:
            w_ = pmm(jnp.where(lm, a_m, 0.0), bd(tinv))
            tinv = tinv - pmm(tinv, bd(w_))

        gv = pmm(jnp.concatenate([g_m, gq_m], axis=0), bd(v))
        kq_y = pmm(tinv, jnp.concatenate([bd(kq), bd(gv[:C])], axis=1))
        kq1 = kq_y[:, :W]
        y = kq_y[:, W:]
        aq_ky = pmm(aq_m, jnp.concatenate([bd(kq1), bd(y)], axis=1))
        rq1 = rq - aq_ky[:, :W]
        o_loc = gv[C:] - aq_ky[:, W:]

        m0 = state_ref[pi]
        st = _dot(jnp.concatenate([rq1, kq1], axis=0).astype(BF16), m0.astype(BF16))
        o = st[:C] + o_loc
        z = -(st[C:] + y)

        bkt = jnp.transpose(jnp.concatenate([b, k2], axis=0))
        cumt = jnp.transpose(jnp.concatenate([cum, cum], axis=0))
        last_col = cumt[:, C - 1:C]
        lhs_t = (bkt * jnp.exp(last_col - cumt)).astype(BF16)
        upd = _dot(lhs_t, jnp.concatenate([z, v], axis=0).astype(BF16))
        state_ref[pi] = m0 * jnp.exp(last_col) + jnp.where(same_head, upd, 0.0)

        mu_o = head_sum(o) * (1.0 / N)
        oc = o - mu_o
        var_o = head_sum(oc * oc) * (1.0 / N)
        out = oc * lax.rsqrt(var_o + GN_EPS) * gng_ref[:, cols] + gnb_ref[:, cols]
        out = out + head_sum(r * k2 * r_k) * v
        o_ref[rows, cols] = out * g_ref[rows, cols]

    def body(ci, carry):
        for pi in range(n_pairs):
            chunk_step(ci, pi)
        return carry

    lax.fori_loop(0, n_chunks, body, 0)


def _rwkv_scan(r, k, v, dl, a, g, k_k, k_a, r_k, gn_g, gn_b, batch, seq, n_pairs=2, tb=512):
    m, d = r.shape
    w = n_pairs * 2 * RWKV_HEAD
    blk = pl.BlockSpec((tb, w), lambda b, p, t: (b * (seq // tb) + t, p))
    par = pl.BlockSpec((1, w), lambda b, p, t: (0, p))
    vec = lambda z: z.reshape(1, d)
    return pl.pallas_call(
        functools.partial(_rwkv_scan_kernel, n_pairs=n_pairs, n_chunks=tb // SCAN_CHUNK),
        grid=(batch, d // w, seq // tb),
        in_specs=[blk] * 6 + [par] * 5,
        out_specs=blk,
        out_shape=jax.ShapeDtypeStruct((m, d), F32),
        scratch_shapes=[pltpu.VMEM((n_pairs, 2 * RWKV_HEAD, 2 * RWKV_HEAD), F32)],
        compiler_params=_cparams("parallel", "parallel", "arbitrary"),
        name="rwkv_scan",
    )(r, k, v, dl, a, g, vec(k_k), vec(k_a), vec(r_k), vec(gn_g), vec(gn_b))


def _proj_ln_kernel(y_ref, x_ref, w_ref, lng_ref, lnb_ref, o_ref):
    mix = _dot(y_ref[...].astype(BF16), w_ref[...])
    o_ref[...] = _layer_norm(DEEPNORM_ALPHA * x_ref[...] + mix, lng_ref[...], lnb_ref[...])


def _proj_ln(y, x2d, w, ln_g, ln_b, tm=512):
    m, d = x2d.shape
    row = lambda i: (i, 0)
    return pl.pallas_call(
        _proj_ln_kernel,
        grid=(m // tm,),
        in_specs=[pl.BlockSpec((tm, y.shape[1]), row), pl.BlockSpec((tm, d), row), _resident(w.shape),
                  _resident((1, d)), _resident((1, d))],
        out_specs=pl.BlockSpec((tm, d), row),
        out_shape=jax.ShapeDtypeStruct((m, d), F32),
        compiler_params=_cparams("parallel"),
        name="proj_ln",
    )(y, x2d, w, ln_g.reshape(1, d), ln_b.reshape(1, d))


def _proj_ln_router_kernel(y_ref, x_ref, w_ref, lng_ref, lnb_ref, wr_ref, br_ref, o_ref, lg_ref):
    mix = _dot(y_ref[...].astype(BF16), w_ref[...])
    xn = _layer_norm(DEEPNORM_ALPHA * x_ref[...] + mix, lng_ref[...], lnb_ref[...])
    o_ref[...] = xn
    xh, xm, xl = _split3(xn)
    wh, wm, wl = wr_ref[0], wr_ref[1], wr_ref[2]
    lg = (_dot(xh, wh) + _dot(xh, wm) + _dot(xm, wh) + _dot(xh, wl) + _dot(xl, wh) + _dot(xm, wm))
    lg_ref[...] = lg + br_ref[...]


def _proj_ln_router(y, x2d, w, ln_g, ln_b, w_router, b_router, tm=512):
    m, d = x2d.shape
    row = lambda i: (i, 0)
    wr = jnp.zeros((d, LANES), F32).at[:, :N_EXPERTS].set(w_router)
    wr3 = jnp.stack(_split3(wr))
    br = jnp.zeros((1, LANES), F32).at[0, :N_EXPERTS].set(b_router)
    return pl.pallas_call(
        _proj_ln_router_kernel,
        grid=(m // tm,),
        in_specs=[pl.BlockSpec((tm, y.shape[1]), row), pl.BlockSpec((tm, d), row), _resident(w.shape),
                  _resident((1, d)), _resident((1, d)), _resident(wr3.shape), _resident(br.shape)],
        out_specs=[pl.BlockSpec((tm, d), row), pl.BlockSpec((tm, LANES), row)],
        out_shape=[jax.ShapeDtypeStruct((m, d), F32), jax.ShapeDtypeStruct((m, LANES), F32)],
        compiler_params=_cparams("parallel"),
        name="proj_ln_router",
    )(y, x2d, w, ln_g.reshape(1, d), ln_b.reshape(1, d), wr3, br)


def _ple(xn, p, plew_ref, gw_ref, gb_ref):
    gate = _sigmoid(_dot(xn.astype(BF16), gw_ref[...]) + gb_ref[...])
    return xn + _dot(p.astype(BF16), plew_ref[...]) * gate


def _ffn_ple_kernel(x_ref, p_ref, wg_ref, wu_ref, wd_ref, lng_ref, lnb_ref, plew_ref, gw_ref, gb_ref,
                    o_ref, *, ff_chunk):
    x = x_ref[...]
    xb = x.astype(BF16)
    acc = jnp.zeros_like(x)
    for c in range(wg_ref.shape[1] // ff_chunk):
        sl = slice(c * ff_chunk, (c + 1) * ff_chunk)
        gt = _dot(xb, wg_ref[:, sl])
        up = _dot(xb, wu_ref[:, sl])
        h = (gt * _sigmoid(gt) * up).astype(BF16)
        acc = acc + _dot(h, wd_ref[sl, :])
    xn = _layer_norm(DEEPNORM_ALPHA * x + acc, lng_ref[...], lnb_ref[...])
    o_ref[...] = _ple(xn, p_ref[...], plew_ref, gw_ref, gb_ref)


def _ffn_ple(x2d, p2d, wg, wu, wd, ln_g, ln_b, ple_w, gate_w, gate_b, tm=256, ff_chunk=1408):
    m, d = x2d.shape
    row = lambda i: (i, 0)
    return pl.pallas_call(
        functools.partial(_ffn_ple_kernel, ff_chunk=ff_chunk),
        grid=(m // tm,),
        in_specs=[pl.BlockSpec((tm, d), row), pl.BlockSpec((tm, p2d.shape[1]), row),
                  _resident(wg.shape), _resident(wu.shape), _resident(wd.shape),
                  _resident((1, d)), _resident((1, d)), _resident(ple_w.shape), _resident(gate_w.shape),
                  _resident((1, d))],
        out_specs=pl.BlockSpec((tm, d), row),
        out_shape=jax.ShapeDtypeStruct((m, d), F32),
        compiler_params=_cparams("parallel"),
        name="ffn_ple",
    )(x2d, p2d, wg, wu, wd, ln_g.reshape(1, d), ln_b.reshape(1, d), ple_w, gate_w, gate_b.reshape(1, d))


def _res_ln_ple_kernel(x_ref, f_ref, p_ref, lng_ref, lnb_ref, plew_ref, gw_ref, gb_ref, o_ref):
    xn = _layer_norm(DEEPNORM_ALPHA * x_ref[...] + f_ref[...], lng_ref[...], lnb_ref[...])
    o_ref[...] = _ple(xn, p_ref[...], plew_ref, gw_ref, gb_ref)


def _res_ln_ple(x2d, f2d, p2d, ln_g, ln_b, ple_w, gate_w, gate_b, tm=512):
    m, d = x2d.shape
    row = lambda i: (i, 0)
    return pl.pallas_call(
        _res_ln_ple_kernel,
        grid=(m // tm,),
        in_specs=[pl.BlockSpec((tm, d), row), pl.BlockSpec((tm, d), row),
                  pl.BlockSpec((tm, p2d.shape[1]), row), _resident((1, d)), _resident((1, d)),
                  _resident(ple_w.shape), _resident(gate_w.shape), _resident((1, d))],
        out_specs=pl.BlockSpec((tm, d), row),
        out_shape=jax.ShapeDtypeStruct((m, d), F32),
        compiler_params=_cparams("parallel"),
        name="res_ln_ple",
    )(x2d, f2d, p2d, ln_g.reshape(1, d), ln_b.reshape(1, d), ple_w, gate_w, gate_b.reshape(1, d))


def _swap_half_cols(w):
    k, n = w.shape
    return w.reshape(k, n // HEAD_DIM, 2, HEAD_DIM // 2)[:, :, ::-1, :].reshape(k, n)


def _rope_tables(seq):
    half = HEAD_DIM // 2
    inv = ROPE_THETA ** (-jnp.arange(half, dtype=F32) / half)
    ang = jnp.arange(seq, dtype=F32)[:, None] * inv[None, :]
    cos, sin = jnp.cos(ang), jnp.sin(ang)
    return jnp.concatenate([cos, cos], -1), jnp.concatenate([-sin, sin], -1)


def _nsa_kv_kernel(x_ref, w_ref, cos_ref, sin_ref, zc_ref, vc_ref, ks_ref, vs_ref, kw_ref, vw_ref):
    G, dh = N_KV_GROUPS, HEAD_DIM
    gw = G * dh
    res = _dot(x_ref[...].astype(BF16), w_ref[...])
    tm = res.shape[0]
    cos = cos_ref[...]
    sin = sin_ref[...]
    ks = res[:, 2 * gw:3 * gw] * cos + res[:, 3 * gw:4 * gw] * sin
    kw = res[:, 5 * gw:6 * gw] * cos + res[:, 6 * gw:7 * gw] * sin
    t_glob = pl.program_id(1) * tm + lax.broadcasted_iota(jnp.int32, (tm, dh), 0)
    onehot = jnp.where(lax.broadcasted_iota(jnp.int32, (tm, dh), 1) == t_glob // SEL_BLOCK, 1.0, 0.0)
    for g in range(G):
        sl = slice(g * dh, (g + 1) * dh)
        zc_ref[0, g] = res[:, sl].astype(BF16)
        vc_ref[0, g] = res[:, gw + g * dh:gw + (g + 1) * dh].astype(BF16)
        ks_ref[0, g] = jnp.concatenate([ks[:, sl], onehot], axis=1).astype(BF16)
        vs_ref[0, g] = res[:, 4 * gw + g * dh:4 * gw + (g + 1) * dh].astype(BF16)
        kw_ref[0, g] = kw[:, sl].astype(BF16)
        vw_ref[0, g] = res[:, 7 * gw + g * dh:7 * gw + (g + 1) * dh].astype(BF16)


def _nsa_kv(x2d, w_kv, batch, seq, tm=256):
    m, d = x2d.shape
    G, dh = N_KV_GROUPS, HEAD_DIM
    gw = G * dh
    parts = [w_kv[:, j * gw:(j + 1) * gw] for j in range(6)]
    w_ext = jnp.concatenate([parts[0], parts[1], parts[2], _swap_half_cols(parts[2]), parts[3],
                             parts[4], _swap_half_cols(parts[4]), parts[5]], axis=1).astype(BF16)
    cos, sin = _rope_tables(seq)
    cos = jnp.tile(cos, (1, G))
    sin = jnp.tile(sin, (1, G))
    nt = seq // tm
    o64 = jax.ShapeDtypeStruct((batch, G, seq, dh), BF16)
    o128 = jax.ShapeDtypeStruct((batch, G, seq, 2 * dh), BF16)
    b64 = pl.BlockSpec((1, G, tm, dh), lambda b, t: (b, 0, t, 0))
    b128 = pl.BlockSpec((1, G, tm, 2 * dh), lambda b, t: (b, 0, t, 0))
    return pl.pallas_call(
        _nsa_kv_kernel,
        grid=(batch, nt),
        in_specs=[pl.BlockSpec((tm, d), lambda b, t: (b * nt + t, 0)), _resident(w_ext.shape),
                  pl.BlockSpec((tm, gw), lambda b, t: (t, 0)), pl.BlockSpec((tm, gw), lambda b, t: (t, 0))],
        out_specs=[b64, b64, b128, b64, b64, b64],
        out_shape=[o64, o64, o128, o64, o64, o64],
        compiler_params=_cparams("parallel", "parallel"),
        name="nsa_kv",
    )(x2d, w_ext, cos, sin)


def _nsa_cmp_kernel(z_ref, pos_ref, w1_ref, b1_ref, w2_ref, o_ref, *, slabs_per_seq):
    half = w1_ref.shape[1] // 2
    z = z_ref[0]
    tm = z.shape[0]
    first = _dot(z, w1_ref[0, :half, :])
    second = _dot(z, w1_ref[0, half:, :])
    const = _dot(pos_ref[0], w1_ref[0])[0:1, :] + b1_ref[0]
    hid = first + pltpu.roll(second, shift=tm - 1, axis=0) + const
    act = 0.5 * hid * (1.0 + jnp.tanh(math.sqrt(2.0 / math.pi) * (hid + 0.044715 * hid * hid * hid)))
    out = _dot(act.astype(BF16), w2_ref[0])
    row = lax.broadcasted_iota(jnp.int32, out.shape, 0)
    o_ref[0] = jnp.where(row % slabs_per_seq == slabs_per_seq - 1, 0.0, out).astype(BF16)


def _nsa_compress(zc, vc, cmp_pos, phi_w1, phi_b1, phi_w2, tm=512):
    batch, G, seq, dh = zc.shape
    slab = CMP_STRIDE * dh
    rows = batch * G * seq // CMP_STRIDE
    tm = min(tm, rows)
    z = jnp.stack([zc.reshape(rows, slab), vc.reshape(rows, slab)])
    pos = jnp.broadcast_to(cmp_pos.reshape(2, 1, CMP_LEN * dh), (2, 8, CMP_LEN * dh)).astype(BF16)
    out = pl.pallas_call(
        functools.partial(_nsa_cmp_kernel, slabs_per_seq=seq // CMP_STRIDE),
        grid=(2, rows // tm),
        in_specs=[pl.BlockSpec((1, tm, slab), lambda j, i: (j, i, 0)),
                  pl.BlockSpec((1, 8, CMP_LEN * dh), lambda j, i: (j, 0, 0)),
                  pl.BlockSpec((1, CMP_LEN * dh, CMP_HIDDEN), lambda j, i: (j, 0, 0)),
                  pl.BlockSpec((1, 1, CMP_HIDDEN), lambda j, i: (j, 0, 0)),
                  pl.BlockSpec((1, CMP_HIDDEN, dh), lambda j, i: (j, 0, 0))],
        out_specs=pl.BlockSpec((1, tm, dh), lambda j, i: (j, i, 0)),
        out_shape=jax.ShapeDtypeStruct((2, rows, dh), BF16),
        compiler_params=_cparams("parallel", "parallel"),
        name="nsa_compress",
    )(z, pos, phi_w1.astype(BF16), phi_b1.reshape(2, 1, CMP_HIDDEN), phi_w2.astype(BF16))
    n_slab = seq // CMP_STRIDE
    return out[0].reshape(batch, G, n_slab, dh), out[1].reshape(batch, G, n_slab, dh)


def _nsa_q_kernel(x_ref, w_ref, cos_ref, sin_ref, q_ref, qr_ref, gate_ref):
    G, R, dh = N_KV_GROUPS, Q_PER_GROUP, HEAD_DIM
    hw = N_HEADS * dh
    scale = HEAD_DIM ** -0.5
    res = _dot(x_ref[...].astype(BF16), w_ref[...])
    q = res[:, :hw]
    qr = q * cos_ref[...] + res[:, hw:2 * hw] * sin_ref[...]
    for g in range(G):
        for r in range(R):
            sl = slice((g * R + r) * dh, (g * R + r + 1) * dh)
            q_ref[0, g, r] = (q[:, sl] * scale).astype(BF16)
            qr_ref[0, g, r] = (qr[:, sl] * scale).astype(BF16)
    gate_ref[...] = _sigmoid(res[:, 2 * hw:])


def _nsa_q(x2d, w_in, batch, seq, tm=256):
    m, d = x2d.shape
    G, R, dh = N_KV_GROUPS, Q_PER_GROUP, HEAD_DIM
    hw = N_HEADS * dh
    ng = N_BRANCH * N_HEADS
    w_gate = jnp.zeros((d, LANES), F32).at[:, :ng].set(w_in[:, hw:])
    w_ext = jnp.concatenate([w_in[:, :hw], _swap_half_cols(w_in[:, :hw]), w_gate], axis=1).astype(BF16)
    cos, sin = _rope_tables(seq)
    cos = jnp.tile(cos, (1, N_HEADS))
    sin = jnp.tile(sin, (1, N_HEADS))
    nt = seq // tm
    oq = jax.ShapeDtypeStruct((batch, G, R, seq, dh), BF16)
    bq = pl.BlockSpec((1, G, R, tm, dh), lambda b, t: (b, 0, 0, t, 0))
    q, qr, gates = pl.pallas_call(
        _nsa_q_kernel,
        grid=(batch, nt),
        in_specs=[pl.BlockSpec((tm, d), lambda b, t: (b * nt + t, 0)), _resident(w_ext.shape),
                  pl.BlockSpec((tm, hw), lambda b, t: (t, 0)), pl.BlockSpec((tm, hw), lambda b, t: (t, 0))],
        out_specs=[bq, bq, pl.BlockSpec((tm, LANES), lambda b, t: (b * nt + t, 0))],
        out_shape=[oq, oq, jax.ShapeDtypeStruct((m, LANES), F32)],
        compiler_params=_cparams("parallel", "parallel"),
        name="nsa_q",
    )(x2d, w_ext, cos, sin)
    gates = gates[:, :ng].reshape(batch, seq, G, R * N_BRANCH).transpose(0, 2, 1, 3)
    return q, qr, gates


def _nsa_attn_kernel(q_ref, qr_ref, gate_ref, kc_ref, vc_ref, ks_ref, vs_ref, kw_ref, vw_ref, ovl_ref,
                     o_ref, *, tq, tk, n_blk):
    R, dh = Q_PER_GROUP, HEAD_DIM
    rows = R * tq
    qi = pl.program_id(2)
    t0 = qi * tq
    q = q_ref[0, 0].reshape(rows, dh)
    qr = qr_ref[0, 0].reshape(rows, dh)

    def row_pos(shape):
        return t0 + (lax.broadcasted_iota(jnp.int32, shape, 0) & (tq - 1))

    s = _dot_nt(q, kc_ref[0, 0])
    cmp_end = lax.broadcasted_iota(jnp.int32, s.shape, 1) * CMP_STRIDE + (CMP_LEN - 1)
    valid = cmp_end <= row_pos(s.shape)
    sm = jnp.where(valid, s, NEG_INF)
    e = jnp.where(valid, jnp.exp(sm - jnp.max(sm, axis=-1, keepdims=True)), 0.0)
    l = jnp.sum(e, axis=-1, keepdims=True)
    p_cmp = e / jnp.where(l > 0.0, l, 1.0)
    o_cmp = _dot(p_cmp.astype(BF16), vc_ref[0, 0])

    p_sum = p_cmp[0:tq]
    for r in range(1, R):
        p_sum = p_sum + p_cmp[r * tq:(r + 1) * tq]
    hi, lo = _split2(p_sum)
    imp = _dot(hi, ovl_ref[...]) + _dot(lo, ovl_ref[...])
    blk = lax.broadcasted_iota(jnp.int32, imp.shape, 1)
    t_q = t0 + lax.broadcasted_iota(jnp.int32, imp.shape, 0)
    cur = t_q // SEL_BLOCK
    forced = (blk == 0) | (blk == cur) | (blk == cur - 1)
    score = jnp.where(forced, FORCE_SCORE, jnp.where(blk * SEL_BLOCK <= t_q, imp, NEG_INF))
    score = jnp.where(blk < n_blk, score, -jnp.inf)
    selected = jnp.zeros(imp.shape, jnp.bool_)
    for _ in range(N_SEL):
        best = jnp.max(score, axis=-1, keepdims=True)
        first = jnp.min(jnp.where(score == best, blk, 2 * LANES), axis=-1, keepdims=True)
        pick = blk == first
        selected = selected | pick
        score = jnp.where(pick, -jnp.inf, score)
    bias = jnp.where(selected | (blk >= n_blk), 0.0, NEG_INF)[:, :dh]
    qa = jnp.concatenate([qr.astype(F32), jnp.concatenate([bias] * R, axis=0)], axis=1).astype(BF16)

    def sweep(q_op, k_ref, v_ref, width, lo_tile, hi_tile, mask_fn):
        def body(j, carry):
            m_i, l_i, acc = carry
            ks = pl.ds(pl.multiple_of(j * width, width), width)
            sc = _dot_nt(q_op, k_ref[0, 0, ks, :])
            kpos = j * width + lax.broadcasted_iota(jnp.int32, sc.shape, 1)
            sc = jnp.where(mask_fn(row_pos(sc.shape), kpos), sc, NEG_INF)
            m_new = jnp.maximum(m_i, jnp.max(sc, axis=-1, keepdims=True))
            alpha = jnp.exp(m_i - m_new)
            p = jnp.exp(sc - m_new)
            l_new = alpha * l_i + jnp.sum(p, axis=-1, keepdims=True)
            acc_new = alpha * acc + _dot(p.astype(BF16), v_ref[0, 0, ks, :])
            return m_new, l_new, acc_new

        init = (jnp.full((rows, 1), NEG_INF, F32), jnp.zeros((rows, 1), F32), jnp.zeros((rows, dh), F32))
        _, l_f, acc_f = lax.fori_loop(lo_tile, hi_tile, body, init)
        return acc_f / l_f

    o_slc = sweep(qa, ks_ref, vs_ref, tk, 0, (t0 + tq + tk - 1) // tk, lambda t, kp: kp <= t)

    lo_tile = jnp.maximum(t0 - WINDOW, 0) // tq
    o_win = sweep(qr, kw_ref, vw_ref, tq, lo_tile, qi + 1,
                  lambda t, kp: (kp <= t) & (t - kp < WINDOW))

    gates = gate_ref[0, 0]
    for r in range(R):
        rs = slice(r * tq, (r + 1) * tq)
        o_ref[0, :, r * dh:(r + 1) * dh] = (gates[:, 3 * r:3 * r + 1] * o_cmp[rs]
                                             + gates[:, 3 * r + 1:3 * r + 2] * o_slc[rs]
                                             + gates[:, 3 * r + 2:3 * r + 3] * o_win[rs])


def _nsa_attention(q, qr, gates, k_cmp, v_cmp, ks, vs, kw, vw, tq=128, tk=256):
    batch, G, R, seq, dh = q.shape
    n_slab = k_cmp.shape[2]
    n_blk = seq // SEL_BLOCK
    cmp_start = jnp.arange(n_slab) * CMP_STRIDE
    blk_start = jnp.arange(LANES) * SEL_BLOCK
    overlap = ((cmp_start[:, None] < blk_start[None, :] + SEL_BLOCK)
               & (cmp_start[:, None] + CMP_LEN - 1 >= blk_start[None, :])
               & (jnp.arange(LANES)[None, :] < n_blk)
               & (jnp.arange(n_slab)[:, None] < (seq - CMP_LEN) // CMP_STRIDE + 1)).astype(BF16)
    qspec = pl.BlockSpec((1, 1, R, tq, dh), lambda b, g, i: (b, g, 0, i, 0))
    full = lambda a: pl.BlockSpec((1, 1) + a.shape[2:], lambda b, g, i: (b, g, 0, 0))
    return pl.pallas_call(
        functools.partial(_nsa_attn_kernel, tq=tq, tk=tk, n_blk=n_blk),
        grid=(batch, G, seq // tq),
        in_specs=[qspec, qspec, pl.BlockSpec((1, 1, tq, R * N_BRANCH), lambda b, g, i: (b, g, i, 0)),
                  full(k_cmp), full(v_cmp), full(ks), full(vs), full(kw), full(vw),
                  _resident(overlap.shape)],
        out_specs=pl.BlockSpec((1, tq, R * dh), lambda b, g, i: (b, i, g)),
        out_shape=jax.ShapeDtypeStruct((batch, seq, G * R * dh), F32),
        compiler_params=_cparams("parallel", "parallel", "parallel"),
        name="nsa_attn",
    )(q, qr, gates, k_cmp, v_cmp, ks, vs, kw, vw, overlap)


def _moe_kernel(be_ref, nb_ref, x_ref, wg_ref, wu_ref, wd_ref, o_ref):
    i = pl.program_id(0)

    @pl.when(i < nb_ref[0])
    def _():
        xb = x_ref[...].astype(BF16)
        gt = _dot(xb, wg_ref[0])
        up = _dot(xb, wu_ref[0])
        h = (gt * _sigmoid(gt) * up).astype(BF16)
        o_ref[...] = _dot(h, wd_ref[0])

    @pl.when(i >= nb_ref[0])
    def _():
        o_ref[...] = jnp.zeros_like(o_ref)


def _moe_experts(xs, block_e, n_used, wg, wu, wd):
    n_rows, d = xs.shape
    n_blocks = n_rows // MOE_ROWS
    ff = wg.shape[2]
    grid_spec = pltpu.PrefetchScalarGridSpec(
        num_scalar_prefetch=2,
        grid=(n_blocks,),
        in_specs=[pl.BlockSpec((MOE_ROWS, d), lambda i, be, nb: (i, 0)),
                  pl.BlockSpec((1, d, ff), lambda i, be, nb: (be[i], 0, 0)),
                  pl.BlockSpec((1, d, ff), lambda i, be, nb: (be[i], 0, 0)),
                  pl.BlockSpec((1, ff, d), lambda i, be, nb: (be[i], 0, 0))],
        out_specs=pl.BlockSpec((MOE_ROWS, d), lambda i, be, nb: (i, 0)),
    )
    return pl.pallas_call(
        _moe_kernel,
        grid_spec=grid_spec,
        out_shape=jax.ShapeDtypeStruct((n_rows, d), F32),
        compiler_params=_cparams("arbitrary"),
        name="moe_experts",
    )(block_e, n_used, xs, wg, wu, wd)


def _moe(x2d, logits, w_gu, w_down):
    n_tok, d = x2d.shape
    n_assign = n_tok * TOP_K
    top_logit, top_e = lax.top_k(logits, TOP_K)
    weights = jax.nn.softmax(top_logit, axis=-1)
    flat_e = top_e.reshape(-1)
    order = jnp.argsort(flat_e)
    e_sorted = flat_e[order]
    tok_sorted = (order // TOP_K).astype(jnp.int32)
    counts = jnp.bincount(flat_e, length=N_EXPERTS)
    padded = (counts + MOE_ROWS - 1) // MOE_ROWS * MOE_ROWS
    pad_end = jnp.cumsum(padded)
    pad_start = pad_end - padded
    grp_start = jnp.cumsum(counts) - counts
    dest = (pad_start[e_sorted] + jnp.arange(n_assign) - grp_start[e_sorted]).astype(jnp.int32)
    n_blocks = -(-n_assign // MOE_ROWS) + N_EXPERTS
    n_rows = n_blocks * MOE_ROWS
    row_tok = jnp.zeros((n_rows,), jnp.int32).at[dest].set(tok_sorted)
    block_e = jnp.minimum(jnp.searchsorted(pad_end, jnp.arange(n_blocks) * MOE_ROWS, side='right'),
                          N_EXPERTS - 1).astype(jnp.int32)
    n_used = (pad_end[-1] // MOE_ROWS).astype(jnp.int32).reshape(1)
    xs = x2d[row_tok]
    ys = _moe_experts(xs, block_e, n_used, w_gu[:, :, :D_FF_EXPERT].astype(BF16),
                      w_gu[:, :, D_FF_EXPERT:].astype(BF16), w_down.astype(BF16))
    pos = jnp.zeros((n_assign,), jnp.int32).at[order].set(dest).reshape(n_tok, TOP_K)
    out = ys[pos[:, 0]] * weights[:, 0:1]
    for s in range(1, TOP_K):
        out = out + ys[pos[:, s]] * weights[:, s:s + 1]
    return out


def kernel(x, p, a_mu, a_w_rkv, a_w0, a_w1, a_w2, a_a0, a_a1, a_a2, a_g1, a_g2, a_k_k, a_k_a, a_r_k,
           a_gn_g, a_gn_b, a_w_o, b_w_kv, b_cmp_pos, b_phi_w1, b_phi_b1, b_phi_w2, b_w_in, b_w_o,
           f_w_gu, f_w_down, m_w_router, m_b_router, m_w_gu, m_w_down, ln_g, ln_b, ple_w, ple_gate_w,
           ple_gate_b):
    batch, seq, d = x.shape
    m = batch * seq
    x0 = x.reshape(m, d)
    p2d = p.reshape(DEPTH, m, PLE_DIM)
    bf = lambda w: w.astype(BF16)

    r, k, v, dl, a, g = _rwkv_proj(
        x0, seq, a_mu[0], bf(a_w_rkv[0]), a_w0[0].reshape(1, d), bf(a_w1[0]), bf(a_w2[0]),
        a_a0[0].reshape(1, d), bf(a_a1[0]), bf(a_a2[0]), bf(a_g1[0]), bf(a_g2[0]))
    y = _rwkv_scan(r, k, v, dl, a, g, a_k_k[0], a_k_a[0], a_r_k[0], a_gn_g[0], a_gn_b[0], batch, seq)
    x1 = _proj_ln(y, x0, bf(a_w_o[0]), ln_g[0, 0], ln_b[0, 0])
    x2 = _ffn_ple(x1, p2d[0], bf(f_w_gu[0][:, :D_FF]), bf(f_w_gu[0][:, D_FF:]), bf(f_w_down[0]),
                  ln_g[0, 1], ln_b[0, 1], bf(ple_w[0]), bf(ple_gate_w[0]), ple_gate_b[0])

    zc, vc, ks, vs, kw, vw = _nsa_kv(x2, b_w_kv, batch, seq)
    k_cmp, v_cmp = _nsa_compress(zc, vc, b_cmp_pos, b_phi_w1, b_phi_b1, b_phi_w2)
    q, qr, gates = _nsa_q(x2, b_w_in[0], batch, seq)
    attn = _nsa_attention(q, qr, gates, k_cmp, v_cmp, ks, vs, kw, vw).reshape(m, d)
    x3, logits = _proj_ln_router(attn, x2, bf(b_w_o[0]), ln_g[1, 0], ln_b[1, 0], m_w_router[0],
                                 m_b_router[0])
    ffn = _moe(x3, logits[:, :N_EXPERTS], m_w_gu[0], m_w_down[0])
    out = _res_ln_ple(x3, ffn, p2d[1], ln_g[1, 1], ln_b[1, 1], bf(ple_w[1]), bf(ple_gate_w[1]),
                      ple_gate_b[1])
    return out.reshape(batch, seq, d)
```

```python
import functools
import math

import jax
import jax.numpy as jnp
from jax import lax
from jax.experimental import pallas as pl
from jax.experimental.pallas import tpu as pltpu

BF16 = jnp.bfloat16
F32 = jnp.float32

LANES = 128
VMEM_LIMIT_BYTES = 56 * 1024 * 1024

D_MODEL = 1024
PLE_DIM = 256
RWKV_HEAD = 64
GN_EPS = 64e-5
N_HEADS = 16
HEAD_DIM = 64
N_KV_GROUPS = 4
Q_PER_GROUP = 4
N_BRANCH = 3
CMP_LEN = 32
CMP_STRIDE = 16
CMP_HIDDEN = 256
SEL_BLOCK = 64
N_SEL = 8
WINDOW = 512
ROPE_THETA = 10000.0
D_FF = 2816
N_EXPERTS = 8
TOP_K = 2
D_FF_EXPERT = 1408
MOE_ROWS = 256
LN_EPS = 1e-5
DEPTH = 2
DEEPNORM_ALPHA = (2.0 * DEPTH) ** 0.25
NEG_INF = -1e30
FORCE_SCORE = 1e4

SCAN_CHUNK = 64


def _cparams(*sem):
    return pltpu.CompilerParams(dimension_semantics=sem, vmem_limit_bytes=VMEM_LIMIT_BYTES)


def _resident(shape):
    nd = len(shape)
    return pl.BlockSpec(shape, lambda *_: (0,) * nd, pipeline_mode=pl.Buffered(1))


def _dot(a, b):
    return jnp.dot(a, b, preferred_element_type=F32)


def _dot_nt(a, b):
    return lax.dot_general(a, b, (((1,), (1,)), ((), ())), preferred_element_type=F32)


def _split2(x):
    hi = x.astype(BF16)
    lo = (x - hi.astype(F32)).astype(BF16)
    return hi, lo


def _split3(x):
    hi = x.astype(BF16)
    r1 = x - hi.astype(F32)
    mid = r1.astype(BF16)
    lo = (r1 - mid.astype(F32)).astype(BF16)
    return hi, mid, lo


def _layer_norm(y, g, b):
    mu = jnp.mean(y, axis=-1, keepdims=True)
    yc = y - mu
    var = jnp.mean(yc * yc, axis=-1, keepdims=True)
    return yc * lax.rsqrt(var + LN_EPS) * g + b


def _sigmoid(z):
    return 1.0 / (1.0 + jnp.exp(-z))


def _rwkv_proj_kernel(x_ref, xh_ref, mu_ref, wrkv_ref, w0_ref, w1_ref, w2_ref, a0_ref, a1_ref,
                      a2_ref, g1_ref, g2_ref, r_ref, k_ref, v_ref, dl_ref, a_ref, g_ref, *,
                      tiles_per_seq):
    x = x_ref[...]
    tm = x.shape[0]
    first = (pl.program_id(0) % tiles_per_seq) == 0
    prev_row = jnp.where(first, 0.0, xh_ref[0, 7:8, :])
    row = lax.broadcasted_iota(jnp.int32, x.shape, 0)
    x_shift = jnp.where(row == 0, prev_row, pltpu.roll(x, shift=1, axis=0))
    xx = x_shift - x

    def mix(i):
        return (x + xx * mu_ref[i:i + 1, :]).astype(BF16)

    r_ref[...] = _dot(mix(0), wrkv_ref[0])
    k_ref[...] = _dot(mix(1), wrkv_ref[1])
    v_ref[...] = _dot(mix(2), wrkv_ref[2])
    z = w0_ref[...] + _dot(jnp.tanh(_dot(mix(3), w1_ref[...])).astype(BF16), w2_ref[...])
    dl_ref[...] = -math.exp(-0.5) * _sigmoid(z)
    a_ref[...] = _sigmoid(a0_ref[...] + _dot(_dot(mix(4), a1_ref[...]).astype(BF16), a2_ref[...]))
    g_ref[...] = _dot(_sigmoid(_dot(mix(5), g1_ref[...])).astype(BF16), g2_ref[...])


def _rwkv_proj(x2d, seq, mu, w_rkv, w0, w1, w2, a0, a1, a2, g1, g2, tm=256):
    m, d = x2d.shape
    xh = x2d.reshape(m // 8, 8, d)
    row = lambda i: (i, 0)
    out = jax.ShapeDtypeStruct((m, d), F32)
    return pl.pallas_call(
        functools.partial(_rwkv_proj_kernel, tiles_per_seq=seq // tm),
        grid=(m // tm,),
        in_specs=[
            pl.BlockSpec((tm, d), row),
            pl.BlockSpec((1, 8, d), lambda i: (jnp.maximum(i * (tm // 8) - 1, 0), 0, 0)),
            _resident(mu.shape), _resident(w_rkv.shape), _resident(w0.shape), _resident(w1.shape),
            _resident(w2.shape), _resident(a0.shape), _resident(a1.shape), _resident(a2.shape),
            _resident(g1.shape), _resident(g2.shape),
        ],
        out_specs=[pl.BlockSpec((tm, d), row)] * 6,
        out_shape=[out] * 6,
        compiler_params=_cparams("parallel"),
        name="rwkv_proj",
    )(x2d, xh, mu, w_rkv, w0, w1, w2, a0, a1, a2, g1, g2)


def _rwkv_scan_kernel(r_ref, k_ref, v_ref, dl_ref, a_ref, g_ref, kk_ref, ka_ref, rk_ref, gng_ref,
                      gnb_ref, o_ref, state_ref, kkn_ref, cum_ref, *, n_pairs, n_chunks):
    C = SCAN_CHUNK
    N = RWKV_HEAD
    W = 2 * N
    tb = n_chunks * C

    @pl.when(pl.program_id(2) == 0)
    def _():
        state_ref[...] = jnp.zeros_like(state_ref)

    tt = lax.broadcasted_iota(jnp.int32, (C, W), 0)
    ln = lax.broadcasted_iota(jnp.int32, (C, W), 1)
    ss = ln & (N - 1)
    strict_lower = ss < tt
    lower = ss <= tt
    eye = jnp.where(ss == tt, 1.0, 0.0)
    level_masks = []
    for lg in range(int(math.log2(C))):
        level_masks.append(((tt >> (lg + 1)) == (ss >> (lg + 1))) & ((tt >> lg) == (ss >> lg) + 1))
    rr = lax.broadcasted_iota(jnp.int32, (W, W), 0)
    cc = lax.broadcasted_iota(jnp.int32, (W, W), 1)
    same_head = (rr < N) == (cc < N)
    ones_bd = jnp.where(same_head, 1.0, 0.0).astype(BF16)
    lane_head0 = ln < N

    def bd(y):
        return jnp.concatenate([jnp.where(lane_head0, y, 0.0), jnp.where(lane_head0, 0.0, y)],
                               axis=0).astype(BF16)

    def pmm(xp, ybd):
        return _dot(xp.astype(BF16), ybd)

    def head_sum(xp):
        hi, lo = _split2(xp)
        return _dot(hi, ones_bd) + _dot(lo, ones_bd)

    tri_r = lax.broadcasted_iota(jnp.int32, (tb, tb), 0)
    tri_c = lax.broadcasted_iota(jnp.int32, (tb, tb), 1)
    lg_c = int(math.log2(C))
    tri_chunk = jnp.where((tri_c <= tri_r) & ((tri_c >> lg_c) == (tri_r >> lg_c)), 1.0, 0.0).astype(BF16)
    d_hi, d_mid, d_lo = _split3(dl_ref[...])
    cum_ref[...] = _dot(tri_chunk, d_hi) + _dot(tri_chunk, d_mid) + _dot(tri_chunk, d_lo)
    for pi in range(n_pairs):
        cols = slice(pi * W, (pi + 1) * W)
        kk_raw = k_ref[:, cols] * kk_ref[:, cols]
        kkn_ref[:, cols] = kk_raw / jnp.maximum(jnp.sqrt(head_sum(kk_raw * kk_raw)), 1e-12)

    def chunk_load(ci, pi):
        rows = pl.ds(pl.multiple_of(ci * C, C), C)
        cols = slice(pi * W, (pi + 1) * W)
        return (r_ref[rows, cols], k_ref[rows, cols], v_ref[rows, cols], dl_ref[rows, cols],
                a_ref[rows, cols], kkn_ref[rows, cols], cum_ref[rows, cols], state_ref[pi])

    def chunk_compute(pi, r, k, v, dl, a, kk, cum, m0):
        cols = slice(pi * W, (pi + 1) * W)
        k2 = k * (1.0 + (a - 1.0) * ka_ref[:, cols])
        b = a * kk
        p_incl = jnp.exp(cum)
        p_excl = jnp.exp(cum - dl)
        p_inv = jnp.exp(-cum)
        kq = kk * p_excl
        rq = r * p_incl
        bk = b * p_inv
        kkd = k2 * p_inv

        bkt = jnp.transpose(jnp.concatenate([b, k2], axis=0))
        cumt = jnp.transpose(jnp.concatenate([cum, cum], axis=0))
        last_col = cumt[:, C - 1:C]
        lhs_t = (bkt * jnp.exp(last_col - cumt)).astype(BF16)
        m_decayed = m0 * jnp.exp(last_col)

        lhs = jnp.concatenate([kq, rq], axis=0).astype(BF16)
        yield
        ab = _dot_nt(lhs, bd(bk))
        gb = _dot_nt(lhs, bd(kkd))
        a_m = jnp.where(strict_lower, ab[:C], 0.0)
        aq_m = jnp.where(lower, ab[C:], 0.0)
        g_m = jnp.where(strict_lower, gb[:C], 0.0)
        gq_m = jnp.where(lower, gb[C:], 0.0)

        tinv = eye - jnp.where(level_masks[0], a_m, 0.0)
        yield
        gv = pmm(jnp.concatenate([g_m, gq_m], axis=0), bd(v))
        for lm in level_masks[1:]:
            w_ = pmm(jnp.where(lm, a_m, 0.0), bd(tinv))
            yield
            tinv = tinv - pmm(tinv, bd(w_))
            yield

        kq_y = pmm(tinv, jnp.concatenate([bd(kq), bd(gv[:C])], axis=1))
        kq1 = kq_y[:, :W]
        y = kq_y[:, W:]
        yield
        aq_ky = pmm(aq_m, jnp.concatenate([bd(kq1), bd(y)], axis=1))
        rq1 = rq - aq_ky[:, :W]
        o_loc = gv[C:] - aq_ky[:, W:]
        yield
        st = _dot(jnp.concatenate([rq1, kq1], axis=0).astype(BF16), m0.astype(BF16))
        o = st[:C] + o_loc
        z = -(st[C:] + y)
        yield
        upd = _dot(lhs_t, jnp.concatenate([z, v], axis=0).astype(BF16))
        return o, m_decayed + jnp.where(same_head, upd, 0.0)

    def interleave(gens):
        results = [None] * len(gens)
        live = list(range(len(gens)))
        while live:
            for i in list(live):
                try:
                    next(gens[i])
                except StopIteration as done:
                    results[i] = done.value
                    live.remove(i)
        return results

    def body(ci, carry):
        loaded = [chunk_load(ci, pi) for pi in range(n_pairs)]
        results = interleave([chunk_compute(pi, *loaded[pi]) for pi in range(n_pairs)])
        rows = pl.ds(pl.multiple_of(ci * C, C), C)
        for pi, (o, m_new) in enumerate(results):
            o_ref[rows, pi * W:(pi + 1) * W] = o
            state_ref[pi] = m_new
        return carry

    lax.fori_loop(0, n_chunks, body, 0)

    for pi in range(n_pairs):
        cols = slice(pi * W, (pi + 1) * W)
        o = o_ref[:, cols]
        r = r_ref[:, cols]
        v = v_ref[:, cols]
        k2 = k_ref[:, cols] * (1.0 + (a_ref[:, cols] - 1.0) * ka_ref[:, cols])
        oc = o - head_sum(o) * (1.0 / N)
        var_o = head_sum(oc * oc) * (1.0 / N)
        out = oc * lax.rsqrt(var_o + GN_EPS) * gng_ref[:, cols] + gnb_ref[:, cols]
        out = out + head_sum(r * k2 * rk_ref[:, cols]) * v
        o_ref[:, cols] = out * g_ref[:, cols]


def _rwkv_scan(r, k, v, dl, a, g, k_k, k_a, r_k, gn_g, gn_b, batch, seq, n_pairs=8, tb=256):
    m, d = r.shape
    w = n_pairs * 2 * RWKV_HEAD
    blk = pl.BlockSpec((tb, w), lambda b, p, t: (b * (seq // tb) + t, p))
    par = pl.BlockSpec((1, w), lambda b, p, t: (0, p))
    vec = lambda z: z.reshape(1, d)
    return pl.pallas_call(
        functools.partial(_rwkv_scan_kernel, n_pairs=n_pairs, n_chunks=tb // SCAN_CHUNK),
        grid=(batch, d // w, seq // tb),
        in_specs=[blk] * 6 + [par] * 5,
        out_specs=blk,
        out_shape=jax.ShapeDtypeStruct((m, d), F32),
        scratch_shapes=[pltpu.VMEM((n_pairs, 2 * RWKV_HEAD, 2 * RWKV_HEAD), F32),
                        pltpu.VMEM((tb, w), F32), pltpu.VMEM((tb, w), F32)],
        compiler_params=_cparams("parallel", "parallel", "arbitrary"),
        name="rwkv_scan",
    )(r, k, v, dl, a, g, vec(k_k), vec(k_a), vec(r_k), vec(gn_g), vec(gn_b))


def _proj_ln_kernel(y_ref, x_ref, w_ref, lng_ref, lnb_ref, o_ref):
    mix = _dot(y_ref[...].astype(BF16), w_ref[...])
    o_ref[...] = _layer_norm(DEEPNORM_ALPHA * x_ref[...] + mix, lng_ref[...], lnb_ref[...])


def _proj_ln(y, x2d, w, ln_g, ln_b, tm=512):
    m, d = x2d.shape
    row = lambda i: (i, 0)
    return pl.pallas_call(
        _proj_ln_kernel,
        grid=(m // tm,),
        in_specs=[pl.BlockSpec((tm, y.shape[1]), row), pl.BlockSpec((tm, d), row), _resident(w.shape),
                  _resident((1, d)), _resident((1, d))],
        out_specs=pl.BlockSpec((tm, d), row),
        out_shape=jax.ShapeDtypeStruct((m, d), F32),
        compiler_params=_cparams("parallel"),
        name="proj_ln",
    )(y, x2d, w, ln_g.reshape(1, d), ln_b.reshape(1, d))


def _proj_ln_router_kernel(y_ref, x_ref, w_ref, lng_ref, lnb_ref, wr_ref, br_ref, o_ref, lg_ref):
    mix = _dot(y_ref[...].astype(BF16), w_ref[...])
    xn = _layer_norm(DEEPNORM_ALPHA * x_ref[...] + mix, lng_ref[...], lnb_ref[...])
    o_ref[...] = xn
    xh, xm, xl = _split3(xn)
    wh, wm, wl = wr_ref[0], wr_ref[1], wr_ref[2]
    lg = (_dot(xh, wh) + _dot(xh, wm) + _dot(xm, wh) + _dot(xh, wl) + _dot(xl, wh) + _dot(xm, wm))
    lg_ref[...] = lg + br_ref[...]


def _proj_ln_router(y, x2d, w, ln_g, ln_b, w_router, b_router, tm=512):
    m, d = x2d.shape
    row = lambda i: (i, 0)
    wr = jnp.zeros((d, LANES), F32).at[:, :N_EXPERTS].set(w_router)
    wr3 = jnp.stack(_split3(wr))
    br = jnp.zeros((1, LANES), F32).at[0, :N_EXPERTS].set(b_router)
    return pl.pallas_call(
        _proj_ln_router_kernel,
        grid=(m // tm,),
        in_specs=[pl.BlockSpec((tm, y.shape[1]), row), pl.BlockSpec((tm, d), row), _resident(w.shape),
                  _resident((1, d)), _resident((1, d)), _resident(wr3.shape), _resident(br.shape)],
        out_specs=[pl.BlockSpec((tm, d), row), pl.BlockSpec((tm, LANES), row)],
        out_shape=[jax.ShapeDtypeStruct((m, d), F32), jax.ShapeDtypeStruct((m, LANES), F32)],
        compiler_params=_cparams("parallel"),
        name="proj_ln_router",
    )(y, x2d, w, ln_g.reshape(1, d), ln_b.reshape(1, d), wr3, br)


def _ple(xn, p, plew_ref, gw_ref, gb_ref):
    gate = _sigmoid(_dot(xn.astype(BF16), gw_ref[...]) + gb_ref[...])
    return xn + _dot(p.astype(BF16), plew_ref[...]) * gate


def _ffn_ple_kernel(x_ref, p_ref, wg_ref, wu_ref, wd_ref, lng_ref, lnb_ref, plew_ref, gw_ref, gb_ref,
                    o_ref, *, ff_chunk):
    x = x_ref[...]
    xb = x.astype(BF16)
    acc = jnp.zeros_like(x)
    for c in range(wg_ref.shape[1] // ff_chunk):
        sl = slice(c * ff_chunk, (c + 1) * ff_chunk)
        gt = _dot(xb, wg_ref[:, sl])
        up = _dot(xb, wu_ref[:, sl])
        h = (gt * _sigmoid(gt) * up).astype(BF16)
        acc = acc + _dot(h, wd_ref[sl, :])
    xn = _layer_norm(DEEPNORM_ALPHA * x + acc, lng_ref[...], lnb_ref[...])
    o_ref[...] = _ple(xn, p_ref[...], plew_ref, gw_ref, gb_ref)


def _ffn_ple(x2d, p2d, wg, wu, wd, ln_g, ln_b, ple_w, gate_w, gate_b, tm=256, ff_chunk=1408):
    m, d = x2d.shape
    row = lambda i: (i, 0)
    return pl.pallas_call(
        functools.partial(_ffn_ple_kernel, ff_chunk=ff_chunk),
        grid=(m // tm,),
        in_specs=[pl.BlockSpec((tm, d), row), pl.BlockSpec((tm, p2d.shape[1]), row),
                  _resident(wg.shape), _resident(wu.shape), _resident(wd.shape),
                  _resident((1, d)), _resident((1, d)), _resident(ple_w.shape), _resident(gate_w.shape),
                  _resident((1, d))],
        out_specs=pl.BlockSpec((tm, d), row),
        out_shape=jax.ShapeDtypeStruct((m, d), F32),
        compiler_params=_cparams("parallel"),
        name="ffn_ple",
    )(x2d, p2d, wg, wu, wd, ln_g.reshape(1, d), ln_b.reshape(1, d), ple_w, gate_w, gate_b.reshape(1, d))


def _res_ln_ple_kernel(x_ref, f_ref, p_ref, lng_ref, lnb_ref, plew_ref, gw_ref, gb_ref, o_ref):
    xn = _layer_norm(DEEPNORM_ALPHA * x_ref[...] + f_ref[...], lng_ref[...], lnb_ref[...])
    o_ref[...] = _ple(xn, p_ref[...], plew_ref, gw_ref, gb_ref)


def _res_ln_ple(x2d, f2d, p2d, ln_g, ln_b, ple_w, gate_w, gate_b, tm=512):
    m, d = x2d.shape
    row = lambda i: (i, 0)
    return pl.pallas_call(
        _res_ln_ple_kernel,
        grid=(m // tm,),
        in_specs=[pl.BlockSpec((tm, d), row), pl.BlockSpec((tm, d), row),
                  pl.BlockSpec((tm, p2d.shape[1]), row), _resident((1, d)), _resident((1, d)),
                  _resident(ple_w.shape), _resident(gate_w.shape), _resident((1, d))],
        out_specs=pl.BlockSpec((tm, d), row),
        out_shape=jax.ShapeDtypeStruct((m, d), F32),
        compiler_params=_cparams("parallel"),
        name="res_ln_ple",
    )(x2d, f2d, p2d, ln_g.reshape(1, d), ln_b.reshape(1, d), ple_w, gate_w, gate_b.reshape(1, d))


def _swap_half_cols(w):
    k, n = w.shape
    return w.reshape(k, n // HEAD_DIM, 2, HEAD_DIM // 2)[:, :, ::-1, :].reshape(k, n)


def _rope_tables(seq):
    half = HEAD_DIM // 2
    inv = ROPE_THETA ** (-jnp.arange(half, dtype=F32) / half)
    ang = jnp.arange(seq, dtype=F32)[:, None] * inv[None, :]
    cos, sin = jnp.cos(ang), jnp.sin(ang)
    return jnp.concatenate([cos, cos], -1), jnp.concatenate([-sin, sin], -1)


def _nsa_kv_kernel(x_ref, wk_ref, wvt_ref, cos_ref, sin_ref, zc_ref, vc_ref, ks_ref, kw_ref, vst_ref,
                   vwt_ref):
    G, dh = N_KV_GROUPS, HEAD_DIM
    gw = G * dh
    xb = x_ref[...].astype(BF16)
    res = _dot(xb, wk_ref[...])
    res_t = _dot_nt(wvt_ref[...], xb)
    tm = res.shape[0]
    cos = cos_ref[...]
    sin = sin_ref[...]
    ks = res[:, 2 * gw:3 * gw] * cos + res[:, 3 * gw:4 * gw] * sin
    kw = res[:, 4 * gw:5 * gw] * cos + res[:, 5 * gw:6 * gw] * sin
    t_glob = pl.program_id(1) * tm + lax.broadcasted_iota(jnp.int32, (tm, dh), 0)
    onehot = jnp.where(lax.broadcasted_iota(jnp.int32, (tm, dh), 1) == t_glob // SEL_BLOCK, 1.0, 0.0)
    for g in range(G):
        sl = slice(g * dh, (g + 1) * dh)
        zc_ref[0, g] = res[:, sl].astype(BF16)
        vc_ref[0, g] = res[:, gw + g * dh:gw + (g + 1) * dh].astype(BF16)
        ks_ref[0, g] = jnp.concatenate([ks[:, sl], onehot], axis=1).astype(BF16)
        kw_ref[0, g] = kw[:, sl].astype(BF16)
    vst_ref[0] = res_t[:gw].astype(BF16)
    vwt_ref[0] = res_t[gw:].astype(BF16)


def _nsa_kv(x2d, w_kv, batch, seq, tm=256):
    m, d = x2d.shape
    G, dh = N_KV_GROUPS, HEAD_DIM
    gw = G * dh
    parts = [w_kv[:, j * gw:(j + 1) * gw] for j in range(6)]
    w_k = jnp.concatenate([parts[0], parts[1], parts[2], _swap_half_cols(parts[2]),
                           parts[4], _swap_half_cols(parts[4])], axis=1).astype(BF16)
    w_vt = jnp.concatenate([parts[3], parts[5]], axis=1).T.astype(BF16)
    cos, sin = _rope_tables(seq)
    cos = jnp.tile(cos, (1, G))
    sin = jnp.tile(sin, (1, G))
    nt = seq // tm
    o64 = jax.ShapeDtypeStruct((batch, G, seq, dh), BF16)
    o128 = jax.ShapeDtypeStruct((batch, G, seq, 2 * dh), BF16)
    ot = jax.ShapeDtypeStruct((batch, gw, seq), BF16)
    b64 = pl.BlockSpec((1, G, tm, dh), lambda b, t: (b, 0, t, 0))
    b128 = pl.BlockSpec((1, G, tm, 2 * dh), lambda b, t: (b, 0, t, 0))
    bt = pl.BlockSpec((1, gw, tm), lambda b, t: (b, 0, t))
    return pl.pallas_call(
        _nsa_kv_kernel,
        grid=(batch, nt),
        in_specs=[pl.BlockSpec((tm, d), lambda b, t: (b * nt + t, 0)), _resident(w_k.shape),
                  _resident(w_vt.shape),
                  pl.BlockSpec((tm, gw), lambda b, t: (t, 0)), pl.BlockSpec((tm, gw), lambda b, t: (t, 0))],
        out_specs=[b64, b64, b128, b64, bt, bt],
        out_shape=[o64, o64, o128, o64, ot, ot],
        compiler_params=_cparams("parallel", "parallel"),
        name="nsa_kv",
    )(x2d, w_k, w_vt, cos, sin)


def _nsa_cmp_kernel(z_ref, pos_ref, w1_ref, b1_ref, w2_ref, o_ref, *, slabs_per_seq):
    half = w1_ref.shape[1] // 2
    z = z_ref[0]
    tm = z.shape[0]
    first = _dot(z, w1_ref[0, :half, :])
    second = _dot(z, w1_ref[0, half:, :])
    const = _dot(pos_ref[0], w1_ref[0])[0:1, :] + b1_ref[0]
    hid = first + pltpu.roll(second, shift=tm - 1, axis=0) + const
    act = 0.5 * hid * (1.0 + jnp.tanh(math.sqrt(2.0 / math.pi) * (hid + 0.044715 * hid * hid * hid)))
    out = _dot(act.astype(BF16), w2_ref[0])
    row = lax.broadcasted_iota(jnp.int32, out.shape, 0)
    o_ref[0] = jnp.where(row % slabs_per_seq == slabs_per_seq - 1, 0.0, out).astype(BF16)


def _nsa_compress(zc, vc, cmp_pos, phi_w1, phi_b1, phi_w2, tm=512):
    batch, G, seq, dh = zc.shape
    slab = CMP_STRIDE * dh
    rows = batch * G * seq // CMP_STRIDE
    tm = min(tm, rows)
    z = jnp.stack([zc.reshape(rows, slab), vc.reshape(rows, slab)])
    pos = jnp.broadcast_to(cmp_pos.reshape(2, 1, CMP_LEN * dh), (2, 8, CMP_LEN * dh)).astype(BF16)
    out = pl.pallas_call(
        functools.partial(_nsa_cmp_kernel, slabs_per_seq=seq // CMP_STRIDE),
        grid=(2, rows // tm),
        in_specs=[pl.BlockSpec((1, tm, slab), lambda j, i: (j, i, 0)),
                  pl.BlockSpec((1, 8, CMP_LEN * dh), lambda j, i: (j, 0, 0)),
                  pl.BlockSpec((1, CMP_LEN * dh, CMP_HIDDEN), lambda j, i: (j, 0, 0)),
                  pl.BlockSpec((1, 1, CMP_HIDDEN), lambda j, i: (j, 0, 0)),
                  pl.BlockSpec((1, CMP_HIDDEN, dh), lambda j, i: (j, 0, 0))],
        out_specs=pl.BlockSpec((1, tm, dh), lambda j, i: (j, i, 0)),
        out_shape=jax.ShapeDtypeStruct((2, rows, dh), BF16),
        compiler_params=_cparams("parallel", "parallel"),
        name="nsa_compress",
    )(z, pos, phi_w1.astype(BF16), phi_b1.reshape(2, 1, CMP_HIDDEN), phi_w2.astype(BF16))
    n_slab = seq // CMP_STRIDE
    return out[0].reshape(batch, G, n_slab, dh), out[1].reshape(batch, G, n_slab, dh)


GATE_ROWS = 16


def _nsa_q_kernel(x_ref, wt_ref, cos_ref, sin_ref, q_ref, qr_ref, gate_ref):
    dh = HEAD_DIM
    hw = N_HEADS * dh
    scale = HEAD_DIM ** -0.5
    res_t = _dot_nt(wt_ref[...], x_ref[...].astype(BF16))
    tm = res_t.shape[1]
    q = res_t[:hw].reshape(N_HEADS, dh, tm)
    q_sw = res_t[hw:2 * hw].reshape(N_HEADS, dh, tm)
    qr = q * cos_ref[...][None] + q_sw * sin_ref[...][None]
    q_ref[0] = (q * scale).reshape(hw, tm).astype(BF16)
    qr_ref[0] = (qr * scale).reshape(hw, tm).astype(BF16)
    gate_ref[0] = _sigmoid(res_t[2 * hw:])


def _nsa_q(x2d, w_in, batch, seq, tm=256):
    m, d = x2d.shape
    G, R, dh = N_KV_GROUPS, Q_PER_GROUP, HEAD_DIM
    hw = N_HEADS * dh
    w_gate = w_in[:, hw:].reshape(d, G, R * N_BRANCH)
    w_gate = jnp.pad(w_gate, ((0, 0), (0, 0), (0, GATE_ROWS - R * N_BRANCH))).reshape(d, G * GATE_ROWS)
    wt = jnp.concatenate([w_in[:, :hw], _swap_half_cols(w_in[:, :hw]), w_gate], axis=1).T.astype(BF16)
    cos, sin = _rope_tables(seq)
    nt = seq // tm
    oq = jax.ShapeDtypeStruct((batch, hw, seq), BF16)
    bq = pl.BlockSpec((1, hw, tm), lambda b, t: (b, 0, t))
    tab = pl.BlockSpec((dh, tm), lambda b, t: (0, t))
    return pl.pallas_call(
        _nsa_q_kernel,
        grid=(batch, nt),
        in_specs=[pl.BlockSpec((tm, d), lambda b, t: (b * nt + t, 0)), _resident(wt.shape), tab, tab],
        out_specs=[bq, bq, pl.BlockSpec((1, G * GATE_ROWS, tm), lambda b, t: (b, 0, t))],
        out_shape=[oq, oq, jax.ShapeDtypeStruct((batch, G * GATE_ROWS, seq), F32)],
        compiler_params=_cparams("parallel", "parallel"),
        name="nsa_q",
    )(x2d, wt, cos.T, sin.T)


def _nsa_attn_kernel(q_ref, qr_ref, gate_ref, kc_ref, vct_ref, ks_ref, vst_ref, kw_ref, vwt_ref, ovlt_ref,
                     o_ref, *, tq, tk, n_blk):
    R, dh = Q_PER_GROUP, HEAD_DIM
    cols = R * tq
    qi = pl.program_id(2)
    t0 = qi * tq
    n_blk_pad = ovlt_ref.shape[0]

    def heads_to_lanes(x):
        return jnp.concatenate([x[r * dh:(r + 1) * dh] for r in range(R)], axis=1)

    q = heads_to_lanes(q_ref[0])
    qr = heads_to_lanes(qr_ref[0])

    def col_pos(shape):
        return t0 + (lax.broadcasted_iota(jnp.int32, shape, 1) & (tq - 1))

    s = _dot(kc_ref[0, 0], q)
    cmp_end = lax.broadcasted_iota(jnp.int32, s.shape, 0) * CMP_STRIDE + (CMP_LEN - 1)
    valid = cmp_end <= col_pos(s.shape)
    sm = jnp.where(valid, s, NEG_INF)
    e = jnp.where(valid, jnp.exp(sm - jnp.max(sm, axis=0, keepdims=True)), 0.0)
    l = jnp.sum(e, axis=0, keepdims=True)
    p_cmp = e / jnp.where(l > 0.0, l, 1.0)
    o_cmp = _dot(vct_ref[0, 0], p_cmp.astype(BF16))

    p_sum = p_cmp[:, 0:tq]
    for r in range(1, R):
        p_sum = p_sum + p_cmp[:, r * tq:(r + 1) * tq]
    hi, lo = _split2(p_sum)
    imp = _dot(ovlt_ref[...], hi) + _dot(ovlt_ref[...], lo)
    blk = lax.broadcasted_iota(jnp.int32, imp.shape, 0)
    t_q = t0 + lax.broadcasted_iota(jnp.int32, imp.shape, 1)
    cur = t_q // SEL_BLOCK
    forced = (blk == 0) | (blk == cur) | (blk == cur - 1)
    score = jnp.where(forced, FORCE_SCORE, jnp.where(blk * SEL_BLOCK <= t_q, imp, NEG_INF))
    score = jnp.where(blk < n_blk, score, -jnp.inf)
    selected = blk >= n_blk
    for _ in range(N_SEL):
        best = jnp.max(score, axis=0, keepdims=True)
        first = jnp.min(jnp.where(score == best, blk, 2 * LANES), axis=0, keepdims=True)
        pick = blk == first
        selected = selected | pick
        score = jnp.where(pick, -jnp.inf, score)
    bias = jnp.where(selected, 0.0, NEG_INF)
    qa = jnp.concatenate([qr, jnp.concatenate([bias] * R, axis=1).astype(BF16),
                          jnp.zeros((dh - n_blk_pad, cols), BF16)], axis=0)

    def sweep(q_op, k_ref, vt_ref, width, lo_tile, hi_tile, mask_fn):
        def body(j, carry):
            m_i, l_i, acc = carry
            ks = pl.ds(pl.multiple_of(j * width, width), width)
            sc = _dot(k_ref[0, 0, ks, :], q_op)
            kpos = j * width + lax.broadcasted_iota(jnp.int32, sc.shape, 0)
            sc = jnp.where(mask_fn(col_pos(sc.shape), kpos), sc, NEG_INF)
            m_new = jnp.maximum(m_i, jnp.max(sc, axis=0, keepdims=True))
            alpha = jnp.exp(m_i - m_new)
            p = jnp.exp(sc - m_new)
            l_new = alpha * l_i + jnp.sum(p, axis=0, keepdims=True)
            acc_new = alpha * acc + _dot(vt_ref[0, :, ks], p.astype(BF16))
            return m_new, l_new, acc_new

        init = (jnp.full((1, cols), NEG_INF, F32), jnp.zeros((1, cols), F32), jnp.zeros((dh, cols), F32))
        _, l_f, acc_f = lax.fori_loop(lo_tile, hi_tile, body, init)
        return acc_f / l_f

    o_slc = sweep(qa, ks_ref, vst_ref, tk, 0, (t0 + tq + tk - 1) // tk, lambda t, kp: kp <= t)

    lo_tile = jnp.maximum(t0 - WINDOW, 0) // tk
    o_win = sweep(qr, kw_ref, vwt_ref, tk, lo_tile, (t0 + tq + tk - 1) // tk,
                  lambda t, kp: (kp <= t) & (t - kp < WINDOW))

    gates = gate_ref[0]
    outs = []
    for r in range(R):
        cs = slice(r * tq, (r + 1) * tq)
        outs.append(gates[3 * r:3 * r + 1] * o_cmp[:, cs] + gates[3 * r + 1:3 * r + 2] * o_slc[:, cs]
                    + gates[3 * r + 2:3 * r + 3] * o_win[:, cs])
    halves = [jnp.transpose(jnp.concatenate(outs[i:i + 2], axis=0)) for i in range(0, R, 2)]
    o_ref[0] = jnp.concatenate(halves, axis=1)


SEL_ROWS = 32


def _nsa_attention(q, qr, gates, k_cmp, v_cmp, ks, vst, kw, vwt, tq=256, tk=256):
    batch, _, seq = q.shape
    G, R, dh = N_KV_GROUPS, Q_PER_GROUP, HEAD_DIM
    n_slab = k_cmp.shape[2]
    n_blk = seq // SEL_BLOCK
    assert n_blk <= SEL_ROWS
    cmp_start = jnp.arange(n_slab) * CMP_STRIDE
    blk_start = jnp.arange(SEL_ROWS) * SEL_BLOCK
    overlap_t = ((cmp_start[None, :] < blk_start[:, None] + SEL_BLOCK)
                 & (cmp_start[None, :] + CMP_LEN - 1 >= blk_start[:, None])
                 & (jnp.arange(SEL_ROWS)[:, None] < n_blk)
                 & (jnp.arange(n_slab)[None, :] < (seq - CMP_LEN) // CMP_STRIDE + 1)).astype(BF16)
    v_cmp_t = jnp.swapaxes(v_cmp, 2, 3)
    qspec = pl.BlockSpec((1, R * dh, tq), lambda b, g, i: (b, g, i))
    full = lambda a: pl.BlockSpec((1, 1) + a.shape[2:], lambda b, g, i: (b, g, 0, 0))
    vspec = pl.BlockSpec((1, dh, seq), lambda b, g, i: (b, g, 0))
    return pl.pallas_call(
        functools.partial(_nsa_attn_kernel, tq=tq, tk=tk, n_blk=n_blk),
        grid=(batch, G, seq // tq),
        in_specs=[qspec, qspec, pl.BlockSpec((1, GATE_ROWS, tq), lambda b, g, i: (b, g, i)),
                  full(k_cmp), full(v_cmp_t), full(ks), vspec, full(kw), vspec,
                  _resident(overlap_t.shape)],
        out_specs=pl.BlockSpec((1, tq, R * dh), lambda b, g, i: (b, i, g)),
        out_shape=jax.ShapeDtypeStruct((batch, seq, G * R * dh), F32),
        compiler_params=_cparams("parallel", "parallel", "parallel"),
        name="nsa_attn",
    )(q, qr, gates, k_cmp, v_cmp_t, ks, vst, kw, vwt, overlap_t)


def _moe_kernel(be_ref, nb_ref, x_ref, wg_ref, wu_ref, wd_ref, o_ref):
    i = pl.program_id(0)

    @pl.when(i < nb_ref[0])
    def _():
        xb = x_ref[...].astype(BF16)
        gt = _dot(xb, wg_ref[0])
        up = _dot(xb, wu_ref[0])
        h = (gt * _sigmoid(gt) * up).astype(BF16)
        o_ref[...] = _dot(h, wd_ref[0])

    @pl.when(i >= nb_ref[0])
    def _():
        o_ref[...] = jnp.zeros_like(o_ref)


def _moe_experts(xs, block_e, n_used, wg, wu, wd):
    n_rows, d = xs.shape
    n_blocks = n_rows // MOE_ROWS
    ff = wg.shape[2]
    grid_spec = pltpu.PrefetchScalarGridSpec(
        num_scalar_prefetch=2,
        grid=(n_blocks,),
        in_specs=[pl.BlockSpec((MOE_ROWS, d), lambda i, be, nb: (i, 0)),
                  pl.BlockSpec((1, d, ff), lambda i, be, nb: (be[i], 0, 0)),
                  pl.BlockSpec((1, d, ff), lambda i, be, nb: (be[i], 0, 0)),
                  pl.BlockSpec((1, ff, d), lambda i, be, nb: (be[i], 0, 0))],
        out_specs=pl.BlockSpec((MOE_ROWS, d), lambda i, be, nb: (i, 0)),
    )
    return pl.pallas_call(
        _moe_kernel,
        grid_spec=grid_spec,
        out_shape=jax.ShapeDtypeStruct((n_rows, d), F32),
        compiler_params=_cparams("arbitrary"),
        name="moe_experts",
    )(block_e, n_used, xs, wg, wu, wd)


def _moe(x2d, logits, w_gu, w_down):
    n_tok, d = x2d.shape
    n_assign = n_tok * TOP_K
    top_logit, top_e = lax.top_k(logits, TOP_K)
    weights = jax.nn.softmax(top_logit, axis=-1)
    flat_e = top_e.reshape(-1)
    order = jnp.argsort(flat_e)
    e_sorted = flat_e[order]
    tok_sorted = (order // TOP_K).astype(jnp.int32)
    counts = jnp.bincount(flat_e, length=N_EXPERTS)
    padded = (counts + MOE_ROWS - 1) // MOE_ROWS * MOE_ROWS
    pad_end = jnp.cumsum(padded)
    pad_start = pad_end - padded
    grp_start = jnp.cumsum(counts) - counts
    dest = (pad_start[e_sorted] + jnp.arange(n_assign) - grp_start[e_sorted]).astype(jnp.int32)
    n_blocks = -(-n_assign // MOE_ROWS) + N_EXPERTS
    n_rows = n_blocks * MOE_ROWS
    row_tok = jnp.zeros((n_rows,), jnp.int32).at[dest].set(tok_sorted)
    block_e = jnp.minimum(jnp.searchsorted(pad_end, jnp.arange(n_blocks) * MOE_ROWS, side='right'),
                          N_EXPERTS - 1).astype(jnp.int32)
    n_used = (pad_end[-1] // MOE_ROWS).astype(jnp.int32).reshape(1)
    xs = x2d[row_tok]
    ys = _moe_experts(xs, block_e, n_used, w_gu[:, :, :D_FF_EXPERT].astype(BF16),
                      w_gu[:, :, D_FF_EXPERT:].astype(BF16), w_down.astype(BF16))
    pos = jnp.zeros((n_assign,), jnp.int32).at[order].set(dest).reshape(n_tok, TOP_K)
    out = ys[pos[:, 0]] * weights[:, 0:1]
    for s in range(1, TOP_K):
        out = out + ys[pos[:, s]] * weights[:, s:s + 1]
    return out


def kernel(x, p, a_mu, a_w_rkv, a_w0, a_w1, a_w2, a_a0, a_a1, a_a2, a_g1, a_g2, a_k_k, a_k_a, a_r_k,
           a_gn_g, a_gn_b, a_w_o, b_w_kv, b_cmp_pos, b_phi_w1, b_phi_b1, b_phi_w2, b_w_in, b_w_o,
           f_w_gu, f_w_down, m_w_router, m_b_router, m_w_gu, m_w_down, ln_g, ln_b, ple_w, ple_gate_w,
           ple_gate_b):
    batch, seq, d = x.shape
    m = batch * seq
    x0 = x.reshape(m, d)
    p2d = p.reshape(DEPTH, m, PLE_DIM)
    bf = lambda w: w.astype(BF16)

    r, k, v, dl, a, g = _rwkv_proj(
        x0, seq, a_mu[0], bf(a_w_rkv[0]), a_w0[0].reshape(1, d), bf(a_w1[0]), bf(a_w2[0]),
        a_a0[0].reshape(1, d), bf(a_a1[0]), bf(a_a2[0]), bf(a_g1[0]), bf(a_g2[0]))
    y = _rwkv_scan(r, k, v, dl, a, g, a_k_k[0], a_k_a[0], a_r_k[0], a_gn_g[0], a_gn_b[0], batch, seq)
    x1 = _proj_ln(y, x0, bf(a_w_o[0]), ln_g[0, 0], ln_b[0, 0])
    x2 = _ffn_ple(x1, p2d[0], bf(f_w_gu[0][:, :D_FF]), bf(f_w_gu[0][:, D_FF:]), bf(f_w_down[0]),
                  ln_g[0, 1], ln_b[0, 1], bf(ple_w[0]), bf(ple_gate_w[0]), ple_gate_b[0])

    zc, vc, ks, kw, vst, vwt = _nsa_kv(x2, b_w_kv, batch, seq)
    k_cmp, v_cmp = _nsa_compress(zc, vc, b_cmp_pos, b_phi_w1, b_phi_b1, b_phi_w2)
    q, qr, gates = _nsa_q(x2, b_w_in[0], batch, seq)
    attn = _nsa_attention(q, qr, gates, k_cmp, v_cmp, ks, vst, kw, vwt).reshape(m, d)
    x3, logits = _proj_ln_router(attn, x2, bf(b_w_o[0]), ln_g[1, 0], ln_b[1, 0], m_w_router[0],
                                 m_b_router[0])
    ffn = _moe(x3, logits[:, :N_EXPERTS], m_w_gu[0], m_w_down[0])
    out = _res_ln_ple(x3, ffn, p2d[1], ln_g[1, 1], ln_b[1, 1], bf(ple_w[1]), bf(ple_gate_w[1]),
                      ple_gate_b[1])
    return out.reshape(batch, seq, d)
```

```python
import functools
import math

import jax
import jax.numpy as jnp
from jax import lax
from jax.experimental import pallas as pl
from jax.experimental.pallas import tpu as pltpu

BF16 = jnp.bfloat16
F32 = jnp.float32

LANES = 128
VMEM_LIMIT_BYTES = 56 * 1024 * 1024

D_MODEL = 1024
PLE_DIM = 256
RWKV_HEAD = 64
GN_EPS = 64e-5
N_HEADS = 16
HEAD_DIM = 64
N_KV_GROUPS = 4
Q_PER_GROUP = 4
N_BRANCH = 3
CMP_LEN = 32
CMP_STRIDE = 16
CMP_HIDDEN = 256
SEL_BLOCK = 64
N_SEL = 8
WINDOW = 512
ROPE_THETA = 10000.0
D_FF = 2816
N_EXPERTS = 8
TOP_K = 2
D_FF_EXPERT = 1408
MOE_ROWS = 256
LN_EPS = 1e-5
DEPTH = 2
DEEPNORM_ALPHA = (2.0 * DEPTH) ** 0.25
NEG_INF = -1e30
FORCE_SCORE = 1e4

SCAN_CHUNK = 64


def _cparams(*sem):
    return pltpu.CompilerParams(dimension_semantics=sem, vmem_limit_bytes=VMEM_LIMIT_BYTES)


def _resident(shape):
    nd = len(shape)
    return pl.BlockSpec(shape, lambda *_: (0,) * nd, pipeline_mode=pl.Buffered(1))


def _dot(a, b):
    return jnp.dot(a, b, preferred_element_type=F32)


def _dot_nt(a, b):
    return lax.dot_general(a, b, (((1,), (1,)), ((), ())), preferred_element_type=F32)


def _split2(x):
    hi = x.astype(BF16)
    lo = (x - hi.astype(F32)).astype(BF16)
    return hi, lo


def _split3(x):
    hi = x.astype(BF16)
    r1 = x - hi.astype(F32)
    mid = r1.astype(BF16)
    lo = (r1 - mid.astype(F32)).astype(BF16)
    return hi, mid, lo


def _layer_norm(y, g, b):
    mu = jnp.mean(y, axis=-1, keepdims=True)
    yc = y - mu
    var = jnp.mean(yc * yc, axis=-1, keepdims=True)
    return yc * lax.rsqrt(var + LN_EPS) * g + b


def _sigmoid(z):
    return 1.0 / (1.0 + jnp.exp(-z))


def _rwkv_proj_kernel(x_ref, xh_ref, mu_ref, wrkv_ref, w0_ref, w1_ref, w2_ref, a0_ref, a1_ref,
                      a2_ref, g1_ref, g2_ref, r_ref, k_ref, v_ref, dl_ref, a_ref, g_ref, *,
                      tiles_per_seq):
    x = x_ref[...]
    tm = x.shape[0]
    first = (pl.program_id(0) % tiles_per_seq) == 0
    prev_row = jnp.where(first, 0.0, xh_ref[0, 7:8, :])
    row = lax.broadcasted_iota(jnp.int32, x.shape, 0)
    x_shift = jnp.where(row == 0, prev_row, pltpu.roll(x, shift=1, axis=0))
    xx = x_shift - x

    def mix(i):
        return (x + xx * mu_ref[i:i + 1, :]).astype(BF16)

    r_ref[...] = _dot(mix(0), wrkv_ref[0])
    k_ref[...] = _dot(mix(1), wrkv_ref[1])
    v_ref[...] = _dot(mix(2), wrkv_ref[2])
    z = w0_ref[...] + _dot(jnp.tanh(_dot(mix(3), w1_ref[...])).astype(BF16), w2_ref[...])
    dl_ref[...] = -math.exp(-0.5) * _sigmoid(z)
    a_ref[...] = _sigmoid(a0_ref[...] + _dot(_dot(mix(4), a1_ref[...]).astype(BF16), a2_ref[...]))
    g_ref[...] = _dot(_sigmoid(_dot(mix(5), g1_ref[...])).astype(BF16), g2_ref[...])


def _rwkv_proj(x2d, seq, mu, w_rkv, w0, w1, w2, a0, a1, a2, g1, g2, tm=256):
    m, d = x2d.shape
    xh = x2d.reshape(m // 8, 8, d)
    row = lambda i: (i, 0)
    out = jax.ShapeDtypeStruct((m, d), F32)
    return pl.pallas_call(
        functools.partial(_rwkv_proj_kernel, tiles_per_seq=seq // tm),
        grid=(m // tm,),
        in_specs=[
            pl.BlockSpec((tm, d), row),
            pl.BlockSpec((1, 8, d), lambda i: (jnp.maximum(i * (tm // 8) - 1, 0), 0, 0)),
            _resident(mu.shape), _resident(w_rkv.shape), _resident(w0.shape), _resident(w1.shape),
            _resident(w2.shape), _resident(a0.shape), _resident(a1.shape), _resident(a2.shape),
            _resident(g1.shape), _resident(g2.shape),
        ],
        out_specs=[pl.BlockSpec((tm, d), row)] * 6,
        out_shape=[out] * 6,
        compiler_params=_cparams("parallel"),
        name="rwkv_proj",
    )(x2d, xh, mu, w_rkv, w0, w1, w2, a0, a1, a2, g1, g2)


def _rwkv_scan_kernel(r_ref, k_ref, v_ref, dl_ref, a_ref, g_ref, kk_ref, ka_ref, rk_ref, gng_ref,
                      gnb_ref, o_ref, state_ref, kkn_ref, cum_ref, *, n_pairs, n_chunks):
    C = SCAN_CHUNK
    N = RWKV_HEAD
    W = 2 * N
    tb = n_chunks * C

    @pl.when(pl.program_id(2) == 0)
    def _():
        state_ref[...] = jnp.zeros_like(state_ref)

    tt = lax.broadcasted_iota(jnp.int32, (C, W), 0)
    ln = lax.broadcasted_iota(jnp.int32, (C, W), 1)
    ss = ln & (N - 1)
    strict_lower = ss < tt
    lower = ss <= tt
    eye = jnp.where(ss == tt, 1.0, 0.0)
    level_masks = []
    for lg in range(int(math.log2(C))):
        level_masks.append(((tt >> (lg + 1)) == (ss >> (lg + 1))) & ((tt >> lg) == (ss >> lg) + 1))
    rr = lax.broadcasted_iota(jnp.int32, (W, W), 0)
    cc = lax.broadcasted_iota(jnp.int32, (W, W), 1)
    same_head = (rr < N) == (cc < N)
    ones_bd = jnp.where(same_head, 1.0, 0.0).astype(BF16)
    lane_head0 = ln < N

    def bd(y):
        return jnp.concatenate([jnp.where(lane_head0, y, 0.0), jnp.where(lane_head0, 0.0, y)],
                               axis=0).astype(BF16)

    def pmm(xp, ybd):
        return _dot(xp.astype(BF16), ybd)

    def head_sum(xp):
        hi, lo = _split2(xp)
        return _dot(hi, ones_bd) + _dot(lo, ones_bd)

    tri_r = lax.broadcasted_iota(jnp.int32, (tb, tb), 0)
    tri_c = lax.broadcasted_iota(jnp.int32, (tb, tb), 1)
    lg_c = int(math.log2(C))
    tri_chunk = jnp.where((tri_c <= tri_r) & ((tri_c >> lg_c) == (tri_r >> lg_c)), 1.0, 0.0).astype(BF16)
    d_hi, d_mid, d_lo = _split3(dl_ref[...])
    cum_ref[...] = _dot(tri_chunk, d_hi) + _dot(tri_chunk, d_mid) + _dot(tri_chunk, d_lo)
    for pi in range(n_pairs):
        cols = slice(pi * W, (pi + 1) * W)
        kk_raw = k_ref[:, cols] * kk_ref[:, cols]
        kkn_ref[:, cols] = kk_raw / jnp.maximum(jnp.sqrt(head_sum(kk_raw * kk_raw)), 1e-12)

    def chunk_load(ci, pi):
        rows = pl.ds(pl.multiple_of(ci * C, C), C)
        cols = slice(pi * W, (pi + 1) * W)
        return (r_ref[rows, cols], k_ref[rows, cols], v_ref[rows, cols], dl_ref[rows, cols],
                a_ref[rows, cols], kkn_ref[rows, cols], cum_ref[rows, cols], state_ref[pi])

    def chunk_compute(pi, r, k, v, dl, a, kk, cum, m0):
        cols = slice(pi * W, (pi + 1) * W)
        k2 = k * (1.0 + (a - 1.0) * ka_ref[:, cols])
        b = a * kk
        p_incl = jnp.exp(cum)
        p_excl = jnp.exp(cum - dl)
        p_inv = jnp.exp(-cum)
        kq = kk * p_excl
        rq = r * p_incl
        bk = b * p_inv
        kkd = k2 * p_inv

        bkt = jnp.transpose(jnp.concatenate([b, k2], axis=0))
        cumt = jnp.transpose(jnp.concatenate([cum, cum], axis=0))
        last_col = cumt[:, C - 1:C]
        lhs_t = (bkt * jnp.exp(last_col - cumt)).astype(BF16)
        m_decayed = m0 * jnp.exp(last_col)

        lhs = jnp.concatenate([kq, rq], axis=0).astype(BF16)
        yield
        ab = _dot_nt(lhs, bd(bk))
        gb = _dot_nt(lhs, bd(kkd))
        a_m = jnp.where(strict_lower, ab[:C], 0.0)
        aq_m = jnp.where(lower, ab[C:], 0.0)
        g_m = jnp.where(strict_lower, gb[:C], 0.0)
        gq_m = jnp.where(lower, gb[C:], 0.0)

        tinv = eye - jnp.where(level_masks[0], a_m, 0.0)
        yield
        gv = pmm(jnp.concatenate([g_m, gq_m], axis=0), bd(v))
        for lm in level_masks[1:]:
            w_ = pmm(jnp.where(lm, a_m, 0.0), bd(tinv))
            yield
            tinv = tinv - pmm(tinv, bd(w_))
            yield

        kq_y = pmm(tinv, jnp.concatenate([bd(kq), bd(gv[:C])], axis=1))
        kq1 = kq_y[:, :W]
        y = kq_y[:, W:]
        yield
        aq_ky = pmm(aq_m, jnp.concatenate([bd(kq1), bd(y)], axis=1))
        rq1 = rq - aq_ky[:, :W]
        o_loc = gv[C:] - aq_ky[:, W:]
        yield
        st = _dot(jnp.concatenate([rq1, kq1], axis=0).astype(BF16), m0.astype(BF16))
        o = st[:C] + o_loc
        z = -(st[C:] + y)
        yield
        upd = _dot(lhs_t, jnp.concatenate([z, v], axis=0).astype(BF16))
        return o, m_decayed + jnp.where(same_head, upd, 0.0)

    def interleave(gens):
        results = [None] * len(gens)
        live = list(range(len(gens)))
        while live:
            for i in list(live):
                try:
                    next(gens[i])
                except StopIteration as done:
                    results[i] = done.value
                    live.remove(i)
        return results

    def body(ci, carry):
        loaded = [chunk_load(ci, pi) for pi in range(n_pairs)]
        results = interleave([chunk_compute(pi, *loaded[pi]) for pi in range(n_pairs)])
        rows = pl.ds(pl.multiple_of(ci * C, C), C)
        for pi, (o, m_new) in enumerate(results):
            o_ref[rows, pi * W:(pi + 1) * W] = o
            state_ref[pi] = m_new
        return carry

    lax.fori_loop(0, n_chunks, body, 0)

    for pi in range(n_pairs):
        cols = slice(pi * W, (pi + 1) * W)
        o = o_ref[:, cols]
        r = r_ref[:, cols]
        v = v_ref[:, cols]
        k2 = k_ref[:, cols] * (1.0 + (a_ref[:, cols] - 1.0) * ka_ref[:, cols])
        oc = o - head_sum(o) * (1.0 / N)
        var_o = head_sum(oc * oc) * (1.0 / N)
        out = oc * lax.rsqrt(var_o + GN_EPS) * gng_ref[:, cols] + gnb_ref[:, cols]
        out = out + head_sum(r * k2 * rk_ref[:, cols]) * v
        o_ref[:, cols] = out * g_ref[:, cols]


def _rwkv_scan(r, k, v, dl, a, g, k_k, k_a, r_k, gn_g, gn_b, batch, seq, n_pairs=8, tb=256):
    m, d = r.shape
    w = n_pairs * 2 * RWKV_HEAD
    blk = pl.BlockSpec((tb, w), lambda b, p, t: (b * (seq // tb) + t, p))
    par = pl.BlockSpec((1, w), lambda b, p, t: (0, p))
    vec = lambda z: z.reshape(1, d)
    return pl.pallas_call(
        functools.partial(_rwkv_scan_kernel, n_pairs=n_pairs, n_chunks=tb // SCAN_CHUNK),
        grid=(batch, d // w, seq // tb),
        in_specs=[blk] * 6 + [par] * 5,
        out_specs=blk,
        out_shape=jax.ShapeDtypeStruct((m, d), F32),
        scratch_shapes=[pltpu.VMEM((n_pairs, 2 * RWKV_HEAD, 2 * RWKV_HEAD), F32),
                        pltpu.VMEM((tb, w), F32), pltpu.VMEM((tb, w), F32)],
        compiler_params=_cparams("parallel", "parallel", "arbitrary"),
        name="rwkv_scan",
    )(r, k, v, dl, a, g, vec(k_k), vec(k_a), vec(r_k), vec(gn_g), vec(gn_b))


def _proj_ln_kernel(y_ref, x_ref, w_ref, lng_ref, lnb_ref, o_ref):
    mix = _dot(y_ref[...].astype(BF16), w_ref[...])
    o_ref[...] = _layer_norm(DEEPNORM_ALPHA * x_ref[...] + mix, lng_ref[...], lnb_ref[...])


def _proj_ln(y, x2d, w, ln_g, ln_b, tm=512):
    m, d = x2d.shape
    row = lambda i: (i, 0)
    return pl.pallas_call(
        _proj_ln_kernel,
        grid=(m // tm,),
        in_specs=[pl.BlockSpec((tm, y.shape[1]), row), pl.BlockSpec((tm, d), row), _resident(w.shape),
                  _resident((1, d)), _resident((1, d))],
        out_specs=pl.BlockSpec((tm, d), row),
        out_shape=jax.ShapeDtypeStruct((m, d), F32),
        compiler_params=_cparams("parallel"),
        name="proj_ln",
    )(y, x2d, w, ln_g.reshape(1, d), ln_b.reshape(1, d))


def _proj_ln_router_kernel(y_ref, x_ref, w_ref, lng_ref, lnb_ref, wr_ref, br_ref, o_ref, lg_ref):
    mix = _dot(y_ref[...].astype(BF16), w_ref[...])
    xn = _layer_norm(DEEPNORM_ALPHA * x_ref[...] + mix, lng_ref[...], lnb_ref[...])
    o_ref[...] = xn
    xh, xm, xl = _split3(xn)
    wh, wm, wl = wr_ref[0], wr_ref[1], wr_ref[2]
    lg = (_dot(xh, wh) + _dot(xh, wm) + _dot(xm, wh) + _dot(xh, wl) + _dot(xl, wh) + _dot(xm, wm))
    lg_ref[...] = lg + br_ref[...]


def _proj_ln_router(y, x2d, w, ln_g, ln_b, w_router, b_router, tm=512):
    m, d = x2d.shape
    row = lambda i: (i, 0)
    wr = jnp.zeros((d, LANES), F32).at[:, :N_EXPERTS].set(w_router)
    wr3 = jnp.stack(_split3(wr))
    br = jnp.zeros((1, LANES), F32).at[0, :N_EXPERTS].set(b_router)
    return pl.pallas_call(
        _proj_ln_router_kernel,
        grid=(m // tm,),
        in_specs=[pl.BlockSpec((tm, y.shape[1]), row), pl.BlockSpec((tm, d), row), _resident(w.shape),
                  _resident((1, d)), _resident((1, d)), _resident(wr3.shape), _resident(br.shape)],
        out_specs=[pl.BlockSpec((tm, d), row), pl.BlockSpec((tm, LANES), row)],
        out_shape=[jax.ShapeDtypeStruct((m, d), F32), jax.ShapeDtypeStruct((m, LANES), F32)],
        compiler_params=_cparams("parallel"),
        name="proj_ln_router",
    )(y, x2d, w, ln_g.reshape(1, d), ln_b.reshape(1, d), wr3, br)


def _ple(xn, p, plew_ref, gw_ref, gb_ref):
    gate = _sigmoid(_dot(xn.astype(BF16), gw_ref[...]) + gb_ref[...])
    return xn + _dot(p.astype(BF16), plew_ref[...]) * gate


def _ffn_ple_kernel(x_ref, p_ref, wg_ref, wu_ref, wd_ref, lng_ref, lnb_ref, plew_ref, gw_ref, gb_ref,
                    o_ref, *, ff_chunk):
    x = x_ref[...]
    xb = x.astype(BF16)
    acc = jnp.zeros_like(x)
    for c in range(wg_ref.shape[1] // ff_chunk):
        sl = slice(c * ff_chunk, (c + 1) * ff_chunk)
        gt = _dot(xb, wg_ref[:, sl])
        up = _dot(xb, wu_ref[:, sl])
        h = (gt * _sigmoid(gt) * up).astype(BF16)
        acc = acc + _dot(h, wd_ref[sl, :])
    xn = _layer_norm(DEEPNORM_ALPHA * x + acc, lng_ref[...], lnb_ref[...])
    o_ref[...] = _ple(xn, p_ref[...], plew_ref, gw_ref, gb_ref)


def _ffn_ple(x2d, p2d, wg, wu, wd, ln_g, ln_b, ple_w, gate_w, gate_b, tm=256, ff_chunk=1408):
    m, d = x2d.shape
    row = lambda i: (i, 0)
    return pl.pallas_call(
        functools.partial(_ffn_ple_kernel, ff_chunk=ff_chunk),
        grid=(m // tm,),
        in_specs=[pl.BlockSpec((tm, d), row), pl.BlockSpec((tm, p2d.shape[1]), row),
                  _resident(wg.shape), _resident(wu.shape), _resident(wd.shape),
                  _resident((1, d)), _resident((1, d)), _resident(ple_w.shape), _resident(gate_w.shape),
                  _resident((1, d))],
        out_specs=pl.BlockSpec((tm, d), row),
        out_shape=jax.ShapeDtypeStruct((m, d), F32),
        compiler_params=_cparams("parallel"),
        name="ffn_ple",
    )(x2d, p2d, wg, wu, wd, ln_g.reshape(1, d), ln_b.reshape(1, d), ple_w, gate_w, gate_b.reshape(1, d))


def _res_ln_ple_kernel(x_ref, f_ref, p_ref, lng_ref, lnb_ref, plew_ref, gw_ref, gb_ref, o_ref):
    xn = _layer_norm(DEEPNORM_ALPHA * x_ref[...] + f_ref[...], lng_ref[...], lnb_ref[...])
    o_ref[...] = _ple(xn, p_ref[...], plew_ref, gw_ref, gb_ref)


def _res_ln_ple(x2d, f2d, p2d, ln_g, ln_b, ple_w, gate_w, gate_b, tm=512):
    m, d = x2d.shape
    row = lambda i: (i, 0)
    return pl.pallas_call(
        _res_ln_ple_kernel,
        grid=(m // tm,),
        in_specs=[pl.BlockSpec((tm, d), row), pl.BlockSpec((tm, d), row),
                  pl.BlockSpec((tm, p2d.shape[1]), row), _resident((1, d)), _resident((1, d)),
                  _resident(ple_w.shape), _resident(gate_w.shape), _resident((1, d))],
        out_specs=pl.BlockSpec((tm, d), row),
        out_shape=jax.ShapeDtypeStruct((m, d), F32),
        compiler_params=_cparams("parallel"),
        name="res_ln_ple",
    )(x2d, f2d, p2d, ln_g.reshape(1, d), ln_b.reshape(1, d), ple_w, gate_w, gate_b.reshape(1, d))


def _swap_half_cols(w):
    k, n = w.shape
    return w.reshape(k, n // HEAD_DIM, 2, HEAD_DIM // 2)[:, :, ::-1, :].reshape(k, n)


def _rope_tables(seq):
    half = HEAD_DIM // 2
    inv = ROPE_THETA ** (-jnp.arange(half, dtype=F32) / half)
    ang = jnp.arange(seq, dtype=F32)[:, None] * inv[None, :]
    cos, sin = jnp.cos(ang), jnp.sin(ang)
    return jnp.concatenate([cos, cos], -1), jnp.concatenate([-sin, sin], -1)


def _nsa_kv_kernel(x_ref, wk_ref, wvt_ref, cos_ref, sin_ref, zc_ref, vc_ref, ks_ref, kw_ref, vst_ref,
                   vwt_ref):
    G, dh = N_KV_GROUPS, HEAD_DIM
    gw = G * dh
    xb = x_ref[...].astype(BF16)
    res = _dot(xb, wk_ref[...])
    res_t = _dot_nt(wvt_ref[...], xb)
    tm = res.shape[0]
    cos = cos_ref[...]
    sin = sin_ref[...]
    ks = res[:, 2 * gw:3 * gw] * cos + res[:, 3 * gw:4 * gw] * sin
    kw = res[:, 4 * gw:5 * gw] * cos + res[:, 5 * gw:6 * gw] * sin
    t_glob = pl.program_id(1) * tm + lax.broadcasted_iota(jnp.int32, (tm, dh), 0)
    onehot = jnp.where(lax.broadcasted_iota(jnp.int32, (tm, dh), 1) == t_glob // SEL_BLOCK, 1.0, 0.0)
    for g in range(G):
        sl = slice(g * dh, (g + 1) * dh)
        zc_ref[0, g] = res[:, sl].astype(BF16)
        vc_ref[0, g] = res[:, gw + g * dh:gw + (g + 1) * dh].astype(BF16)
        ks_ref[0, g] = jnp.concatenate([ks[:, sl], onehot], axis=1).astype(BF16)
        kw_ref[0, g] = kw[:, sl].astype(BF16)
    vst_ref[0] = res_t[:gw].astype(BF16)
    vwt_ref[0] = res_t[gw:].astype(BF16)


def _nsa_kv(x2d, w_kv, batch, seq, tm=256):
    m, d = x2d.shape
    G, dh = N_KV_GROUPS, HEAD_DIM
    gw = G * dh
    parts = [w_kv[:, j * gw:(j + 1) * gw] for j in range(6)]
    w_k = jnp.concatenate([parts[0], parts[1], parts[2], _swap_half_cols(parts[2]),
                           parts[4], _swap_half_cols(parts[4])], axis=1).astype(BF16)
    w_vt = jnp.concatenate([parts[3], parts[5]], axis=1).T.astype(BF16)
    cos, sin = _rope_tables(seq)
    cos = jnp.tile(cos, (1, G))
    sin = jnp.tile(sin, (1, G))
    nt = seq // tm
    o64 = jax.ShapeDtypeStruct((batch, G, seq, dh), BF16)
    o128 = jax.ShapeDtypeStruct((batch, G, seq, 2 * dh), BF16)
    ot = jax.ShapeDtypeStruct((batch, gw, seq), BF16)
    b64 = pl.BlockSpec((1, G, tm, dh), lambda b, t: (b, 0, t, 0))
    b128 = pl.BlockSpec((1, G, tm, 2 * dh), lambda b, t: (b, 0, t, 0))
    bt = pl.BlockSpec((1, gw, tm), lambda b, t: (b, 0, t))
    return pl.pallas_call(
        _nsa_kv_kernel,
        grid=(batch, nt),
        in_specs=[pl.BlockSpec((tm, d), lambda b, t: (b * nt + t, 0)), _resident(w_k.shape),
                  _resident(w_vt.shape),
                  pl.BlockSpec((tm, gw), lambda b, t: (t, 0)), pl.BlockSpec((tm, gw), lambda b, t: (t, 0))],
        out_specs=[b64, b64, b128, b64, bt, bt],
        out_shape=[o64, o64, o128, o64, ot, ot],
        compiler_params=_cparams("parallel", "parallel"),
        name="nsa_kv",
    )(x2d, w_k, w_vt, cos, sin)


def _nsa_cmp_kernel(z_ref, pos_ref, w1_ref, b1_ref, w2_ref, o_ref, *, slabs_per_seq):
    half = w1_ref.shape[1] // 2
    z = z_ref[0]
    tm = z.shape[0]
    first = _dot(z, w1_ref[0, :half, :])
    second = _dot(z, w1_ref[0, half:, :])
    const = _dot(pos_ref[0], w1_ref[0])[0:1, :] + b1_ref[0]
    hid = first + pltpu.roll(second, shift=tm - 1, axis=0) + const
    act = 0.5 * hid * (1.0 + jnp.tanh(math.sqrt(2.0 / math.pi) * (hid + 0.044715 * hid * hid * hid)))
    out = _dot(act.astype(BF16), w2_ref[0])
    row = lax.broadcasted_iota(jnp.int32, out.shape, 0)
    o_ref[0] = jnp.where(row % slabs_per_seq == slabs_per_seq - 1, 0.0, out).astype(BF16)


def _nsa_compress(zc, vc, cmp_pos, phi_w1, phi_b1, phi_w2, tm=512):
    batch, G, seq, dh = zc.shape
    slab = CMP_STRIDE * dh
    rows = batch * G * seq // CMP_STRIDE
    tm = min(tm, rows)
    z = jnp.stack([zc.reshape(rows, slab), vc.reshape(rows, slab)])
    pos = jnp.broadcast_to(cmp_pos.reshape(2, 1, CMP_LEN * dh), (2, 8, CMP_LEN * dh)).astype(BF16)
    out = pl.pallas_call(
        functools.partial(_nsa_cmp_kernel, slabs_per_seq=seq // CMP_STRIDE),
        grid=(2, rows // tm),
        in_specs=[pl.BlockSpec((1, tm, slab), lambda j, i: (j, i, 0)),
                  pl.BlockSpec((1, 8, CMP_LEN * dh), lambda j, i: (j, 0, 0)),
                  pl.BlockSpec((1, CMP_LEN * dh, CMP_HIDDEN), lambda j, i: (j, 0, 0)),
                  pl.BlockSpec((1, 1, CMP_HIDDEN), lambda j, i: (j, 0, 0)),
                  pl.BlockSpec((1, CMP_HIDDEN, dh), lambda j, i: (j, 0, 0))],
        out_specs=pl.BlockSpec((1, tm, dh), lambda j, i: (j, i, 0)),
        out_shape=jax.ShapeDtypeStruct((2, rows, dh), BF16),
        compiler_params=_cparams("parallel", "parallel"),
        name="nsa_compress",
    )(z, pos, phi_w1.astype(BF16), phi_b1.reshape(2, 1, CMP_HIDDEN), phi_w2.astype(BF16))
    n_slab = seq // CMP_STRIDE
    return out[0].reshape(batch, G, n_slab, dh), out[1].reshape(batch, G, n_slab, dh)


GATE_ROWS = 16


def _nsa_q_kernel(x_ref, wt_ref, cos_ref, sin_ref, q_ref, qr_ref, gate_ref):
    dh = HEAD_DIM
    hw = N_HEADS * dh
    scale = HEAD_DIM ** -0.5
    res_t = _dot_nt(wt_ref[...], x_ref[...].astype(BF16))
    tm = res_t.shape[1]
    q = res_t[:hw].reshape(N_HEADS, dh, tm)
    q_sw = res_t[hw:2 * hw].reshape(N_HEADS, dh, tm)
    qr = q * cos_ref[...][None] + q_sw * sin_ref[...][None]
    q_ref[0] = (q * scale).reshape(hw, tm).astype(BF16)
    qr_ref[0] = (qr * scale).reshape(hw, tm).astype(BF16)
    gate_ref[0] = _sigmoid(res_t[2 * hw:])


def _nsa_q(x2d, w_in, batch, seq, tm=256):
    m, d = x2d.shape
    G, R, dh = N_KV_GROUPS, Q_PER_GROUP, HEAD_DIM
    hw = N_HEADS * dh
    w_gate = w_in[:, hw:].reshape(d, G, R * N_BRANCH)
    w_gate = jnp.pad(w_gate, ((0, 0), (0, 0), (0, GATE_ROWS - R * N_BRANCH))).reshape(d, G * GATE_ROWS)
    wt = jnp.concatenate([w_in[:, :hw], _swap_half_cols(w_in[:, :hw]), w_gate], axis=1).T.astype(BF16)
    cos, sin = _rope_tables(seq)
    nt = seq // tm
    oq = jax.ShapeDtypeStruct((batch, hw, seq), BF16)
    bq = pl.BlockSpec((1, hw, tm), lambda b, t: (b, 0, t))
    tab = pl.BlockSpec((dh, tm), lambda b, t: (0, t))
    return pl.pallas_call(
        _nsa_q_kernel,
        grid=(batch, nt),
        in_specs=[pl.BlockSpec((tm, d), lambda b, t: (b * nt + t, 0)), _resident(wt.shape), tab, tab],
        out_specs=[bq, bq, pl.BlockSpec((1, G * GATE_ROWS, tm), lambda b, t: (b, 0, t))],
        out_shape=[oq, oq, jax.ShapeDtypeStruct((batch, G * GATE_ROWS, seq), F32)],
        compiler_params=_cparams("parallel", "parallel"),
        name="nsa_q",
    )(x2d, wt, cos.T, sin.T)


def _nsa_attn_kernel(q_ref, qr_ref, gate_ref, kc_ref, vct_ref, ks_ref, vst_ref, kw_ref, vwt_ref, ovlt_ref,
                     o_ref, *, tq, tk, n_blk):
    R, dh = Q_PER_GROUP, HEAD_DIM
    cols = R * tq
    qi = pl.program_id(2)
    t0 = qi * tq
    n_blk_pad = ovlt_ref.shape[0]

    def heads_to_lanes(x):
        return jnp.concatenate([x[r * dh:(r + 1) * dh] for r in range(R)], axis=1)

    q = heads_to_lanes(q_ref[0])
    qr = heads_to_lanes(qr_ref[0])

    def col_pos(shape):
        return t0 + (lax.broadcasted_iota(jnp.int32, shape, 1) & (tq - 1))

    def key_tile(j):
        return pl.ds(pl.multiple_of(j * tk, tk), tk)

    ct = min(cols, 256)
    col_tiles = [slice(c, c + ct) for c in range(0, cols, ct)]

    def tile_scores(k_ref, q_op, tiles):
        return [[_dot(k_ref[0, 0, key_tile(j), :], q_op[:, cs]) for j, _ in tiles] for cs in col_tiles]

    def attend(carry, sc, vt_ref, tiles):
        m_i, l_i, acc = carry
        outs = []
        for cs, sc_c in zip(col_tiles, sc):
            shape = sc_c[0].shape
            t_pos = t0 + ((cs.start + lax.broadcasted_iota(jnp.int32, shape, 1)) & (tq - 1))
            masked = []
            for s_j, (j, mask_fn) in zip(sc_c, tiles):
                if mask_fn is not None:
                    k_pos = j * tk + lax.broadcasted_iota(jnp.int32, shape, 0)
                    s_j = jnp.where(mask_fn(k_pos, t_pos), s_j, NEG_INF)
                masked.append(s_j)
            m_new = m_i[:, cs]
            for s_j in masked:
                m_new = jnp.maximum(m_new, jnp.max(s_j, axis=0, keepdims=True))
            alpha = jnp.exp(m_i[:, cs] - m_new)
            l_new = alpha * l_i[:, cs]
            acc_new = alpha * acc[:, cs]
            for s_j, (j, _) in zip(masked, tiles):
                p = jnp.exp(s_j - m_new)
                l_new = l_new + jnp.sum(p, axis=0, keepdims=True)
                acc_new = acc_new + _dot(vt_ref[0, :, key_tile(j)], p.astype(BF16))
            outs.append((m_new, l_new, acc_new))
        return tuple(jnp.concatenate([o[i] for o in outs], axis=1) for i in range(3))

    init = (jnp.full((1, cols), NEG_INF, F32), jnp.zeros((1, cols), F32), jnp.zeros((dh, cols), F32))

    causal = lambda k_pos, t_pos: k_pos <= t_pos
    win_tiles = [(jnp.maximum(qi - 2, 0), lambda k_pos, t_pos: (t_pos - k_pos < WINDOW) & (qi >= 2)),
                 (jnp.maximum(qi - 1, 0), lambda k_pos, t_pos: (k_pos <= t_pos) & (qi >= 1)),
                 (qi, causal)]
    sc_win = tile_scores(kw_ref, qr, win_tiles)

    s = _dot(kc_ref[0, 0], q)
    cmp_end = lax.broadcasted_iota(jnp.int32, s.shape, 0) * CMP_STRIDE + (CMP_LEN - 1)
    valid = cmp_end <= col_pos(s.shape)
    sm = jnp.where(valid, s, NEG_INF)
    e = jnp.where(valid, jnp.exp(sm - jnp.max(sm, axis=0, keepdims=True)), 0.0)
    l = jnp.sum(e, axis=0, keepdims=True)
    p_cmp = e / jnp.where(l > 0.0, l, 1.0)
    o_cmp = _dot(vct_ref[0, 0], p_cmp.astype(BF16))

    p_sum = p_cmp[:, 0:tq]
    for r in range(1, R):
        p_sum = p_sum + p_cmp[:, r * tq:(r + 1) * tq]
    hi, lo = _split2(p_sum)
    imp = _dot(ovlt_ref[...], hi) + _dot(ovlt_ref[...], lo)
    blk = lax.broadcasted_iota(jnp.int32, imp.shape, 0)
    t_q = t0 + lax.broadcasted_iota(jnp.int32, imp.shape, 1)
    cur = t_q // SEL_BLOCK
    forced = (blk == 0) | (blk == cur) | (blk == cur - 1)
    score = jnp.where(forced, FORCE_SCORE, jnp.where(blk * SEL_BLOCK <= t_q, imp, NEG_INF))
    score = jnp.where(blk < n_blk, score, -jnp.inf)
    selected = blk >= n_blk
    for _ in range(N_SEL):
        best = jnp.max(score, axis=0, keepdims=True)
        first = jnp.min(jnp.where(score == best, blk, 2 * LANES), axis=0, keepdims=True)
        pick = blk == first
        selected = selected | pick
        score = jnp.where(pick, -jnp.inf, score)
    bias = jnp.where(selected, 0.0, NEG_INF)
    qa = jnp.concatenate([qr, jnp.concatenate([bias] * R, axis=1).astype(BF16),
                          jnp.zeros((dh - n_blk_pad, cols), BF16)], axis=0)

    _, l_w, acc_w = attend(init, sc_win, vwt_ref, win_tiles)
    o_win = acc_w / l_w

    n_pairs = qi // 2

    def pair_body(i, carry):
        tiles = [(2 * i, None), (2 * i + 1, None)]
        return attend(carry, tile_scores(ks_ref, qa, tiles), vst_ref, tiles)

    carry = lax.fori_loop(0, n_pairs, pair_body, init)
    j_odd = 2 * n_pairs
    tail_tiles = [(j_odd, lambda k_pos, t_pos: (k_pos <= t_pos) & (j_odd < qi)), (qi, causal)]
    _, l_s, acc_s = attend(carry, tile_scores(ks_ref, qa, tail_tiles), vst_ref, tail_tiles)
    o_slc = acc_s / l_s

    gates = gate_ref[0]
    outs = []
    for r in range(R):
        cs = slice(r * tq, (r + 1) * tq)
        outs.append(gates[3 * r:3 * r + 1] * o_cmp[:, cs] + gates[3 * r + 1:3 * r + 2] * o_slc[:, cs]
                    + gates[3 * r + 2:3 * r + 3] * o_win[:, cs])
    halves = [jnp.transpose(jnp.concatenate(outs[i:i + 2], axis=0)) for i in range(0, R, 2)]
    o_ref[0] = jnp.concatenate(halves, axis=1)


SEL_ROWS = 32


def _nsa_attention(q, qr, gates, k_cmp, v_cmp, ks, vst, kw, vwt, tq=256, tk=256):
    batch, _, seq = q.shape
    G, R, dh = N_KV_GROUPS, Q_PER_GROUP, HEAD_DIM
    n_slab = k_cmp.shape[2]
    n_blk = seq // SEL_BLOCK
    assert n_blk <= SEL_ROWS
    assert tq == tk and WINDOW == 2 * tk, "the window branch is written as exactly three key tiles"
    cmp_start = jnp.arange(n_slab) * CMP_STRIDE
    blk_start = jnp.arange(SEL_ROWS) * SEL_BLOCK
    overlap_t = ((cmp_start[None, :] < blk_start[:, None] + SEL_BLOCK)
                 & (cmp_start[None, :] + CMP_LEN - 1 >= blk_start[:, None])
                 & (jnp.arange(SEL_ROWS)[:, None] < n_blk)
                 & (jnp.arange(n_slab)[None, :] < (seq - CMP_LEN) // CMP_STRIDE + 1)).astype(BF16)
    v_cmp_t = jnp.swapaxes(v_cmp, 2, 3)
    qspec = pl.BlockSpec((1, R * dh, tq), lambda b, g, i: (b, g, i))
    full = lambda a: pl.BlockSpec((1, 1) + a.shape[2:], lambda b, g, i: (b, g, 0, 0))
    vspec = pl.BlockSpec((1, dh, seq), lambda b, g, i: (b, g, 0))
    return pl.pallas_call(
        functools.partial(_nsa_attn_kernel, tq=tq, tk=tk, n_blk=n_blk),
        grid=(batch, G, seq // tq),
        in_specs=[qspec, qspec, pl.BlockSpec((1, GATE_ROWS, tq), lambda b, g, i: (b, g, i)),
                  full(k_cmp), full(v_cmp_t), full(ks), vspec, full(kw), vspec,
                  _resident(overlap_t.shape)],
        out_specs=pl.BlockSpec((1, tq, R * dh), lambda b, g, i: (b, i, g)),
        out_shape=jax.ShapeDtypeStruct((batch, seq, G * R * dh), F32),
        compiler_params=_cparams("parallel", "parallel", "parallel"),
        name="nsa_attn",
    )(q, qr, gates, k_cmp, v_cmp_t, ks, vst, kw, vwt, overlap_t)


def _moe_kernel(be_ref, nb_ref, x_ref, wg_ref, wu_ref, wd_ref, o_ref):
    i = pl.program_id(0)

    @pl.when(i < nb_ref[0])
    def _():
        xb = x_ref[...].astype(BF16)
        gt = _dot(xb, wg_ref[0])
        up = _dot(xb, wu_ref[0])
        h = (gt * _sigmoid(gt) * up).astype(BF16)
        o_ref[...] = _dot(h, wd_ref[0])

    @pl.when(i >= nb_ref[0])
    def _():
        o_ref[...] = jnp.zeros_like(o_ref)


def _moe_experts(xs, block_e, n_used, wg, wu, wd):
    n_rows, d = xs.shape
    n_blocks = n_rows // MOE_ROWS
    ff = wg.shape[2]
    grid_spec = pltpu.PrefetchScalarGridSpec(
        num_scalar_prefetch=2,
        grid=(n_blocks,),
        in_specs=[pl.BlockSpec((MOE_ROWS, d), lambda i, be, nb: (i, 0)),
                  pl.BlockSpec((1, d, ff), lambda i, be, nb: (be[i], 0, 0)),
                  pl.BlockSpec((1, d, ff), lambda i, be, nb: (be[i], 0, 0)),
                  pl.BlockSpec((1, ff, d), lambda i, be, nb: (be[i], 0, 0))],
        out_specs=pl.BlockSpec((MOE_ROWS, d), lambda i, be, nb: (i, 0)),
    )
    return pl.pallas_call(
        _moe_kernel,
        grid_spec=grid_spec,
        out_shape=jax.ShapeDtypeStruct((n_rows, d), F32),
        compiler_params=_cparams("arbitrary"),
        name="moe_experts",
    )(block_e, n_used, xs, wg, wu, wd)


def _moe(x2d, logits, w_gu, w_down):
    n_tok, d = x2d.shape
    n_assign = n_tok * TOP_K
    top_logit, top_e = lax.top_k(logits, TOP_K)
    weights = jax.nn.softmax(top_logit, axis=-1)
    flat_e = top_e.reshape(-1).astype(jnp.int32)
    onehot = (flat_e[:, None] == jnp.arange(N_EXPERTS, dtype=jnp.int32)[None, :]).astype(jnp.int32)
    running = jnp.cumsum(onehot, axis=0)
    counts = running[-1]
    padded = (counts + MOE_ROWS - 1) // MOE_ROWS * MOE_ROWS
    pad_end = jnp.cumsum(padded)
    pad_start = pad_end - padded
    grp_start = jnp.cumsum(counts) - counts
    pos = jnp.sum(onehot * (running - 1 + pad_start[None, :]), axis=1).reshape(n_tok, TOP_K)
    n_blocks = -(-n_assign // MOE_ROWS) + N_EXPERTS
    block_e = jnp.minimum(jnp.searchsorted(pad_end, jnp.arange(n_blocks) * MOE_ROWS, side='right'),
                          N_EXPERTS - 1).astype(jnp.int32)
    n_used = (pad_end[-1] // MOE_ROWS).astype(jnp.int32).reshape(1)
    order = jnp.argsort(flat_e)
    tok_sorted = jnp.concatenate([(order // TOP_K).astype(jnp.int32), jnp.zeros((MOE_ROWS,), jnp.int32)])
    blk_off = jnp.arange(n_blocks, dtype=jnp.int32) * MOE_ROWS - pad_start[block_e]
    src = jnp.clip(grp_start[block_e] + blk_off, 0, n_assign)
    row_tok = jax.vmap(lambda s: lax.dynamic_slice(tok_sorted, (s,), (MOE_ROWS,)))(src)
    in_group = (blk_off[:, None] + jnp.arange(MOE_ROWS, dtype=jnp.int32)[None, :]) < counts[block_e][:, None]
    row_tok = jnp.where(in_group, row_tok, 0).reshape(-1)
    xs = x2d[row_tok]
    ys = _moe_experts(xs, block_e, n_used, w_gu[:, :, :D_FF_EXPERT].astype(BF16),
                      w_gu[:, :, D_FF_EXPERT:].astype(BF16), w_down.astype(BF16))
    out = ys[pos[:, 0]] * weights[:, 0:1]
    for s in range(1, TOP_K):
        out = out + ys[pos[:, s]] * weights[:, s:s + 1]
    return out


def kernel(x, p, a_mu, a_w_rkv, a_w0, a_w1, a_w2, a_a0, a_a1, a_a2, a_g1, a_g2, a_k_k, a_k_a, a_r_k,
           a_gn_g, a_gn_b, a_w_o, b_w_kv, b_cmp_pos, b_phi_w1, b_phi_b1, b_phi_w2, b_w_in, b_w_o,
           f_w_gu, f_w_down, m_w_router, m_b_router, m_w_gu, m_w_down, ln_g, ln_b, ple_w, ple_gate_w,
           ple_gate_b):
    batch, seq, d = x.shape
    m = batch * seq
    x0 = x.reshape(m, d)
    p2d = p.reshape(DEPTH, m, PLE_DIM)
    bf = lambda w: w.astype(BF16)

    r, k, v, dl, a, g = _rwkv_proj(
        x0, seq, a_mu[0], bf(a_w_rkv[0]), a_w0[0].reshape(1, d), bf(a_w1[0]), bf(a_w2[0]),
        a_a0[0].reshape(1, d), bf(a_a1[0]), bf(a_a2[0]), bf(a_g1[0]), bf(a_g2[0]))
    y = _rwkv_scan(r, k, v, dl, a, g, a_k_k[0], a_k_a[0], a_r_k[0], a_gn_g[0], a_gn_b[0], batch, seq)
    x1 = _proj_ln(y, x0, bf(a_w_o[0]), ln_g[0, 0], ln_b[0, 0])
    x2 = _ffn_ple(x1, p2d[0], bf(f_w_gu[0][:, :D_FF]), bf(f_w_gu[0][:, D_FF:]), bf(f_w_down[0]),
                  ln_g[0, 1], ln_b[0, 1], bf(ple_w[0]), bf(ple_gate_w[0]), ple_gate_b[0])

    zc, vc, ks, kw, vst, vwt = _nsa_kv(x2, b_w_kv, batch, seq)
    k_cmp, v_cmp = _nsa_compress(zc, vc, b_cmp_pos, b_phi_w1, b_phi_b1, b_phi_w2)
    q, qr, gates = _nsa_q(x2, b_w_in[0], batch, seq)
    attn = _nsa_attention(q, qr, gates, k_cmp, v_cmp, ks, vst, kw, vwt).reshape(m, d)
    x3, logits = _proj_ln_router(attn, x2, bf(b_w_o[0]), ln_g[1, 0], ln_b[1, 0], m_w_router[0],
                                 m_b_router[0])
    ffn = _moe(x3, logits[:, :N_EXPERTS], m_w_gu[0], m_w_down[0])
    out = _res_ln_ple(x3, ffn, p2d[1], ln_g[1, 1], ln_b[1, 1], bf(ple_w[1]), bf(ple_gate_w[1]),
                      ple_gate_b[1])
    return out.reshape(batch, seq, d)
```

```python
import functools
import math

import jax
import jax.numpy as jnp
from jax import lax
from jax.experimental import pallas as pl
from jax.experimental.pallas import tpu as pltpu

BF16 = jnp.bfloat16
F32 = jnp.float32

LANES = 128
VMEM_LIMIT_BYTES = 56 * 1024 * 1024

D_MODEL = 1024
PLE_DIM = 256
RWKV_HEAD = 64
GN_EPS = 64e-5
N_HEADS = 16
HEAD_DIM = 64
N_KV_GROUPS = 4
Q_PER_GROUP = 4
N_BRANCH = 3
CMP_LEN = 32
CMP_STRIDE = 16
CMP_HIDDEN = 256
SEL_BLOCK = 64
N_SEL = 8
WINDOW = 512
ROPE_THETA = 10000.0
D_FF = 2816
N_EXPERTS = 8
TOP_K = 2
D_FF_EXPERT = 1408
MOE_ROWS = 256
LN_EPS = 1e-5
DEPTH = 2
DEEPNORM_ALPHA = (2.0 * DEPTH) ** 0.25
NEG_INF = -1e30
FORCE_SCORE = 1e4

SCAN_CHUNK = 64


def _cparams(*sem):
    return pltpu.CompilerParams(dimension_semantics=sem, vmem_limit_bytes=VMEM_LIMIT_BYTES)


def _resident(shape):
    nd = len(shape)
    return pl.BlockSpec(shape, lambda *_: (0,) * nd, pipeline_mode=pl.Buffered(1))


def _dot(a, b):
    return jnp.dot(a, b, preferred_element_type=F32)


def _dot_nt(a, b):
    return lax.dot_general(a, b, (((1,), (1,)), ((), ())), preferred_element_type=F32)


def _split2(x):
    hi = x.astype(BF16)
    lo = (x - hi.astype(F32)).astype(BF16)
    return hi, lo


def _split3(x):
    hi = x.astype(BF16)
    r1 = x - hi.astype(F32)
    mid = r1.astype(BF16)
    lo = (r1 - mid.astype(F32)).astype(BF16)
    return hi, mid, lo


def _layer_norm(y, g, b):
    mu = jnp.mean(y, axis=-1, keepdims=True)
    yc = y - mu
    var = jnp.mean(yc * yc, axis=-1, keepdims=True)
    return yc * lax.rsqrt(var + LN_EPS) * g + b


def _sigmoid(z):
    return 1.0 / (1.0 + jnp.exp(-z))


def _rwkv_project(first, x_ref, xh_ref, mu_ref, wrkv_ref, w0_ref, w1_ref, w2_ref, a0_ref, a1_ref,
                  a2_ref, g1_ref, g2_ref, r_ref, k_ref, v_ref, dl_ref, a_ref, g_ref):
    x = x_ref[...]
    prev_row = jnp.where(first, 0.0, xh_ref[0, 7:8, :])
    row = lax.broadcasted_iota(jnp.int32, x.shape, 0)
    x_shift = jnp.where(row == 0, prev_row, pltpu.roll(x, shift=1, axis=0))
    xx = x_shift - x

    def mix(i):
        return (x + xx * mu_ref[i:i + 1, :]).astype(BF16)

    r_ref[...] = _dot(mix(0), wrkv_ref[0])
    k_ref[...] = _dot(mix(1), wrkv_ref[1])
    v_ref[...] = _dot(mix(2), wrkv_ref[2])
    z = w0_ref[...] + _dot(jnp.tanh(_dot(mix(3), w1_ref[...])).astype(BF16), w2_ref[...])
    dl_ref[...] = -math.exp(-0.5) * _sigmoid(z)
    a_ref[...] = _sigmoid(a0_ref[...] + _dot(_dot(mix(4), a1_ref[...]).astype(BF16), a2_ref[...]))
    g_ref[...] = _dot(_sigmoid(_dot(mix(5), g1_ref[...])).astype(BF16), g2_ref[...])


def _rwkv_recurrence(first, r_ref, k_ref, v_ref, dl_ref, a_ref, g_ref, kk_ref, ka_ref, rk_ref, gng_ref,
                     gnb_ref, o_ref, state_ref, kkn_ref, cum_ref, *, n_pairs, n_chunks):
    C = SCAN_CHUNK
    N = RWKV_HEAD
    W = 2 * N
    tb = n_chunks * C

    @pl.when(first)
    def _():
        state_ref[...] = jnp.zeros_like(state_ref)

    tt = lax.broadcasted_iota(jnp.int32, (C, W), 0)
    ln = lax.broadcasted_iota(jnp.int32, (C, W), 1)
    ss = ln & (N - 1)
    strict_lower = ss < tt
    lower = ss <= tt
    eye = jnp.where(ss == tt, 1.0, 0.0)
    level_masks = []
    for lg in range(int(math.log2(C))):
        level_masks.append(((tt >> (lg + 1)) == (ss >> (lg + 1))) & ((tt >> lg) == (ss >> lg) + 1))
    rr = lax.broadcasted_iota(jnp.int32, (W, W), 0)
    cc = lax.broadcasted_iota(jnp.int32, (W, W), 1)
    same_head = (rr < N) == (cc < N)
    ones_bd = jnp.where(same_head, 1.0, 0.0).astype(BF16)
    lane_head0 = ln < N

    def bd(y):
        return jnp.concatenate([jnp.where(lane_head0, y, 0.0), jnp.where(lane_head0, 0.0, y)],
                               axis=0).astype(BF16)

    def pmm(xp, ybd):
        return _dot(xp.astype(BF16), ybd)

    def head_sum(xp):
        hi, lo = _split2(xp)
        return _dot(hi, ones_bd) + _dot(lo, ones_bd)

    tri_r = lax.broadcasted_iota(jnp.int32, (tb, tb), 0)
    tri_c = lax.broadcasted_iota(jnp.int32, (tb, tb), 1)
    lg_c = int(math.log2(C))
    tri_chunk = jnp.where((tri_c <= tri_r) & ((tri_c >> lg_c) == (tri_r >> lg_c)), 1.0, 0.0).astype(BF16)
    d_hi, d_mid, d_lo = _split3(dl_ref[...])
    cum_ref[...] = _dot(tri_chunk, d_hi) + _dot(tri_chunk, d_mid) + _dot(tri_chunk, d_lo)
    for pi in range(n_pairs):
        cols = slice(pi * W, (pi + 1) * W)
        kk_raw = k_ref[:, cols] * kk_ref[:, cols]
        kkn_ref[:, cols] = kk_raw / jnp.maximum(jnp.sqrt(head_sum(kk_raw * kk_raw)), 1e-12)

    def chunk_load(ci, pi):
        rows = pl.ds(pl.multiple_of(ci * C, C), C)
        cols = slice(pi * W, (pi + 1) * W)
        return (r_ref[rows, cols], k_ref[rows, cols], v_ref[rows, cols], dl_ref[rows, cols],
                a_ref[rows, cols], kkn_ref[rows, cols], cum_ref[rows, cols], state_ref[pi])

    def chunk_compute(pi, r, k, v, dl, a, kk, cum, m0):
        cols = slice(pi * W, (pi + 1) * W)
        k2 = k * (1.0 + (a - 1.0) * ka_ref[:, cols])
        b = a * kk
        p_incl = jnp.exp(cum)
        p_excl = jnp.exp(cum - dl)
        p_inv = jnp.exp(-cum)
        kq = kk * p_excl
        rq = r * p_incl
        bk = b * p_inv
        kkd = k2 * p_inv

        bkt = jnp.transpose(jnp.concatenate([b, k2], axis=0))
        cumt = jnp.transpose(jnp.concatenate([cum, cum], axis=0))
        last_col = cumt[:, C - 1:C]
        lhs_t = (bkt * jnp.exp(last_col - cumt)).astype(BF16)
        m_decayed = m0 * jnp.exp(last_col)

        lhs = jnp.concatenate([kq, rq], axis=0).astype(BF16)
        yield
        ab = _dot_nt(lhs, bd(bk))
        gb = _dot_nt(lhs, bd(kkd))
        a_m = jnp.where(strict_lower, ab[:C], 0.0)
        aq_m = jnp.where(lower, ab[C:], 0.0)
        g_m = jnp.where(strict_lower, gb[:C], 0.0)
        gq_m = jnp.where(lower, gb[C:], 0.0)

        tinv = eye - jnp.where(level_masks[0], a_m, 0.0)
        yield
        gv = pmm(jnp.concatenate([g_m, gq_m], axis=0), bd(v))
        for lm in level_masks[1:]:
            w_ = pmm(jnp.where(lm, a_m, 0.0), bd(tinv))
            yield
            tinv = tinv - pmm(tinv, bd(w_))
            yield

        kq_y = pmm(tinv, jnp.concatenate([bd(kq), bd(gv[:C])], axis=1))
        kq1 = kq_y[:, :W]
        y = kq_y[:, W:]
        yield
        aq_ky = pmm(aq_m, jnp.concatenate([bd(kq1), bd(y)], axis=1))
        rq1 = rq - aq_ky[:, :W]
        o_loc = gv[C:] - aq_ky[:, W:]
        yield
        st = _dot(jnp.concatenate([rq1, kq1], axis=0).astype(BF16), m0.astype(BF16))
        o = st[:C] + o_loc
        z = -(st[C:] + y)
        yield
        upd = _dot(lhs_t, jnp.concatenate([z, v], axis=0).astype(BF16))
        return o, m_decayed + jnp.where(same_head, upd, 0.0)

    def interleave(gens):
        results = [None] * len(gens)
        live = list(range(len(gens)))
        while live:
            for i in list(live):
                try:
                    next(gens[i])
                except StopIteration as done:
                    results[i] = done.value
                    live.remove(i)
        return results

    def body(ci, carry):
        loaded = [chunk_load(ci, pi) for pi in range(n_pairs)]
        results = interleave([chunk_compute(pi, *loaded[pi]) for pi in range(n_pairs)])
        rows = pl.ds(pl.multiple_of(ci * C, C), C)
        for pi, (o, m_new) in enumerate(results):
            o_ref[rows, pi * W:(pi + 1) * W] = o
            state_ref[pi] = m_new
        return carry

    lax.fori_loop(0, n_chunks, body, 0)

    for pi in range(n_pairs):
        cols = slice(pi * W, (pi + 1) * W)
        o = o_ref[:, cols]
        r = r_ref[:, cols]
        v = v_ref[:, cols]
        k2 = k_ref[:, cols] * (1.0 + (a_ref[:, cols] - 1.0) * ka_ref[:, cols])
        oc = o - head_sum(o) * (1.0 / N)
        var_o = head_sum(oc * oc) * (1.0 / N)
        out = oc * lax.rsqrt(var_o + GN_EPS) * gng_ref[:, cols] + gnb_ref[:, cols]
        out = out + head_sum(r * k2 * rk_ref[:, cols]) * v
        o_ref[:, cols] = out * g_ref[:, cols]


def _rwkv_layer_kernel(x_ref, xh_ref, mu_ref, wrkv_ref, w0_ref, w1_ref, w2_ref, a0_ref, a1_ref, a2_ref,
                       g1_ref, g2_ref, kk_ref, ka_ref, rk_ref, gng_ref, gnb_ref, wo_ref, lng_ref, lnb_ref,
                       o_ref, state_ref, r_s, k_s, v_s, dl_s, a_s, g_s, kkn_s, cum_s, y_s, *, n_chunks):
    first = pl.program_id(1) == 0
    _rwkv_project(first, x_ref, xh_ref, mu_ref, wrkv_ref, w0_ref, w1_ref, w2_ref, a0_ref, a1_ref, a2_ref,
                  g1_ref, g2_ref, r_s, k_s, v_s, dl_s, a_s, g_s)
    _rwkv_recurrence(first, r_s, k_s, v_s, dl_s, a_s, g_s, kk_ref, ka_ref, rk_ref, gng_ref, gnb_ref, y_s,
                     state_ref, kkn_s, cum_s, n_pairs=x_ref.shape[1] // (2 * RWKV_HEAD), n_chunks=n_chunks)
    mix = _dot(y_s[...].astype(BF16), wo_ref[...])
    o_ref[...] = _layer_norm(DEEPNORM_ALPHA * x_ref[...] + mix, lng_ref[...], lnb_ref[...])


def _rwkv_layer(x2d, batch, seq, mu, w_rkv, w0, w1, w2, a0, a1, a2, g1, g2, k_k, k_a, r_k, gn_g, gn_b,
                w_o, ln_g, ln_b, tb=256):
    m, d = x2d.shape
    nt = seq // tb
    xh = x2d.reshape(m // 8, 8, d)
    vec = lambda z: z.reshape(1, d)
    blk = pl.BlockSpec((tb, d), lambda b, t: (b * nt + t, 0))
    weights = [mu, w_rkv, vec(w0), w1, w2, vec(a0), a1, a2, g1, g2, vec(k_k), vec(k_a), vec(r_k),
               vec(gn_g), vec(gn_b), w_o, vec(ln_g), vec(ln_b)]
    n_pairs = d // (2 * RWKV_HEAD)
    return pl.pallas_call(
        functools.partial(_rwkv_layer_kernel, n_chunks=tb // SCAN_CHUNK),
        grid=(batch, nt),
        in_specs=[blk, pl.BlockSpec((1, 8, d), lambda b, t: (jnp.maximum((b * nt + t) * (tb // 8) - 1, 0), 0, 0))]
        + [_resident(w.shape) for w in weights],
        out_specs=blk,
        out_shape=jax.ShapeDtypeStruct((m, d), F32),
        scratch_shapes=[pltpu.VMEM((n_pairs, 2 * RWKV_HEAD, 2 * RWKV_HEAD), F32)]
        + [pltpu.VMEM((tb, d), F32)] * 9,
        compiler_params=_cparams("parallel", "arbitrary"),
        name="rwkv_layer",
    )(x2d, xh, *weights)


def _proj_ln_router_kernel(y_ref, x_ref, w_ref, lng_ref, lnb_ref, wr_ref, br_ref, o_ref, lg_ref):
    mix = _dot(y_ref[...].astype(BF16), w_ref[...])
    xn = _layer_norm(DEEPNORM_ALPHA * x_ref[...] + mix, lng_ref[...], lnb_ref[...])
    o_ref[...] = xn
    xh, xm, xl = _split3(xn)
    wh, wm, wl = wr_ref[0], wr_ref[1], wr_ref[2]
    lg = (_dot(xh, wh) + _dot(xh, wm) + _dot(xm, wh) + _dot(xh, wl) + _dot(xl, wh) + _dot(xm, wm))
    lg_ref[...] = lg + br_ref[...]


def _proj_ln_router(y, x2d, w, ln_g, ln_b, w_router, b_router, tm=512):
    m, d = x2d.shape
    row = lambda i: (i, 0)
    wr = jnp.zeros((d, LANES), F32).at[:, :N_EXPERTS].set(w_router)
    wr3 = jnp.stack(_split3(wr))
    br = jnp.zeros((1, LANES), F32).at[0, :N_EXPERTS].set(b_router)
    return pl.pallas_call(
        _proj_ln_router_kernel,
        grid=(m // tm,),
        in_specs=[pl.BlockSpec((tm, y.shape[1]), row), pl.BlockSpec((tm, d), row), _resident(w.shape),
                  _resident((1, d)), _resident((1, d)), _resident(wr3.shape), _resident(br.shape)],
        out_specs=[pl.BlockSpec((tm, d), row), pl.BlockSpec((tm, LANES), row)],
        out_shape=[jax.ShapeDtypeStruct((m, d), F32), jax.ShapeDtypeStruct((m, LANES), F32)],
        compiler_params=_cparams("parallel"),
        name="proj_ln_router",
    )(y, x2d, w, ln_g.reshape(1, d), ln_b.reshape(1, d), wr3, br)


def _ple(xn, p, plew_ref, gw_ref, gb_ref):
    gate = _sigmoid(_dot(xn.astype(BF16), gw_ref[...]) + gb_ref[...])
    return xn + _dot(p.astype(BF16), plew_ref[...]) * gate


def _ffn_ple_kernel(x_ref, p_ref, wg_ref, wu_ref, wd_ref, lng_ref, lnb_ref, plew_ref, gw_ref, gb_ref,
                    o_ref, *, ff_chunk):
    x = x_ref[...]
    xb = x.astype(BF16)
    acc = jnp.zeros_like(x)
    for c in range(wg_ref.shape[1] // ff_chunk):
        sl = slice(c * ff_chunk, (c + 1) * ff_chunk)
        gt = _dot(xb, wg_ref[:, sl])
        up = _dot(xb, wu_ref[:, sl])
        h = (gt * _sigmoid(gt) * up).astype(BF16)
        acc = acc + _dot(h, wd_ref[sl, :])
    xn = _layer_norm(DEEPNORM_ALPHA * x + acc, lng_ref[...], lnb_ref[...])
    o_ref[...] = _ple(xn, p_ref[...], plew_ref, gw_ref, gb_ref)


def _ffn_ple(x2d, p2d, wg, wu, wd, ln_g, ln_b, ple_w, gate_w, gate_b, tm=256, ff_chunk=1408):
    m, d = x2d.shape
    row = lambda i: (i, 0)
    return pl.pallas_call(
        functools.partial(_ffn_ple_kernel, ff_chunk=ff_chunk),
        grid=(m // tm,),
        in_specs=[pl.BlockSpec((tm, d), row), pl.BlockSpec((tm, p2d.shape[1]), row),
                  _resident(wg.shape), _resident(wu.shape), _resident(wd.shape),
                  _resident((1, d)), _resident((1, d)), _resident(ple_w.shape), _resident(gate_w.shape),
                  _resident((1, d))],
        out_specs=pl.BlockSpec((tm, d), row),
        out_shape=jax.ShapeDtypeStruct((m, d), F32),
        compiler_params=_cparams("parallel"),
        name="ffn_ple",
    )(x2d, p2d, wg, wu, wd, ln_g.reshape(1, d), ln_b.reshape(1, d), ple_w, gate_w, gate_b.reshape(1, d))


def _res_ln_ple_kernel(x_ref, f_ref, p_ref, lng_ref, lnb_ref, plew_ref, gw_ref, gb_ref, o_ref):
    xn = _layer_norm(DEEPNORM_ALPHA * x_ref[...] + f_ref[...], lng_ref[...], lnb_ref[...])
    o_ref[...] = _ple(xn, p_ref[...], plew_ref, gw_ref, gb_ref)


def _res_ln_ple(x2d, f2d, p2d, ln_g, ln_b, ple_w, gate_w, gate_b, tm=512):
    m, d = x2d.shape
    row = lambda i: (i, 0)
    return pl.pallas_call(
        _res_ln_ple_kernel,
        grid=(m // tm,),
        in_specs=[pl.BlockSpec((tm, d), row), pl.BlockSpec((tm, d), row),
                  pl.BlockSpec((tm, p2d.shape[1]), row), _resident((1, d)), _resident((1, d)),
                  _resident(ple_w.shape), _resident(gate_w.shape), _resident((1, d))],
        out_specs=pl.BlockSpec((tm, d), row),
        out_shape=jax.ShapeDtypeStruct((m, d), F32),
        compiler_params=_cparams("parallel"),
        name="res_ln_ple",
    )(x2d, f2d, p2d, ln_g.reshape(1, d), ln_b.reshape(1, d), ple_w, gate_w, gate_b.reshape(1, d))


def _swap_half_cols(w):
    k, n = w.shape
    return w.reshape(k, n // HEAD_DIM, 2, HEAD_DIM // 2)[:, :, ::-1, :].reshape(k, n)


def _rope_tables(seq):
    half = HEAD_DIM // 2
    inv = ROPE_THETA ** (-jnp.arange(half, dtype=F32) / half)
    ang = jnp.arange(seq, dtype=F32)[:, None] * inv[None, :]
    cos, sin = jnp.cos(ang), jnp.sin(ang)
    return jnp.concatenate([cos, cos], -1), jnp.concatenate([-sin, sin], -1)


def _nsa_kv_kernel(x_ref, wk_ref, wvt_ref, cos_ref, sin_ref, zc_ref, vc_ref, ks_ref, kw_ref, vst_ref,
                   vwt_ref):
    G, dh = N_KV_GROUPS, HEAD_DIM
    gw = G * dh
    xb = x_ref[...].astype(BF16)
    res = _dot(xb, wk_ref[...])
    res_t = _dot_nt(wvt_ref[...], xb)
    tm = res.shape[0]
    cos = cos_ref[...]
    sin = sin_ref[...]
    ks = res[:, 2 * gw:3 * gw] * cos + res[:, 3 * gw:4 * gw] * sin
    kw = res[:, 4 * gw:5 * gw] * cos + res[:, 5 * gw:6 * gw] * sin
    t_glob = pl.program_id(1) * tm + lax.broadcasted_iota(jnp.int32, (tm, dh), 0)
    onehot = jnp.where(lax.broadcasted_iota(jnp.int32, (tm, dh), 1) == t_glob // SEL_BLOCK, 1.0, 0.0)
    for g in range(G):
        sl = slice(g * dh, (g + 1) * dh)
        zc_ref[0, g] = res[:, sl].astype(BF16)
        vc_ref[0, g] = res[:, gw + g * dh:gw + (g + 1) * dh].astype(BF16)
        ks_ref[0, g] = jnp.concatenate([ks[:, sl], onehot], axis=1).astype(BF16)
        kw_ref[0, g] = kw[:, sl].astype(BF16)
    vst_ref[0] = res_t[:gw].astype(BF16)
    vwt_ref[0] = res_t[gw:].astype(BF16)


def _nsa_kv(x2d, w_kv, batch, seq, tm=256):
    m, d = x2d.shape
    G, dh = N_KV_GROUPS, HEAD_DIM
    gw = G * dh
    parts = [w_kv[:, j * gw:(j + 1) * gw] for j in range(6)]
    w_k = jnp.concatenate([parts[0], parts[1], parts[2], _swap_half_cols(parts[2]),
                           parts[4], _swap_half_cols(parts[4])], axis=1).astype(BF16)
    w_vt = jnp.concatenate([parts[3], parts[5]], axis=1).T.astype(BF16)
    cos, sin = _rope_tables(seq)
    cos = jnp.tile(cos, (1, G))
    sin = jnp.tile(sin, (1, G))
    nt = seq // tm
    o64 = jax.ShapeDtypeStruct((batch, G, seq, dh), BF16)
    o128 = jax.ShapeDtypeStruct((batch, G, seq, 2 * dh), BF16)
    ot = jax.ShapeDtypeStruct((batch, gw, seq), BF16)
    b64 = pl.BlockSpec((1, G, tm, dh), lambda b, t: (b, 0, t, 0))
    b128 = pl.BlockSpec((1, G, tm, 2 * dh), lambda b, t: (b, 0, t, 0))
    bt = pl.BlockSpec((1, gw, tm), lambda b, t: (b, 0, t))
    return pl.pallas_call(
        _nsa_kv_kernel,
        grid=(batch, nt),
        in_specs=[pl.BlockSpec((tm, d), lambda b, t: (b * nt + t, 0)), _resident(w_k.shape),
                  _resident(w_vt.shape),
                  pl.BlockSpec((tm, gw), lambda b, t: (t, 0)), pl.BlockSpec((tm, gw), lambda b, t: (t, 0))],
        out_specs=[b64, b64, b128, b64, bt, bt],
        out_shape=[o64, o64, o128, o64, ot, ot],
        compiler_params=_cparams("parallel", "parallel"),
        name="nsa_kv",
    )(x2d, w_k, w_vt, cos, sin)


def _nsa_cmp_kernel(z_ref, pos_ref, w1_ref, b1_ref, w2_ref, o_ref, *, slabs_per_seq):
    half = w1_ref.shape[1] // 2
    z = z_ref[0]
    tm = z.shape[0]
    first = _dot(z, w1_ref[0, :half, :])
    second = _dot(z, w1_ref[0, half:, :])
    const = _dot(pos_ref[0], w1_ref[0])[0:1, :] + b1_ref[0]
    hid = first + pltpu.roll(second, shift=tm - 1, axis=0) + const
    act = 0.5 * hid * (1.0 + jnp.tanh(math.sqrt(2.0 / math.pi) * (hid + 0.044715 * hid * hid * hid)))
    out = _dot(act.astype(BF16), w2_ref[0])
    row = lax.broadcasted_iota(jnp.int32, out.shape, 0)
    o_ref[0] = jnp.where(row % slabs_per_seq == slabs_per_seq - 1, 0.0, out).astype(BF16)


def _nsa_compress(zc, vc, cmp_pos, phi_w1, phi_b1, phi_w2, tm=512):
    batch, G, seq, dh = zc.shape
    slab = CMP_STRIDE * dh
    rows = batch * G * seq // CMP_STRIDE
    tm = min(tm, rows)
    z = jnp.stack([zc.reshape(rows, slab), vc.reshape(rows, slab)])
    pos = jnp.broadcast_to(cmp_pos.reshape(2, 1, CMP_LEN * dh), (2, 8, CMP_LEN * dh)).astype(BF16)
    out = pl.pallas_call(
        functools.partial(_nsa_cmp_kernel, slabs_per_seq=seq // CMP_STRIDE),
        grid=(2, rows // tm),
        in_specs=[pl.BlockSpec((1, tm, slab), lambda j, i: (j, i, 0)),
                  pl.BlockSpec((1, 8, CMP_LEN * dh), lambda j, i: (j, 0, 0)),
                  pl.BlockSpec((1, CMP_LEN * dh, CMP_HIDDEN), lambda j, i: (j, 0, 0)),
                  pl.BlockSpec((1, 1, CMP_HIDDEN), lambda j, i: (j, 0, 0)),
                  pl.BlockSpec((1, CMP_HIDDEN, dh), lambda j, i: (j, 0, 0))],
        out_specs=pl.BlockSpec((1, tm, dh), lambda j, i: (j, i, 0)),
        out_shape=jax.ShapeDtypeStruct((2, rows, dh), BF16),
        compiler_params=_cparams("parallel", "parallel"),
        name="nsa_compress",
    )(z, pos, phi_w1.astype(BF16), phi_b1.reshape(2, 1, CMP_HIDDEN), phi_w2.astype(BF16))
    n_slab = seq // CMP_STRIDE
    return out[0].reshape(batch, G, n_slab, dh), out[1].reshape(batch, G, n_slab, dh)


GATE_ROWS = 16


def _nsa_q_kernel(x_ref, wt_ref, cos_ref, sin_ref, q_ref, qr_ref, gate_ref):
    dh = HEAD_DIM
    hw = N_HEADS * dh
    scale = HEAD_DIM ** -0.5
    res_t = _dot_nt(wt_ref[...], x_ref[...].astype(BF16))
    tm = res_t.shape[1]
    q = res_t[:hw].reshape(N_HEADS, dh, tm)
    q_sw = res_t[hw:2 * hw].reshape(N_HEADS, dh, tm)
    qr = q * cos_ref[...][None] + q_sw * sin_ref[...][None]
    q_ref[0] = (q * scale).reshape(hw, tm).astype(BF16)
    qr_ref[0] = (qr * scale).reshape(hw, tm).astype(BF16)
    gate_ref[0] = _sigmoid(res_t[2 * hw:])


def _nsa_q(x2d, w_in, batch, seq, tm=256):
    m, d = x2d.shape
    G, R, dh = N_KV_GROUPS, Q_PER_GROUP, HEAD_DIM
    hw = N_HEADS * dh
    w_gate = w_in[:, hw:].reshape(d, G, R * N_BRANCH)
    w_gate = jnp.pad(w_gate, ((0, 0), (0, 0), (0, GATE_ROWS - R * N_BRANCH))).reshape(d, G * GATE_ROWS)
    wt = jnp.concatenate([w_in[:, :hw], _swap_half_cols(w_in[:, :hw]), w_gate], axis=1).T.astype(BF16)
    cos, sin = _rope_tables(seq)
    nt = seq // tm
    oq = jax.ShapeDtypeStruct((batch, hw, seq), BF16)
    bq = pl.BlockSpec((1, hw, tm), lambda b, t: (b, 0, t))
    tab = pl.BlockSpec((dh, tm), lambda b, t: (0, t))
    return pl.pallas_call(
        _nsa_q_kernel,
        grid=(batch, nt),
        in_specs=[pl.BlockSpec((tm, d), lambda b, t: (b * nt + t, 0)), _resident(wt.shape), tab, tab],
        out_specs=[bq, bq, pl.BlockSpec((1, G * GATE_ROWS, tm), lambda b, t: (b, 0, t))],
        out_shape=[oq, oq, jax.ShapeDtypeStruct((batch, G * GATE_ROWS, seq), F32)],
        compiler_params=_cparams("parallel", "parallel"),
        name="nsa_q",
    )(x2d, wt, cos.T, sin.T)


def _nsa_attn_kernel(q_ref, qr_ref, gate_ref, kc_ref, vct_ref, ks_ref, vst_ref, kw_ref, vwt_ref, ovlt_ref,
                     o_ref, *, tq, tk, n_blk):
    R, dh = Q_PER_GROUP, HEAD_DIM
    cols = R * tq
    qi = pl.program_id(2)
    t0 = qi * tq
    n_blk_pad = ovlt_ref.shape[0]

    def heads_to_lanes(x):
        return jnp.concatenate([x[r * dh:(r + 1) * dh] for r in range(R)], axis=1)

    q = heads_to_lanes(q_ref[0])
    qr = heads_to_lanes(qr_ref[0])

    def col_pos(shape):
        return t0 + (lax.broadcasted_iota(jnp.int32, shape, 1) & (tq - 1))

    def key_tile(j):
        return pl.ds(pl.multiple_of(j * tk, tk), tk)

    ct = min(cols, 256)
    col_tiles = [slice(c, c + ct) for c in range(0, cols, ct)]

    def tile_scores(k_ref, q_op, tiles):
        return [[_dot(k_ref[0, 0, key_tile(j), :], q_op[:, cs]) for j, _ in tiles] for cs in col_tiles]

    def attend(carry, sc, vt_ref, tiles):
        m_i, l_i, acc = carry
        outs = []
        for cs, sc_c in zip(col_tiles, sc):
            shape = sc_c[0].shape
            t_pos = t0 + ((cs.start + lax.broadcasted_iota(jnp.int32, shape, 1)) & (tq - 1))
            masked = []
            for s_j, (j, mask_fn) in zip(sc_c, tiles):
                if mask_fn is not None:
                    k_pos = j * tk + lax.broadcasted_iota(jnp.int32, shape, 0)
                    s_j = jnp.where(mask_fn(k_pos, t_pos), s_j, NEG_INF)
                masked.append(s_j)
            m_new = m_i[:, cs]
            for s_j in masked:
                m_new = jnp.maximum(m_new, jnp.max(s_j, axis=0, keepdims=True))
            alpha = jnp.exp(m_i[:, cs] - m_new)
            l_new = alpha * l_i[:, cs]
            acc_new = alpha * acc[:, cs]
            for s_j, (j, _) in zip(masked, tiles):
                p = jnp.exp(s_j - m_new)
                l_new = l_new + jnp.sum(p, axis=0, keepdims=True)
                acc_new = acc_new + _dot(vt_ref[0, :, key_tile(j)], p.astype(BF16))
            outs.append((m_new, l_new, acc_new))
        return tuple(jnp.concatenate([o[i] for o in outs], axis=1) for i in range(3))

    init = (jnp.full((1, cols), NEG_INF, F32), jnp.zeros((1, cols), F32), jnp.zeros((dh, cols), F32))

    causal = lambda k_pos, t_pos: k_pos <= t_pos
    win_tiles = [(jnp.maximum(qi - 2, 0), lambda k_pos, t_pos: (t_pos - k_pos < WINDOW) & (qi >= 2)),
                 (jnp.maximum(qi - 1, 0), lambda k_pos, t_pos: (k_pos <= t_pos) & (qi >= 1)),
                 (qi, causal)]
    sc_win = tile_scores(kw_ref, qr, win_tiles)

    s = _dot(kc_ref[0, 0], q)
    cmp_end = lax.broadcasted_iota(jnp.int32, s.shape, 0) * CMP_STRIDE + (CMP_LEN - 1)
    valid = cmp_end <= col_pos(s.shape)
    sm = jnp.where(valid, s, NEG_INF)
    e = jnp.where(valid, jnp.exp(sm - jnp.max(sm, axis=0, keepdims=True)), 0.0)
    l = jnp.sum(e, axis=0, keepdims=True)
    p_cmp = e / jnp.where(l > 0.0, l, 1.0)
    o_cmp = _dot(vct_ref[0, 0], p_cmp.astype(BF16))

    p_sum = p_cmp[:, 0:tq]
    for r in range(1, R):
        p_sum = p_sum + p_cmp[:, r * tq:(r + 1) * tq]
    hi, lo = _split2(p_sum)
    imp = _dot(ovlt_ref[...], hi) + _dot(ovlt_ref[...], lo)
    blk = lax.broadcasted_iota(jnp.int32, imp.shape, 0)
    t_q = t0 + lax.broadcasted_iota(jnp.int32, imp.shape, 1)
    cur = t_q // SEL_BLOCK
    forced = (blk == 0) | (blk == cur) | (blk == cur - 1)
    score = jnp.where(forced, FORCE_SCORE, jnp.where(blk * SEL_BLOCK <= t_q, imp, NEG_INF))
    score = jnp.where(blk < n_blk, score, -jnp.inf)
    selected = blk >= n_blk
    for _ in range(N_SEL):
        best = jnp.max(score, axis=0, keepdims=True)
        first = jnp.min(jnp.where(score == best, blk, 2 * LANES), axis=0, keepdims=True)
        pick = blk == first
        selected = selected | pick
        score = jnp.where(pick, -jnp.inf, score)
    bias = jnp.where(selected, 0.0, NEG_INF)
    qa = jnp.concatenate([qr, jnp.concatenate([bias] * R, axis=1).astype(BF16),
                          jnp.zeros((dh - n_blk_pad, cols), BF16)], axis=0)

    _, l_w, acc_w = attend(init, sc_win, vwt_ref, win_tiles)
    o_win = acc_w / l_w

    n_pairs = qi // 2

    def pair_body(i, carry):
        tiles = [(2 * i, None), (2 * i + 1, None)]
        return attend(carry, tile_scores(ks_ref, qa, tiles), vst_ref, tiles)

    carry = lax.fori_loop(0, n_pairs, pair_body, init)
    j_odd = 2 * n_pairs
    tail_tiles = [(j_odd, lambda k_pos, t_pos: (k_pos <= t_pos) & (j_odd < qi)), (qi, causal)]
    _, l_s, acc_s = attend(carry, tile_scores(ks_ref, qa, tail_tiles), vst_ref, tail_tiles)
    o_slc = acc_s / l_s

    gates = gate_ref[0]
    outs = []
    for r in range(R):
        cs = slice(r * tq, (r + 1) * tq)
        outs.append(gates[3 * r:3 * r + 1] * o_cmp[:, cs] + gates[3 * r + 1:3 * r + 2] * o_slc[:, cs]
                    + gates[3 * r + 2:3 * r + 3] * o_win[:, cs])
    halves = [jnp.transpose(jnp.concatenate(outs[i:i + 2], axis=0)) for i in range(0, R, 2)]
    o_ref[0] = jnp.concatenate(halves, axis=1)


SEL_ROWS = 32


def _nsa_attention(q, qr, gates, k_cmp, v_cmp, ks, vst, kw, vwt, tq=256, tk=256):
    batch, _, seq = q.shape
    G, R, dh = N_KV_GROUPS, Q_PER_GROUP, HEAD_DIM
    n_slab = k_cmp.shape[2]
    n_blk = seq // SEL_BLOCK
    assert n_blk <= SEL_ROWS
    assert tq == tk and WINDOW == 2 * tk, "the window branch is written as exactly three key tiles"
    cmp_start = jnp.arange(n_slab) * CMP_STRIDE
    blk_start = jnp.arange(SEL_ROWS) * SEL_BLOCK
    overlap_t = ((cmp_start[None, :] < blk_start[:, None] + SEL_BLOCK)
                 & (cmp_start[None, :] + CMP_LEN - 1 >= blk_start[:, None])
                 & (jnp.arange(SEL_ROWS)[:, None] < n_blk)
                 & (jnp.arange(n_slab)[None, :] < (seq - CMP_LEN) // CMP_STRIDE + 1)).astype(BF16)
    v_cmp_t = jnp.swapaxes(v_cmp, 2, 3)
    qspec = pl.BlockSpec((1, R * dh, tq), lambda b, g, i: (b, g, i))
    full = lambda a: pl.BlockSpec((1, 1) + a.shape[2:], lambda b, g, i: (b, g, 0, 0))
    vspec = pl.BlockSpec((1, dh, seq), lambda b, g, i: (b, g, 0))
    return pl.pallas_call(
        functools.partial(_nsa_attn_kernel, tq=tq, tk=tk, n_blk=n_blk),
        grid=(batch, G, seq // tq),
        in_specs=[qspec, qspec, pl.BlockSpec((1, GATE_ROWS, tq), lambda b, g, i: (b, g, i)),
                  full(k_cmp), full(v_cmp_t), full(ks), vspec, full(kw), vspec,
                  _resident(overlap_t.shape)],
        out_specs=pl.BlockSpec((1, tq, R * dh), lambda b, g, i: (b, i, g)),
        out_shape=jax.ShapeDtypeStruct((batch, seq, G * R * dh), F32),
        compiler_params=_cparams("parallel", "parallel", "parallel"),
        name="nsa_attn",
    )(q, qr, gates, k_cmp, v_cmp_t, ks, vst, kw, vwt, overlap_t)


def _moe_kernel(be_ref, nb_ref, x_ref, wg_ref, wu_ref, wd_ref, o_ref):
    i = pl.program_id(0)

    @pl.when(i < nb_ref[0])
    def _():
        xb = x_ref[...].astype(BF16)
        gt = _dot(xb, wg_ref[0])
        up = _dot(xb, wu_ref[0])
        h = (gt * _sigmoid(gt) * up).astype(BF16)
        o_ref[...] = _dot(h, wd_ref[0])

    @pl.when(i >= nb_ref[0])
    def _():
        o_ref[...] = jnp.zeros_like(o_ref)


def _moe_experts(xs, block_e, n_used, wg, wu, wd):
    n_rows, d = xs.shape
    n_blocks = n_rows // MOE_ROWS
    ff = wg.shape[2]
    grid_spec = pltpu.PrefetchScalarGridSpec(
        num_scalar_prefetch=2,
        grid=(n_blocks,),
        in_specs=[pl.BlockSpec((MOE_ROWS, d), lambda i, be, nb: (i, 0)),
                  pl.BlockSpec((1, d, ff), lambda i, be, nb: (be[i], 0, 0)),
                  pl.BlockSpec((1, d, ff), lambda i, be, nb: (be[i], 0, 0)),
                  pl.BlockSpec((1, ff, d), lambda i, be, nb: (be[i], 0, 0))],
        out_specs=pl.BlockSpec((MOE_ROWS, d), lambda i, be, nb: (i, 0)),
    )
    return pl.pallas_call(
        _moe_kernel,
        grid_spec=grid_spec,
        out_shape=jax.ShapeDtypeStruct((n_rows, d), F32),
        compiler_params=_cparams("arbitrary"),
        name="moe_experts",
    )(block_e, n_used, xs, wg, wu, wd)


def _moe(x2d, logits, w_gu, w_down):
    n_tok, d = x2d.shape
    n_assign = n_tok * TOP_K
    top_logit, top_e = lax.top_k(logits, TOP_K)
    weights = jax.nn.softmax(top_logit, axis=-1)
    flat_e = top_e.reshape(-1).astype(jnp.int32)
    onehot = (flat_e[:, None] == jnp.arange(N_EXPERTS, dtype=jnp.int32)[None, :]).astype(jnp.int32)
    running = jnp.cumsum(onehot, axis=0)
    counts = running[-1]
    padded = (counts + MOE_ROWS - 1) // MOE_ROWS * MOE_ROWS
    pad_end = jnp.cumsum(padded)
    pad_start = pad_end - padded
    grp_start = jnp.cumsum(counts) - counts
    pos = jnp.sum(onehot * (running - 1 + pad_start[None, :]), axis=1).reshape(n_tok, TOP_K)
    n_blocks = -(-n_assign // MOE_ROWS) + N_EXPERTS
    block_e = jnp.minimum(jnp.searchsorted(pad_end, jnp.arange(n_blocks) * MOE_ROWS, side='right'),
                          N_EXPERTS - 1).astype(jnp.int32)
    n_used = (pad_end[-1] // MOE_ROWS).astype(jnp.int32).reshape(1)
    order = jnp.argsort(flat_e)
    tok_sorted = (order // TOP_K).astype(jnp.int32)
    row_in_grp = (jnp.arange(n_blocks, dtype=jnp.int32) * MOE_ROWS - pad_start[block_e])[:, None] \
        + jnp.arange(MOE_ROWS, dtype=jnp.int32)[None, :]
    src = jnp.clip(grp_start[block_e][:, None] + row_in_grp, 0, n_assign - 1)
    row_tok = jnp.where(row_in_grp < counts[block_e][:, None], tok_sorted[src], 0).reshape(-1)
    xs = x2d[row_tok]
    ys = _moe_experts(xs, block_e, n_used, w_gu[:, :, :D_FF_EXPERT].astype(BF16),
                      w_gu[:, :, D_FF_EXPERT:].astype(BF16), w_down.astype(BF16))
    out = ys[pos[:, 0]] * weights[:, 0:1]
    for s in range(1, TOP_K):
        out = out + ys[pos[:, s]] * weights[:, s:s + 1]
    return out


def kernel(x, p, a_mu, a_w_rkv, a_w0, a_w1, a_w2, a_a0, a_a1, a_a2, a_g1, a_g2, a_k_k, a_k_a, a_r_k,
           a_gn_g, a_gn_b, a_w_o, b_w_kv, b_cmp_pos, b_phi_w1, b_phi_b1, b_phi_w2, b_w_in, b_w_o,
           f_w_gu, f_w_down, m_w_router, m_b_router, m_w_gu, m_w_down, ln_g, ln_b, ple_w, ple_gate_w,
           ple_gate_b):
    batch, seq, d = x.shape
    m = batch * seq
    x0 = x.reshape(m, d)
    p2d = p.reshape(DEPTH, m, PLE_DIM)
    bf = lambda w: w.astype(BF16)

    x1 = _rwkv_layer(x0, batch, seq, a_mu[0], bf(a_w_rkv[0]), a_w0[0], bf(a_w1[0]), bf(a_w2[0]), a_a0[0],
                     bf(a_a1[0]), bf(a_a2[0]), bf(a_g1[0]), bf(a_g2[0]), a_k_k[0], a_k_a[0], a_r_k[0],
                     a_gn_g[0], a_gn_b[0], bf(a_w_o[0]), ln_g[0, 0], ln_b[0, 0])
    x2 = _ffn_ple(x1, p2d[0], bf(f_w_gu[0][:, :D_FF]), bf(f_w_gu[0][:, D_FF:]), bf(f_w_down[0]),
                  ln_g[0, 1], ln_b[0, 1], bf(ple_w[0]), bf(ple_gate_w[0]), ple_gate_b[0])

    zc, vc, ks, kw, vst, vwt = _nsa_kv(x2, b_w_kv, batch, seq)
    k_cmp, v_cmp = _nsa_compress(zc, vc, b_cmp_pos, b_phi_w1, b_phi_b1, b_phi_w2)
    q, qr, gates = _nsa_q(x2, b_w_in[0], batch, seq)
    attn = _nsa_attention(q, qr, gates, k_cmp, v_cmp, ks, vst, kw, vwt).reshape(m, d)
    x3, logits = _proj_ln_router(attn, x2, bf(b_w_o[0]), ln_g[1, 0], ln_b[1, 0], m_w_router[0],
                                 m_b_router[0])
    ffn = _moe(x3, logits[:, :N_EXPERTS], m_w_gu[0], m_w_down[0])
    out = _res_ln_ple(x3, ffn, p2d[1], ln_g[1, 1], ln_b[1, 1], bf(ple_w[1]), bf(ple_gate_w[1]),
                      ple_gate_b[1])
    return out.reshape(batch, seq, d)
```

```python
import functools
import math

import jax
import jax.numpy as jnp
from jax import lax
from jax.experimental import pallas as pl
from jax.experimental.pallas import tpu as pltpu

BF16 = jnp.bfloat16
F32 = jnp.float32

LANES = 128
VMEM_LIMIT_BYTES = 56 * 1024 * 1024

D_MODEL = 1024
PLE_DIM = 256
RWKV_HEAD = 64
GN_EPS = 64e-5
N_HEADS = 16
HEAD_DIM = 64
N_KV_GROUPS = 4
Q_PER_GROUP = 4
N_BRANCH = 3
CMP_LEN = 32
CMP_STRIDE = 16
CMP_HIDDEN = 256
SEL_BLOCK = 64
N_SEL = 8
WINDOW = 512
ROPE_THETA = 10000.0
D_FF = 2816
N_EXPERTS = 8
TOP_K = 2
D_FF_EXPERT = 1408
MOE_ROWS = 256
LN_EPS = 1e-5
DEPTH = 2
DEEPNORM_ALPHA = (2.0 * DEPTH) ** 0.25
NEG_INF = -1e30
FORCE_SCORE = 1e4

SCAN_CHUNK = 64


def _cparams(*sem):
    return pltpu.CompilerParams(dimension_semantics=sem, vmem_limit_bytes=VMEM_LIMIT_BYTES)


def _resident(shape):
    nd = len(shape)
    return pl.BlockSpec(shape, lambda *_: (0,) * nd, pipeline_mode=pl.Buffered(1))


def _dot(a, b):
    return jnp.dot(a, b, preferred_element_type=F32)


def _dot_nt(a, b):
    return lax.dot_general(a, b, (((1,), (1,)), ((), ())), preferred_element_type=F32)


def _split2(x):
    hi = x.astype(BF16)
    lo = (x - hi.astype(F32)).astype(BF16)
    return hi, lo


def _split3(x):
    hi = x.astype(BF16)
    r1 = x - hi.astype(F32)
    mid = r1.astype(BF16)
    lo = (r1 - mid.astype(F32)).astype(BF16)
    return hi, mid, lo


def _layer_norm(y, g, b):
    mu = jnp.mean(y, axis=-1, keepdims=True)
    yc = y - mu
    var = jnp.mean(yc * yc, axis=-1, keepdims=True)
    return yc * lax.rsqrt(var + LN_EPS) * g + b


def _sigmoid(z):
    return 1.0 / (1.0 + jnp.exp(-z))


def _rwkv_project(first, x_ref, xh_ref, mu_ref, wrkv_ref, w0_ref, w1_ref, w2_ref, a0_ref, a1_ref,
                  a2_ref, g1_ref, g2_ref, r_ref, k_ref, v_ref, dl_ref, a_ref, g_ref):
    x = x_ref[...]
    prev_row = jnp.where(first, 0.0, xh_ref[0, 7:8, :])
    row = lax.broadcasted_iota(jnp.int32, x.shape, 0)
    x_shift = jnp.where(row == 0, prev_row, pltpu.roll(x, shift=1, axis=0))
    xx = x_shift - x

    def mix(i):
        return (x + xx * mu_ref[i:i + 1, :]).astype(BF16)

    r_ref[...] = _dot(mix(0), wrkv_ref[0])
    k_ref[...] = _dot(mix(1), wrkv_ref[1])
    v_ref[...] = _dot(mix(2), wrkv_ref[2])
    z = w0_ref[...] + _dot(jnp.tanh(_dot(mix(3), w1_ref[...])).astype(BF16), w2_ref[...])
    dl_ref[...] = -math.exp(-0.5) * _sigmoid(z)
    a_ref[...] = _sigmoid(a0_ref[...] + _dot(_dot(mix(4), a1_ref[...]).astype(BF16), a2_ref[...]))
    g_ref[...] = _dot(_sigmoid(_dot(mix(5), g1_ref[...])).astype(BF16), g2_ref[...])


def _rwkv_recurrence(first, r_ref, k_ref, v_ref, dl_ref, a_ref, g_ref, kk_ref, ka_ref, rk_ref, gng_ref,
                     gnb_ref, o_ref, state_ref, kkn_ref, cum_ref, *, n_pairs, n_chunks):
    C = SCAN_CHUNK
    N = RWKV_HEAD
    W = 2 * N
    tb = n_chunks * C

    @pl.when(first)
    def _():
        state_ref[...] = jnp.zeros_like(state_ref)

    tt = lax.broadcasted_iota(jnp.int32, (C, W), 0)
    ln = lax.broadcasted_iota(jnp.int32, (C, W), 1)
    ss = ln & (N - 1)
    strict_lower = ss < tt
    lower = ss <= tt
    eye = jnp.where(ss == tt, 1.0, 0.0)
    level_masks = []
    for lg in range(int(math.log2(C))):
        level_masks.append(((tt >> (lg + 1)) == (ss >> (lg + 1))) & ((tt >> lg) == (ss >> lg) + 1))
    rr = lax.broadcasted_iota(jnp.int32, (W, W), 0)
    cc = lax.broadcasted_iota(jnp.int32, (W, W), 1)
    same_head = (rr < N) == (cc < N)
    ones_bd = jnp.where(same_head, 1.0, 0.0).astype(BF16)
    lane_head0 = ln < N

    def bd(y):
        return jnp.concatenate([jnp.where(lane_head0, y, 0.0), jnp.where(lane_head0, 0.0, y)],
                               axis=0).astype(BF16)

    def pmm(xp, ybd):
        return _dot(xp.astype(BF16), ybd)

    def head_sum(xp):
        hi, lo = _split2(xp)
        return _dot(hi, ones_bd) + _dot(lo, ones_bd)

    tri_r = lax.broadcasted_iota(jnp.int32, (tb, tb), 0)
    tri_c = lax.broadcasted_iota(jnp.int32, (tb, tb), 1)
    lg_c = int(math.log2(C))
    tri_chunk = jnp.where((tri_c <= tri_r) & ((tri_c >> lg_c) == (tri_r >> lg_c)), 1.0, 0.0).astype(BF16)
    d_hi, d_mid, d_lo = _split3(dl_ref[...])
    cum_ref[...] = _dot(tri_chunk, d_hi) + _dot(tri_chunk, d_mid) + _dot(tri_chunk, d_lo)
    for pi in range(n_pairs):
        cols = slice(pi * W, (pi + 1) * W)
        kk_raw = k_ref[:, cols] * kk_ref[:, cols]
        kkn_ref[:, cols] = kk_raw / jnp.maximum(jnp.sqrt(head_sum(kk_raw * kk_raw)), 1e-12)

    def chunk_load(ci, pi):
        rows = pl.ds(pl.multiple_of(ci * C, C), C)
        cols = slice(pi * W, (pi + 1) * W)
        return (r_ref[rows, cols], k_ref[rows, cols], v_ref[rows, cols], dl_ref[rows, cols],
                a_ref[rows, cols], kkn_ref[rows, cols], cum_ref[rows, cols], state_ref[pi])

    def chunk_compute(pi, r, k, v, dl, a, kk, cum, m0):
        cols = slice(pi * W, (pi + 1) * W)
        k2 = k * (1.0 + (a - 1.0) * ka_ref[:, cols])
        b = a * kk
        p_incl = jnp.exp(cum)
        p_excl = jnp.exp(cum - dl)
        p_inv = jnp.exp(-cum)
        kq = kk * p_excl
        rq = r * p_incl
        bk = b * p_inv
        kkd = k2 * p_inv

        bkt = jnp.transpose(jnp.concatenate([b, k2], axis=0))
        cumt = jnp.transpose(jnp.concatenate([cum, cum], axis=0))
        last_col = cumt[:, C - 1:C]
        lhs_t = (bkt * jnp.exp(last_col - cumt)).astype(BF16)
        m_decayed = m0 * jnp.exp(last_col)

        lhs = jnp.concatenate([kq, rq], axis=0).astype(BF16)
        yield
        ab = _dot_nt(lhs, bd(bk))
        gb = _dot_nt(lhs, bd(kkd))
        a_m = jnp.where(strict_lower, ab[:C], 0.0)
        aq_m = jnp.where(lower, ab[C:], 0.0)
        g_m = jnp.where(strict_lower, gb[:C], 0.0)
        gq_m = jnp.where(lower, gb[C:], 0.0)

        tinv = eye - jnp.where(level_masks[0], a_m, 0.0)
        yield
        gv = pmm(jnp.concatenate([g_m, gq_m], axis=0), bd(v))
        for lm in level_masks[1:]:
            w_ = pmm(jnp.where(lm, a_m, 0.0), bd(tinv))
            yield
            tinv = tinv - pmm(tinv, bd(w_))
            yield

        kq_y = pmm(tinv, jnp.concatenate([bd(kq), bd(gv[:C])], axis=1))
        kq1 = kq_y[:, :W]
        y = kq_y[:, W:]
        yield
        aq_ky = pmm(aq_m, jnp.concatenate([bd(kq1), bd(y)], axis=1))
        rq1 = rq - aq_ky[:, :W]
        o_loc = gv[C:] - aq_ky[:, W:]
        yield
        st = _dot(jnp.concatenate([rq1, kq1], axis=0).astype(BF16), m0.astype(BF16))
        o = st[:C] + o_loc
        z = -(st[C:] + y)
        yield
        upd = _dot(lhs_t, jnp.concatenate([z, v], axis=0).astype(BF16))
        return o, m_decayed + jnp.where(same_head, upd, 0.0)

    def interleave(gens):
        results = [None] * len(gens)
        live = list(range(len(gens)))
        while live:
            for i in list(live):
                try:
                    next(gens[i])
                except StopIteration as done:
                    results[i] = done.value
                    live.remove(i)
        return results

    def body(ci, carry):
        loaded = [chunk_load(ci, pi) for pi in range(n_pairs)]
        results = interleave([chunk_compute(pi, *loaded[pi]) for pi in range(n_pairs)])
        rows = pl.ds(pl.multiple_of(ci * C, C), C)
        for pi, (o, m_new) in enumerate(results):
            o_ref[rows, pi * W:(pi + 1) * W] = o
            state_ref[pi] = m_new
        return carry

    lax.fori_loop(0, n_chunks, body, 0)

    for pi in range(n_pairs):
        cols = slice(pi * W, (pi + 1) * W)
        o = o_ref[:, cols]
        r = r_ref[:, cols]
        v = v_ref[:, cols]
        k2 = k_ref[:, cols] * (1.0 + (a_ref[:, cols] - 1.0) * ka_ref[:, cols])
        oc = o - head_sum(o) * (1.0 / N)
        var_o = head_sum(oc * oc) * (1.0 / N)
        out = oc * lax.rsqrt(var_o + GN_EPS) * gng_ref[:, cols] + gnb_ref[:, cols]
        out = out + head_sum(r * k2 * rk_ref[:, cols]) * v
        o_ref[:, cols] = out * g_ref[:, cols]


def _rwkv_layer_kernel(x_ref, xh_ref, mu_ref, wrkv_ref, w0_ref, w1_ref, w2_ref, a0_ref, a1_ref, a2_ref,
                       g1_ref, g2_ref, kk_ref, ka_ref, rk_ref, gng_ref, gnb_ref, wo_ref, lng_ref, lnb_ref,
                       o_ref, state_ref, r_s, k_s, v_s, dl_s, a_s, g_s, kkn_s, cum_s, y_s, *, n_chunks):
    first = pl.program_id(1) == 0
    _rwkv_project(first, x_ref, xh_ref, mu_ref, wrkv_ref, w0_ref, w1_ref, w2_ref, a0_ref, a1_ref, a2_ref,
                  g1_ref, g2_ref, r_s, k_s, v_s, dl_s, a_s, g_s)
    _rwkv_recurrence(first, r_s, k_s, v_s, dl_s, a_s, g_s, kk_ref, ka_ref, rk_ref, gng_ref, gnb_ref, y_s,
                     state_ref, kkn_s, cum_s, n_pairs=x_ref.shape[1] // (2 * RWKV_HEAD), n_chunks=n_chunks)
    mix = _dot(y_s[...].astype(BF16), wo_ref[...])
    o_ref[...] = _layer_norm(DEEPNORM_ALPHA * x_ref[...] + mix, lng_ref[...], lnb_ref[...])


def _rwkv_layer(x2d, batch, seq, mu, w_rkv, w0, w1, w2, a0, a1, a2, g1, g2, k_k, k_a, r_k, gn_g, gn_b,
                w_o, ln_g, ln_b, tb=256):
    m, d = x2d.shape
    nt = seq // tb
    xh = x2d.reshape(m // 8, 8, d)
    vec = lambda z: z.reshape(1, d)
    blk = pl.BlockSpec((tb, d), lambda b, t: (b * nt + t, 0))
    weights = [mu, w_rkv, vec(w0), w1, w2, vec(a0), a1, a2, g1, g2, vec(k_k), vec(k_a), vec(r_k),
               vec(gn_g), vec(gn_b), w_o, vec(ln_g), vec(ln_b)]
    n_pairs = d // (2 * RWKV_HEAD)
    return pl.pallas_call(
        functools.partial(_rwkv_layer_kernel, n_chunks=tb // SCAN_CHUNK),
        grid=(batch, nt),
        in_specs=[blk, pl.BlockSpec((1, 8, d), lambda b, t: (jnp.maximum((b * nt + t) * (tb // 8) - 1, 0), 0, 0))]
        + [_resident(w.shape) for w in weights],
        out_specs=blk,
        out_shape=jax.ShapeDtypeStruct((m, d), F32),
        scratch_shapes=[pltpu.VMEM((n_pairs, 2 * RWKV_HEAD, 2 * RWKV_HEAD), F32)]
        + [pltpu.VMEM((tb, d), F32)] * 9,
        compiler_params=_cparams("parallel", "arbitrary"),
        name="rwkv_layer",
    )(x2d, xh, *weights)


def _proj_ln_router_kernel(y_ref, x_ref, w_ref, lng_ref, lnb_ref, wr_ref, br_ref, o_ref, lg_ref):
    mix = _dot(y_ref[...].astype(BF16), w_ref[...])
    xn = _layer_norm(DEEPNORM_ALPHA * x_ref[...] + mix, lng_ref[...], lnb_ref[...])
    o_ref[...] = xn
    xh, xl = _split2(xn)
    lg_ref[...] = _dot(xh, wr_ref[0]) + _dot(xh, wr_ref[1]) + _dot(xl, wr_ref[0]) + br_ref[...]


def _proj_ln_router(y, x2d, w, ln_g, ln_b, w_router, b_router, tm=512):
    m, d = x2d.shape
    row = lambda i: (i, 0)
    wr = jnp.zeros((d, LANES), F32).at[:, :N_EXPERTS].set(w_router)
    wr3 = jnp.stack(_split2(wr))
    br =jnp.zeros((1, LANES), F32).at[0, :N_EXPERTS].set(b_router)
    return pl.pallas_call(
        _proj_ln_router_kernel,
        grid=(m // tm,),
        in_specs=[pl.BlockSpec((tm, y.shape[1]), row), pl.BlockSpec((tm, d), row), _resident(w.shape),
                  _resident((1, d)), _resident((1, d)), _resident(wr3.shape), _resident(br.shape)],
        out_specs=[pl.BlockSpec((tm, d), row), pl.BlockSpec((tm, LANES), row)],
        out_shape=[jax.ShapeDtypeStruct((m, d), F32), jax.ShapeDtypeStruct((m, LANES), F32)],
        compiler_params=_cparams("parallel"),
        name="proj_ln_router",
    )(y, x2d, w, ln_g.reshape(1, d), ln_b.reshape(1, d), wr3, br)


def _ple(xn, p, plew_ref, gw_ref, gb_ref):
    gate = _sigmoid(_dot(xn.astype(BF16), gw_ref[...]) + gb_ref[...])
    return xn + _dot(p.astype(BF16), plew_ref[...]) * gate


def _ffn_ple_kernel(x_ref, p_ref, wg_ref, wu_ref, wd_ref, lng_ref, lnb_ref, plew_ref, gw_ref, gb_ref,
                    o_ref, *, ff_chunk):
    x = x_ref[...]
    xb = x.astype(BF16)
    acc = jnp.zeros_like(x)
    for c in range(wg_ref.shape[1] // ff_chunk):
        sl = slice(c * ff_chunk, (c + 1) * ff_chunk)
        gt = _dot(xb, wg_ref[:, sl])
        up = _dot(xb, wu_ref[:, sl])
        h = (gt * _sigmoid(gt) * up).astype(BF16)
        acc = acc + _dot(h, wd_ref[sl, :])
    xn = _layer_norm(DEEPNORM_ALPHA * x + acc, lng_ref[...], lnb_ref[...])
    o_ref[...] = _ple(xn, p_ref[0], plew_ref, gw_ref, gb_ref)


def _ple_block(p3d, layer, tm):
    return pl.BlockSpec((1, tm, p3d.shape[2]), lambda i: (layer, i, 0))


def _ffn_ple(x2d, p3d, layer, wg, wu, wd, ln_g, ln_b, ple_w, gate_w, gate_b, tm=512, ff_chunk=1408):
    m, d = x2d.shape
    row = lambda i: (i, 0)
    return pl.pallas_call(
        functools.partial(_ffn_ple_kernel, ff_chunk=ff_chunk),
        grid=(m // tm,),
        in_specs=[pl.BlockSpec((tm, d), row), _ple_block(p3d, layer, tm),
                  _resident(wg.shape), _resident(wu.shape), _resident(wd.shape),
                  _resident((1, d)), _resident((1, d)), _resident(ple_w.shape), _resident(gate_w.shape),
                  _resident((1, d))],
        out_specs=pl.BlockSpec((tm, d), row),
        out_shape=jax.ShapeDtypeStruct((m, d), F32),
        compiler_params=_cparams("parallel"),
        name="ffn_ple",
    )(x2d, p3d, wg, wu, wd, ln_g.reshape(1, d), ln_b.reshape(1, d), ple_w, gate_w, gate_b.reshape(1, d))


def _moe_combine_ln_ple_kernel(x_ref, y0_ref, y1_ref, rw_ref, p_ref, lng_ref, lnb_ref, plew_ref, gw_ref,
                               gb_ref, o_ref):
    rw = rw_ref[...]
    ffn = y0_ref[...] * rw[:, 0:1] + y1_ref[...] * rw[:, 1:2]
    xn = _layer_norm(DEEPNORM_ALPHA * x_ref[...] + ffn, lng_ref[...], lnb_ref[...])
    o_ref[...] = _ple(xn, p_ref[0], plew_ref, gw_ref, gb_ref)


def _moe_combine_ln_ple(x2d, y0, y1, route_w, p3d, layer, ln_g, ln_b, ple_w, gate_w, gate_b, tm=512):
    m, d = x2d.shape
    row = lambda i: (i, 0)
    tile = pl.BlockSpec((tm, d), row)
    return pl.pallas_call(
        _moe_combine_ln_ple_kernel,
        grid=(m // tm,),
        in_specs=[tile, tile, tile, pl.BlockSpec((tm, TOP_K), row), _ple_block(p3d, layer, tm),
                  _resident((1, d)), _resident((1, d)), _resident(ple_w.shape), _resident(gate_w.shape),
                  _resident((1, d))],
        out_specs=tile,
        out_shape=jax.ShapeDtypeStruct((m, d), F32),
        compiler_params=_cparams("parallel"),
        name="moe_combine_ln_ple",
    )(x2d, y0, y1, route_w, p3d, ln_g.reshape(1, d), ln_b.reshape(1, d), ple_w, gate_w, gate_b.reshape(1, d))


def _swap_half_cols(w):
    k, n = w.shape
    return w.reshape(k, n // HEAD_DIM, 2, HEAD_DIM // 2)[:, :, ::-1, :].reshape(k, n)


def _rope_tables(seq):
    half = HEAD_DIM // 2
    inv = ROPE_THETA ** (-jnp.arange(half, dtype=F32) / half)
    ang = jnp.arange(seq, dtype=F32)[:, None] * inv[None, :]
    cos, sin = jnp.cos(ang), jnp.sin(ang)
    return jnp.concatenate([cos, cos], -1), jnp.concatenate([-sin, sin], -1)


def _nsa_kv_kernel(x_ref, wk_ref, wvt_ref, cos_ref, sin_ref, zc_ref, vc_ref, ks_ref, kw_ref, vst_ref,
                   vwt_ref):
    G, dh = N_KV_GROUPS, HEAD_DIM
    gw = G * dh
    xb = x_ref[...].astype(BF16)
    res = _dot(xb, wk_ref[...])
    res_t = _dot_nt(wvt_ref[...], xb)
    tm = res.shape[0]
    cos = cos_ref[...]
    sin = sin_ref[...]
    ks = res[:, 2 * gw:3 * gw] * cos + res[:, 3 * gw:4 * gw] * sin
    kw = res[:, 4 * gw:5 * gw] * cos + res[:, 5 * gw:6 * gw] * sin
    t_glob = pl.program_id(1) * tm + lax.broadcasted_iota(jnp.int32, (tm, dh), 0)
    onehot = jnp.where(lax.broadcasted_iota(jnp.int32, (tm, dh), 1) == t_glob // SEL_BLOCK, 1.0, 0.0)
    for g in range(G):
        sl = slice(g * dh, (g + 1) * dh)
        zc_ref[0, g] = res[:, sl].astype(BF16)
        vc_ref[0, g] = res[:, gw + g * dh:gw + (g + 1) * dh].astype(BF16)
        ks_ref[0, g] = jnp.concatenate([ks[:, sl], onehot], axis=1).astype(BF16)
        kw_ref[0, g] = kw[:, sl].astype(BF16)
    vst_ref[0] = res_t[:gw].astype(BF16)
    vwt_ref[0] = res_t[gw:].astype(BF16)


def _nsa_kv(x2d, w_kv, batch, seq, tm=256):
    m, d = x2d.shape
    G, dh = N_KV_GROUPS, HEAD_DIM
    gw = G * dh
    parts = [w_kv[:, j * gw:(j + 1) * gw] for j in range(6)]
    w_k = jnp.concatenate([parts[0], parts[1], parts[2], _swap_half_cols(parts[2]),
                           parts[4], _swap_half_cols(parts[4])], axis=1).astype(BF16)
    w_vt = jnp.concatenate([parts[3], parts[5]], axis=1).T.astype(BF16)
    cos, sin = _rope_tables(seq)
    cos = jnp.tile(cos, (1, G))
    sin = jnp.tile(sin, (1, G))
    nt = seq // tm
    o64 = jax.ShapeDtypeStruct((batch, G, seq, dh), BF16)
    o128 = jax.ShapeDtypeStruct((batch, G, seq, 2 * dh), BF16)
    ot = jax.ShapeDtypeStruct((batch, gw, seq), BF16)
    b64 = pl.BlockSpec((1, G, tm, dh), lambda b, t: (b, 0, t, 0))
    b128 = pl.BlockSpec((1, G, tm, 2 * dh), lambda b, t: (b, 0, t, 0))
    bt = pl.BlockSpec((1, gw, tm), lambda b, t: (b, 0, t))
    return pl.pallas_call(
        _nsa_kv_kernel,
        grid=(batch, nt),
        in_specs=[pl.BlockSpec((tm, d), lambda b, t: (b * nt + t, 0)), _resident(w_k.shape),
                  _resident(w_vt.shape),
                  pl.BlockSpec((tm, gw), lambda b, t: (t, 0)), pl.BlockSpec((tm, gw), lambda b, t: (t, 0))],
        out_specs=[b64, b64, b128, b64, bt, bt],
        out_shape=[o64, o64, o128, o64, ot, ot],
        compiler_params=_cparams("parallel", "parallel"),
        name="nsa_kv",
    )(x2d, w_k, w_vt, cos, sin)


def _nsa_cmp_kernel(z_ref, pos_ref, w1_ref, b1_ref, w2_ref, o_ref, *, slabs_per_seq):
    half = w1_ref.shape[1] // 2
    z = z_ref[0]
    tm = z.shape[0]
    first = _dot(z, w1_ref[0, :half, :])
    second = _dot(z, w1_ref[0, half:, :])
    const = _dot(pos_ref[0], w1_ref[0])[0:1, :] + b1_ref[0]
    hid = first + pltpu.roll(second, shift=tm - 1, axis=0) + const
    act = 0.5 * hid * (1.0 + jnp.tanh(math.sqrt(2.0 / math.pi) * (hid + 0.044715 * hid * hid * hid)))
    out = _dot(act.astype(BF16), w2_ref[0])
    row = lax.broadcasted_iota(jnp.int32, out.shape, 0)
    o_ref[0] = jnp.where(row % slabs_per_seq == slabs_per_seq - 1, 0.0, out).astype(BF16)


def _nsa_compress(zc, vc, cmp_pos, phi_w1, phi_b1, phi_w2, tm=512):
    batch, G, seq, dh = zc.shape
    slab = CMP_STRIDE * dh
    rows = batch * G * seq // CMP_STRIDE
    tm = min(tm, rows)
    z = jnp.stack([zc.reshape(rows, slab), vc.reshape(rows, slab)])
    pos = jnp.broadcast_to(cmp_pos.reshape(2, 1, CMP_LEN * dh), (2, 8, CMP_LEN * dh)).astype(BF16)
    out = pl.pallas_call(
        functools.partial(_nsa_cmp_kernel, slabs_per_seq=seq // CMP_STRIDE),
        grid=(2, rows // tm),
        in_specs=[pl.BlockSpec((1, tm, slab), lambda j, i: (j, i, 0)),
                  pl.BlockSpec((1, 8, CMP_LEN * dh), lambda j, i: (j, 0, 0)),
                  pl.BlockSpec((1, CMP_LEN * dh, CMP_HIDDEN), lambda j, i: (j, 0, 0)),
                  pl.BlockSpec((1, 1, CMP_HIDDEN), lambda j, i: (j, 0, 0)),
                  pl.BlockSpec((1, CMP_HIDDEN, dh), lambda j, i: (j, 0, 0))],
        out_specs=pl.BlockSpec((1, tm, dh), lambda j, i: (j, i, 0)),
        out_shape=jax.ShapeDtypeStruct((2, rows, dh), BF16),
        compiler_params=_cparams("parallel", "parallel"),
        name="nsa_compress",
    )(z, pos, phi_w1.astype(BF16), phi_b1.reshape(2, 1, CMP_HIDDEN), phi_w2.astype(BF16))
    n_slab = seq // CMP_STRIDE
    return out[0].reshape(batch, G, n_slab, dh), out[1].reshape(batch, G, n_slab, dh)


GATE_ROWS = 16


def _nsa_q_kernel(x_ref, wt_ref, cos_ref, sin_ref, q_ref, qr_ref, gate_ref):
    dh = HEAD_DIM
    hw = N_HEADS * dh
    scale = HEAD_DIM ** -0.5 * math.log2(math.e)
    res_t = _dot_nt(wt_ref[...], x_ref[...].astype(BF16))
    tm = res_t.shape[1]
    q = res_t[:hw].reshape(N_HEADS, dh, tm)
    q_sw = res_t[hw:2 * hw].reshape(N_HEADS, dh, tm)
    qr = q * cos_ref[...][None] + q_sw * sin_ref[...][None]
    q_ref[0] = (q * scale).reshape(hw, tm).astype(BF16)
    qr_ref[0] = (qr * scale).reshape(hw, tm).astype(BF16)
    gate_ref[0] = _sigmoid(res_t[2 * hw:])


def _nsa_q(x2d, w_in, batch, seq, tm=256):
    m, d = x2d.shape
    G, R, dh = N_KV_GROUPS, Q_PER_GROUP, HEAD_DIM
    hw = N_HEADS * dh
    w_gate = w_in[:, hw:].reshape(d, G, R * N_BRANCH)
    w_gate = jnp.pad(w_gate, ((0, 0), (0, 0), (0, GATE_ROWS - R * N_BRANCH))).reshape(d, G * GATE_ROWS)
    wt = jnp.concatenate([w_in[:, :hw], _swap_half_cols(w_in[:, :hw]), w_gate], axis=1).T.astype(BF16)
    cos, sin = _rope_tables(seq)
    nt = seq // tm
    oq = jax.ShapeDtypeStruct((batch, hw, seq), BF16)
    bq = pl.BlockSpec((1, hw, tm), lambda b, t: (b, 0, t))
    tab = pl.BlockSpec((dh, tm), lambda b, t: (0, t))
    return pl.pallas_call(
        _nsa_q_kernel,
        grid=(batch, nt),
        in_specs=[pl.BlockSpec((tm, d), lambda b, t: (b * nt + t, 0)), _resident(wt.shape), tab, tab],
        out_specs=[bq, bq, pl.BlockSpec((1, G * GATE_ROWS, tm), lambda b, t: (b, 0, t))],
        out_shape=[oq, oq, jax.ShapeDtypeStruct((batch, G * GATE_ROWS, seq), F32)],
        compiler_params=_cparams("parallel", "parallel"),
        name="nsa_q",
    )(x2d, wt, cos.T, sin.T)


def _nsa_attn_kernel(q_ref, qr_ref, gate_ref, kc_ref, vct_ref, ks_ref, vst_ref, kw_ref, vwt_ref, ovlt_ref,
                     o_ref, *, tq, tk, n_blk):
    R, dh = Q_PER_GROUP, HEAD_DIM
    cols = R * tq
    qi = pl.program_id(2)
    t0 = qi * tq
    n_blk_pad = ovlt_ref.shape[0]

    def heads_to_lanes(x):
        return jnp.concatenate([x[r * dh:(r + 1) * dh] for r in range(R)], axis=1)

    q = heads_to_lanes(q_ref[0])
    qr = heads_to_lanes(qr_ref[0])

    def col_pos(shape):
        return t0 + (lax.broadcasted_iota(jnp.int32, shape, 1) & (tq - 1))

    def key_tile(j):
        return pl.ds(pl.multiple_of(j * tk, tk), tk)

    ct = tq
    col_tiles = [slice(c, c + ct) for c in range(0, cols, ct)]
    ones_rows = 16
    ones_blk = jnp.ones((ones_rows, tk), BF16)

    def tile_scores(k_ref, q_op, tiles):
        return [[_dot(k_ref[0, 0, key_tile(j), :], q_op[:, cs]) for j, _ in tiles] for cs in col_tiles]

    def attend(carry, sc, vt_ref, tiles):
        m_i, acc = carry
        outs = []
        for cs, sc_c in zip(col_tiles, sc):
            masked = [s_j if bias is None else s_j + bias for s_j, (_, bias) in zip(sc_c, tiles)]
            m_new = m_i[:, cs]
            for s_j in masked:
                m_new = jnp.maximum(m_new, jnp.max(s_j, axis=0, keepdims=True))
            acc_new = jnp.exp2(m_i[:, cs] - m_new) * acc[:, cs]
            for s_j, (j, _) in zip(masked, tiles):
                v_ext = jnp.concatenate([vt_ref[0, :, key_tile(j)], ones_blk], axis=0)
                acc_new = acc_new + _dot(v_ext, jnp.exp2(s_j - m_new).astype(BF16))
            outs.append((m_new, acc_new))
        return tuple(jnp.concatenate([o[i] for o in outs], axis=1) for i in range(2))

    def normalised(acc):
        return acc[:dh] / acc[dh:dh + 1]

    init = (jnp.full((1, cols), NEG_INF, F32), jnp.zeros((dh + ones_rows, cols), F32))

    k_off = lax.broadcasted_iota(jnp.int32, (tk, ct), 0)
    q_off = lax.broadcasted_iota(jnp.int32, (tk, ct), 1)
    causal = jnp.where(k_off <= q_off, 0.0, NEG_INF)
    absent = lambda cond: jnp.where(cond, 0.0, NEG_INF)

    win_tiles = [(jnp.maximum(qi - 2, 0), jnp.where(q_off < k_off, 0.0, NEG_INF) + absent(qi >= 2)),
                 (jnp.maximum(qi - 1, 0), jnp.zeros((tk, ct), F32) + absent(qi >= 1)),
                 (qi, causal)]
    sc_win = tile_scores(kw_ref, qr, win_tiles)

    s = _dot(kc_ref[0, 0], q)
    cmp_end = lax.broadcasted_iota(jnp.int32, s.shape, 0) * CMP_STRIDE + (CMP_LEN - 1)
    valid = cmp_end <= col_pos(s.shape)
    sm = jnp.where(valid, s, NEG_INF)
    e = jnp.where(valid, jnp.exp2(sm - jnp.max(sm, axis=0, keepdims=True)), 0.0)
    l = jnp.sum(e, axis=0, keepdims=True)
    p_cmp = e / jnp.where(l > 0.0, l, 1.0)
    o_cmp = _dot(vct_ref[0, 0], p_cmp.astype(BF16))

    p_sum = p_cmp[:, 0:tq]
    for r in range(1, R):
        p_sum = p_sum + p_cmp[:, r * tq:(r + 1) * tq]
    hi, lo = _split2(p_sum)
    imp = _dot(ovlt_ref[...], hi) + _dot(ovlt_ref[...], lo)
    blk = lax.broadcasted_iota(jnp.int32, imp.shape, 0)
    t_q = t0 + lax.broadcasted_iota(jnp.int32, imp.shape, 1)
    cur = t_q // SEL_BLOCK
    forced = (blk == 0) | (blk == cur) | (blk == cur - 1)
    score = jnp.where(forced, FORCE_SCORE, jnp.where(blk * SEL_BLOCK <= t_q, imp, NEG_INF))
    score = jnp.where(blk < n_blk, score, -jnp.inf)
    selected = blk >= n_blk
    for _ in range(N_SEL):
        best = jnp.max(score, axis=0, keepdims=True)
        first = jnp.min(jnp.where(score == best, blk, 2 * LANES), axis=0, keepdims=True)
        pick = blk == first
        selected = selected | pick
        score = jnp.where(pick, -jnp.inf, score)
    bias = jnp.where(selected, 0.0, NEG_INF)
    qa = jnp.concatenate([qr, jnp.concatenate([bias] * R, axis=1).astype(BF16),
                          jnp.zeros((dh - n_blk_pad, cols), BF16)], axis=0)

    o_win = normalised(attend(init, sc_win, vwt_ref, win_tiles)[1])

    n_pairs = qi // 2

    def pair_body(i, carry):
        tiles = [(2 * i, None), (2 * i + 1, None)]
        return attend(carry, tile_scores(ks_ref, qa, tiles), vst_ref, tiles)

    carry = lax.fori_loop(0, n_pairs, pair_body, init)
    j_odd = 2 * n_pairs
    tail_tiles = [(j_odd, jnp.zeros((tk, ct), F32) + absent(j_odd < qi)), (qi, causal)]
    o_slc = normalised(attend(carry, tile_scores(ks_ref, qa, tail_tiles), vst_ref, tail_tiles)[1])

    gates = gate_ref[0]
    outs = []
    for r in range(R):
        cs = slice(r * tq, (r + 1) * tq)
        outs.append(gates[3 * r:3 * r + 1] * o_cmp[:, cs] + gates[3 * r + 1:3 * r + 2] * o_slc[:, cs]
                    + gates[3 * r + 2:3 * r + 3] * o_win[:, cs])
    halves = [jnp.transpose(jnp.concatenate(outs[i:i + 2], axis=0)) for i in range(0, R, 2)]
    o_ref[0] = jnp.concatenate(halves, axis=1)


SEL_ROWS = 32


def _nsa_attention(q, qr, gates, k_cmp, v_cmp, ks, vst, kw, vwt, tq=256, tk=256):
    batch, _, seq = q.shape
    G, R, dh = N_KV_GROUPS, Q_PER_GROUP, HEAD_DIM
    n_slab = k_cmp.shape[2]
    n_blk = seq // SEL_BLOCK
    assert n_blk <= SEL_ROWS
    assert tq == tk and WINDOW == 2 * tk, "the window branch is written as exactly three key tiles"
    cmp_start = jnp.arange(n_slab) * CMP_STRIDE
    blk_start = jnp.arange(SEL_ROWS) * SEL_BLOCK
    overlap_t = ((cmp_start[None, :] < blk_start[:, None] + SEL_BLOCK)
                 & (cmp_start[None, :] + CMP_LEN - 1 >= blk_start[:, None])
                 & (jnp.arange(SEL_ROWS)[:, None] < n_blk)
                 & (jnp.arange(n_slab)[None, :] < (seq - CMP_LEN) // CMP_STRIDE + 1)).astype(BF16)
    v_cmp_t = jnp.swapaxes(v_cmp, 2, 3)
    qspec = pl.BlockSpec((1, R * dh, tq), lambda b, g, i: (b, g, i))
    full = lambda a: pl.BlockSpec((1, 1) + a.shape[2:], lambda b, g, i: (b, g, 0, 0))
    vspec = pl.BlockSpec((1, dh, seq), lambda b, g, i: (b, g, 0))
    return pl.pallas_call(
        functools.partial(_nsa_attn_kernel, tq=tq, tk=tk, n_blk=n_blk),
        grid=(batch, G, seq // tq),
        in_specs=[qspec, qspec, pl.BlockSpec((1, GATE_ROWS, tq), lambda b, g, i: (b, g, i)),
                  full(k_cmp), full(v_cmp_t), full(ks), vspec, full(kw), vspec,
                  _resident(overlap_t.shape)],
        out_specs=pl.BlockSpec((1, tq, R * dh), lambda b, g, i: (b, i, g)),
        out_shape=jax.ShapeDtypeStruct((batch, seq, G * R * dh), F32),
        compiler_params=_cparams("parallel", "parallel", "parallel"),
        name="nsa_attn",
    )(q, qr, gates, k_cmp, v_cmp_t, ks, vst, kw, vwt, overlap_t)


def _moe_kernel(be_ref, nb_ref, x_ref, wg_ref, wu_ref, wd_ref, o_ref):
    i = pl.program_id(0)

    @pl.when(i < nb_ref[0])
    def _():
        xb = x_ref[...].astype(BF16)
        gt = _dot(xb, wg_ref[0])
        up = _dot(xb, wu_ref[0])
        h = (gt * _sigmoid(gt) * up).astype(BF16)
        o_ref[...] = _dot(h, wd_ref[0])

    @pl.when(i >= nb_ref[0])
    def _():
        o_ref[...] = jnp.zeros_like(o_ref)


def _moe_experts(xs, block_e, n_used, wg, wu, wd):
    n_rows, d = xs.shape
    n_blocks = n_rows // MOE_ROWS
    ff = wg.shape[2]
    grid_spec = pltpu.PrefetchScalarGridSpec(
        num_scalar_prefetch=2,
        grid=(n_blocks,),
        in_specs=[pl.BlockSpec((MOE_ROWS, d), lambda i, be, nb: (i, 0)),
                  pl.BlockSpec((1, d, ff), lambda i, be, nb: (be[i], 0, 0)),
                  pl.BlockSpec((1, d, ff), lambda i, be, nb: (be[i], 0, 0)),
                  pl.BlockSpec((1, ff, d), lambda i, be, nb: (be[i], 0, 0))],
        out_specs=pl.BlockSpec((MOE_ROWS, d), lambda i, be, nb: (i, 0)),
    )
    return pl.pallas_call(
        _moe_kernel,
        grid_spec=grid_spec,
        out_shape=jax.ShapeDtypeStruct((n_rows, d), F32),
        compiler_params=_cparams("arbitrary"),
        name="moe_experts",
    )(block_e, n_used, xs, wg, wu, wd)


def _moe(x2d, logits, w_gu, w_down):
    n_tok, d = x2d.shape
    n_assign = n_tok * TOP_K
    top_logit, top_e = lax.top_k(logits, TOP_K)
    weights = jax.nn.softmax(top_logit, axis=-1)
    flat_e = top_e.reshape(-1).astype(jnp.int32)
    onehot = (flat_e[:, None] == jnp.arange(N_EXPERTS, dtype=jnp.int32)[None, :]).astype(jnp.int32)
    running = jnp.cumsum(onehot, axis=0)
    counts = running[-1]
    padded = (counts + MOE_ROWS - 1) // MOE_ROWS * MOE_ROWS
    pad_end = jnp.cumsum(padded)
    pad_start = pad_end - padded
    grp_start = jnp.cumsum(counts) - counts
    pos = jnp.sum(onehot * (running - 1 + pad_start[None, :]), axis=1).reshape(n_tok, TOP_K)
    n_blocks = -(-n_assign // MOE_ROWS) + N_EXPERTS
    block_e = jnp.minimum(jnp.searchsorted(pad_end, jnp.arange(n_blocks) * MOE_ROWS, side='right'),
                          N_EXPERTS - 1).astype(jnp.int32)
    n_used = (pad_end[-1] // MOE_ROWS).astype(jnp.int32).reshape(1)
    order = jnp.argsort(flat_e)
    tok_sorted = (order // TOP_K).astype(jnp.int32)
    row_in_grp = (jnp.arange(n_blocks, dtype=jnp.int32) * MOE_ROWS - pad_start[block_e])[:, None] \
        + jnp.arange(MOE_ROWS, dtype=jnp.int32)[None, :]
    src = jnp.clip(grp_start[block_e][:, None] + row_in_grp, 0, n_assign - 1)
    row_tok = jnp.where(row_in_grp < counts[block_e][:, None], tok_sorted[src], 0).reshape(-1)
    xs = x2d[row_tok]
    ys = _moe_experts(xs, block_e, n_used, w_gu[:, :, :D_FF_EXPERT].astype(BF16),
                      w_gu[:, :, D_FF_EXPERT:].astype(BF16), w_down.astype(BF16))
    return [ys[pos[:, s]] for s in range(TOP_K)], weights


def kernel(x, p, a_mu, a_w_rkv, a_w0, a_w1, a_w2, a_a0, a_a1, a_a2, a_g1, a_g2, a_k_k, a_k_a, a_r_k,
           a_gn_g, a_gn_b, a_w_o, b_w_kv, b_cmp_pos, b_phi_w1, b_phi_b1, b_phi_w2, b_w_in, b_w_o,
           f_w_gu, f_w_down, m_w_router, m_b_router, m_w_gu, m_w_down, ln_g, ln_b, ple_w, ple_gate_w,
           ple_gate_b):
    batch, seq, d = x.shape
    m = batch * seq
    x0 = x.reshape(m, d)
    p2d = p.reshape(DEPTH, m, PLE_DIM)
    bf = lambda w: w.astype(BF16)

    x1 = _rwkv_layer(x0, batch, seq, a_mu[0], bf(a_w_rkv[0]), a_w0[0], bf(a_w1[0]), bf(a_w2[0]), a_a0[0],
                     bf(a_a1[0]), bf(a_a2[0]), bf(a_g1[0]), bf(a_g2[0]), a_k_k[0], a_k_a[0], a_r_k[0],
                     a_gn_g[0], a_gn_b[0], bf(a_w_o[0]), ln_g[0, 0], ln_b[0, 0])
    x2 = _ffn_ple(x1, p2d, 0, bf(f_w_gu[0][:, :D_FF]), bf(f_w_gu[0][:, D_FF:]), bf(f_w_down[0]),
                  ln_g[0, 1], ln_b[0, 1], bf(ple_w[0]), bf(ple_gate_w[0]), ple_gate_b[0])

    zc, vc, ks, kw, vst, vwt = _nsa_kv(x2, b_w_kv, batch, seq)
    k_cmp, v_cmp = _nsa_compress(zc, vc, b_cmp_pos, b_phi_w1, b_phi_b1, b_phi_w2)
    q, qr, gates = _nsa_q(x2, b_w_in[0], batch, seq)
    attn = _nsa_attention(q, qr, gates, k_cmp, v_cmp, ks, vst, kw, vwt).reshape(m, d)
    x3, logits = _proj_ln_router(attn, x2, bf(b_w_o[0]), ln_g[1, 0], ln_b[1, 0], m_w_router[0],
                                 m_b_router[0])
    (y0, y1), route_w = _moe(x3, logits[:, :N_EXPERTS], m_w_gu[0], m_w_down[0])
    out = _moe_combine_ln_ple(x3, y0, y1, route_w, p2d, 1, ln_g[1, 1], ln_b[1, 1], bf(ple_w[1]),
                              bf(ple_gate_w[1]), ple_gate_b[1])
    return out.reshape(batch, seq, d)
```

```python
import functools
import math

import jax
import jax.numpy as jnp
from jax import lax
from jax.experimental import pallas as pl
from jax.experimental.pallas import tpu as pltpu

BF16 = jnp.bfloat16
F32 = jnp.float32

LANES = 128
VMEM_LIMIT_BYTES = 56 * 1024 * 1024

D_MODEL = 1024
PLE_DIM = 256
RWKV_HEAD = 64
GN_EPS = 64e-5
N_HEADS = 16
HEAD_DIM = 64
N_KV_GROUPS = 4
Q_PER_GROUP = 4
N_BRANCH = 3
CMP_LEN = 32
CMP_STRIDE = 16
CMP_HIDDEN = 256
SEL_BLOCK = 64
N_SEL = 8
WINDOW = 512
ROPE_THETA = 10000.0
D_FF = 2816
N_EXPERTS = 8
TOP_K = 2
D_FF_EXPERT = 1408
MOE_ROWS = 256
LN_EPS = 1e-5
DEPTH = 2
DEEPNORM_ALPHA = (2.0 * DEPTH) ** 0.25
NEG_INF = -1e30
FORCE_SCORE = 1e4

SCAN_CHUNK = 64


def _cparams(*sem):
    return pltpu.CompilerParams(dimension_semantics=sem, vmem_limit_bytes=VMEM_LIMIT_BYTES)


def _resident(shape):
    nd = len(shape)
    return pl.BlockSpec(shape, lambda *_: (0,) * nd, pipeline_mode=pl.Buffered(1))


def _dot(a, b):
    return jnp.dot(a, b, preferred_element_type=F32)


def _dot_nt(a, b):
    return lax.dot_general(a, b, (((1,), (1,)), ((), ())), preferred_element_type=F32)


def _split2(x):
    hi = x.astype(BF16)
    lo = (x - hi.astype(F32)).astype(BF16)
    return hi, lo


def _split3(x):
    hi = x.astype(BF16)
    r1 = x - hi.astype(F32)
    mid = r1.astype(BF16)
    lo = (r1 - mid.astype(F32)).astype(BF16)
    return hi, mid, lo


def _layer_norm(y, g, b):
    mu = jnp.mean(y, axis=-1, keepdims=True)
    yc = y - mu
    var = jnp.mean(yc * yc, axis=-1, keepdims=True)
    return yc * lax.rsqrt(var + LN_EPS) * g + b


def _sigmoid(z):
    return 1.0 / (1.0 + jnp.exp(-z))


def _rwkv_project(first, x_ref, xh_ref, mu_ref, wrkv_ref, w0_ref, w1_ref, w2_ref, a0_ref, a1_ref,
                  a2_ref, g1_ref, g2_ref, r_ref, k_ref, v_ref, dl_ref, a_ref, g_ref):
    x = x_ref[...]
    prev_row = jnp.where(first, 0.0, xh_ref[0, 7:8, :])
    row = lax.broadcasted_iota(jnp.int32, x.shape, 0)
    x_shift = jnp.where(row == 0, prev_row, pltpu.roll(x, shift=1, axis=0))
    xx = x_shift - x

    def mix(i):
        return (x + xx * mu_ref[i:i + 1, :]).astype(BF16)

    r_ref[...] = _dot(mix(0), wrkv_ref[0])
    k_ref[...] = _dot(mix(1), wrkv_ref[1])
    v_ref[...] = _dot(mix(2), wrkv_ref[2])
    z = w0_ref[...] + _dot(jnp.tanh(_dot(mix(3), w1_ref[...])).astype(BF16), w2_ref[...])
    dl_ref[...] = -math.exp(-0.5) * _sigmoid(z)
    a_ref[...] = _sigmoid(a0_ref[...] + _dot(_dot(mix(4), a1_ref[...]).astype(BF16), a2_ref[...]))
    g_ref[...] = _dot(_sigmoid(_dot(mix(5), g1_ref[...])).astype(BF16), g2_ref[...])


def _rwkv_recurrence(first, r_ref, k_ref, v_ref, dl_ref, a_ref, g_ref, kk_ref, ka_ref, rk_ref, gng_ref,
                     gnb_ref, o_ref, state_ref, kkn_ref, cum_ref, *, n_pairs, n_chunks):
    C = SCAN_CHUNK
    N = RWKV_HEAD
    W = 2 * N
    tb = n_chunks * C

    @pl.when(first)
    def _():
        state_ref[...] = jnp.zeros_like(state_ref)

    tt = lax.broadcasted_iota(jnp.int32, (C, W), 0)
    ln = lax.broadcasted_iota(jnp.int32, (C, W), 1)
    ss = ln & (N - 1)
    strict_lower = ss < tt
    lower = ss <= tt
    eye = jnp.where(ss == tt, 1.0, 0.0)
    level_masks = []
    for lg in range(int(math.log2(C))):
        level_masks.append(((tt >> (lg + 1)) == (ss >> (lg + 1))) & ((tt >> lg) == (ss >> lg) + 1))
    rr = lax.broadcasted_iota(jnp.int32, (W, W), 0)
    cc = lax.broadcasted_iota(jnp.int32, (W, W), 1)
    same_head = (rr < N) == (cc < N)
    ones_bd = jnp.where(same_head, 1.0, 0.0).astype(BF16)
    lane_head0 = ln < N

    def bd(y):
        return jnp.concatenate([jnp.where(lane_head0, y, 0.0), jnp.where(lane_head0, 0.0, y)],
                               axis=0).astype(BF16)

    def pmm(xp, ybd):
        return _dot(xp.astype(BF16), ybd)

    def head_sum(xp):
        hi, lo = _split2(xp)
        return _dot(hi, ones_bd) + _dot(lo, ones_bd)

    tri_r = lax.broadcasted_iota(jnp.int32, (tb, tb), 0)
    tri_c = lax.broadcasted_iota(jnp.int32, (tb, tb), 1)
    lg_c = int(math.log2(C))
    tri_chunk = jnp.where((tri_c <= tri_r) & ((tri_c >> lg_c) == (tri_r >> lg_c)), 1.0, 0.0).astype(BF16)
    d_hi, d_mid, d_lo = _split3(dl_ref[...])
    cum_ref[...] = _dot(tri_chunk, d_hi) + _dot(tri_chunk, d_mid) + _dot(tri_chunk, d_lo)
    for pi in range(n_pairs):
        cols = slice(pi * W, (pi + 1) * W)
        kk_raw = k_ref[:, cols] * kk_ref[:, cols]
        kkn_ref[:, cols] = kk_raw / jnp.maximum(jnp.sqrt(head_sum(kk_raw * kk_raw)), 1e-12)

    def chunk_load(ci, pi):
        rows = pl.ds(pl.multiple_of(ci * C, C), C)
        cols = slice(pi * W, (pi + 1) * W)
        return (r_ref[rows, cols], k_ref[rows, cols], v_ref[rows, cols], dl_ref[rows, cols],
                a_ref[rows, cols], kkn_ref[rows, cols], cum_ref[rows, cols], state_ref[pi])

    def chunk_compute(pi, r, k, v, dl, a, kk, cum, m0):
        cols = slice(pi * W, (pi + 1) * W)
        k2 = k * (1.0 + (a - 1.0) * ka_ref[:, cols])
        b = a * kk
        p_incl = jnp.exp(cum)
        p_excl = jnp.exp(cum - dl)
        p_inv = jnp.exp(-cum)
        kq = kk * p_excl
        rq = r * p_incl
        bk = b * p_inv
        kkd = k2 * p_inv

        bkt = jnp.transpose(jnp.concatenate([b, k2], axis=0))
        cumt = jnp.transpose(jnp.concatenate([cum, cum], axis=0))
        last_col = cumt[:, C - 1:C]
        lhs_t = (bkt * jnp.exp(last_col - cumt)).astype(BF16)
        m_decayed = m0 * jnp.exp(last_col)

        lhs = jnp.concatenate([kq, rq], axis=0).astype(BF16)
        yield
        ab = _dot_nt(lhs, bd(bk))
        gb = _dot_nt(lhs, bd(kkd))
        a_m = jnp.where(strict_lower, ab[:C], 0.0)
        aq_m = jnp.where(lower, ab[C:], 0.0)
        g_m = jnp.where(strict_lower, gb[:C], 0.0)
        gq_m = jnp.where(lower, gb[C:], 0.0)

        tinv = eye - jnp.where(level_masks[0], a_m, 0.0)
        yield
        gv = pmm(jnp.concatenate([g_m, gq_m], axis=0), bd(v))
        for lm in level_masks[1:]:
            w_ = pmm(jnp.where(lm, a_m, 0.0), bd(tinv))
            yield
            tinv = tinv - pmm(tinv, bd(w_))
            yield

        kq_y = pmm(tinv, jnp.concatenate([bd(kq), bd(gv[:C])], axis=1))
        kq1 = kq_y[:, :W]
        y = kq_y[:, W:]
        yield
        aq_ky = pmm(aq_m, jnp.concatenate([bd(kq1), bd(y)], axis=1))
        rq1 = rq - aq_ky[:, :W]
        o_loc = gv[C:] - aq_ky[:, W:]
        yield
        st = _dot(jnp.concatenate([rq1, kq1], axis=0).astype(BF16), m0.astype(BF16))
        o = st[:C] + o_loc
        z = -(st[C:] + y)
        yield
        upd = _dot(lhs_t, jnp.concatenate([z, v], axis=0).astype(BF16))
        return o, m_decayed + jnp.where(same_head, upd, 0.0)

    def interleave(gens):
        results = [None] * len(gens)
        live = list(range(len(gens)))
        while live:
            for i in list(live):
                try:
                    next(gens[i])
                except StopIteration as done:
                    results[i] = done.value
                    live.remove(i)
        return results

    def body(ci, carry):
        loaded = [chunk_load(ci, pi) for pi in range(n_pairs)]
        results = interleave([chunk_compute(pi, *loaded[pi]) for pi in range(n_pairs)])
        rows = pl.ds(pl.multiple_of(ci * C, C), C)
        for pi, (o, m_new) in enumerate(results):
            o_ref[rows, pi * W:(pi + 1) * W] = o
            state_ref[pi] = m_new
        return carry

    lax.fori_loop(0, n_chunks, body, 0)

    for pi in range(n_pairs):
        cols = slice(pi * W, (pi + 1) * W)
        o = o_ref[:, cols]
        r = r_ref[:, cols]
        v = v_ref[:, cols]
        k2 = k_ref[:, cols] * (1.0 + (a_ref[:, cols] - 1.0) * ka_ref[:, cols])
        oc = o - head_sum(o) * (1.0 / N)
        var_o = head_sum(oc * oc) * (1.0 / N)
        out = oc * lax.rsqrt(var_o + GN_EPS) * gng_ref[:, cols] + gnb_ref[:, cols]
        out = out + head_sum(r * k2 * rk_ref[:, cols]) * v
        o_ref[:, cols] = out * g_ref[:, cols]


def _rwkv_layer_kernel(x_ref, xh_ref, mu_ref, wrkv_ref, w0_ref, w1_ref, w2_ref, a0_ref, a1_ref, a2_ref,
                       g1_ref, g2_ref, kk_ref, ka_ref, rk_ref, gng_ref, gnb_ref, wo_ref, lng_ref, lnb_ref,
                       o_ref, state_ref, r_s, k_s, v_s, dl_s, a_s, g_s, kkn_s, cum_s, y_s, *, n_chunks):
    first = pl.program_id(1) == 0
    _rwkv_project(first, x_ref, xh_ref, mu_ref, wrkv_ref, w0_ref, w1_ref, w2_ref, a0_ref, a1_ref, a2_ref,
                  g1_ref, g2_ref, r_s, k_s, v_s, dl_s, a_s, g_s)
    _rwkv_recurrence(first, r_s, k_s, v_s, dl_s, a_s, g_s, kk_ref, ka_ref, rk_ref, gng_ref, gnb_ref, y_s,
                     state_ref, kkn_s, cum_s, n_pairs=x_ref.shape[1] // (2 * RWKV_HEAD), n_chunks=n_chunks)
    mix = _dot(y_s[...].astype(BF16), wo_ref[...])
    o_ref[...] = _layer_norm(DEEPNORM_ALPHA * x_ref[...] + mix, lng_ref[...], lnb_ref[...])


def _rwkv_layer(x2d, batch, seq, mu, w_rkv, w0, w1, w2, a0, a1, a2, g1, g2, k_k, k_a, r_k, gn_g, gn_b,
                w_o, ln_g, ln_b, tb=256):
    m, d = x2d.shape
    nt = seq // tb
    xh = x2d.reshape(m // 8, 8, d)
    vec = lambda z: z.reshape(1, d)
    blk = pl.BlockSpec((tb, d), lambda b, t: (b * nt + t, 0))
    weights = [mu, w_rkv, vec(w0), w1, w2, vec(a0), a1, a2, g1, g2, vec(k_k), vec(k_a), vec(r_k),
               vec(gn_g), vec(gn_b), w_o, vec(ln_g), vec(ln_b)]
    n_pairs = d // (2 * RWKV_HEAD)
    return pl.pallas_call(
        functools.partial(_rwkv_layer_kernel, n_chunks=tb // SCAN_CHUNK),
        grid=(batch, nt),
        in_specs=[blk, pl.BlockSpec((1, 8, d), lambda b, t: (jnp.maximum((b * nt + t) * (tb // 8) - 1, 0), 0, 0))]
        + [_resident(w.shape) for w in weights],
        out_specs=blk,
        out_shape=jax.ShapeDtypeStruct((m, d), F32),
        scratch_shapes=[pltpu.VMEM((n_pairs, 2 * RWKV_HEAD, 2 * RWKV_HEAD), F32)]
        + [pltpu.VMEM((tb, d), F32)] * 9,
        compiler_params=_cparams("parallel", "arbitrary"),
        name="rwkv_layer",
    )(x2d, xh, *weights)


def _proj_ln_router_kernel(y_ref, x_ref, w_ref, lng_ref, lnb_ref, wr_ref, br_ref, o_ref, lg_ref):
    mix = _dot(y_ref[...].astype(BF16), w_ref[...])
    xn = _layer_norm(DEEPNORM_ALPHA * x_ref[...] + mix, lng_ref[...], lnb_ref[...])
    o_ref[...] = xn
    xh, xl = _split2(xn)
    lg_ref[...] = _dot(xh, wr_ref[0]) + _dot(xh, wr_ref[1]) + _dot(xl, wr_ref[0]) + br_ref[...]


def _proj_ln_router(y, x2d, w, ln_g, ln_b, w_router, b_router, tm=512):
    m, d = x2d.shape
    row = lambda i: (i, 0)
    wr = jnp.zeros((d, LANES), F32).at[:, :N_EXPERTS].set(w_router)
    wr3 = jnp.stack(_split2(wr))
    br =jnp.zeros((1, LANES), F32).at[0, :N_EXPERTS].set(b_router)
    return pl.pallas_call(
        _proj_ln_router_kernel,
        grid=(m // tm,),
        in_specs=[pl.BlockSpec((tm, y.shape[1]), row), pl.BlockSpec((tm, d), row), _resident(w.shape),
                  _resident((1, d)), _resident((1, d)), _resident(wr3.shape), _resident(br.shape)],
        out_specs=[pl.BlockSpec((tm, d), row), pl.BlockSpec((tm, LANES), row)],
        out_shape=[jax.ShapeDtypeStruct((m, d), F32), jax.ShapeDtypeStruct((m, LANES), F32)],
        compiler_params=_cparams("parallel"),
        name="proj_ln_router",
    )(y, x2d, w, ln_g.reshape(1, d), ln_b.reshape(1, d), wr3, br)


def _ple(xn, p, plew_ref, gw_ref, gb_ref):
    gate = _sigmoid(_dot(xn.astype(BF16), gw_ref[...]) + gb_ref[...])
    return xn + _dot(p.astype(BF16), plew_ref[...]) * gate


def _ffn_ple_kernel(x_ref, p_ref, wg_ref, wu_ref, wd_ref, lng_ref, lnb_ref, plew_ref, gw_ref, gb_ref,
                    o_ref, *, ff_chunk):
    x = x_ref[...]
    xb = x.astype(BF16)
    acc = jnp.zeros_like(x)
    for c in range(wg_ref.shape[1] // ff_chunk):
        sl = slice(c * ff_chunk, (c + 1) * ff_chunk)
        gt = _dot(xb, wg_ref[:, sl])
        up = _dot(xb, wu_ref[:, sl])
        h = (gt * _sigmoid(gt) * up).astype(BF16)
        acc = acc + _dot(h, wd_ref[sl, :])
    xn = _layer_norm(DEEPNORM_ALPHA * x + acc, lng_ref[...], lnb_ref[...])
    o_ref[...] = _ple(xn, p_ref[0], plew_ref, gw_ref, gb_ref)


def _ple_block(p3d, layer, tm):
    return pl.BlockSpec((1, tm, p3d.shape[2]), lambda i: (layer, i, 0))


def _ffn_ple(x2d, p3d, layer, wg, wu, wd, ln_g, ln_b, ple_w, gate_w, gate_b, tm=512, ff_chunk=1408):
    m, d = x2d.shape
    row = lambda i: (i, 0)
    return pl.pallas_call(
        functools.partial(_ffn_ple_kernel, ff_chunk=ff_chunk),
        grid=(m // tm,),
        in_specs=[pl.BlockSpec((tm, d), row), _ple_block(p3d, layer, tm),
                  _resident(wg.shape), _resident(wu.shape), _resident(wd.shape),
                  _resident((1, d)), _resident((1, d)), _resident(ple_w.shape), _resident(gate_w.shape),
                  _resident((1, d))],
        out_specs=pl.BlockSpec((tm, d), row),
        out_shape=jax.ShapeDtypeStruct((m, d), F32),
        compiler_params=_cparams("parallel"),
        name="ffn_ple",
    )(x2d, p3d, wg, wu, wd, ln_g.reshape(1, d), ln_b.reshape(1, d), ple_w, gate_w, gate_b.reshape(1, d))


def _moe_combine_ln_ple_kernel(x_ref, y0_ref, y1_ref, rw_ref, p_ref, lng_ref, lnb_ref, plew_ref, gw_ref,
                               gb_ref, o_ref):
    rw = rw_ref[...]
    ffn = y0_ref[...] * rw[:, 0:1] + y1_ref[...] * rw[:, 1:2]
    xn = _layer_norm(DEEPNORM_ALPHA * x_ref[...] + ffn, lng_ref[...], lnb_ref[...])
    o_ref[...] = _ple(xn, p_ref[0], plew_ref, gw_ref, gb_ref)


def _moe_combine_ln_ple(x2d, y0, y1, route_w, p3d, layer, ln_g, ln_b, ple_w, gate_w, gate_b, tm=512):
    m, d = x2d.shape
    row = lambda i: (i, 0)
    tile = pl.BlockSpec((tm, d), row)
    return pl.pallas_call(
        _moe_combine_ln_ple_kernel,
        grid=(m // tm,),
        in_specs=[tile, tile, tile, pl.BlockSpec((tm, TOP_K), row), _ple_block(p3d, layer, tm),
                  _resident((1, d)), _resident((1, d)), _resident(ple_w.shape), _resident(gate_w.shape),
                  _resident((1, d))],
        out_specs=tile,
        out_shape=jax.ShapeDtypeStruct((m, d), F32),
        compiler_params=_cparams("parallel"),
        name="moe_combine_ln_ple",
    )(x2d, y0, y1, route_w, p3d, ln_g.reshape(1, d), ln_b.reshape(1, d), ple_w, gate_w, gate_b.reshape(1, d))


def _swap_half_cols(w):
    k, n = w.shape
    return w.reshape(k, n // HEAD_DIM, 2, HEAD_DIM // 2)[:, :, ::-1, :].reshape(k, n)


def _rope_tables(seq):
    half = HEAD_DIM // 2
    inv = ROPE_THETA ** (-jnp.arange(half, dtype=F32) / half)
    ang = jnp.arange(seq, dtype=F32)[:, None] * inv[None, :]
    cos, sin = jnp.cos(ang), jnp.sin(ang)
    return jnp.concatenate([cos, cos], -1), jnp.concatenate([-sin, sin], -1)


def _nsa_kv_kernel(x_ref, wk_ref, wvt_ref, cos_ref, sin_ref, zc_ref, vc_ref, ks_ref, kw_ref, vst_ref,
                   vwt_ref):
    G, dh = N_KV_GROUPS, HEAD_DIM
    gw = G * dh
    xb = x_ref[...].astype(BF16)
    res = _dot(xb, wk_ref[...])
    res_t = _dot_nt(wvt_ref[...], xb)
    tm = res.shape[0]
    cos = cos_ref[...]
    sin = sin_ref[...]
    ks = res[:, 2 * gw:3 * gw] * cos + res[:, 3 * gw:4 * gw] * sin
    kw = res[:, 4 * gw:5 * gw] * cos + res[:, 5 * gw:6 * gw] * sin
    t_glob = pl.program_id(1) * tm + lax.broadcasted_iota(jnp.int32, (tm, dh), 0)
    onehot = jnp.where(lax.broadcasted_iota(jnp.int32, (tm, dh), 1) == t_glob // SEL_BLOCK, 1.0, 0.0)
    for g in range(G):
        sl = slice(g * dh, (g + 1) * dh)
        zc_ref[0, g] = res[:, sl].astype(BF16)
        vc_ref[0, g] = res[:, gw + g * dh:gw + (g + 1) * dh].astype(BF16)
        ks_ref[0, g] = jnp.concatenate([ks[:, sl], onehot], axis=1).astype(BF16)
        kw_ref[0, g] = kw[:, sl].astype(BF16)
    vst_ref[0] = res_t[:gw].astype(BF16)
    vwt_ref[0] = res_t[gw:].astype(BF16)


def _nsa_kv(x2d, w_kv, batch, seq, tm=256):
    m, d = x2d.shape
    G, dh = N_KV_GROUPS, HEAD_DIM
    gw = G * dh
    parts = [w_kv[:, j * gw:(j + 1) * gw] for j in range(6)]
    w_k = jnp.concatenate([parts[0], parts[1], parts[2], _swap_half_cols(parts[2]),
                           parts[4], _swap_half_cols(parts[4])], axis=1).astype(BF16)
    w_vt = jnp.concatenate([parts[3], parts[5]], axis=1).T.astype(BF16)
    cos, sin = _rope_tables(seq)
    cos = jnp.tile(cos, (1, G))
    sin = jnp.tile(sin, (1, G))
    nt = seq // tm
    o64 = jax.ShapeDtypeStruct((batch, G, seq, dh), BF16)
    o128 = jax.ShapeDtypeStruct((batch, G, seq, 2 * dh), BF16)
    ot = jax.ShapeDtypeStruct((batch, gw, seq), BF16)
    b64 = pl.BlockSpec((1, G, tm, dh), lambda b, t: (b, 0, t, 0))
    b128 = pl.BlockSpec((1, G, tm, 2 * dh), lambda b, t: (b, 0, t, 0))
    bt = pl.BlockSpec((1, gw, tm), lambda b, t: (b, 0, t))
    return pl.pallas_call(
        _nsa_kv_kernel,
        grid=(batch, nt),
        in_specs=[pl.BlockSpec((tm, d), lambda b, t: (b * nt + t, 0)), _resident(w_k.shape),
                  _resident(w_vt.shape),
                  pl.BlockSpec((tm, gw), lambda b, t: (t, 0)), pl.BlockSpec((tm, gw), lambda b, t: (t, 0))],
        out_specs=[b64, b64, b128, b64, bt, bt],
        out_shape=[o64, o64, o128, o64, ot, ot],
        compiler_params=_cparams("parallel", "parallel"),
        name="nsa_kv",
    )(x2d, w_k, w_vt, cos, sin)


def _nsa_cmp_kernel(z_ref, pos_ref, w1_ref, b1_ref, w2_ref, o_ref, *, slabs_per_seq):
    half = w1_ref.shape[1] // 2
    z = z_ref[0]
    tm = z.shape[0]
    first = _dot(z, w1_ref[0, :half, :])
    second = _dot(z, w1_ref[0, half:, :])
    const = _dot(pos_ref[0], w1_ref[0])[0:1, :] + b1_ref[0]
    hid = first + pltpu.roll(second, shift=tm - 1, axis=0) + const
    act = 0.5 * hid * (1.0 + jnp.tanh(math.sqrt(2.0 / math.pi) * (hid + 0.044715 * hid * hid * hid)))
    out = _dot(act.astype(BF16), w2_ref[0])
    row = lax.broadcasted_iota(jnp.int32, out.shape, 0)
    o_ref[0] = jnp.where(row % slabs_per_seq == slabs_per_seq - 1, 0.0, out).astype(BF16)


def _nsa_compress(zc, vc, cmp_pos, phi_w1, phi_b1, phi_w2, tm=512):
    batch, G, seq, dh = zc.shape
    slab = CMP_STRIDE * dh
    rows = batch * G * seq // CMP_STRIDE
    tm = min(tm, rows)
    z = jnp.stack([zc.reshape(rows, slab), vc.reshape(rows, slab)])
    pos = jnp.broadcast_to(cmp_pos.reshape(2, 1, CMP_LEN * dh), (2, 8, CMP_LEN * dh)).astype(BF16)
    out = pl.pallas_call(
        functools.partial(_nsa_cmp_kernel, slabs_per_seq=seq // CMP_STRIDE),
        grid=(2, rows // tm),
        in_specs=[pl.BlockSpec((1, tm, slab), lambda j, i: (j, i, 0)),
                  pl.BlockSpec((1, 8, CMP_LEN * dh), lambda j, i: (j, 0, 0)),
                  pl.BlockSpec((1, CMP_LEN * dh, CMP_HIDDEN), lambda j, i: (j, 0, 0)),
                  pl.BlockSpec((1, 1, CMP_HIDDEN), lambda j, i: (j, 0, 0)),
                  pl.BlockSpec((1, CMP_HIDDEN, dh), lambda j, i: (j, 0, 0))],
        out_specs=pl.BlockSpec((1, tm, dh), lambda j, i: (j, i, 0)),
        out_shape=jax.ShapeDtypeStruct((2, rows, dh), BF16),
        compiler_params=_cparams("parallel", "parallel"),
        name="nsa_compress",
    )(z, pos, phi_w1.astype(BF16), phi_b1.reshape(2, 1, CMP_HIDDEN), phi_w2.astype(BF16))
    n_slab = seq // CMP_STRIDE
    return out[0].reshape(batch, G, n_slab, dh), out[1].reshape(batch, G, n_slab, dh)


GATE_ROWS = 16


def _nsa_q_kernel(x_ref, wt_ref, cos_ref, sin_ref, q_ref, qr_ref, gate_ref):
    dh = HEAD_DIM
    hw = N_HEADS * dh
    scale = HEAD_DIM ** -0.5 * math.log2(math.e)
    res_t = _dot_nt(wt_ref[...], x_ref[...].astype(BF16))
    tm = res_t.shape[1]
    q = res_t[:hw].reshape(N_HEADS, dh, tm)
    q_sw = res_t[hw:2 * hw].reshape(N_HEADS, dh, tm)
    qr = q * cos_ref[...][None] + q_sw * sin_ref[...][None]
    q_ref[0] = (q * scale).reshape(hw, tm).astype(BF16)
    qr_ref[0] = (qr * scale).reshape(hw, tm).astype(BF16)
    gate_ref[0] = _sigmoid(res_t[2 * hw:])


def _nsa_q(x2d, w_in, batch, seq, tm=256):
    m, d = x2d.shape
    G, R, dh = N_KV_GROUPS, Q_PER_GROUP, HEAD_DIM
    hw = N_HEADS * dh
    w_gate = w_in[:, hw:].reshape(d, G, R * N_BRANCH)
    w_gate = jnp.pad(w_gate, ((0, 0), (0, 0), (0, GATE_ROWS - R * N_BRANCH))).reshape(d, G * GATE_ROWS)
    wt = jnp.concatenate([w_in[:, :hw], _swap_half_cols(w_in[:, :hw]), w_gate], axis=1).T.astype(BF16)
    cos, sin = _rope_tables(seq)
    nt = seq // tm
    oq = jax.ShapeDtypeStruct((batch, hw, seq), BF16)
    bq = pl.BlockSpec((1, hw, tm), lambda b, t: (b, 0, t))
    tab = pl.BlockSpec((dh, tm), lambda b, t: (0, t))
    return pl.pallas_call(
        _nsa_q_kernel,
        grid=(batch, nt),
        in_specs=[pl.BlockSpec((tm, d), lambda b, t: (b * nt + t, 0)), _resident(wt.shape), tab, tab],
        out_specs=[bq, bq, pl.BlockSpec((1, G * GATE_ROWS, tm), lambda b, t: (b, 0, t))],
        out_shape=[oq, oq, jax.ShapeDtypeStruct((batch, G * GATE_ROWS, seq), F32)],
        compiler_params=_cparams("parallel", "parallel"),
        name="nsa_q",
    )(x2d, wt, cos.T, sin.T)


def _nsa_attn_kernel(q_ref, qr_ref, gate_ref, kc_ref, vct_ref, ks_ref, vst_ref, kw_ref, vwt_ref, ovlt_ref,
                     o_ref, *, tq, tk, n_blk):
    R, dh = Q_PER_GROUP, HEAD_DIM
    cols = R * tq
    qi = pl.program_id(2)
    t0 = qi * tq
    n_blk_pad = ovlt_ref.shape[0]

    def heads_to_lanes(x):
        return jnp.concatenate([x[r * dh:(r + 1) * dh] for r in range(R)], axis=1)

    q = heads_to_lanes(q_ref[0])
    qr = heads_to_lanes(qr_ref[0])

    def col_pos(shape):
        return t0 + (lax.broadcasted_iota(jnp.int32, shape, 1) & (tq - 1))

    def key_tile(j):
        return pl.ds(pl.multiple_of(j * tk, tk), tk)

    ct = tq
    col_tiles = [slice(c, c + ct) for c in range(0, cols, ct)]
    ones_rows = 16
    ones_blk = jnp.ones((ones_rows, tk), BF16)

    def tile_scores(k_ref, q_op, tiles):
        return [[_dot(k_ref[0, 0, key_tile(j), :], q_op[:, cs]) for j, _ in tiles] for cs in col_tiles]

    def attend(carry, sc, vt_ref, tiles):
        m_i, acc = carry
        outs = []
        for cs, sc_c in zip(col_tiles, sc):
            masked = [s_j if mask_fn is None else jnp.where(mask_fn(k_off, q_off), s_j, NEG_INF)
                      for s_j, (_, mask_fn) in zip(sc_c, tiles)]
            m_new = m_i[:, cs]
            for s_j in masked:
                m_new = jnp.maximum(m_new, jnp.max(s_j, axis=0, keepdims=True))
            acc_new = jnp.exp2(m_i[:, cs] - m_new) * acc[:, cs]
            for s_j, (j, _) in zip(masked, tiles):
                v_ext = jnp.concatenate([vt_ref[0, :, key_tile(j)], ones_blk], axis=0)
                acc_new = acc_new + _dot(v_ext, jnp.exp2(s_j - m_new).astype(BF16))
            outs.append((m_new, acc_new))
        return tuple(jnp.concatenate([o[i] for o in outs], axis=1) for i in range(2))

    def normalised(acc):
        return acc[:dh] / acc[dh:dh + 1]

    init = (jnp.full((1, cols), NEG_INF, F32), jnp.zeros((dh + ones_rows, cols), F32))

    k_off = lax.broadcasted_iota(jnp.int32, (tk, ct), 0)
    q_off = lax.broadcasted_iota(jnp.int32, (tk, ct), 1)
    causal = lambda k, q_: k <= q_
    present = lambda cond: (lambda k, q_: (k >= 0) & cond)

    win_tiles = [(jnp.maximum(qi - 2, 0), lambda k, q_: (q_ < k) & (qi >= 2)),
                 (jnp.maximum(qi - 1, 0), present(qi >= 1)),
                 (qi, causal)]
    sc_win = tile_scores(kw_ref, qr, win_tiles)

    s = _dot(kc_ref[0, 0], q)
    cmp_end = lax.broadcasted_iota(jnp.int32, s.shape, 0) * CMP_STRIDE + (CMP_LEN - 1)
    valid = cmp_end <= col_pos(s.shape)
    sm = jnp.where(valid, s, NEG_INF)
    e = jnp.where(valid, jnp.exp2(sm - jnp.max(sm, axis=0, keepdims=True)), 0.0)
    l = jnp.sum(e, axis=0, keepdims=True)
    p_cmp = e / jnp.where(l > 0.0, l, 1.0)
    o_cmp = _dot(vct_ref[0, 0], p_cmp.astype(BF16))

    p_sum = p_cmp[:, 0:tq]
    for r in range(1, R):
        p_sum = p_sum + p_cmp[:, r * tq:(r + 1) * tq]
    hi, lo = _split2(p_sum)
    imp = _dot(ovlt_ref[...], hi) + _dot(ovlt_ref[...], lo)
    blk = lax.broadcasted_iota(jnp.int32, imp.shape, 0)
    t_q = t0 + lax.broadcasted_iota(jnp.int32, imp.shape, 1)
    cur = t_q // SEL_BLOCK
    forced = (blk == 0) | (blk == cur) | (blk == cur - 1)
    score = jnp.where(forced, FORCE_SCORE, jnp.where(blk * SEL_BLOCK <= t_q, imp, NEG_INF))
    score = jnp.where(blk < n_blk, score, -jnp.inf)
    selected = blk >= n_blk
    for _ in range(N_SEL):
        best = jnp.max(score, axis=0, keepdims=True)
        first = jnp.min(jnp.where(score == best, blk, 2 * LANES), axis=0, keepdims=True)
        pick = blk == first
        selected = selected | pick
        score = jnp.where(pick, -jnp.inf, score)
    bias = jnp.where(selected, 0.0, NEG_INF)
    qa = jnp.concatenate([qr, jnp.concatenate([bias] * R, axis=1).astype(BF16),
                          jnp.zeros((dh - n_blk_pad, cols), BF16)], axis=0)

    o_win = normalised(attend(init, sc_win, vwt_ref, win_tiles)[1])

    n_pairs = qi // 2

    def pair_body(i, carry):
        tiles = [(2 * i, None), (2 * i + 1, None)]
        return attend(carry, tile_scores(ks_ref, qa, tiles), vst_ref, tiles)

    carry = lax.fori_loop(0, n_pairs, pair_body, init)
    j_odd = 2 * n_pairs
    tail_tiles = [(j_odd, present(j_odd < qi)), (qi, causal)]
    o_slc = normalised(attend(carry, tile_scores(ks_ref, qa, tail_tiles), vst_ref, tail_tiles)[1])

    gates = gate_ref[0]
    outs = []
    for r in range(R):
        cs = slice(r * tq, (r + 1) * tq)
        outs.append(gates[3 * r:3 * r + 1] * o_cmp[:, cs] + gates[3 * r + 1:3 * r + 2] * o_slc[:, cs]
                    + gates[3 * r + 2:3 * r + 3] * o_win[:, cs])
    halves = [jnp.transpose(jnp.concatenate(outs[i:i + 2], axis=0)) for i in range(0, R, 2)]
    o_ref[0] = jnp.concatenate(halves, axis=1)


SEL_ROWS = 32


def _nsa_attention(q, qr, gates, k_cmp, v_cmp, ks, vst, kw, vwt, tq=256, tk=256):
    batch, _, seq = q.shape
    G, R, dh = N_KV_GROUPS, Q_PER_GROUP, HEAD_DIM
    n_slab = k_cmp.shape[2]
    n_blk = seq // SEL_BLOCK
    assert n_blk <= SEL_ROWS
    assert tq == tk and WINDOW == 2 * tk, "the window branch is written as exactly three key tiles"
    cmp_start = jnp.arange(n_slab) * CMP_STRIDE
    blk_start = jnp.arange(SEL_ROWS) * SEL_BLOCK
    overlap_t = ((cmp_start[None, :] < blk_start[:, None] + SEL_BLOCK)
                 & (cmp_start[None, :] + CMP_LEN - 1 >= blk_start[:, None])
                 & (jnp.arange(SEL_ROWS)[:, None] < n_blk)
                 & (jnp.arange(n_slab)[None, :] < (seq - CMP_LEN) // CMP_STRIDE + 1)).astype(BF16)
    v_cmp_t = jnp.swapaxes(v_cmp, 2, 3)
    qspec = pl.BlockSpec((1, R * dh, tq), lambda b, g, i: (b, g, i))
    full = lambda a: pl.BlockSpec((1, 1) + a.shape[2:], lambda b, g, i: (b, g, 0, 0))
    vspec = pl.BlockSpec((1, dh, seq), lambda b, g, i: (b, g, 0))
    return pl.pallas_call(
        functools.partial(_nsa_attn_kernel, tq=tq, tk=tk, n_blk=n_blk),
        grid=(batch, G, seq // tq),
        in_specs=[qspec, qspec, pl.BlockSpec((1, GATE_ROWS, tq), lambda b, g, i: (b, g, i)),
                  full(k_cmp), full(v_cmp_t), full(ks), vspec, full(kw), vspec,
                  _resident(overlap_t.shape)],
        out_specs=pl.BlockSpec((1, tq, R * dh), lambda b, g, i: (b, i, g)),
        out_shape=jax.ShapeDtypeStruct((batch, seq, G * R * dh), F32),
        compiler_params=_cparams("parallel", "parallel", "parallel"),
        name="nsa_attn",
    )(q, qr, gates, k_cmp, v_cmp_t, ks, vst, kw, vwt, overlap_t)


def _moe_kernel(be_ref, nb_ref, x_ref, wg_ref, wu_ref, wd_ref, o_ref):
    i = pl.program_id(0)

    @pl.when(i < nb_ref[0])
    def _():
        xb = x_ref[...].astype(BF16)
        gt = _dot(xb, wg_ref[0])
        up = _dot(xb, wu_ref[0])
        h = (gt * _sigmoid(gt) * up).astype(BF16)
        o_ref[...] = _dot(h, wd_ref[0])

    @pl.when(i >= nb_ref[0])
    def _():
        o_ref[...] = jnp.zeros_like(o_ref)


def _moe_experts(xs, block_e, n_used, wg, wu, wd):
    n_rows, d = xs.shape
    n_blocks = n_rows // MOE_ROWS
    ff = wg.shape[2]
    grid_spec = pltpu.PrefetchScalarGridSpec(
        num_scalar_prefetch=2,
        grid=(n_blocks,),
        in_specs=[pl.BlockSpec((MOE_ROWS, d), lambda i, be, nb: (i, 0)),
                  pl.BlockSpec((1, d, ff), lambda i, be, nb: (be[i], 0, 0)),
                  pl.BlockSpec((1, d, ff), lambda i, be, nb: (be[i], 0, 0)),
                  pl.BlockSpec((1, ff, d), lambda i, be, nb: (be[i], 0, 0))],
        out_specs=pl.BlockSpec((MOE_ROWS, d), lambda i, be, nb: (i, 0)),
    )
    return pl.pallas_call(
        _moe_kernel,
        grid_spec=grid_spec,
        out_shape=jax.ShapeDtypeStruct((n_rows, d), F32),
        compiler_params=_cparams("arbitrary"),
        name="moe_experts",
    )(block_e, n_used, xs, wg, wu, wd)


def _moe(x2d, logits, w_gu, w_down):
    n_tok, d = x2d.shape
    n_assign = n_tok * TOP_K
    top_logit, top_e = lax.top_k(logits, TOP_K)
    weights = jax.nn.softmax(top_logit, axis=-1)
    flat_e = top_e.reshape(-1).astype(jnp.int32)
    onehot = (flat_e[:, None] == jnp.arange(N_EXPERTS, dtype=jnp.int32)[None, :]).astype(jnp.int32)
    running = jnp.cumsum(onehot, axis=0)
    counts = running[-1]
    padded = (counts + MOE_ROWS - 1) // MOE_ROWS * MOE_ROWS
    pad_end = jnp.cumsum(padded)
    pad_start = pad_end - padded
    grp_start = jnp.cumsum(counts) - counts
    pos = jnp.sum(onehot * (running - 1 + pad_start[None, :]), axis=1).reshape(n_tok, TOP_K)
    n_blocks = -(-n_assign // MOE_ROWS) + N_EXPERTS
    block_e = jnp.minimum(jnp.searchsorted(pad_end, jnp.arange(n_blocks) * MOE_ROWS, side='right'),
                          N_EXPERTS - 1).astype(jnp.int32)
    n_used = (pad_end[-1] // MOE_ROWS).astype(jnp.int32).reshape(1)
    order = jnp.argsort(flat_e)
    tok_sorted = (order // TOP_K).astype(jnp.int32)
    row_in_grp = (jnp.arange(n_blocks, dtype=jnp.int32) * MOE_ROWS - pad_start[block_e])[:, None] \
        + jnp.arange(MOE_ROWS, dtype=jnp.int32)[None, :]
    src = jnp.clip(grp_start[block_e][:, None] + row_in_grp, 0, n_assign - 1)
    row_tok = jnp.where(row_in_grp < counts[block_e][:, None], tok_sorted[src], 0).reshape(-1)
    xs = x2d[row_tok]
    ys = _moe_experts(xs, block_e, n_used, w_gu[:, :, :D_FF_EXPERT].astype(BF16),
                      w_gu[:, :, D_FF_EXPERT:].astype(BF16), w_down.astype(BF16))
    return [ys[pos[:, s]] for s in range(TOP_K)], weights


def kernel(x, p, a_mu, a_w_rkv, a_w0, a_w1, a_w2, a_a0, a_a1, a_a2, a_g1, a_g2, a_k_k, a_k_a, a_r_k,
           a_gn_g, a_gn_b, a_w_o, b_w_kv, b_cmp_pos, b_phi_w1, b_phi_b1, b_phi_w2, b_w_in, b_w_o,
           f_w_gu, f_w_down, m_w_router, m_b_router, m_w_gu, m_w_down, ln_g, ln_b, ple_w, ple_gate_w,
           ple_gate_b):
    batch, seq, d = x.shape
    m = batch * seq
    x0 = x.reshape(m, d)
    p2d = p.reshape(DEPTH, m, PLE_DIM)
    bf = lambda w: w.astype(BF16)

    x1 = _rwkv_layer(x0, batch, seq, a_mu[0], bf(a_w_rkv[0]), a_w0[0], bf(a_w1[0]), bf(a_w2[0]), a_a0[0],
                     bf(a_a1[0]), bf(a_a2[0]), bf(a_g1[0]), bf(a_g2[0]), a_k_k[0], a_k_a[0], a_r_k[0],
                     a_gn_g[0], a_gn_b[0], bf(a_w_o[0]), ln_g[0, 0], ln_b[0, 0])
    x2 = _ffn_ple(x1, p2d, 0, bf(f_w_gu[0][:, :D_FF]), bf(f_w_gu[0][:, D_FF:]), bf(f_w_down[0]),
                  ln_g[0, 1], ln_b[0, 1], bf(ple_w[0]), bf(ple_gate_w[0]), ple_gate_b[0])

    zc, vc, ks, kw, vst, vwt = _nsa_kv(x2, b_w_kv, batch, seq)
    k_cmp, v_cmp = _nsa_compress(zc, vc, b_cmp_pos, b_phi_w1, b_phi_b1, b_phi_w2)
    q, qr, gates = _nsa_q(x2, b_w_in[0], batch, seq)
    attn = _nsa_attention(q, qr, gates, k_cmp, v_cmp, ks, vst, kw, vwt).reshape(m, d)
    x3, logits = _proj_ln_router(attn, x2, bf(b_w_o[0]), ln_g[1, 0], ln_b[1, 0], m_w_router[0],
                                 m_b_router[0])
    (y0, y1), route_w = _moe(x3, logits[:, :N_EXPERTS], m_w_gu[0], m_w_down[0])
    out = _moe_combine_ln_ple(x3, y0, y1, route_w, p2d, 1, ln_g[1, 1], ln_b[1, 1], bf(ple_w[1]),
                              bf(ple_gate_w[1]), ple_gate_b[1])
    return out.reshape(batch, seq, d)
```

```python
import functools
import math

import jax
import jax.numpy as jnp
from jax import lax
from jax.experimental import pallas as pl
from jax.experimental.pallas import tpu as pltpu

BF16 = jnp.bfloat16
F32 = jnp.float32

LANES = 128
VMEM_LIMIT_BYTES = 56 * 1024 * 1024

D_MODEL = 1024
PLE_DIM = 256
RWKV_HEAD = 64
GN_EPS = 64e-5
N_HEADS = 16
HEAD_DIM = 64
N_KV_GROUPS = 4
Q_PER_GROUP = 4
N_BRANCH = 3
CMP_LEN = 32
CMP_STRIDE = 16
CMP_HIDDEN = 256
SEL_BLOCK = 64
N_SEL = 8
WINDOW = 512
ROPE_THETA = 10000.0
D_FF = 2816
N_EXPERTS = 8
TOP_K = 2
D_FF_EXPERT = 1408
MOE_ROWS = 256
LN_EPS = 1e-5
DEPTH = 2
DEEPNORM_ALPHA = (2.0 * DEPTH) ** 0.25
NEG_INF = -1e30
FORCE_SCORE = 1e4

SCAN_CHUNK = 64


def _cparams(*sem):
    return pltpu.CompilerParams(dimension_semantics=sem, vmem_limit_bytes=VMEM_LIMIT_BYTES)


def _resident(shape):
    nd = len(shape)
    return pl.BlockSpec(shape, lambda *_: (0,) * nd, pipeline_mode=pl.Buffered(1))


def _dot(a, b):
    return jnp.dot(a, b, preferred_element_type=F32)


def _dot_nt(a, b):
    return lax.dot_general(a, b, (((1,), (1,)), ((), ())), preferred_element_type=F32)


def _split2(x):
    hi = x.astype(BF16)
    lo = (x - hi.astype(F32)).astype(BF16)
    return hi, lo


def _split3(x):
    hi = x.astype(BF16)
    r1 = x - hi.astype(F32)
    mid = r1.astype(BF16)
    lo = (r1 - mid.astype(F32)).astype(BF16)
    return hi, mid, lo


def _layer_norm(y, g, b):
    mu = jnp.mean(y, axis=-1, keepdims=True)
    yc = y - mu
    var = jnp.mean(yc * yc, axis=-1, keepdims=True)
    return yc * lax.rsqrt(var + LN_EPS) * g + b


def _sigmoid(z):
    return 1.0 / (1.0 + jnp.exp(-z))


def _rwkv_project(first, x_ref, xh_ref, mu_ref, wrkv_ref, w0_ref, w1_ref, w2_ref, a0_ref, a1_ref,
                  a2_ref, g1_ref, g2_ref, r_ref, k_ref, v_ref, dl_ref, a_ref, g_ref):
    x = x_ref[...]
    prev_row = jnp.where(first, 0.0, xh_ref[0, 7:8, :])
    row = lax.broadcasted_iota(jnp.int32, x.shape, 0)
    x_shift = jnp.where(row == 0, prev_row, pltpu.roll(x, shift=1, axis=0))
    xx = x_shift - x

    def mix(i):
        return (x + xx * mu_ref[i:i + 1, :]).astype(BF16)

    r_ref[...] = _dot(mix(0), wrkv_ref[0])
    k_ref[...] = _dot(mix(1), wrkv_ref[1])
    v_ref[...] = _dot(mix(2), wrkv_ref[2])
    z = w0_ref[...] + _dot(jnp.tanh(_dot(mix(3), w1_ref[...])).astype(BF16), w2_ref[...])
    dl_ref[...] = -math.exp(-0.5) * _sigmoid(z)
    a_ref[...] = _sigmoid(a0_ref[...] + _dot(_dot(mix(4), a1_ref[...]).astype(BF16), a2_ref[...]))
    g_ref[...] = _dot(_sigmoid(_dot(mix(5), g1_ref[...])).astype(BF16), g2_ref[...])


def _rwkv_recurrence(first, r_ref, k_ref, v_ref, dl_ref, a_ref, g_ref, kk_ref, ka_ref, rk_ref, gng_ref,
                     gnb_ref, o_ref, state_ref, kkn_ref, cum_ref, *, n_pairs, n_chunks):
    C = SCAN_CHUNK
    N = RWKV_HEAD
    W = 2 * N
    tb = n_chunks * C

    @pl.when(first)
    def _():
        state_ref[...] = jnp.zeros_like(state_ref)

    tt = lax.broadcasted_iota(jnp.int32, (C, W), 0)
    ln = lax.broadcasted_iota(jnp.int32, (C, W), 1)
    ss = ln & (N - 1)
    strict_lower = ss < tt
    lower = ss <= tt
    eye = jnp.where(ss == tt, 1.0, 0.0)
    level_masks = []
    for lg in range(int(math.log2(C))):
        level_masks.append(((tt >> (lg + 1)) == (ss >> (lg + 1))) & ((tt >> lg) == (ss >> lg) + 1))
    rr = lax.broadcasted_iota(jnp.int32, (W, W), 0)
    cc = lax.broadcasted_iota(jnp.int32, (W, W), 1)
    same_head = (rr < N) == (cc < N)
    lane_head0 = ln < N
    WG = 2 * W
    lg_n = int(math.log2(N))
    ones_grp = jnp.where((lax.broadcasted_iota(jnp.int32, (WG, WG), 0) >> lg_n)
                         == (lax.broadcasted_iota(jnp.int32, (WG, WG), 1) >> lg_n), 1.0, 0.0).astype(BF16)
    col_groups = [slice(c, c + WG) for c in range(0, n_pairs * W, WG)]

    def bd(y):
        return jnp.concatenate([jnp.where(lane_head0, y, 0.0), jnp.where(lane_head0, 0.0, y)],
                               axis=0).astype(BF16)

    def pmm(xp, ybd):
        return _dot(xp.astype(BF16), ybd)

    def head_sum(xg):
        return _dot(xg.astype(BF16), ones_grp)

    tri_r = lax.broadcasted_iota(jnp.int32, (tb, tb), 0)
    tri_c = lax.broadcasted_iota(jnp.int32, (tb, tb), 1)
    lg_c = int(math.log2(C))
    tri_chunk = jnp.where((tri_c <= tri_r) & ((tri_c >> lg_c) == (tri_r >> lg_c)), 1.0, 0.0).astype(BF16)
    d_hi, d_mid, d_lo = _split3(dl_ref[...])
    cum_ref[...] = _dot(tri_chunk, d_hi) + _dot(tri_chunk, d_mid) + _dot(tri_chunk, d_lo)
    for cols in col_groups:
        kk_raw = k_ref[:, cols] * kk_ref[:, cols]
        kkn_ref[:, cols] = kk_raw / jnp.maximum(jnp.sqrt(head_sum(kk_raw * kk_raw)), 1e-12)

    def chunk_load(ci, pi):
        rows = pl.ds(pl.multiple_of(ci * C, C), C)
        cols = slice(pi * W, (pi + 1) * W)
        return (r_ref[rows, cols], k_ref[rows, cols], v_ref[rows, cols], dl_ref[rows, cols],
                a_ref[rows, cols], kkn_ref[rows, cols], cum_ref[rows, cols], state_ref[pi])

    def chunk_compute(pi, r, k, v, dl, a, kk, cum, m0):
        cols = slice(pi * W, (pi + 1) * W)
        k2 = k * (1.0 + (a - 1.0) * ka_ref[:, cols])
        b = a * kk
        p_incl = jnp.exp(cum)
        p_excl = jnp.exp(cum - dl)
        p_inv = jnp.exp(-cum)
        kq = kk * p_excl
        rq = r * p_incl
        bk = b * p_inv
        kkd = k2 * p_inv

        bkt = jnp.transpose(jnp.concatenate([b, k2], axis=0))
        cumt = jnp.transpose(jnp.concatenate([cum, cum], axis=0))
        last_col = cumt[:, C - 1:C]
        lhs_t = (bkt * jnp.exp(last_col - cumt)).astype(BF16)
        m_decayed = m0 * jnp.exp(last_col)

        lhs = jnp.concatenate([kq, rq], axis=0).astype(BF16)
        yield
        ab = _dot_nt(lhs, bd(bk))
        gb = _dot_nt(lhs, bd(kkd))
        a_m = jnp.where(strict_lower, ab[:C], 0.0)
        aq_m = jnp.where(lower, ab[C:], 0.0)
        g_m = jnp.where(strict_lower, gb[:C], 0.0)
        gq_m = jnp.where(lower, gb[C:], 0.0)

        tinv = eye - jnp.where(level_masks[0], a_m, 0.0)
        yield
        gv = pmm(jnp.concatenate([g_m, gq_m], axis=0), bd(v))
        for lm in level_masks[1:]:
            w_ = pmm(jnp.where(lm, a_m, 0.0), bd(tinv))
            yield
            tinv = tinv - pmm(tinv, bd(w_))
            yield

        kq_y = pmm(tinv, jnp.concatenate([bd(kq), bd(gv[:C])], axis=1))
        kq1 = kq_y[:, :W]
        y = kq_y[:, W:]
        yield
        aq_ky = pmm(aq_m, jnp.concatenate([bd(kq1), bd(y)], axis=1))
        rq1 = rq - aq_ky[:, :W]
        o_loc = gv[C:] - aq_ky[:, W:]
        yield
        st = _dot(jnp.concatenate([rq1, kq1], axis=0).astype(BF16), m0.astype(BF16))
        o = st[:C] + o_loc
        z = -(st[C:] + y)
        yield
        upd = _dot(lhs_t, jnp.concatenate([z, v], axis=0).astype(BF16))
        return o, m_decayed + jnp.where(same_head, upd, 0.0)

    def interleave(gens):
        results = [None] * len(gens)
        live = list(range(len(gens)))
        while live:
            for i in list(live):
                try:
                    next(gens[i])
                except StopIteration as done:
                    results[i] = done.value
                    live.remove(i)
        return results

    def body(ci, carry):
        loaded = [chunk_load(ci, pi) for pi in range(n_pairs)]
        results = interleave([chunk_compute(pi, *loaded[pi]) for pi in range(n_pairs)])
        rows = pl.ds(pl.multiple_of(ci * C, C), C)
        for pi, (o, m_new) in enumerate(results):
            o_ref[rows, pi * W:(pi + 1) * W] = o
            state_ref[pi] = m_new
        return carry

    lax.fori_loop(0, n_chunks, body, 0)

    for cols in col_groups:
        o = o_ref[:, cols]
        r = r_ref[:, cols]
        v = v_ref[:, cols]
        k2 = k_ref[:, cols] * (1.0 + (a_ref[:, cols] - 1.0) * ka_ref[:, cols])
        oc = o - head_sum(o) * (1.0 / N)
        var_o = head_sum(oc * oc) * (1.0 / N)
        out = oc * lax.rsqrt(var_o + GN_EPS) * gng_ref[:, cols] + gnb_ref[:, cols]
        out = out + head_sum(r * k2 * rk_ref[:, cols]) * v
        o_ref[:, cols] = out * g_ref[:, cols]


def _rwkv_layer_kernel(x_ref, xh_ref, mu_ref, wrkv_ref, w0_ref, w1_ref, w2_ref, a0_ref, a1_ref, a2_ref,
                       g1_ref, g2_ref, kk_ref, ka_ref, rk_ref, gng_ref, gnb_ref, wo_ref, lng_ref, lnb_ref,
                       o_ref, state_ref, r_s, k_s, v_s, dl_s, a_s, g_s, kkn_s, cum_s, y_s, *, n_chunks):
    first = pl.program_id(1) == 0
    _rwkv_project(first, x_ref, xh_ref, mu_ref, wrkv_ref, w0_ref, w1_ref, w2_ref, a0_ref, a1_ref, a2_ref,
                  g1_ref, g2_ref, r_s, k_s, v_s, dl_s, a_s, g_s)
    _rwkv_recurrence(first, r_s, k_s, v_s, dl_s, a_s, g_s, kk_ref, ka_ref, rk_ref, gng_ref, gnb_ref, y_s,
                     state_ref, kkn_s, cum_s, n_pairs=x_ref.shape[1] // (2 * RWKV_HEAD), n_chunks=n_chunks)
    mix = _dot(y_s[...].astype(BF16), wo_ref[...])
    o_ref[...] = _layer_norm(DEEPNORM_ALPHA * x_ref[...] + mix, lng_ref[...], lnb_ref[...])


def _rwkv_layer(x2d, batch, seq, mu, w_rkv, w0, w1, w2, a0, a1, a2, g1, g2, k_k, k_a, r_k, gn_g, gn_b,
                w_o, ln_g, ln_b, tb=256):
    m, d = x2d.shape
    nt = seq // tb
    xh = x2d.reshape(m // 8, 8, d)
    vec = lambda z: z.reshape(1, d)
    blk = pl.BlockSpec((tb, d), lambda b, t: (b * nt + t, 0))
    weights = [mu, w_rkv, vec(w0), w1, w2, vec(a0), a1, a2, g1, g2, vec(k_k), vec(k_a), vec(r_k),
               vec(gn_g), vec(gn_b), w_o, vec(ln_g), vec(ln_b)]
    n_pairs = d // (2 * RWKV_HEAD)
    return pl.pallas_call(
        functools.partial(_rwkv_layer_kernel, n_chunks=tb // SCAN_CHUNK),
        grid=(batch, nt),
        in_specs=[blk, pl.BlockSpec((1, 8, d), lambda b, t: (jnp.maximum((b * nt + t) * (tb // 8) - 1, 0), 0, 0))]
        + [_resident(w.shape) for w in weights],
        out_specs=blk,
        out_shape=jax.ShapeDtypeStruct((m, d), F32),
        scratch_shapes=[pltpu.VMEM((n_pairs, 2 * RWKV_HEAD, 2 * RWKV_HEAD), F32)]
        + [pltpu.VMEM((tb, d), F32)] * 9,
        compiler_params=_cparams("parallel", "arbitrary"),
        name="rwkv_layer",
    )(x2d, xh, *weights)


def _proj_ln_router_kernel(y_ref, x_ref, w_ref, lng_ref, lnb_ref, wr_ref, br_ref, o_ref, lg_ref):
    mix = _dot(y_ref[...].astype(BF16), w_ref[...])
    xn = _layer_norm(DEEPNORM_ALPHA * x_ref[...] + mix, lng_ref[...], lnb_ref[...])
    o_ref[...] = xn
    xh, xl = _split2(xn)
    lg_ref[...] = _dot(xh, wr_ref[0]) + _dot(xh, wr_ref[1]) + _dot(xl, wr_ref[0]) + br_ref[...]


def _proj_ln_router(y, x2d, w, ln_g, ln_b, w_router, b_router, tm=512):
    m, d = x2d.shape
    row = lambda i: (i, 0)
    wr = jnp.zeros((d, LANES), F32).at[:, :N_EXPERTS].set(w_router)
    wr3 = jnp.stack(_split2(wr))
    br =jnp.zeros((1, LANES), F32).at[0, :N_EXPERTS].set(b_router)
    return pl.pallas_call(
        _proj_ln_router_kernel,
        grid=(m // tm,),
        in_specs=[pl.BlockSpec((tm, y.shape[1]), row), pl.BlockSpec((tm, d), row), _resident(w.shape),
                  _resident((1, d)), _resident((1, d)), _resident(wr3.shape), _resident(br.shape)],
        out_specs=[pl.BlockSpec((tm, d), row), pl.BlockSpec((tm, LANES), row)],
        out_shape=[jax.ShapeDtypeStruct((m, d), F32), jax.ShapeDtypeStruct((m, LANES), F32)],
        compiler_params=_cparams("parallel"),
        name="proj_ln_router",
    )(y, x2d, w, ln_g.reshape(1, d), ln_b.reshape(1, d), wr3, br)


def _ple(xn, p, plew_ref, gw_ref, gb_ref):
    gate = _sigmoid(_dot(xn.astype(BF16), gw_ref[...]) + gb_ref[...])
    return xn + _dot(p.astype(BF16), plew_ref[...]) * gate


def _ffn_ple_kernel(x_ref, p_ref, wg_ref, wu_ref, wd_ref, lng_ref, lnb_ref, plew_ref, gw_ref, gb_ref,
                    o_ref, *, ff_chunk):
    x = x_ref[...]
    xb = x.astype(BF16)
    acc = jnp.zeros_like(x)
    for c in range(wg_ref.shape[1] // ff_chunk):
        sl = slice(c * ff_chunk, (c + 1) * ff_chunk)
        gt = _dot(xb, wg_ref[:, sl])
        up = _dot(xb, wu_ref[:, sl])
        h = (gt * _sigmoid(gt) * up).astype(BF16)
        acc = acc + _dot(h, wd_ref[sl, :])
    xn = _layer_norm(DEEPNORM_ALPHA * x + acc, lng_ref[...], lnb_ref[...])
    o_ref[...] = _ple(xn, p_ref[0], plew_ref, gw_ref, gb_ref)


def _ple_block(p3d, layer, tm):
    return pl.BlockSpec((1, tm, p3d.shape[2]), lambda i: (layer, i, 0))


def _ffn_ple(x2d, p3d, layer, wg, wu, wd, ln_g, ln_b, ple_w, gate_w, gate_b, tm=512, ff_chunk=1408):
    m, d = x2d.shape
    row = lambda i: (i, 0)
    return pl.pallas_call(
        functools.partial(_ffn_ple_kernel, ff_chunk=ff_chunk),
        grid=(m // tm,),
        in_specs=[pl.BlockSpec((tm, d), row), _ple_block(p3d, layer, tm),
                  _resident(wg.shape), _resident(wu.shape), _resident(wd.shape),
                  _resident((1, d)), _resident((1, d)), _resident(ple_w.shape), _resident(gate_w.shape),
                  _resident((1, d))],
        out_specs=pl.BlockSpec((tm, d), row),
        out_shape=jax.ShapeDtypeStruct((m, d), F32),
        compiler_params=_cparams("parallel"),
        name="ffn_ple",
    )(x2d, p3d, wg, wu, wd, ln_g.reshape(1, d), ln_b.reshape(1, d), ple_w, gate_w, gate_b.reshape(1, d))


def _moe_combine_ln_ple_kernel(x_ref, y0_ref, y1_ref, rw_ref, p_ref, lng_ref, lnb_ref, plew_ref, gw_ref,
                               gb_ref, o_ref):
    rw = rw_ref[...]
    ffn = y0_ref[...] * rw[:, 0:1] + y1_ref[...] * rw[:, 1:2]
    xn = _layer_norm(DEEPNORM_ALPHA * x_ref[...] + ffn, lng_ref[...], lnb_ref[...])
    o_ref[...] = _ple(xn, p_ref[0], plew_ref, gw_ref, gb_ref)


def _moe_combine_ln_ple(x2d, y0, y1, route_w, p3d, layer, ln_g, ln_b, ple_w, gate_w, gate_b, tm=512):
    m, d = x2d.shape
    row = lambda i: (i, 0)
    tile = pl.BlockSpec((tm, d), row)
    return pl.pallas_call(
        _moe_combine_ln_ple_kernel,
        grid=(m // tm,),
        in_specs=[tile, tile, tile, pl.BlockSpec((tm, TOP_K), row), _ple_block(p3d, layer, tm),
                  _resident((1, d)), _resident((1, d)), _resident(ple_w.shape), _resident(gate_w.shape),
                  _resident((1, d))],
        out_specs=tile,
        out_shape=jax.ShapeDtypeStruct((m, d), F32),
        compiler_params=_cparams("parallel"),
        name="moe_combine_ln_ple",
    )(x2d, y0, y1, route_w, p3d, ln_g.reshape(1, d), ln_b.reshape(1, d), ple_w, gate_w, gate_b.reshape(1, d))


def _swap_half_cols(w):
    k, n = w.shape
    return w.reshape(k, n // HEAD_DIM, 2, HEAD_DIM // 2)[:, :, ::-1, :].reshape(k, n)


def _rope_tables(seq):
    half = HEAD_DIM // 2
    inv = ROPE_THETA ** (-jnp.arange(half, dtype=F32) / half)
    ang = jnp.arange(seq, dtype=F32)[:, None] * inv[None, :]
    cos, sin = jnp.cos(ang), jnp.sin(ang)
    return jnp.concatenate([cos, cos], -1), jnp.concatenate([-sin, sin], -1)


def _nsa_kv_kernel(x_ref, wk_ref, wvt_ref, cos_ref, sin_ref, zc_ref, vc_ref, ks_ref, kw_ref, vst_ref,
                   vwt_ref):
    G, dh = N_KV_GROUPS, HEAD_DIM
    gw = G * dh
    xb = x_ref[...].astype(BF16)
    res = _dot(xb, wk_ref[...])
    res_t = _dot_nt(wvt_ref[...], xb)
    tm = res.shape[0]
    cos = cos_ref[...]
    sin = sin_ref[...]
    ks = res[:, 2 * gw:3 * gw] * cos + res[:, 3 * gw:4 * gw] * sin
    kw = res[:, 4 * gw:5 * gw] * cos + res[:, 5 * gw:6 * gw] * sin
    t_glob = pl.program_id(1) * tm + lax.broadcasted_iota(jnp.int32, (tm, dh), 0)
    onehot = jnp.where(lax.broadcasted_iota(jnp.int32, (tm, dh), 1) == t_glob // SEL_BLOCK, 1.0, 0.0)
    for g in range(G):
        sl = slice(g * dh, (g + 1) * dh)
        zc_ref[0, g] = res[:, sl].astype(BF16)
        vc_ref[0, g] = res[:, gw + g * dh:gw + (g + 1) * dh].astype(BF16)
        ks_ref[0, g] = jnp.concatenate([ks[:, sl], onehot], axis=1).astype(BF16)
        kw_ref[0, g] = kw[:, sl].astype(BF16)
    vst_ref[0] = res_t[:gw].astype(BF16)
    vwt_ref[0] = res_t[gw:].astype(BF16)


def _nsa_kv(x2d, w_kv, batch, seq, tm=256):
    m, d = x2d.shape
    G, dh = N_KV_GROUPS, HEAD_DIM
    gw = G * dh
    parts = [w_kv[:, j * gw:(j + 1) * gw] for j in range(6)]
    w_k = jnp.concatenate([parts[0], parts[1], parts[2], _swap_half_cols(parts[2]),
                           parts[4], _swap_half_cols(parts[4])], axis=1).astype(BF16)
    w_vt = jnp.concatenate([parts[3], parts[5]], axis=1).T.astype(BF16)
    cos, sin = _rope_tables(seq)
    cos = jnp.tile(cos, (1, G))
    sin = jnp.tile(sin, (1, G))
    nt = seq // tm
    o64 = jax.ShapeDtypeStruct((batch, G, seq, dh), BF16)
    o128 = jax.ShapeDtypeStruct((batch, G, seq, 2 * dh), BF16)
    ot = jax.ShapeDtypeStruct((batch, gw, seq), BF16)
    b64 = pl.BlockSpec((1, G, tm, dh), lambda b, t: (b, 0, t, 0))
    b128 = pl.BlockSpec((1, G, tm, 2 * dh), lambda b, t: (b, 0, t, 0))
    bt = pl.BlockSpec((1, gw, tm), lambda b, t: (b, 0, t))
    return pl.pallas_call(
        _nsa_kv_kernel,
        grid=(batch, nt),
        in_specs=[pl.BlockSpec((tm, d), lambda b, t: (b * nt + t, 0)), _resident(w_k.shape),
                  _resident(w_vt.shape),
                  pl.BlockSpec((tm, gw), lambda b, t: (t, 0)), pl.BlockSpec((tm, gw), lambda b, t: (t, 0))],
        out_specs=[b64, b64, b128, b64, bt, bt],
        out_shape=[o64, o64, o128, o64, ot, ot],
        compiler_params=_cparams("parallel", "parallel"),
        name="nsa_kv",
    )(x2d, w_k, w_vt, cos, sin)


def _nsa_cmp_kernel(z_ref, pos_ref, w1_ref, b1_ref, w2_ref, o_ref, *, slabs_per_seq):
    half = w1_ref.shape[1] // 2
    z = z_ref[0]
    tm = z.shape[0]
    first = _dot(z, w1_ref[0, :half, :])
    second = _dot(z, w1_ref[0, half:, :])
    const = _dot(pos_ref[0], w1_ref[0])[0:1, :] + b1_ref[0]
    hid = first + pltpu.roll(second, shift=tm - 1, axis=0) + const
    act = 0.5 * hid * (1.0 + jnp.tanh(math.sqrt(2.0 / math.pi) * (hid + 0.044715 * hid * hid * hid)))
    out = _dot(act.astype(BF16), w2_ref[0])
    row = lax.broadcasted_iota(jnp.int32, out.shape, 0)
    o_ref[0] = jnp.where(row % slabs_per_seq == slabs_per_seq - 1, 0.0, out).astype(BF16)


def _nsa_compress(zc, vc, cmp_pos, phi_w1, phi_b1, phi_w2, tm=512):
    batch, G, seq, dh = zc.shape
    slab = CMP_STRIDE * dh
    rows = batch * G * seq // CMP_STRIDE
    tm = min(tm, rows)
    z = jnp.stack([zc.reshape(rows, slab), vc.reshape(rows, slab)])
    pos = jnp.broadcast_to(cmp_pos.reshape(2, 1, CMP_LEN * dh), (2, 8, CMP_LEN * dh)).astype(BF16)
    out = pl.pallas_call(
        functools.partial(_nsa_cmp_kernel, slabs_per_seq=seq // CMP_STRIDE),
        grid=(2, rows // tm),
        in_specs=[pl.BlockSpec((1, tm, slab), lambda j, i: (j, i, 0)),
                  pl.BlockSpec((1, 8, CMP_LEN * dh), lambda j, i: (j, 0, 0)),
                  pl.BlockSpec((1, CMP_LEN * dh, CMP_HIDDEN), lambda j, i: (j, 0, 0)),
                  pl.BlockSpec((1, 1, CMP_HIDDEN), lambda j, i: (j, 0, 0)),
                  pl.BlockSpec((1, CMP_HIDDEN, dh), lambda j, i: (j, 0, 0))],
        out_specs=pl.BlockSpec((1, tm, dh), lambda j, i: (j, i, 0)),
        out_shape=jax.ShapeDtypeStruct((2, rows, dh), BF16),
        compiler_params=_cparams("parallel", "parallel"),
        name="nsa_compress",
    )(z, pos, phi_w1.astype(BF16), phi_b1.reshape(2, 1, CMP_HIDDEN), phi_w2.astype(BF16))
    n_slab = seq // CMP_STRIDE
    return out[0].reshape(batch, G, n_slab, dh), out[1].reshape(batch, G, n_slab, dh)


GATE_ROWS = 16


def _nsa_q_kernel(x_ref, wt_ref, cos_ref, sin_ref, q_ref, qr_ref, gate_ref):
    dh = HEAD_DIM
    hw = N_HEADS * dh
    scale = HEAD_DIM ** -0.5 * math.log2(math.e)
    res_t = _dot_nt(wt_ref[...], x_ref[...].astype(BF16))
    tm = res_t.shape[1]
    q = res_t[:hw].reshape(N_HEADS, dh, tm)
    q_sw = jnp.concatenate([q[:, dh // 2:], q[:, :dh // 2]], axis=1)
    qr = q * cos_ref[...][None] + q_sw * sin_ref[...][None]
    q_ref[0] = (q * scale).reshape(hw, tm).astype(BF16)
    qr_ref[0] = (qr * scale).reshape(hw, tm).astype(BF16)
    gate_ref[0] = _sigmoid(res_t[hw:])


def _nsa_q(x2d, w_in, batch, seq, tm=256):
    m, d = x2d.shape
    G, R, dh = N_KV_GROUPS, Q_PER_GROUP, HEAD_DIM
    hw = N_HEADS * dh
    w_gate = w_in[:, hw:].reshape(d, G, R * N_BRANCH)
    w_gate = jnp.pad(w_gate, ((0, 0), (0, 0), (0, GATE_ROWS - R * N_BRANCH))).reshape(d, G * GATE_ROWS)
    wt = jnp.concatenate([w_in[:, :hw], w_gate], axis=1).T.astype(BF16)
    cos, sin = _rope_tables(seq)
    nt = seq // tm
    oq = jax.ShapeDtypeStruct((batch, hw, seq), BF16)
    bq = pl.BlockSpec((1, hw, tm), lambda b, t: (b, 0, t))
    tab = pl.BlockSpec((dh, tm), lambda b, t: (0, t))
    return pl.pallas_call(
        _nsa_q_kernel,
        grid=(batch, nt),
        in_specs=[pl.BlockSpec((tm, d), lambda b, t: (b * nt + t, 0)), _resident(wt.shape), tab, tab],
        out_specs=[bq, bq, pl.BlockSpec((1, G * GATE_ROWS, tm), lambda b, t: (b, 0, t))],
        out_shape=[oq, oq, jax.ShapeDtypeStruct((batch, G * GATE_ROWS, seq), F32)],
        compiler_params=_cparams("parallel", "parallel"),
        name="nsa_q",
    )(x2d, wt, cos.T, sin.T)


def _nsa_attn_kernel(q_ref, qr_ref, gate_ref, kc_ref, vct_ref, ks_ref, vst_ref, kw_ref, vwt_ref, ovlt_ref,
                     o_ref, *, tq, tk, n_blk):
    R, dh = Q_PER_GROUP, HEAD_DIM
    cols = R * tq
    qi = pl.program_id(2)
    t0 = qi * tq
    n_blk_pad = ovlt_ref.shape[0]

    def heads_to_lanes(x):
        return jnp.concatenate([x[r * dh:(r + 1) * dh] for r in range(R)], axis=1)

    q = heads_to_lanes(q_ref[0])
    qr = heads_to_lanes(qr_ref[0])

    def col_pos(shape):
        return t0 + (lax.broadcasted_iota(jnp.int32, shape, 1) & (tq - 1))

    def key_tile(j):
        return pl.ds(pl.multiple_of(j * tk, tk), tk)

    ct = tq
    col_tiles = [slice(c, c + ct) for c in range(0, cols, ct)]
    ones_rows = 16
    ones_blk = jnp.ones((ones_rows, tk), BF16)

    def tile_scores(k_ref, q_op, tiles):
        return [[_dot(k_ref[0, 0, key_tile(j), :], q_op[:, cs]) for j, _ in tiles] for cs in col_tiles]

    def attend(carry, sc, vt_ref, tiles):
        m_i, acc = carry
        outs = []
        for cs, sc_c in zip(col_tiles, sc):
            masked = [s_j if mask_fn is None else jnp.where(mask_fn(k_off, q_off), s_j, NEG_INF)
                      for s_j, (_, mask_fn) in zip(sc_c, tiles)]
            m_new = m_i[:, cs]
            for s_j in masked:
                m_new = jnp.maximum(m_new, jnp.max(s_j, axis=0, keepdims=True))
            acc_new = jnp.exp2(m_i[:, cs] - m_new) * acc[:, cs]
            for s_j, (j, _) in zip(masked, tiles):
                v_ext = jnp.concatenate([vt_ref[0, :, key_tile(j)], ones_blk], axis=0)
                acc_new = acc_new + _dot(v_ext, jnp.exp2(s_j - m_new).astype(BF16))
            outs.append((m_new, acc_new))
        return tuple(jnp.concatenate([o[i] for o in outs], axis=1) for i in range(2))

    def normalised(acc):
        return acc[:dh] / acc[dh:dh + 1]

    init = (jnp.full((1, cols), NEG_INF, F32), jnp.zeros((dh + ones_rows, cols), F32))

    k_off = lax.broadcasted_iota(jnp.int32, (tk, ct), 0)
    q_off = lax.broadcasted_iota(jnp.int32, (tk, ct), 1)
    causal = lambda k, q_: k <= q_
    present = lambda cond: (lambda k, q_: (k >= 0) & cond)

    win_tiles = [(jnp.maximum(qi - 2, 0), lambda k, q_: (q_ < k) & (qi >= 2)),
                 (jnp.maximum(qi - 1, 0), present(qi >= 1)),
                 (qi, causal)]
    sc_win = tile_scores(kw_ref, qr, win_tiles)

    s = _dot(kc_ref[0, 0], q)
    cmp_end = lax.broadcasted_iota(jnp.int32, s.shape, 0) * CMP_STRIDE + (CMP_LEN - 1)
    valid = cmp_end <= col_pos(s.shape)
    sm = jnp.where(valid, s, NEG_INF)
    e = jnp.where(valid, jnp.exp2(sm - jnp.max(sm, axis=0, keepdims=True)), 0.0)
    l = jnp.sum(e, axis=0, keepdims=True)
    p_cmp = e / jnp.where(l > 0.0, l, 1.0)
    o_cmp = _dot(vct_ref[0, 0], p_cmp.astype(BF16))

    p_sum = p_cmp[:, 0:tq]
    for r in range(1, R):
        p_sum = p_sum + p_cmp[:, r * tq:(r + 1) * tq]
    hi, lo = _split2(p_sum)
    imp = _dot(ovlt_ref[...], hi) + _dot(ovlt_ref[...], lo)
    blk = lax.broadcasted_iota(jnp.int32, imp.shape, 0)
    t_q = t0 + lax.broadcasted_iota(jnp.int32, imp.shape, 1)
    cur = t_q // SEL_BLOCK
    forced = (blk == 0) | (blk == cur) | (blk == cur - 1)
    score = jnp.where(forced, FORCE_SCORE, jnp.where(blk * SEL_BLOCK <= t_q, imp, NEG_INF))
    score = jnp.where(blk < n_blk, score, -jnp.inf)
    selected = blk >= n_blk
    for _ in range(N_SEL):
        best = jnp.max(score, axis=0, keepdims=True)
        first = jnp.min(jnp.where(score == best, blk, 2 * LANES), axis=0, keepdims=True)
        pick = blk == first
        selected = selected | pick
        score = jnp.where(pick, -jnp.inf, score)
    bias = jnp.where(selected, 0.0, NEG_INF)
    qa = jnp.concatenate([qr, jnp.concatenate([bias] * R, axis=1).astype(BF16),
                          jnp.zeros((dh - n_blk_pad, cols), BF16)], axis=0)

    o_win = normalised(attend(init, sc_win, vwt_ref, win_tiles)[1])

    n_pairs = qi // 2

    def pair_body(i, carry):
        tiles = [(2 * i, None), (2 * i + 1, None)]
        return attend(carry, tile_scores(ks_ref, qa, tiles), vst_ref, tiles)

    carry = lax.fori_loop(0, n_pairs, pair_body, init)
    j_odd = 2 * n_pairs
    tail_tiles = [(j_odd, present(j_odd < qi)), (qi, causal)]
    o_slc = normalised(attend(carry, tile_scores(ks_ref, qa, tail_tiles), vst_ref, tail_tiles)[1])

    gates = gate_ref[0]
    outs = []
    for r in range(R):
        cs = slice(r * tq, (r + 1) * tq)
        outs.append(gates[3 * r:3 * r + 1] * o_cmp[:, cs] + gates[3 * r + 1:3 * r + 2] * o_slc[:, cs]
                    + gates[3 * r + 2:3 * r + 3] * o_win[:, cs])
    halves = [jnp.transpose(jnp.concatenate(outs[i:i + 2], axis=0)) for i in range(0, R, 2)]
    o_ref[0] = jnp.concatenate(halves, axis=1)


SEL_ROWS = 32


def _nsa_attention(q, qr, gates, k_cmp, v_cmp, ks, vst, kw, vwt, tq=256, tk=256):
    batch, _, seq = q.shape
    G, R, dh = N_KV_GROUPS, Q_PER_GROUP, HEAD_DIM
    n_slab = k_cmp.shape[2]
    n_blk = seq // SEL_BLOCK
    assert n_blk <= SEL_ROWS
    assert tq == tk and WINDOW == 2 * tk, "the window branch is written as exactly three key tiles"
    cmp_start = jnp.arange(n_slab) * CMP_STRIDE
    blk_start = jnp.arange(SEL_ROWS) * SEL_BLOCK
    overlap_t = ((cmp_start[None, :] < blk_start[:, None] + SEL_BLOCK)
                 & (cmp_start[None, :] + CMP_LEN - 1 >= blk_start[:, None])
                 & (jnp.arange(SEL_ROWS)[:, None] < n_blk)
                 & (jnp.arange(n_slab)[None, :] < (seq - CMP_LEN) // CMP_STRIDE + 1)).astype(BF16)
    v_cmp_t = jnp.swapaxes(v_cmp, 2, 3)
    qspec = pl.BlockSpec((1, R * dh, tq), lambda b, g, i: (b, g, i))
    full = lambda a: pl.BlockSpec((1, 1) + a.shape[2:], lambda b, g, i: (b, g, 0, 0))
    vspec = pl.BlockSpec((1, dh, seq), lambda b, g, i: (b, g, 0))
    return pl.pallas_call(
        functools.partial(_nsa_attn_kernel, tq=tq, tk=tk, n_blk=n_blk),
        grid=(batch, G, seq // tq),
        in_specs=[qspec, qspec, pl.BlockSpec((1, GATE_ROWS, tq), lambda b, g, i: (b, g, i)),
                  full(k_cmp), full(v_cmp_t), full(ks), vspec, full(kw), vspec,
                  _resident(overlap_t.shape)],
        out_specs=pl.BlockSpec((1, tq, R * dh), lambda b, g, i: (b, i, g)),
        out_shape=jax.ShapeDtypeStruct((batch, seq, G * R * dh), F32),
        compiler_params=_cparams("parallel", "parallel", "parallel"),
        name="nsa_attn",
    )(q, qr, gates, k_cmp, v_cmp_t, ks, vst, kw, vwt, overlap_t)


def _moe_kernel(be_ref, nb_ref, x_ref, wgu_ref, wd_ref, o_ref, wgu_bf, wd_bf):
    i = pl.program_id(0)
    ff = wd_ref.shape[1]
    used = i < nb_ref[0]

    @pl.when(used & ((i == 0) | (be_ref[i] != be_ref[jnp.maximum(i - 1, 0)])))
    def _():
        wgu_bf[...] = wgu_ref[0].astype(BF16)
        wd_bf[...] = wd_ref[0].astype(BF16)

    @pl.when(used)
    def _():
        xb = x_ref[...].astype(BF16)
        gt = _dot(xb, wgu_bf[:, :ff])
        up = _dot(xb, wgu_bf[:, ff:])
        h = (gt * _sigmoid(gt) * up).astype(BF16)
        o_ref[...] = _dot(h, wd_bf[...])

    @pl.when(jnp.logical_not(used))
    def _():
        o_ref[...] = jnp.zeros_like(o_ref)


def _moe_experts(xs, block_e, n_used, w_gu, w_down):
    n_rows, d = xs.shape
    n_blocks = n_rows // MOE_ROWS
    ff = w_down.shape[1]
    grid_spec = pltpu.PrefetchScalarGridSpec(
        num_scalar_prefetch=2,
        grid=(n_blocks,),
        in_specs=[pl.BlockSpec((MOE_ROWS, d), lambda i, be, nb: (i, 0)),
                  pl.BlockSpec((1, d, 2 * ff), lambda i, be, nb: (be[i], 0, 0)),
                  pl.BlockSpec((1, ff, d), lambda i, be, nb: (be[i], 0, 0))],
        out_specs=pl.BlockSpec((MOE_ROWS, d), lambda i, be, nb: (i, 0)),
        scratch_shapes=[pltpu.VMEM((d, 2 * ff), BF16), pltpu.VMEM((ff, d), BF16)],
    )
    return pl.pallas_call(
        _moe_kernel,
        grid_spec=grid_spec,
        out_shape=jax.ShapeDtypeStruct((n_rows, d), F32),
        compiler_params=_cparams("arbitrary"),
        name="moe_experts",
    )(block_e, n_used, xs, w_gu, w_down)


def _moe(x2d, logits, w_gu, w_down):
    n_tok, d = x2d.shape
    n_assign = n_tok * TOP_K
    top_logit, top_e = lax.top_k(logits, TOP_K)
    weights = jax.nn.softmax(top_logit, axis=-1)
    flat_e = top_e.reshape(-1).astype(jnp.int32)
    onehot = (flat_e[:, None] == jnp.arange(N_EXPERTS, dtype=jnp.int32)[None, :]).astype(jnp.int32)
    running = jnp.cumsum(onehot, axis=0)
    counts = running[-1]
    padded = (counts + MOE_ROWS - 1) // MOE_ROWS * MOE_ROWS
    pad_end = jnp.cumsum(padded)
    pad_start = pad_end - padded
    grp_start = jnp.cumsum(counts) - counts
    pos = jnp.sum(onehot * (running - 1 + pad_start[None, :]), axis=1).reshape(n_tok, TOP_K)
    n_blocks = -(-n_assign // MOE_ROWS) + N_EXPERTS
    block_e = jnp.minimum(jnp.searchsorted(pad_end, jnp.arange(n_blocks) * MOE_ROWS, side='right'),
                          N_EXPERTS - 1).astype(jnp.int32)
    n_used = (pad_end[-1] // MOE_ROWS).astype(jnp.int32).reshape(1)
    order = jnp.argsort(flat_e)
    tok_sorted = (order // TOP_K).astype(jnp.int32)
    row_in_grp = (jnp.arange(n_blocks, dtype=jnp.int32) * MOE_ROWS - pad_start[block_e])[:, None] \
        + jnp.arange(MOE_ROWS, dtype=jnp.int32)[None, :]
    src = jnp.clip(grp_start[block_e][:, None] + row_in_grp, 0, n_assign - 1)
    row_tok = jnp.where(row_in_grp < counts[block_e][:, None], tok_sorted[src], 0).reshape(-1)
    xs = x2d[row_tok]
    ys = _moe_experts(xs, block_e, n_used, w_gu, w_down)
    return [ys[pos[:, s]] for s in range(TOP_K)], weights


def kernel(x, p, a_mu, a_w_rkv, a_w0, a_w1, a_w2, a_a0, a_a1, a_a2, a_g1, a_g2, a_k_k, a_k_a, a_r_k,
           a_gn_g, a_gn_b, a_w_o, b_w_kv, b_cmp_pos, b_phi_w1, b_phi_b1, b_phi_w2, b_w_in, b_w_o,
           f_w_gu, f_w_down, m_w_router, m_b_router, m_w_gu, m_w_down, ln_g, ln_b, ple_w, ple_gate_w,
           ple_gate_b):
    batch, seq, d = x.shape
    m = batch * seq
    x0 = x.reshape(m, d)
    p2d = p.reshape(DEPTH, m, PLE_DIM)
    bf = lambda w: w.astype(BF16)

    x1 = _rwkv_layer(x0, batch, seq, a_mu[0], bf(a_w_rkv[0]), a_w0[0], bf(a_w1[0]), bf(a_w2[0]), a_a0[0],
                     bf(a_a1[0]), bf(a_a2[0]), bf(a_g1[0]), bf(a_g2[0]), a_k_k[0], a_k_a[0], a_r_k[0],
                     a_gn_g[0], a_gn_b[0], bf(a_w_o[0]), ln_g[0, 0], ln_b[0, 0])
    x2 = _ffn_ple(x1, p2d, 0, bf(f_w_gu[0][:, :D_FF]), bf(f_w_gu[0][:, D_FF:]), bf(f_w_down[0]),
                  ln_g[0, 1], ln_b[0, 1], bf(ple_w[0]), bf(ple_gate_w[0]), ple_gate_b[0])

    zc, vc, ks, kw, vst, vwt = _nsa_kv(x2, b_w_kv, batch, seq)
    k_cmp, v_cmp = _nsa_compress(zc, vc, b_cmp_pos, b_phi_w1, b_phi_b1, b_phi_w2)
    q, qr, gates = _nsa_q(x2, b_w_in[0], batch, seq)
    attn = _nsa_attention(q, qr, gates, k_cmp, v_cmp, ks, vst, kw, vwt).reshape(m, d)
    x3, logits = _proj_ln_router(attn, x2, bf(b_w_o[0]), ln_g[1, 0], ln_b[1, 0], m_w_router[0],
                                 m_b_router[0])
    (y0, y1), route_w = _moe(x3, logits[:, :N_EXPERTS], m_w_gu[0], m_w_down[0])
    out = _moe_combine_ln_ple(x3, y0, y1, route_w, p2d, 1, ln_g[1, 1], ln_b[1, 1], bf(ple_w[1]),
                              bf(ple_gate_w[1]), ple_gate_b[1])
    return out.reshape(batch, seq, d)
```

```python
import functools
import math

import jax
import jax.numpy as jnp
from jax import lax
from jax.experimental import pallas as pl
from jax.experimental.pallas import tpu as pltpu

BF16 = jnp.bfloat16
F32 = jnp.float32

LANES = 128
VMEM_LIMIT_BYTES = 56 * 1024 * 1024

D_MODEL = 1024
PLE_DIM = 256
RWKV_HEAD = 64
GN_EPS = 64e-5
N_HEADS = 16
HEAD_DIM = 64
N_KV_GROUPS = 4
Q_PER_GROUP = 4
N_BRANCH = 3
CMP_LEN = 32
CMP_STRIDE = 16
CMP_HIDDEN = 256
SEL_BLOCK = 64
N_SEL = 8
WINDOW = 512
ROPE_THETA = 10000.0
D_FF = 2816
N_EXPERTS = 8
TOP_K = 2
D_FF_EXPERT = 1408
MOE_ROWS = 256
LN_EPS = 1e-5
DEPTH = 2
DEEPNORM_ALPHA = (2.0 * DEPTH) ** 0.25
NEG_INF = -1e30
FORCE_SCORE = 1e4

SCAN_CHUNK = 64


def _cparams(*sem):
    return pltpu.CompilerParams(dimension_semantics=sem, vmem_limit_bytes=VMEM_LIMIT_BYTES)


def _resident(shape):
    nd = len(shape)
    return pl.BlockSpec(shape, lambda *_: (0,) * nd, pipeline_mode=pl.Buffered(1))


def _dot(a, b):
    return jnp.dot(a, b, preferred_element_type=F32)


def _dot_nt(a, b):
    return lax.dot_general(a, b, (((1,), (1,)), ((), ())), preferred_element_type=F32)


def _split2(x):
    hi = x.astype(BF16)
    lo = (x - hi.astype(F32)).astype(BF16)
    return hi, lo


def _split3(x):
    hi = x.astype(BF16)
    r1 = x - hi.astype(F32)
    mid = r1.astype(BF16)
    lo = (r1 - mid.astype(F32)).astype(BF16)
    return hi, mid, lo


def _layer_norm(y, g, b):
    mu = jnp.mean(y, axis=-1, keepdims=True)
    yc = y - mu
    var = jnp.mean(yc * yc, axis=-1, keepdims=True)
    return yc * lax.rsqrt(var + LN_EPS) * g + b


def _sigmoid(z):
    return 1.0 / (1.0 + jnp.exp(-z))


def _rwkv_project(first, x_ref, xh_ref, mu_ref, wrkv_ref, w0_ref, w1_ref, w2_ref, a0_ref, a1_ref,
                  a2_ref, g1_ref, g2_ref, r_ref, k_ref, v_ref, dl_ref, a_ref, g_ref):
    x = x_ref[...]
    prev_row = jnp.where(first, 0.0, xh_ref[0, 7:8, :])
    row = lax.broadcasted_iota(jnp.int32, x.shape, 0)
    x_shift = jnp.where(row == 0, prev_row, pltpu.roll(x, shift=1, axis=0))
    xx = x_shift - x

    def mix(i):
        return (x + xx * mu_ref[i:i + 1, :]).astype(BF16)

    r_ref[...] = _dot(mix(0), wrkv_ref[0])
    k_ref[...] = _dot(mix(1), wrkv_ref[1])
    v_ref[...] = _dot(mix(2), wrkv_ref[2])
    z = w0_ref[...] + _dot(jnp.tanh(_dot(mix(3), w1_ref[...])).astype(BF16), w2_ref[...])
    dl_ref[...] = -math.exp(-0.5) * _sigmoid(z)
    a_ref[...] = _sigmoid(a0_ref[...] + _dot(_dot(mix(4), a1_ref[...]).astype(BF16), a2_ref[...]))
    g_ref[...] = _dot(_sigmoid(_dot(mix(5), g1_ref[...])).astype(BF16), g2_ref[...])


def _rwkv_recurrence(first, r_ref, k_ref, v_ref, dl_ref, a_ref, g_ref, kk_ref, ka_ref, rk_ref, gng_ref,
                     gnb_ref, o_ref, state_ref, kkn_ref, cum_ref, *, n_pairs, n_chunks):
    C = SCAN_CHUNK
    N = RWKV_HEAD
    W = 2 * N
    tb = n_chunks * C

    @pl.when(first)
    def _():
        state_ref[...] = jnp.zeros_like(state_ref)

    tt = lax.broadcasted_iota(jnp.int32, (C, W), 0)
    ln = lax.broadcasted_iota(jnp.int32, (C, W), 1)
    ss = ln & (N - 1)
    strict_lower = ss < tt
    lower = ss <= tt
    eye = jnp.where(ss == tt, 1.0, 0.0)
    level_masks = []
    for lg in range(int(math.log2(C))):
        level_masks.append(((tt >> (lg + 1)) == (ss >> (lg + 1))) & ((tt >> lg) == (ss >> lg) + 1))
    rr = lax.broadcasted_iota(jnp.int32, (W, W), 0)
    cc = lax.broadcasted_iota(jnp.int32, (W, W), 1)
    same_head = (rr < N) == (cc < N)
    lane_head0 = ln < N
    WG = 2 * W
    lg_n = int(math.log2(N))
    ones_grp = jnp.where((lax.broadcasted_iota(jnp.int32, (WG, WG), 0) >> lg_n)
                         == (lax.broadcasted_iota(jnp.int32, (WG, WG), 1) >> lg_n), 1.0, 0.0).astype(BF16)
    col_groups = [slice(c, c + WG) for c in range(0, n_pairs * W, WG)]

    def bd(y):
        return jnp.concatenate([jnp.where(lane_head0, y, 0.0), jnp.where(lane_head0, 0.0, y)],
                               axis=0).astype(BF16)

    def pmm(xp, ybd):
        return _dot(xp.astype(BF16), ybd)

    def head_sum(xg):
        return _dot(xg.astype(BF16), ones_grp)

    tri_r = lax.broadcasted_iota(jnp.int32, (tb, tb), 0)
    tri_c = lax.broadcasted_iota(jnp.int32, (tb, tb), 1)
    lg_c = int(math.log2(C))
    tri_chunk = jnp.where((tri_c <= tri_r) & ((tri_c >> lg_c) == (tri_r >> lg_c)), 1.0, 0.0).astype(BF16)
    d_hi, d_mid, d_lo = _split3(dl_ref[...])
    cum_ref[...] = _dot(tri_chunk, d_hi) + _dot(tri_chunk, d_mid) + _dot(tri_chunk, d_lo)
    for cols in col_groups:
        kk_raw = k_ref[:, cols] * kk_ref[:, cols]
        kkn_ref[:, cols] = kk_raw / jnp.maximum(jnp.sqrt(head_sum(kk_raw * kk_raw)), 1e-12)

    def chunk_load(ci, pi):
        rows = pl.ds(pl.multiple_of(ci * C, C), C)
        cols = slice(pi * W, (pi + 1) * W)
        return (r_ref[rows, cols], k_ref[rows, cols], v_ref[rows, cols], dl_ref[rows, cols],
                a_ref[rows, cols], kkn_ref[rows, cols], cum_ref[rows, cols], state_ref[pi])

    def chunk_compute(pi, r, k, v, dl, a, kk, cum, m0):
        cols = slice(pi * W, (pi + 1) * W)
        k2 = k * (1.0 + (a - 1.0) * ka_ref[:, cols])
        b = a * kk
        p_incl = jnp.exp(cum)
        p_excl = jnp.exp(cum - dl)
        p_inv = jnp.exp(-cum)
        kq = kk * p_excl
        rq = r * p_incl
        bk = b * p_inv
        kkd = k2 * p_inv

        bkt = jnp.transpose(jnp.concatenate([b, k2], axis=0))
        cumt = jnp.transpose(jnp.concatenate([cum, cum], axis=0))
        last_col = cumt[:, C - 1:C]
        lhs_t = (bkt * jnp.exp(last_col - cumt)).astype(BF16)
        m_decayed = m0 * jnp.exp(last_col)

        lhs = jnp.concatenate([kq, rq], axis=0).astype(BF16)
        yield
        ab = _dot_nt(lhs, bd(bk))
        gb = _dot_nt(lhs, bd(kkd))
        a_m = jnp.where(strict_lower, ab[:C], 0.0)
        aq_m = jnp.where(lower, ab[C:], 0.0)
        g_m = jnp.where(strict_lower, gb[:C], 0.0)
        gq_m = jnp.where(lower, gb[C:], 0.0)

        tinv = eye - jnp.where(level_masks[0], a_m, 0.0)
        yield
        gv = pmm(jnp.concatenate([g_m, gq_m], axis=0), bd(v))
        for lm in level_masks[1:]:
            w_ = pmm(jnp.where(lm, a_m, 0.0), bd(tinv))
            yield
            tinv = tinv - pmm(tinv, bd(w_))
            yield

        kq_y = pmm(tinv, jnp.concatenate([bd(kq), bd(gv[:C])], axis=1))
        kq1 = kq_y[:, :W]
        y = kq_y[:, W:]
        yield
        aq_ky = pmm(aq_m, jnp.concatenate([bd(kq1), bd(y)], axis=1))
        rq1 = rq - aq_ky[:, :W]
        o_loc = gv[C:] - aq_ky[:, W:]
        yield
        st = _dot(jnp.concatenate([rq1, kq1], axis=0).astype(BF16), m0.astype(BF16))
        o = st[:C] + o_loc
        z = -(st[C:] + y)
        yield
        upd = _dot(lhs_t, jnp.concatenate([z, v], axis=0).astype(BF16))
        return o, m_decayed + jnp.where(same_head, upd, 0.0)

    def interleave(gens):
        results = [None] * len(gens)
        live = list(range(len(gens)))
        while live:
            for i in list(live):
                try:
                    next(gens[i])
                except StopIteration as done:
                    results[i] = done.value
                    live.remove(i)
        return results

    def body(ci, carry):
        loaded = [chunk_load(ci, pi) for pi in range(n_pairs)]
        results = interleave([chunk_compute(pi, *loaded[pi]) for pi in range(n_pairs)])
        rows = pl.ds(pl.multiple_of(ci * C, C), C)
        for pi, (o, m_new) in enumerate(results):
            o_ref[rows, pi * W:(pi + 1) * W] = o
            state_ref[pi] = m_new
        return carry

    lax.fori_loop(0, n_chunks, body, 0)

    for cols in col_groups:
        o = o_ref[:, cols]
        r = r_ref[:, cols]
        v = v_ref[:, cols]
        k2 = k_ref[:, cols] * (1.0 + (a_ref[:, cols] - 1.0) * ka_ref[:, cols])
        oc = o - head_sum(o) * (1.0 / N)
        var_o = head_sum(oc * oc) * (1.0 / N)
        out = oc * lax.rsqrt(var_o + GN_EPS) * gng_ref[:, cols] + gnb_ref[:, cols]
        out = out + head_sum(r * k2 * rk_ref[:, cols]) * v
        o_ref[:, cols] = out * g_ref[:, cols]


def _rwkv_layer_kernel(x_ref, xh_ref, mu_ref, wrkv_ref, w0_ref, w1_ref, w2_ref, a0_ref, a1_ref, a2_ref,
                       g1_ref, g2_ref, kk_ref, ka_ref, rk_ref, gng_ref, gnb_ref, wo_ref, lng_ref, lnb_ref,
                       o_ref, state_ref, r_s, k_s, v_s, dl_s, a_s, g_s, kkn_s, cum_s, y_s, *, n_chunks):
    first = pl.program_id(1) == 0
    _rwkv_project(first, x_ref, xh_ref, mu_ref, wrkv_ref, w0_ref, w1_ref, w2_ref, a0_ref, a1_ref, a2_ref,
                  g1_ref, g2_ref, r_s, k_s, v_s, dl_s, a_s, g_s)
    _rwkv_recurrence(first, r_s, k_s, v_s, dl_s, a_s, g_s, kk_ref, ka_ref, rk_ref, gng_ref, gnb_ref, y_s,
                     state_ref, kkn_s, cum_s, n_pairs=x_ref.shape[1] // (2 * RWKV_HEAD), n_chunks=n_chunks)
    mix = _dot(y_s[...].astype(BF16), wo_ref[...])
    o_ref[...] = _layer_norm(DEEPNORM_ALPHA * x_ref[...] + mix, lng_ref[...], lnb_ref[...])


def _rwkv_layer(x2d, batch, seq, mu, w_rkv, w0, w1, w2, a0, a1, a2, g1, g2, k_k, k_a, r_k, gn_g, gn_b,
                w_o, ln_g, ln_b, tb=256):
    m, d = x2d.shape
    nt = seq // tb
    xh = x2d.reshape(m // 8, 8, d)
    vec = lambda z: z.reshape(1, d)
    blk = pl.BlockSpec((tb, d), lambda b, t: (b * nt + t, 0))
    weights = [mu, w_rkv, vec(w0), w1, w2, vec(a0), a1, a2, g1, g2, vec(k_k), vec(k_a), vec(r_k),
               vec(gn_g), vec(gn_b), w_o, vec(ln_g), vec(ln_b)]
    n_pairs = d // (2 * RWKV_HEAD)
    return pl.pallas_call(
        functools.partial(_rwkv_layer_kernel, n_chunks=tb // SCAN_CHUNK),
        grid=(batch, nt),
        in_specs=[blk, pl.BlockSpec((1, 8, d), lambda b, t: (jnp.maximum((b * nt + t) * (tb // 8) - 1, 0), 0, 0))]
        + [_resident(w.shape) for w in weights],
        out_specs=blk,
        out_shape=jax.ShapeDtypeStruct((m, d), F32),
        scratch_shapes=[pltpu.VMEM((n_pairs, 2 * RWKV_HEAD, 2 * RWKV_HEAD), F32)]
        + [pltpu.VMEM((tb, d), F32)] * 9,
        compiler_params=_cparams("parallel", "arbitrary"),
        name="rwkv_layer",
    )(x2d, xh, *weights)


def _proj_ln_router_kernel(y_ref, x_ref, w_ref, lng_ref, lnb_ref, wr_ref, br_ref, o_ref, ob_ref, lg_ref):
    mix = _dot(y_ref[...].astype(BF16), w_ref[...])
    xn = _layer_norm(DEEPNORM_ALPHA * x_ref[...] + mix, lng_ref[...], lnb_ref[...])
    o_ref[...] = xn
    xh, xl = _split2(xn)
    ob_ref[...] = xh
    lg_ref[...] = _dot(xh, wr_ref[0]) + _dot(xh, wr_ref[1]) + _dot(xl, wr_ref[0]) + br_ref[...]


def _proj_ln_router(y, x2d, w, ln_g, ln_b, w_router, b_router, tm=512):
    m, d = x2d.shape
    row = lambda i: (i, 0)
    wr = jnp.zeros((d, LANES), F32).at[:, :N_EXPERTS].set(w_router)
    wr3 = jnp.stack(_split2(wr))
    br =jnp.zeros((1, LANES), F32).at[0, :N_EXPERTS].set(b_router)
    return pl.pallas_call(
        _proj_ln_router_kernel,
        grid=(m // tm,),
        in_specs=[pl.BlockSpec((tm, y.shape[1]), row), pl.BlockSpec((tm, d), row), _resident(w.shape),
                  _resident((1, d)), _resident((1, d)), _resident(wr3.shape), _resident(br.shape)],
        out_specs=[pl.BlockSpec((tm, d), row), pl.BlockSpec((tm, d), row), pl.BlockSpec((tm, LANES), row)],
        out_shape=[jax.ShapeDtypeStruct((m, d), F32), jax.ShapeDtypeStruct((m, d), BF16),
                   jax.ShapeDtypeStruct((m, LANES), F32)],
        compiler_params=_cparams("parallel"),
        name="proj_ln_router",
    )(y, x2d, w, ln_g.reshape(1, d), ln_b.reshape(1, d), wr3, br)


def _ple(xn, p, plew_ref, gw_ref, gb_ref):
    gate = _sigmoid(_dot(xn.astype(BF16), gw_ref[...]) + gb_ref[...])
    return xn + _dot(p.astype(BF16), plew_ref[...]) * gate


def _ffn_ple_kernel(x_ref, p_ref, wg_ref, wu_ref, wd_ref, lng_ref, lnb_ref, plew_ref, gw_ref, gb_ref,
                    o_ref, *, ff_chunk):
    x = x_ref[...]
    xb = x.astype(BF16)
    acc = jnp.zeros_like(x)
    for c in range(wg_ref.shape[1] // ff_chunk):
        sl = slice(c * ff_chunk, (c + 1) * ff_chunk)
        gt = _dot(xb, wg_ref[:, sl])
        up = _dot(xb, wu_ref[:, sl])
        h = (gt * _sigmoid(gt) * up).astype(BF16)
        acc = acc + _dot(h, wd_ref[sl, :])
    xn = _layer_norm(DEEPNORM_ALPHA * x + acc, lng_ref[...], lnb_ref[...])
    o_ref[...] = _ple(xn, p_ref[0], plew_ref, gw_ref, gb_ref)


def _ple_block(p3d, layer, tm):
    return pl.BlockSpec((1, tm, p3d.shape[2]), lambda i: (layer, i, 0))


def _ffn_ple(x2d, p3d, layer, wg, wu, wd, ln_g, ln_b, ple_w, gate_w, gate_b, tm=512, ff_chunk=1408):
    m, d = x2d.shape
    row = lambda i: (i, 0)
    return pl.pallas_call(
        functools.partial(_ffn_ple_kernel, ff_chunk=ff_chunk),
        grid=(m // tm,),
        in_specs=[pl.BlockSpec((tm, d), row), _ple_block(p3d, layer, tm),
                  _resident(wg.shape), _resident(wu.shape), _resident(wd.shape),
                  _resident((1, d)), _resident((1, d)), _resident(ple_w.shape), _resident(gate_w.shape),
                  _resident((1, d))],
        out_specs=pl.BlockSpec((tm, d), row),
        out_shape=jax.ShapeDtypeStruct((m, d), F32),
        compiler_params=_cparams("parallel"),
        name="ffn_ple",
    )(x2d, p3d, wg, wu, wd, ln_g.reshape(1, d), ln_b.reshape(1, d), ple_w, gate_w, gate_b.reshape(1, d))


def _moe_combine_ln_ple_kernel(x_ref, y0_ref, y1_ref, rw_ref, p_ref, lng_ref, lnb_ref, plew_ref, gw_ref,
                               gb_ref, o_ref):
    rw = rw_ref[...]
    ffn = y0_ref[...] * rw[:, 0:1] + y1_ref[...] * rw[:, 1:2]
    xn = _layer_norm(DEEPNORM_ALPHA * x_ref[...] + ffn, lng_ref[...], lnb_ref[...])
    o_ref[...] = _ple(xn, p_ref[0], plew_ref, gw_ref, gb_ref)


def _moe_combine_ln_ple(x2d, y0, y1, route_w, p3d, layer, ln_g, ln_b, ple_w, gate_w, gate_b, tm=512):
    m, d = x2d.shape
    row = lambda i: (i, 0)
    tile = pl.BlockSpec((tm, d), row)
    return pl.pallas_call(
        _moe_combine_ln_ple_kernel,
        grid=(m // tm,),
        in_specs=[tile, tile, tile, pl.BlockSpec((tm, TOP_K), row), _ple_block(p3d, layer, tm),
                  _resident((1, d)), _resident((1, d)), _resident(ple_w.shape), _resident(gate_w.shape),
                  _resident((1, d))],
        out_specs=tile,
        out_shape=jax.ShapeDtypeStruct((m, d), F32),
        compiler_params=_cparams("parallel"),
        name="moe_combine_ln_ple",
    )(x2d, y0, y1, route_w, p3d, ln_g.reshape(1, d), ln_b.reshape(1, d), ple_w, gate_w, gate_b.reshape(1, d))


def _swap_half_cols(w):
    k, n = w.shape
    return w.reshape(k, n // HEAD_DIM, 2, HEAD_DIM // 2)[:, :, ::-1, :].reshape(k, n)


def _rope_tables(seq):
    half = HEAD_DIM // 2
    inv = ROPE_THETA ** (-jnp.arange(half, dtype=F32) / half)
    ang = jnp.arange(seq, dtype=F32)[:, None] * inv[None, :]
    cos, sin = jnp.cos(ang), jnp.sin(ang)
    return jnp.concatenate([cos, cos], -1), jnp.concatenate([-sin, sin], -1)


def _nsa_kv_kernel(x_ref, wk_ref, wvt_ref, cos_ref, sin_ref, zc_ref, vc_ref, ks_ref, kw_ref, vst_ref,
                   vwt_ref):
    G, dh = N_KV_GROUPS, HEAD_DIM
    gw = G * dh
    xb = x_ref[...].astype(BF16)
    res = _dot(xb, wk_ref[...])
    res_t = _dot_nt(wvt_ref[...], xb)
    tm = res.shape[0]
    cos = cos_ref[...]
    sin = sin_ref[...]
    ks = res[:, 2 * gw:3 * gw] * cos + res[:, 3 * gw:4 * gw] * sin
    kw = res[:, 4 * gw:5 * gw] * cos + res[:, 5 * gw:6 * gw] * sin
    t_glob = pl.program_id(1) * tm + lax.broadcasted_iota(jnp.int32, (tm, dh), 0)
    onehot = jnp.where(lax.broadcasted_iota(jnp.int32, (tm, dh), 1) == t_glob // SEL_BLOCK, 1.0, 0.0)
    for g in range(G):
        sl = slice(g * dh, (g + 1) * dh)
        zc_ref[0, g] = res[:, sl].astype(BF16)
        vc_ref[0, g] = res[:, gw + g * dh:gw + (g + 1) * dh].astype(BF16)
        ks_ref[0, g] = jnp.concatenate([ks[:, sl], onehot], axis=1).astype(BF16)
        kw_ref[0, g] = kw[:, sl].astype(BF16)
    vst_ref[0] = res_t[:gw].astype(BF16)
    vwt_ref[0] = res_t[gw:].astype(BF16)


def _nsa_kv(x2d, w_kv, batch, seq, tm=256):
    m, d = x2d.shape
    G, dh = N_KV_GROUPS, HEAD_DIM
    gw = G * dh
    parts = [w_kv[:, j * gw:(j + 1) * gw] for j in range(6)]
    w_k = jnp.concatenate([parts[0], parts[1], parts[2], _swap_half_cols(parts[2]),
                           parts[4], _swap_half_cols(parts[4])], axis=1).astype(BF16)
    w_vt = jnp.concatenate([parts[3], parts[5]], axis=1).T.astype(BF16)
    cos, sin = _rope_tables(seq)
    cos = jnp.tile(cos, (1, G))
    sin = jnp.tile(sin, (1, G))
    nt = seq // tm
    o64 = jax.ShapeDtypeStruct((batch, G, seq, dh), BF16)
    o128 = jax.ShapeDtypeStruct((batch, G, seq, 2 * dh), BF16)
    ot = jax.ShapeDtypeStruct((batch, gw, seq), BF16)
    b64 = pl.BlockSpec((1, G, tm, dh), lambda b, t: (b, 0, t, 0))
    b128 = pl.BlockSpec((1, G, tm, 2 * dh), lambda b, t: (b, 0, t, 0))
    bt = pl.BlockSpec((1, gw, tm), lambda b, t: (b, 0, t))
    return pl.pallas_call(
        _nsa_kv_kernel,
        grid=(batch, nt),
        in_specs=[pl.BlockSpec((tm, d), lambda b, t: (b * nt + t, 0)), _resident(w_k.shape),
                  _resident(w_vt.shape),
                  pl.BlockSpec((tm, gw), lambda b, t: (t, 0)), pl.BlockSpec((tm, gw), lambda b, t: (t, 0))],
        out_specs=[b64, b64, b128, b64, bt, bt],
        out_shape=[o64, o64, o128, o64, ot, ot],
        compiler_params=_cparams("parallel", "parallel"),
        name="nsa_kv",
    )(x2d, w_k, w_vt, cos, sin)


def _nsa_cmp_kernel(z_ref, pos_ref, w1_ref, b1_ref, w2_ref, o_ref, *, slabs_per_seq):
    half = w1_ref.shape[1] // 2
    z = z_ref[0]
    tm = z.shape[0]
    first = _dot(z, w1_ref[0, :half, :])
    second = _dot(z, w1_ref[0, half:, :])
    const = _dot(pos_ref[0], w1_ref[0])[0:1, :] + b1_ref[0]
    hid = first + pltpu.roll(second, shift=tm - 1, axis=0) + const
    act = 0.5 * hid * (1.0 + jnp.tanh(math.sqrt(2.0 / math.pi) * (hid + 0.044715 * hid * hid * hid)))
    out = _dot(act.astype(BF16), w2_ref[0])
    row = lax.broadcasted_iota(jnp.int32, out.shape, 0)
    o_ref[0] = jnp.where(row % slabs_per_seq == slabs_per_seq - 1, 0.0, out).astype(BF16)


def _nsa_compress(zc, vc, cmp_pos, phi_w1, phi_b1, phi_w2, tm=512):
    batch, G, seq, dh = zc.shape
    slab = CMP_STRIDE * dh
    rows = batch * G * seq // CMP_STRIDE
    tm = min(tm, rows)
    z = jnp.stack([zc.reshape(rows, slab), vc.reshape(rows, slab)])
    pos = jnp.broadcast_to(cmp_pos.reshape(2, 1, CMP_LEN * dh), (2, 8, CMP_LEN * dh)).astype(BF16)
    out = pl.pallas_call(
        functools.partial(_nsa_cmp_kernel, slabs_per_seq=seq // CMP_STRIDE),
        grid=(2, rows // tm),
        in_specs=[pl.BlockSpec((1, tm, slab), lambda j, i: (j, i, 0)),
                  pl.BlockSpec((1, 8, CMP_LEN * dh), lambda j, i: (j, 0, 0)),
                  pl.BlockSpec((1, CMP_LEN * dh, CMP_HIDDEN), lambda j, i: (j, 0, 0)),
                  pl.BlockSpec((1, 1, CMP_HIDDEN), lambda j, i: (j, 0, 0)),
                  pl.BlockSpec((1, CMP_HIDDEN, dh), lambda j, i: (j, 0, 0))],
        out_specs=pl.BlockSpec((1, tm, dh), lambda j, i: (j, i, 0)),
        out_shape=jax.ShapeDtypeStruct((2, rows, dh), BF16),
        compiler_params=_cparams("parallel", "parallel"),
        name="nsa_compress",
    )(z, pos, phi_w1.astype(BF16), phi_b1.reshape(2, 1, CMP_HIDDEN), phi_w2.astype(BF16))
    n_slab = seq // CMP_STRIDE
    return out[0].reshape(batch, G, n_slab, dh), out[1].reshape(batch, G, n_slab, dh)


GATE_ROWS = 16


def _nsa_q_kernel(x_ref, wt_ref, cos_ref, sin_ref, q_ref, qr_ref, gate_ref):
    dh = HEAD_DIM
    hw = N_HEADS * dh
    scale = HEAD_DIM ** -0.5 * math.log2(math.e)
    res_t = _dot_nt(wt_ref[...], x_ref[...].astype(BF16))
    tm = res_t.shape[1]
    q = res_t[:hw].reshape(N_HEADS, dh, tm)
    q_sw = jnp.concatenate([q[:, dh // 2:], q[:, :dh // 2]], axis=1)
    qr = q * cos_ref[...][None] + q_sw * sin_ref[...][None]
    q_ref[0] = (q * scale).reshape(hw, tm).astype(BF16)
    qr_ref[0] = (qr * scale).reshape(hw, tm).astype(BF16)
    gate_ref[0] = _sigmoid(res_t[hw:])


def _nsa_q(x2d, w_in, batch, seq, tm=512):
    m, d = x2d.shape
    G, R, dh = N_KV_GROUPS, Q_PER_GROUP, HEAD_DIM
    hw = N_HEADS * dh
    w_gate = w_in[:, hw:].reshape(d, G, R * N_BRANCH)
    w_gate = jnp.pad(w_gate, ((0, 0), (0, 0), (0, GATE_ROWS - R * N_BRANCH))).reshape(d, G * GATE_ROWS)
    wt = jnp.concatenate([w_in[:, :hw], w_gate], axis=1).T.astype(BF16)
    cos, sin = _rope_tables(seq)
    nt = seq // tm
    oq = jax.ShapeDtypeStruct((batch, hw, seq), BF16)
    bq = pl.BlockSpec((1, hw, tm), lambda b, t: (b, 0, t))
    tab = pl.BlockSpec((dh, tm), lambda b, t: (0, t))
    return pl.pallas_call(
        _nsa_q_kernel,
        grid=(batch, nt),
        in_specs=[pl.BlockSpec((tm, d), lambda b, t: (b * nt + t, 0)), _resident(wt.shape), tab, tab],
        out_specs=[bq, bq, pl.BlockSpec((1, G * GATE_ROWS, tm), lambda b, t: (b, 0, t))],
        out_shape=[oq, oq, jax.ShapeDtypeStruct((batch, G * GATE_ROWS, seq), F32)],
        compiler_params=_cparams("parallel", "parallel"),
        name="nsa_q",
    )(x2d, wt, cos.T, sin.T)


def _nsa_attn_kernel(q_ref, qr_ref, gate_ref, kc_ref, vct_ref, ks_ref, vst_ref, kw_ref, vwt_ref, ovlt_ref,
                     o_ref, *, tq, tk, n_blk):
    R, dh = Q_PER_GROUP, HEAD_DIM
    cols = R * tq
    qi = pl.program_id(2)
    t0 = qi * tq
    n_blk_pad = ovlt_ref.shape[0]

    def heads_to_lanes(x):
        return jnp.concatenate([x[r * dh:(r + 1) * dh] for r in range(R)], axis=1)

    q = heads_to_lanes(q_ref[0])
    qr = heads_to_lanes(qr_ref[0])

    def col_pos(shape):
        return t0 + (lax.broadcasted_iota(jnp.int32, shape, 1) & (tq - 1))

    def key_tile(j):
        return pl.ds(pl.multiple_of(j * tk, tk), tk)

    ct = tq
    col_tiles = [slice(c, c + ct) for c in range(0, cols, ct)]
    ones_rows = 16
    ones_blk = jnp.ones((ones_rows, tk), BF16)

    def tile_scores(k_ref, q_op, tiles):
        return [[_dot(k_ref[0, 0, key_tile(j), :], q_op[:, cs]) for j, _ in tiles] for cs in col_tiles]

    def attend(carry, sc, vt_ref, tiles):
        m_i, acc = carry
        outs = []
        for cs, sc_c in zip(col_tiles, sc):
            masked = [s_j if mask_fn is None else jnp.where(mask_fn(k_off, q_off), s_j, NEG_INF)
                      for s_j, (_, mask_fn) in zip(sc_c, tiles)]
            m_new = m_i[:, cs]
            for s_j in masked:
                m_new = jnp.maximum(m_new, jnp.max(s_j, axis=0, keepdims=True))
            acc_new = jnp.exp2(m_i[:, cs] - m_new) * acc[:, cs]
            for s_j, (j, _) in zip(masked, tiles):
                v_ext = jnp.concatenate([vt_ref[0, :, key_tile(j)], ones_blk], axis=0)
                acc_new = acc_new + _dot(v_ext, jnp.exp2(s_j - m_new).astype(BF16))
            outs.append((m_new, acc_new))
        return tuple(jnp.concatenate([o[i] for o in outs], axis=1) for i in range(2))

    def normalised(acc):
        return acc[:dh] / acc[dh:dh + 1]

    init = (jnp.full((1, cols), NEG_INF, F32), jnp.zeros((dh + ones_rows, cols), F32))

    k_off = lax.broadcasted_iota(jnp.int32, (tk, ct), 0)
    q_off = lax.broadcasted_iota(jnp.int32, (tk, ct), 1)
    causal = lambda k, q_: k <= q_
    present = lambda cond: (lambda k, q_: (k >= 0) & cond)

    win_tiles = [(jnp.maximum(qi - 2, 0), lambda k, q_: (q_ < k) & (qi >= 2)),
                 (jnp.maximum(qi - 1, 0), present(qi >= 1)),
                 (qi, causal)]
    sc_win = tile_scores(kw_ref, qr, win_tiles)

    s = _dot(kc_ref[0, 0], q)
    cmp_end = lax.broadcasted_iota(jnp.int32, s.shape, 0) * CMP_STRIDE + (CMP_LEN - 1)
    valid = cmp_end <= col_pos(s.shape)
    sm = jnp.where(valid, s, NEG_INF)
    e = jnp.where(valid, jnp.exp2(sm - jnp.max(sm, axis=0, keepdims=True)), 0.0)
    l = jnp.sum(e, axis=0, keepdims=True)
    p_cmp = e / jnp.where(l > 0.0, l, 1.0)
    o_cmp = _dot(vct_ref[0, 0], p_cmp.astype(BF16))

    p_sum = p_cmp[:, 0:tq]
    for r in range(1, R):
        p_sum = p_sum + p_cmp[:, r * tq:(r + 1) * tq]
    hi, lo = _split2(p_sum)
    imp = _dot(ovlt_ref[...], hi) + _dot(ovlt_ref[...], lo)
    blk = lax.broadcasted_iota(jnp.int32, imp.shape, 0)
    t_q = t0 + lax.broadcasted_iota(jnp.int32, imp.shape, 1)
    cur = t_q // SEL_BLOCK
    forced = (blk == 0) | (blk == cur) | (blk == cur - 1)
    score = jnp.where(forced, FORCE_SCORE, jnp.where(blk * SEL_BLOCK <= t_q, imp, NEG_INF))
    score = jnp.where(blk < n_blk, score, -jnp.inf)
    selected = blk >= n_blk
    for _ in range(N_SEL):
        best = jnp.max(score, axis=0, keepdims=True)
        first = jnp.min(jnp.where(score == best, blk, 2 * LANES), axis=0, keepdims=True)
        pick = blk == first
        selected = selected | pick
        score = jnp.where(pick, -jnp.inf, score)
    bias = jnp.where(selected, 0.0, NEG_INF)
    qa = jnp.concatenate([qr, jnp.concatenate([bias] * R, axis=1).astype(BF16),
                          jnp.zeros((dh - n_blk_pad, cols), BF16)], axis=0)

    o_win = normalised(attend(init, sc_win, vwt_ref, win_tiles)[1])

    n_pairs = qi // 2

    def pair_body(i, carry):
        tiles = [(2 * i, None), (2 * i + 1, None)]
        return attend(carry, tile_scores(ks_ref, qa, tiles), vst_ref, tiles)

    carry = lax.fori_loop(0, n_pairs, pair_body, init)
    j_odd = 2 * n_pairs
    tail_tiles = [(j_odd, present(j_odd < qi)), (qi, causal)]
    o_slc = normalised(attend(carry, tile_scores(ks_ref, qa, tail_tiles), vst_ref, tail_tiles)[1])

    gates = gate_ref[0]
    outs = []
    for r in range(R):
        cs = slice(r * tq, (r + 1) * tq)
        outs.append(gates[3 * r:3 * r + 1] * o_cmp[:, cs] + gates[3 * r + 1:3 * r + 2] * o_slc[:, cs]
                    + gates[3 * r + 2:3 * r + 3] * o_win[:, cs])
    halves = [jnp.transpose(jnp.concatenate(outs[i:i + 2], axis=0)) for i in range(0, R, 2)]
    o_ref[0] = jnp.concatenate(halves, axis=1)


SEL_ROWS = 32


def _nsa_attention(q, qr, gates, k_cmp, v_cmp, ks, vst, kw, vwt, tq=256, tk=256):
    batch, _, seq = q.shape
    G, R, dh = N_KV_GROUPS, Q_PER_GROUP, HEAD_DIM
    n_slab = k_cmp.shape[2]
    n_blk = seq // SEL_BLOCK
    assert n_blk <= SEL_ROWS
    assert tq == tk and WINDOW == 2 * tk, "the window branch is written as exactly three key tiles"
    cmp_start = jnp.arange(n_slab) * CMP_STRIDE
    blk_start = jnp.arange(SEL_ROWS) * SEL_BLOCK
    overlap_t = ((cmp_start[None, :] < blk_start[:, None] + SEL_BLOCK)
                 & (cmp_start[None, :] + CMP_LEN - 1 >= blk_start[:, None])
                 & (jnp.arange(SEL_ROWS)[:, None] < n_blk)
                 & (jnp.arange(n_slab)[None, :] < (seq - CMP_LEN) // CMP_STRIDE + 1)).astype(BF16)
    v_cmp_t = jnp.swapaxes(v_cmp, 2, 3)
    qspec = pl.BlockSpec((1, R * dh, tq), lambda b, g, i: (b, g, i))
    full = lambda a: pl.BlockSpec((1, 1) + a.shape[2:], lambda b, g, i: (b, g, 0, 0))
    vspec = pl.BlockSpec((1, dh, seq), lambda b, g, i: (b, g, 0))
    return pl.pallas_call(
        functools.partial(_nsa_attn_kernel, tq=tq, tk=tk, n_blk=n_blk),
        grid=(batch, G, seq // tq),
        in_specs=[qspec, qspec, pl.BlockSpec((1, GATE_ROWS, tq), lambda b, g, i: (b, g, i)),
                  full(k_cmp), full(v_cmp_t), full(ks), vspec, full(kw), vspec,
                  _resident(overlap_t.shape)],
        out_specs=pl.BlockSpec((1, tq, R * dh), lambda b, g, i: (b, i, g)),
        out_shape=jax.ShapeDtypeStruct((batch, seq, G * R * dh), F32),
        compiler_params=_cparams("parallel", "parallel", "parallel"),
        name="nsa_attn",
    )(q, qr, gates, k_cmp, v_cmp_t, ks, vst, kw, vwt, overlap_t)


def _moe_kernel(be_ref, nb_ref, x_ref, wgu_ref, wd_ref, o_ref, wgu_bf, wd_bf):
    i = pl.program_id(0)
    ff = wd_ref.shape[1]
    used = i < nb_ref[0]

    @pl.when(used & ((i == 0) | (be_ref[i] != be_ref[jnp.maximum(i - 1, 0)])))
    def _():
        wgu_bf[...] = wgu_ref[0].astype(BF16)
        wd_bf[...] = wd_ref[0].astype(BF16)

    @pl.when(used)
    def _():
        xb = x_ref[...]
        gt = _dot(xb, wgu_bf[:, :ff])
        up = _dot(xb, wgu_bf[:, ff:])
        h = (gt * _sigmoid(gt) * up).astype(BF16)
        o_ref[...] = _dot(h, wd_bf[...])

    @pl.when(jnp.logical_not(used))
    def _():
        o_ref[...] = jnp.zeros_like(o_ref)


def _moe_experts(xs, block_e, n_used, w_gu, w_down):
    n_rows, d = xs.shape
    n_blocks = n_rows // MOE_ROWS
    ff = w_down.shape[1]
    grid_spec = pltpu.PrefetchScalarGridSpec(
        num_scalar_prefetch=2,
        grid=(n_blocks,),
        in_specs=[pl.BlockSpec((MOE_ROWS, d), lambda i, be, nb: (i, 0)),
                  pl.BlockSpec((1, d, 2 * ff), lambda i, be, nb: (be[i], 0, 0)),
                  pl.BlockSpec((1, ff, d), lambda i, be, nb: (be[i], 0, 0))],
        out_specs=pl.BlockSpec((MOE_ROWS, d), lambda i, be, nb: (i, 0)),
        scratch_shapes=[pltpu.VMEM((d, 2 * ff), BF16), pltpu.VMEM((ff, d), BF16)],
    )
    return pl.pallas_call(
        _moe_kernel,
        grid_spec=grid_spec,
        out_shape=jax.ShapeDtypeStruct((n_rows, d), F32),
        compiler_params=_cparams("arbitrary"),
        name="moe_experts",
    )(block_e, n_used, xs, w_gu, w_down)


def _moe(x2d, logits, w_gu, w_down):
    n_tok, d = x2d.shape
    n_assign = n_tok * TOP_K
    top_logit, top_e = lax.top_k(logits, TOP_K)
    weights = jax.nn.softmax(top_logit, axis=-1)
    flat_e = top_e.reshape(-1).astype(jnp.int32)
    onehot = (flat_e[:, None] == jnp.arange(N_EXPERTS, dtype=jnp.int32)[None, :]).astype(jnp.int32)
    running = jnp.cumsum(onehot, axis=0)
    counts = running[-1]
    padded = (counts + MOE_ROWS - 1) // MOE_ROWS * MOE_ROWS
    pad_end = jnp.cumsum(padded)
    pad_start = pad_end - padded
    grp_start = jnp.cumsum(counts) - counts
    pos = jnp.sum(onehot * (running - 1 + pad_start[None, :]), axis=1).reshape(n_tok, TOP_K)
    n_blocks = -(-n_assign // MOE_ROWS) + N_EXPERTS
    block_e = jnp.minimum(jnp.searchsorted(pad_end, jnp.arange(n_blocks) * MOE_ROWS, side='right'),
                          N_EXPERTS - 1).astype(jnp.int32)
    n_used = (pad_end[-1] // MOE_ROWS).astype(jnp.int32).reshape(1)
    order = jnp.argsort(flat_e)
    tok_sorted = (order // TOP_K).astype(jnp.int32)
    row_in_grp = (jnp.arange(n_blocks, dtype=jnp.int32) * MOE_ROWS - pad_start[block_e])[:, None] \
        + jnp.arange(MOE_ROWS, dtype=jnp.int32)[None, :]
    src = jnp.clip(grp_start[block_e][:, None] + row_in_grp, 0, n_assign - 1)
    row_tok = jnp.where(row_in_grp < counts[block_e][:, None], tok_sorted[src], 0).reshape(-1)
    xs = x2d[row_tok]
    ys = _moe_experts(xs, block_e, n_used, w_gu, w_down)
    return [ys[pos[:, s]] for s in range(TOP_K)], weights


def kernel(x, p, a_mu, a_w_rkv, a_w0, a_w1, a_w2, a_a0, a_a1, a_a2, a_g1, a_g2, a_k_k, a_k_a, a_r_k,
           a_gn_g, a_gn_b, a_w_o, b_w_kv, b_cmp_pos, b_phi_w1, b_phi_b1, b_phi_w2, b_w_in, b_w_o,
           f_w_gu, f_w_down, m_w_router, m_b_router, m_w_gu, m_w_down, ln_g, ln_b, ple_w, ple_gate_w,
           ple_gate_b):
    batch, seq, d = x.shape
    m = batch * seq
    x0 = x.reshape(m, d)
    p2d = p.reshape(DEPTH, m, PLE_DIM)
    bf = lambda w: w.astype(BF16)

    x1 = _rwkv_layer(x0, batch, seq, a_mu[0], bf(a_w_rkv[0]), a_w0[0], bf(a_w1[0]), bf(a_w2[0]), a_a0[0],
                     bf(a_a1[0]), bf(a_a2[0]), bf(a_g1[0]), bf(a_g2[0]), a_k_k[0], a_k_a[0], a_r_k[0],
                     a_gn_g[0], a_gn_b[0], bf(a_w_o[0]), ln_g[0, 0], ln_b[0, 0])
    x2 = _ffn_ple(x1, p2d, 0, bf(f_w_gu[0][:, :D_FF]), bf(f_w_gu[0][:, D_FF:]), bf(f_w_down[0]),
                  ln_g[0, 1], ln_b[0, 1], bf(ple_w[0]), bf(ple_gate_w[0]), ple_gate_b[0])

    zc, vc, ks, kw, vst, vwt = _nsa_kv(x2, b_w_kv, batch, seq)
    k_cmp, v_cmp = _nsa_compress(zc, vc, b_cmp_pos, b_phi_w1, b_phi_b1, b_phi_w2)
    q, qr, gates = _nsa_q(x2, b_w_in[0], batch, seq)
    attn = _nsa_attention(q, qr, gates, k_cmp, v_cmp, ks, vst, kw, vwt).reshape(m, d)
    x3, x3_bf, logits = _proj_ln_router(attn, x2, bf(b_w_o[0]), ln_g[1, 0], ln_b[1, 0], m_w_router[0],
                                        m_b_router[0])
    (y0, y1), route_w = _moe(x3_bf, logits[:, :N_EXPERTS], m_w_gu[0], m_w_down[0])
    out = _moe_combine_ln_ple(x3, y0, y1, route_w, p2d, 1, ln_g[1, 1], ln_b[1, 1], bf(ple_w[1]),
                              bf(ple_gate_w[1]), ple_gate_b[1])
    return out.reshape(batch, seq, d)
```

```python
import functools
import math

import jax
import jax.numpy as jnp
from jax import lax
from jax.experimental import pallas as pl
from jax.experimental.pallas import tpu as pltpu

BF16 = jnp.bfloat16
F32 = jnp.float32

LANES = 128
VMEM_LIMIT_BYTES = 56 * 1024 * 1024

D_MODEL = 1024
PLE_DIM = 256
RWKV_HEAD = 64
GN_EPS = 64e-5
N_HEADS = 16
HEAD_DIM = 64
N_KV_GROUPS = 4
Q_PER_GROUP = 4
N_BRANCH = 3
CMP_LEN = 32
CMP_STRIDE = 16
CMP_HIDDEN = 256
SEL_BLOCK = 64
N_SEL = 8
WINDOW = 512
ROPE_THETA = 10000.0
D_FF = 2816
N_EXPERTS = 8
TOP_K = 2
D_FF_EXPERT = 1408
MOE_ROWS = 256
LN_EPS = 1e-5
DEPTH = 2
DEEPNORM_ALPHA = (2.0 * DEPTH) ** 0.25
NEG_INF = -1e30
FORCE_SCORE = 1e4

SCAN_CHUNK = 64


def _cparams(*sem):
    return pltpu.CompilerParams(dimension_semantics=sem, vmem_limit_bytes=VMEM_LIMIT_BYTES)


def _resident(shape):
    nd = len(shape)
    return pl.BlockSpec(shape, lambda *_: (0,) * nd, pipeline_mode=pl.Buffered(1))


def _dot(a, b):
    return jnp.dot(a, b, preferred_element_type=F32)


def _dot_nt(a, b):
    return lax.dot_general(a, b, (((1,), (1,)), ((), ())), preferred_element_type=F32)


def _split2(x):
    hi = x.astype(BF16)
    lo = (x - hi.astype(F32)).astype(BF16)
    return hi, lo


def _split3(x):
    hi = x.astype(BF16)
    r1 = x - hi.astype(F32)
    mid = r1.astype(BF16)
    lo = (r1 - mid.astype(F32)).astype(BF16)
    return hi, mid, lo


def _layer_norm(y, g, b):
    mu = jnp.mean(y, axis=-1, keepdims=True)
    yc = y - mu
    var = jnp.mean(yc * yc, axis=-1, keepdims=True)
    return yc * lax.rsqrt(var + LN_EPS) * g + b


def _sigmoid(z):
    return 1.0 / (1.0 + jnp.exp(-z))


def _rwkv_project(first, x_ref, xh_ref, mu_ref, wrkv_ref, w0_ref, w1_ref, w2_ref, a0_ref, a1_ref,
                  a2_ref, g1_ref, g2_ref, r_ref, k_ref, v_ref, dl_ref, a_ref, g_ref):
    x = x_ref[...]
    prev_row = jnp.where(first, 0.0, xh_ref[0, 7:8, :])
    row = lax.broadcasted_iota(jnp.int32, x.shape, 0)
    x_shift = jnp.where(row == 0, prev_row, pltpu.roll(x, shift=1, axis=0))
    xx = x_shift - x

    def mix(i):
        return (x + xx * mu_ref[i:i + 1, :]).astype(BF16)

    r_ref[...] = _dot(mix(0), wrkv_ref[0])
    k_ref[...] = _dot(mix(1), wrkv_ref[1])
    v_ref[...] = _dot(mix(2), wrkv_ref[2])
    z = w0_ref[...] + _dot(jnp.tanh(_dot(mix(3), w1_ref[...])).astype(BF16), w2_ref[...])
    dl_ref[...] = -math.exp(-0.5) * _sigmoid(z)
    a_ref[...] = _sigmoid(a0_ref[...] + _dot(_dot(mix(4), a1_ref[...]).astype(BF16), a2_ref[...]))
    g_ref[...] = _dot(_sigmoid(_dot(mix(5), g1_ref[...])).astype(BF16), g2_ref[...])


def _rwkv_recurrence(first, r_ref, k_ref, v_ref, dl_ref, a_ref, g_ref, kk_ref, ka_ref, rk_ref, gng_ref,
                     gnb_ref, o_ref, state_ref, kkn_ref, cum_ref, *, n_pairs, n_chunks):
    C = SCAN_CHUNK
    N = RWKV_HEAD
    W = 2 * N
    tb = n_chunks * C

    @pl.when(first)
    def _():
        state_ref[...] = jnp.zeros_like(state_ref)

    tt = lax.broadcasted_iota(jnp.int32, (C, W), 0)
    ln = lax.broadcasted_iota(jnp.int32, (C, W), 1)
    ss = ln & (N - 1)
    strict_lower = ss < tt
    lower = ss <= tt
    eye = jnp.where(ss == tt, 1.0, 0.0)
    level_masks = []
    for lg in range(int(math.log2(C))):
        level_masks.append(((tt >> (lg + 1)) == (ss >> (lg + 1))) & ((tt >> lg) == (ss >> lg) + 1))
    rr = lax.broadcasted_iota(jnp.int32, (W, W), 0)
    cc = lax.broadcasted_iota(jnp.int32, (W, W), 1)
    same_head = (rr < N) == (cc < N)
    lane_head0 = ln < N
    WG = 2 * W
    lg_n = int(math.log2(N))
    ones_grp = jnp.where((lax.broadcasted_iota(jnp.int32, (WG, WG), 0) >> lg_n)
                         == (lax.broadcasted_iota(jnp.int32, (WG, WG), 1) >> lg_n), 1.0, 0.0).astype(BF16)
    col_groups = [slice(c, c + WG) for c in range(0, n_pairs * W, WG)]

    def bd(y):
        return jnp.concatenate([jnp.where(lane_head0, y, 0.0), jnp.where(lane_head0, 0.0, y)],
                               axis=0).astype(BF16)

    def pmm(xp, ybd):
        return _dot(xp.astype(BF16), ybd)

    def head_sum(xg):
        return _dot(xg.astype(BF16), ones_grp)

    tri_r = lax.broadcasted_iota(jnp.int32, (tb, tb), 0)
    tri_c = lax.broadcasted_iota(jnp.int32, (tb, tb), 1)
    lg_c = int(math.log2(C))
    tri_chunk = jnp.where((tri_c <= tri_r) & ((tri_c >> lg_c) == (tri_r >> lg_c)), 1.0, 0.0).astype(BF16)
    d_hi, d_mid, d_lo = _split3(dl_ref[...])
    cum_ref[...] = _dot(tri_chunk, d_hi) + _dot(tri_chunk, d_mid) + _dot(tri_chunk, d_lo)
    for cols in col_groups:
        kk_raw = k_ref[:, cols] * kk_ref[:, cols]
        kkn_ref[:, cols] = kk_raw / jnp.maximum(jnp.sqrt(head_sum(kk_raw * kk_raw)), 1e-12)

    def chunk_load(ci, pi):
        rows = pl.ds(pl.multiple_of(ci * C, C), C)
        cols = slice(pi * W, (pi + 1) * W)
        return (r_ref[rows, cols], k_ref[rows, cols], v_ref[rows, cols], dl_ref[rows, cols],
                a_ref[rows, cols], kkn_ref[rows, cols], cum_ref[rows, cols], state_ref[pi])

    def chunk_compute(pi, r, k, v, dl, a, kk, cum, m0):
        cols = slice(pi * W, (pi + 1) * W)
        k2 = k * (1.0 + (a - 1.0) * ka_ref[:, cols])
        b = a * kk
        p_incl = jnp.exp(cum)
        p_excl = jnp.exp(cum - dl)
        p_inv = jnp.exp(-cum)
        kq = kk * p_excl
        rq = r * p_incl
        bk = b * p_inv
        kkd = k2 * p_inv

        bkt = jnp.transpose(jnp.concatenate([b, k2], axis=0))
        cumt = jnp.transpose(jnp.concatenate([cum, cum], axis=0))
        last_col = cumt[:, C - 1:C]
        lhs_t = (bkt * jnp.exp(last_col - cumt)).astype(BF16)
        m_decayed = m0 * jnp.exp(last_col)

        lhs = jnp.concatenate([kq, rq], axis=0).astype(BF16)
        yield
        abg = _dot_nt(lhs, jnp.concatenate([bd(bk), bd(kkd)], axis=0))
        a_m = jnp.where(strict_lower, abg[:C, :W], 0.0)
        aq_m = jnp.where(lower, abg[C:, :W], 0.0)
        g_m = jnp.where(strict_lower, abg[:C, W:], 0.0)
        gq_m = jnp.where(lower, abg[C:, W:], 0.0)

        tinv = eye - jnp.where(level_masks[0], a_m, 0.0)
        yield
        gv = pmm(jnp.concatenate([g_m, gq_m], axis=0), bd(v))
        for lm in level_masks[1:]:
            w_ = pmm(jnp.where(lm, a_m, 0.0), bd(tinv))
            yield
            tinv = tinv - pmm(tinv, bd(w_))
            yield

        kq_y = pmm(tinv, jnp.concatenate([bd(kq), bd(gv[:C])], axis=1))
        kq1 = kq_y[:, :W]
        y = kq_y[:, W:]
        yield
        aq_ky = pmm(aq_m, jnp.concatenate([bd(kq1), bd(y)], axis=1))
        rq1 = rq - aq_ky[:, :W]
        o_loc = gv[C:] - aq_ky[:, W:]
        yield
        st = _dot(jnp.concatenate([rq1, kq1], axis=0).astype(BF16), m0.astype(BF16))
        o = st[:C] + o_loc
        z = -(st[C:] + y)
        yield
        upd = _dot(lhs_t, jnp.concatenate([z, v], axis=0).astype(BF16))
        return o, m_decayed + jnp.where(same_head, upd, 0.0)

    def interleave(gens):
        results = [None] * len(gens)
        live = list(range(len(gens)))
        while live:
            for i in list(live):
                try:
                    next(gens[i])
                except StopIteration as done:
                    results[i] = done.value
                    live.remove(i)
        return results

    def body(ci, carry):
        loaded = [chunk_load(ci, pi) for pi in range(n_pairs)]
        results = interleave([chunk_compute(pi, *loaded[pi]) for pi in range(n_pairs)])
        rows = pl.ds(pl.multiple_of(ci * C, C), C)
        for pi, (o, m_new) in enumerate(results):
            o_ref[rows, pi * W:(pi + 1) * W] = o
            state_ref[pi] = m_new
        return carry

    lax.fori_loop(0, n_chunks, body, 0)

    for cols in col_groups:
        o = o_ref[:, cols]
        r = r_ref[:, cols]
        v = v_ref[:, cols]
        k2 = k_ref[:, cols] * (1.0 + (a_ref[:, cols] - 1.0) * ka_ref[:, cols])
        oc = o - head_sum(o) * (1.0 / N)
        var_o = head_sum(oc * oc) * (1.0 / N)
        out = oc * lax.rsqrt(var_o + GN_EPS) * gng_ref[:, cols] + gnb_ref[:, cols]
        out = out + head_sum(r * k2 * rk_ref[:, cols]) * v
        o_ref[:, cols] = out * g_ref[:, cols]


def _rwkv_layer_kernel(x_ref, xh_ref, mu_ref, wrkv_ref, w0_ref, w1_ref, w2_ref, a0_ref, a1_ref, a2_ref,
                       g1_ref, g2_ref, kk_ref, ka_ref, rk_ref, gng_ref, gnb_ref, wo_ref, lng_ref, lnb_ref,
                       o_ref, state_ref, r_s, k_s, v_s, dl_s, a_s, g_s, kkn_s, cum_s, y_s, *, n_chunks):
    first = pl.program_id(1) == 0
    _rwkv_project(first, x_ref, xh_ref, mu_ref, wrkv_ref, w0_ref, w1_ref, w2_ref, a0_ref, a1_ref, a2_ref,
                  g1_ref, g2_ref, r_s, k_s, v_s, dl_s, a_s, g_s)
    _rwkv_recurrence(first, r_s, k_s, v_s, dl_s, a_s, g_s, kk_ref, ka_ref, rk_ref, gng_ref, gnb_ref, y_s,
                     state_ref, kkn_s, cum_s, n_pairs=x_ref.shape[1] // (2 * RWKV_HEAD), n_chunks=n_chunks)
    mix = _dot(y_s[...].astype(BF16), wo_ref[...])
    o_ref[...] = _layer_norm(DEEPNORM_ALPHA * x_ref[...] + mix, lng_ref[...], lnb_ref[...])


def _rwkv_layer(x2d, batch, seq, mu, w_rkv, w0, w1, w2, a0, a1, a2, g1, g2, k_k, k_a, r_k, gn_g, gn_b,
                w_o, ln_g, ln_b, tb=256):
    m, d = x2d.shape
    nt = seq // tb
    xh = x2d.reshape(m // 8, 8, d)
    vec = lambda z: z.reshape(1, d)
    blk = pl.BlockSpec((tb, d), lambda b, t: (b * nt + t, 0))
    weights = [mu, w_rkv, vec(w0), w1, w2, vec(a0), a1, a2, g1, g2, vec(k_k), vec(k_a), vec(r_k),
               vec(gn_g), vec(gn_b), w_o, vec(ln_g), vec(ln_b)]
    n_pairs = d // (2 * RWKV_HEAD)
    return pl.pallas_call(
        functools.partial(_rwkv_layer_kernel, n_chunks=tb // SCAN_CHUNK),
        grid=(batch, nt),
        in_specs=[blk, pl.BlockSpec((1, 8, d), lambda b, t: (jnp.maximum((b * nt + t) * (tb // 8) - 1, 0), 0, 0))]
        + [_resident(w.shape) for w in weights],
        out_specs=blk,
        out_shape=jax.ShapeDtypeStruct((m, d), F32),
        scratch_shapes=[pltpu.VMEM((n_pairs, 2 * RWKV_HEAD, 2 * RWKV_HEAD), F32)]
        + [pltpu.VMEM((tb, d), F32)] * 9,
        compiler_params=_cparams("parallel", "arbitrary"),
        name="rwkv_layer",
    )(x2d, xh, *weights)


def _proj_ln_router_kernel(y_ref, x_ref, w_ref, lng_ref, lnb_ref, wr_ref, br_ref, o_ref, ob_ref, lg_ref):
    mix = _dot(y_ref[...].astype(BF16), w_ref[...])
    xn = _layer_norm(DEEPNORM_ALPHA * x_ref[...] + mix, lng_ref[...], lnb_ref[...])
    o_ref[...] = xn
    xh, xl = _split2(xn)
    ob_ref[...] = xh
    lg_ref[...] = _dot(xh, wr_ref[0]) + _dot(xh, wr_ref[1]) + _dot(xl, wr_ref[0]) + br_ref[...]


def _proj_ln_router(y, x2d, w, ln_g, ln_b, w_router, b_router, tm=512):
    m, d = x2d.shape
    row = lambda i: (i, 0)
    wr = jnp.zeros((d, LANES), F32).at[:, :N_EXPERTS].set(w_router)
    wr3 = jnp.stack(_split2(wr))
    br =jnp.zeros((1, LANES), F32).at[0, :N_EXPERTS].set(b_router)
    return pl.pallas_call(
        _proj_ln_router_kernel,
        grid=(m // tm,),
        in_specs=[pl.BlockSpec((tm, y.shape[1]), row), pl.BlockSpec((tm, d), row), _resident(w.shape),
                  _resident((1, d)), _resident((1, d)), _resident(wr3.shape), _resident(br.shape)],
        out_specs=[pl.BlockSpec((tm, d), row), pl.BlockSpec((tm, d), row), pl.BlockSpec((tm, LANES), row)],
        out_shape=[jax.ShapeDtypeStruct((m, d), F32), jax.ShapeDtypeStruct((m, d), BF16),
                   jax.ShapeDtypeStruct((m, LANES), F32)],
        compiler_params=_cparams("parallel"),
        name="proj_ln_router",
    )(y, x2d, w, ln_g.reshape(1, d), ln_b.reshape(1, d), wr3, br)


def _ple(xn, p, plew_ref, gw_ref, gb_ref):
    gate = _sigmoid(_dot(xn.astype(BF16), gw_ref[...]) + gb_ref[...])
    return xn + _dot(p.astype(BF16), plew_ref[...]) * gate


def _ffn_ple_kernel(x_ref, p_ref, wg_ref, wu_ref, wd_ref, lng_ref, lnb_ref, plew_ref, gw_ref, gb_ref,
                    o_ref, *, ff_chunk):
    x = x_ref[...]
    xb = x.astype(BF16)
    acc = jnp.zeros_like(x)
    for c in range(wg_ref.shape[1] // ff_chunk):
        sl = slice(c * ff_chunk, (c + 1) * ff_chunk)
        gt = _dot(xb, wg_ref[:, sl])
        up = _dot(xb, wu_ref[:, sl])
        h = (gt * _sigmoid(gt) * up).astype(BF16)
        acc = acc + _dot(h, wd_ref[sl, :])
    xn = _layer_norm(DEEPNORM_ALPHA * x + acc, lng_ref[...], lnb_ref[...])
    o_ref[...] = _ple(xn, p_ref[0], plew_ref, gw_ref, gb_ref)


def _ple_block(p3d, layer, tm):
    return pl.BlockSpec((1, tm, p3d.shape[2]), lambda i: (layer, i, 0))


def _ffn_ple(x2d, p3d, layer, wg, wu, wd, ln_g, ln_b, ple_w, gate_w, gate_b, tm=512, ff_chunk=1408):
    m, d = x2d.shape
    row = lambda i: (i, 0)
    return pl.pallas_call(
        functools.partial(_ffn_ple_kernel, ff_chunk=ff_chunk),
        grid=(m // tm,),
        in_specs=[pl.BlockSpec((tm, d), row), _ple_block(p3d, layer, tm),
                  _resident(wg.shape), _resident(wu.shape), _resident(wd.shape),
                  _resident((1, d)), _resident((1, d)), _resident(ple_w.shape), _resident(gate_w.shape),
                  _resident((1, d))],
        out_specs=pl.BlockSpec((tm, d), row),
        out_shape=jax.ShapeDtypeStruct((m, d), F32),
        compiler_params=_cparams("parallel"),
        name="ffn_ple",
    )(x2d, p3d, wg, wu, wd, ln_g.reshape(1, d), ln_b.reshape(1, d), ple_w, gate_w, gate_b.reshape(1, d))


def _moe_combine_ln_ple_kernel(x_ref, y0_ref, y1_ref, rw_ref, p_ref, lng_ref, lnb_ref, plew_ref, gw_ref,
                               gb_ref, o_ref):
    rw = rw_ref[...]
    ffn = y0_ref[...] * rw[:, 0:1] + y1_ref[...] * rw[:, 1:2]
    xn = _layer_norm(DEEPNORM_ALPHA * x_ref[...] + ffn, lng_ref[...], lnb_ref[...])
    o_ref[...] = _ple(xn, p_ref[0], plew_ref, gw_ref, gb_ref)


def _moe_combine_ln_ple(x2d, y0, y1, route_w, p3d, layer, ln_g, ln_b, ple_w, gate_w, gate_b, tm=512):
    m, d = x2d.shape
    row = lambda i: (i, 0)
    tile = pl.BlockSpec((tm, d), row)
    return pl.pallas_call(
        _moe_combine_ln_ple_kernel,
        grid=(m // tm,),
        in_specs=[tile, tile, tile, pl.BlockSpec((tm, TOP_K), row), _ple_block(p3d, layer, tm),
                  _resident((1, d)), _resident((1, d)), _resident(ple_w.shape), _resident(gate_w.shape),
                  _resident((1, d))],
        out_specs=tile,
        out_shape=jax.ShapeDtypeStruct((m, d), F32),
        compiler_params=_cparams("parallel"),
        name="moe_combine_ln_ple",
    )(x2d, y0, y1, route_w, p3d, ln_g.reshape(1, d), ln_b.reshape(1, d), ple_w, gate_w, gate_b.reshape(1, d))


def _swap_half_cols(w):
    k, n = w.shape
    return w.reshape(k, n // HEAD_DIM, 2, HEAD_DIM // 2)[:, :, ::-1, :].reshape(k, n)


def _rope_tables(seq):
    half = HEAD_DIM // 2
    inv = ROPE_THETA ** (-jnp.arange(half, dtype=F32) / half)
    ang = jnp.arange(seq, dtype=F32)[:, None] * inv[None, :]
    cos, sin = jnp.cos(ang), jnp.sin(ang)
    return jnp.concatenate([cos, cos], -1), jnp.concatenate([-sin, sin], -1)


def _nsa_kv_kernel(x_ref, wk_ref, wvt_ref, cos_ref, sin_ref, slab_ref, ks_ref, kw_ref, vst_ref, vwt_ref,
                   cz_s):
    G, dh = N_KV_GROUPS, HEAD_DIM
    gw = G * dh
    xb = x_ref[...].astype(BF16)
    res = _dot(xb, wk_ref[...])
    res_t = _dot_nt(wvt_ref[...], xb)
    tm = res.shape[0]
    n_slab = tm // CMP_STRIDE
    for c in range(cz_s.shape[0]):
        cz_s[c] = res[:, c * LANES:(c + 1) * LANES]
        rows = [cz_s[c, pl.ds(l, n_slab, stride=CMP_STRIDE), :] for l in range(CMP_STRIDE)]
        for half in range(LANES // dh):
            jg = c * (LANES // dh) + half
            slab_ref[jg // G, 0, jg % G] = jnp.concatenate(
                [r[:, half * dh:(half + 1) * dh] for r in rows], axis=1).astype(BF16)
    cos = cos_ref[...]
    sin = sin_ref[...]
    ks = res[:, 2 * gw:3 * gw] * cos + res[:, 3 * gw:4 * gw] * sin
    kw = res[:, 4 * gw:5 * gw] * cos + res[:, 5 * gw:6 * gw] * sin
    t_glob = pl.program_id(1) * tm + lax.broadcasted_iota(jnp.int32, (tm, dh), 0)
    onehot = jnp.where(lax.broadcasted_iota(jnp.int32, (tm, dh), 1) == t_glob // SEL_BLOCK, 1.0, 0.0)
    for g in range(G):
        sl = slice(g * dh, (g + 1) * dh)
        ks_ref[0, g] = jnp.concatenate([ks[:, sl], onehot], axis=1).astype(BF16)
        kw_ref[0, g] = kw[:, sl].astype(BF16)
    vst_ref[0] = res_t[:gw].astype(BF16)
    vwt_ref[0] = res_t[gw:].astype(BF16)


def _nsa_kv(x2d, w_kv, batch, seq, tm=256):
    m, d = x2d.shape
    G, dh = N_KV_GROUPS, HEAD_DIM
    gw = G * dh
    parts = [w_kv[:, j * gw:(j + 1) * gw] for j in range(6)]
    w_k = jnp.concatenate([parts[0], parts[1], parts[2], _swap_half_cols(parts[2]),
                           parts[4], _swap_half_cols(parts[4])], axis=1).astype(BF16)
    w_vt = jnp.concatenate([parts[3], parts[5]], axis=1).T.astype(BF16)
    cos, sin = _rope_tables(seq)
    cos = jnp.tile(cos, (1, G))
    sin = jnp.tile(sin, (1, G))
    nt = seq // tm
    o64 = jax.ShapeDtypeStruct((batch, G, seq, dh), BF16)
    o128 = jax.ShapeDtypeStruct((batch, G, seq, 2 * dh), BF16)
    ot = jax.ShapeDtypeStruct((batch, gw, seq), BF16)
    b64 = pl.BlockSpec((1, G, tm, dh), lambda b, t: (b, 0, t, 0))
    b128 = pl.BlockSpec((1, G, tm, 2 * dh), lambda b, t: (b, 0, t, 0))
    bt = pl.BlockSpec((1, gw, tm), lambda b, t: (b, 0, t))
    slab_w = CMP_STRIDE * dh
    oslab = jax.ShapeDtypeStruct((2, batch, G, seq // CMP_STRIDE, slab_w), BF16)
    bslab = pl.BlockSpec((2, 1, G, tm // CMP_STRIDE, slab_w), lambda b, t: (0, b, 0, t, 0))
    return pl.pallas_call(
        _nsa_kv_kernel,
        grid=(batch, nt),
        in_specs=[pl.BlockSpec((tm, d), lambda b, t: (b * nt + t, 0)), _resident(w_k.shape),
                  _resident(w_vt.shape),
                  pl.BlockSpec((tm, gw), lambda b, t: (t, 0)), pl.BlockSpec((tm, gw), lambda b, t: (t, 0))],
        out_specs=[bslab, b128, b64, bt, bt],
        out_shape=[oslab, o128, o64, ot, ot],
        scratch_shapes=[pltpu.VMEM((2 * gw // LANES, tm, LANES), F32)],
        compiler_params=_cparams("parallel", "parallel"),
        name="nsa_kv",
    )(x2d, w_k, w_vt, cos, sin)


def _nsa_cmp_kernel(z_ref, pos_ref, w1_ref, b1_ref, w2_ref, o_ref, *, slabs_per_seq):
    half = w1_ref.shape[1] // 2
    z = z_ref[0]
    tm = z.shape[0]
    first = _dot(z, w1_ref[0, :half, :])
    second = _dot(z, w1_ref[0, half:, :])
    const = _dot(pos_ref[0], w1_ref[0])[0:1, :] + b1_ref[0]
    hid = first + pltpu.roll(second, shift=tm - 1, axis=0) + const
    act = 0.5 * hid * (1.0 + jnp.tanh(math.sqrt(2.0 / math.pi) * (hid + 0.044715 * hid * hid * hid)))
    out = _dot(act.astype(BF16), w2_ref[0])
    row = lax.broadcasted_iota(jnp.int32, out.shape, 0)
    o_ref[0] = jnp.where(row % slabs_per_seq == slabs_per_seq - 1, 0.0, out).astype(BF16)


def _nsa_compress(slabs, cmp_pos, phi_w1, phi_b1, phi_w2, tm=512):
    _, batch, G, n_slab, slab = slabs.shape
    dh = slab // CMP_STRIDE
    seq = n_slab * CMP_STRIDE
    rows = batch * G * n_slab
    tm = min(tm, rows)
    z = slabs.reshape(2, rows, slab)
    pos = jnp.broadcast_to(cmp_pos.reshape(2, 1, CMP_LEN * dh), (2, 8, CMP_LEN * dh)).astype(BF16)
    out = pl.pallas_call(
        functools.partial(_nsa_cmp_kernel, slabs_per_seq=seq // CMP_STRIDE),
        grid=(2, rows // tm),
        in_specs=[pl.BlockSpec((1, tm, slab), lambda j, i: (j, i, 0)),
                  pl.BlockSpec((1, 8, CMP_LEN * dh), lambda j, i: (j, 0, 0)),
                  pl.BlockSpec((1, CMP_LEN * dh, CMP_HIDDEN), lambda j, i: (j, 0, 0)),
                  pl.BlockSpec((1, 1, CMP_HIDDEN), lambda j, i: (j, 0, 0)),
                  pl.BlockSpec((1, CMP_HIDDEN, dh), lambda j, i: (j, 0, 0))],
        out_specs=pl.BlockSpec((1, tm, dh), lambda j, i: (j, i, 0)),
        out_shape=jax.ShapeDtypeStruct((2, rows, dh), BF16),
        compiler_params=_cparams("parallel", "parallel"),
        name="nsa_compress",
    )(z, pos, phi_w1.astype(BF16), phi_b1.reshape(2, 1, CMP_HIDDEN), phi_w2.astype(BF16))
    return out[0].reshape(batch, G, n_slab, dh), out[1].reshape(batch, G, n_slab, dh)


GATE_ROWS = 16


def _nsa_q_kernel(x_ref, wt_ref, cos_ref, sin_ref, q_ref, qr_ref, gate_ref):
    dh = HEAD_DIM
    hw = N_HEADS * dh
    scale = HEAD_DIM ** -0.5 * math.log2(math.e)
    res_t = _dot_nt(wt_ref[...], x_ref[...].astype(BF16))
    tm = res_t.shape[1]
    q = res_t[:hw].reshape(N_HEADS, dh, tm)
    q_sw = jnp.concatenate([q[:, dh // 2:], q[:, :dh // 2]], axis=1)
    qr = q * cos_ref[...][None] + q_sw * sin_ref[...][None]
    q_ref[0] = (q * scale).reshape(hw, tm).astype(BF16)
    qr_ref[0] = (qr * scale).reshape(hw, tm).astype(BF16)
    gate_ref[0] = _sigmoid(res_t[hw:])


def _nsa_q(x2d, w_in, batch, seq, tm=512):
    m, d = x2d.shape
    G, R, dh = N_KV_GROUPS, Q_PER_GROUP, HEAD_DIM
    hw = N_HEADS * dh
    w_gate = w_in[:, hw:].reshape(d, G, R * N_BRANCH)
    w_gate = jnp.pad(w_gate, ((0, 0), (0, 0), (0, GATE_ROWS - R * N_BRANCH))).reshape(d, G * GATE_ROWS)
    wt = jnp.concatenate([w_in[:, :hw], w_gate], axis=1).T.astype(BF16)
    cos, sin = _rope_tables(seq)
    nt = seq // tm
    oq = jax.ShapeDtypeStruct((batch, hw, seq), BF16)
    bq = pl.BlockSpec((1, hw, tm), lambda b, t: (b, 0, t))
    tab = pl.BlockSpec((dh, tm), lambda b, t: (0, t))
    return pl.pallas_call(
        _nsa_q_kernel,
        grid=(batch, nt),
        in_specs=[pl.BlockSpec((tm, d), lambda b, t: (b * nt + t, 0)), _resident(wt.shape), tab, tab],
        out_specs=[bq, bq, pl.BlockSpec((1, G * GATE_ROWS, tm), lambda b, t: (b, 0, t))],
        out_shape=[oq, oq, jax.ShapeDtypeStruct((batch, G * GATE_ROWS, seq), F32)],
        compiler_params=_cparams("parallel", "parallel"),
        name="nsa_q",
    )(x2d, wt, cos.T, sin.T)


def _nsa_attn_kernel(q_ref, qr_ref, gate_ref, kc_ref, vct_ref, ks_ref, vst_ref, kw_ref, vwt_ref, ovlt_ref,
                     o_ref, *, tq, tk, n_blk):
    R, dh = Q_PER_GROUP, HEAD_DIM
    cols = R * tq
    qi = pl.program_id(2)
    t0 = qi * tq
    n_blk_pad = ovlt_ref.shape[0]

    def heads_to_lanes(x):
        return jnp.concatenate([x[r * dh:(r + 1) * dh] for r in range(R)], axis=1)

    q = heads_to_lanes(q_ref[0])
    qr = heads_to_lanes(qr_ref[0])

    def col_pos(shape):
        return t0 + (lax.broadcasted_iota(jnp.int32, shape, 1) & (tq - 1))

    def key_tile(j):
        return pl.ds(pl.multiple_of(j * tk, tk), tk)

    ct = tq
    col_tiles = [slice(c, c + ct) for c in range(0, cols, ct)]
    ones_rows = 16
    ones_blk = jnp.ones((ones_rows, tk), BF16)

    def tile_scores(k_ref, q_op, tiles):
        return [[_dot(k_ref[0, 0, key_tile(j), :], q_op[:, cs]) for j, _ in tiles] for cs in col_tiles]

    def attend(carry, sc, vt_ref, tiles):
        m_i, acc = carry
        outs = []
        for cs, sc_c in zip(col_tiles, sc):
            masked = [s_j if mask_fn is None else jnp.where(mask_fn(k_off, q_off), s_j, NEG_INF)
                      for s_j, (_, mask_fn) in zip(sc_c, tiles)]
            m_new = m_i[:, cs]
            for s_j in masked:
                m_new = jnp.maximum(m_new, jnp.max(s_j, axis=0, keepdims=True))
            acc_new = jnp.exp2(m_i[:, cs] - m_new) * acc[:, cs]
            for s_j, (j, _) in zip(masked, tiles):
                v_ext = jnp.concatenate([vt_ref[0, :, key_tile(j)], ones_blk], axis=0)
                acc_new = acc_new + _dot(v_ext, jnp.exp2(s_j - m_new).astype(BF16))
            outs.append((m_new, acc_new))
        return tuple(jnp.concatenate([o[i] for o in outs], axis=1) for i in range(2))

    def normalised(acc):
        return acc[:dh] / acc[dh:dh + 1]

    init = (jnp.full((1, cols), NEG_INF, F32), jnp.zeros((dh + ones_rows, cols), F32))

    k_off = lax.broadcasted_iota(jnp.int32, (tk, ct), 0)
    q_off = lax.broadcasted_iota(jnp.int32, (tk, ct), 1)
    causal = lambda k, q_: k <= q_
    present = lambda cond: (lambda k, q_: (k >= 0) & cond)

    win_tiles = [(jnp.maximum(qi - 2, 0), lambda k, q_: (q_ < k) & (qi >= 2)),
                 (jnp.maximum(qi - 1, 0), present(qi >= 1)),
                 (qi, causal)]
    sc_win = tile_scores(kw_ref, qr, win_tiles)

    s = _dot(kc_ref[0, 0], q)
    cmp_end = lax.broadcasted_iota(jnp.int32, s.shape, 0) * CMP_STRIDE + (CMP_LEN - 1)
    valid = cmp_end <= col_pos(s.shape)
    sm = jnp.where(valid, s, NEG_INF)
    e = jnp.where(valid, jnp.exp2(sm - jnp.max(sm, axis=0, keepdims=True)), 0.0)
    l = jnp.sum(e, axis=0, keepdims=True)
    p_cmp = e / jnp.where(l > 0.0, l, 1.0)
    o_cmp = _dot(vct_ref[0, 0], p_cmp.astype(BF16))

    p_sum = p_cmp[:, 0:tq]
    for r in range(1, R):
        p_sum = p_sum + p_cmp[:, r * tq:(r + 1) * tq]
    hi, lo = _split2(p_sum)
    imp = _dot(ovlt_ref[...], hi) + _dot(ovlt_ref[...], lo)
    blk = lax.broadcasted_iota(jnp.int32, imp.shape, 0)
    t_q = t0 + lax.broadcasted_iota(jnp.int32, imp.shape, 1)
    cur = t_q // SEL_BLOCK
    forced = (blk == 0) | (blk == cur) | (blk == cur - 1)
    score = jnp.where(forced, FORCE_SCORE, jnp.where(blk * SEL_BLOCK <= t_q, imp, NEG_INF))
    score = jnp.where(blk < n_blk, score, -jnp.inf)
    selected = blk >= n_blk
    for _ in range(N_SEL):
        best = jnp.max(score, axis=0, keepdims=True)
        first = jnp.min(jnp.where(score == best, blk, 2 * LANES), axis=0, keepdims=True)
        pick = blk == first
        selected = selected | pick
        score = jnp.where(pick, -jnp.inf, score)
    bias = jnp.where(selected, 0.0, NEG_INF)
    qa = jnp.concatenate([qr, jnp.concatenate([bias] * R, axis=1).astype(BF16),
                          jnp.zeros((dh - n_blk_pad, cols), BF16)], axis=0)

    o_win = normalised(attend(init, sc_win, vwt_ref, win_tiles)[1])

    n_pairs = qi // 2

    def pair_body(i, carry):
        tiles = [(2 * i, None), (2 * i + 1, None)]
        return attend(carry, tile_scores(ks_ref, qa, tiles), vst_ref, tiles)

    carry = lax.fori_loop(0, n_pairs, pair_body, init)
    j_odd = 2 * n_pairs
    tail_tiles = [(j_odd, present(j_odd < qi)), (qi, causal)]
    o_slc = normalised(attend(carry, tile_scores(ks_ref, qa, tail_tiles), vst_ref, tail_tiles)[1])

    gates = gate_ref[0]
    outs = []
    for r in range(R):
        cs = slice(r * tq, (r + 1) * tq)
        outs.append(gates[3 * r:3 * r + 1] * o_cmp[:, cs] + gates[3 * r + 1:3 * r + 2] * o_slc[:, cs]
                    + gates[3 * r + 2:3 * r + 3] * o_win[:, cs])
    halves = [jnp.transpose(jnp.concatenate(outs[i:i + 2], axis=0)) for i in range(0, R, 2)]
    o_ref[0] = jnp.concatenate(halves, axis=1)


SEL_ROWS = 32


def _nsa_attention(q, qr, gates, k_cmp, v_cmp, ks, vst, kw, vwt, tq=256, tk=256):
    batch, _, seq = q.shape
    G, R, dh = N_KV_GROUPS, Q_PER_GROUP, HEAD_DIM
    n_slab = k_cmp.shape[2]
    n_blk = seq // SEL_BLOCK
    assert n_blk <= SEL_ROWS
    assert tq == tk and WINDOW == 2 * tk, "the window branch is written as exactly three key tiles"
    cmp_start = jnp.arange(n_slab) * CMP_STRIDE
    blk_start = jnp.arange(SEL_ROWS) * SEL_BLOCK
    overlap_t = ((cmp_start[None, :] < blk_start[:, None] + SEL_BLOCK)
                 & (cmp_start[None, :] + CMP_LEN - 1 >= blk_start[:, None])
                 & (jnp.arange(SEL_ROWS)[:, None] < n_blk)
                 & (jnp.arange(n_slab)[None, :] < (seq - CMP_LEN) // CMP_STRIDE + 1)).astype(BF16)
    v_cmp_t = jnp.swapaxes(v_cmp, 2, 3)
    qspec = pl.BlockSpec((1, R * dh, tq), lambda b, g, i: (b, g, i))
    full = lambda a: pl.BlockSpec((1, 1) + a.shape[2:], lambda b, g, i: (b, g, 0, 0))
    vspec = pl.BlockSpec((1, dh, seq), lambda b, g, i: (b, g, 0))
    return pl.pallas_call(
        functools.partial(_nsa_attn_kernel, tq=tq, tk=tk, n_blk=n_blk),
        grid=(batch, G, seq // tq),
        in_specs=[qspec, qspec, pl.BlockSpec((1, GATE_ROWS, tq), lambda b, g, i: (b, g, i)),
                  full(k_cmp), full(v_cmp_t), full(ks), vspec, full(kw), vspec,
                  _resident(overlap_t.shape)],
        out_specs=pl.BlockSpec((1, tq, R * dh), lambda b, g, i: (b, i, g)),
        out_shape=jax.ShapeDtypeStruct((batch, seq, G * R * dh), F32),
        compiler_params=_cparams("parallel", "parallel", "parallel"),
        name="nsa_attn",
    )(q, qr, gates, k_cmp, v_cmp_t, ks, vst, kw, vwt, overlap_t)


def _moe_kernel(be_ref, nb_ref, x_ref, wgu_ref, wd_ref, o_ref, wgu_bf, wd_bf):
    i = pl.program_id(0)
    ff = wd_ref.shape[1]
    used = i < nb_ref[0]

    @pl.when(used & ((i == 0) | (be_ref[i] != be_ref[jnp.maximum(i - 1, 0)])))
    def _():
        wgu_bf[...] = wgu_ref[0].astype(BF16)
        wd_bf[...] = wd_ref[0].astype(BF16)

    @pl.when(used)
    def _():
        xb = x_ref[...]
        gt = _dot(xb, wgu_bf[:, :ff])
        up = _dot(xb, wgu_bf[:, ff:])
        h = (gt * _sigmoid(gt) * up).astype(BF16)
        o_ref[...] = _dot(h, wd_bf[...])

    @pl.when(jnp.logical_not(used))
    def _():
        o_ref[...] = jnp.zeros_like(o_ref)


def _moe_experts(xs, block_e, n_used, w_gu, w_down):
    n_rows, d = xs.shape
    n_blocks = n_rows // MOE_ROWS
    ff = w_down.shape[1]
    grid_spec = pltpu.PrefetchScalarGridSpec(
        num_scalar_prefetch=2,
        grid=(n_blocks,),
        in_specs=[pl.BlockSpec((MOE_ROWS, d), lambda i, be, nb: (i, 0)),
                  pl.BlockSpec((1, d, 2 * ff), lambda i, be, nb: (be[i], 0, 0)),
                  pl.BlockSpec((1, ff, d), lambda i, be, nb: (be[i], 0, 0))],
        out_specs=pl.BlockSpec((MOE_ROWS, d), lambda i, be, nb: (i, 0)),
        scratch_shapes=[pltpu.VMEM((d, 2 * ff), BF16), pltpu.VMEM((ff, d), BF16)],
    )
    return pl.pallas_call(
        _moe_kernel,
        grid_spec=grid_spec,
        out_shape=jax.ShapeDtypeStruct((n_rows, d), F32),
        compiler_params=_cparams("arbitrary"),
        name="moe_experts",
    )(block_e, n_used, xs, w_gu, w_down)


def _moe(x2d, logits, w_gu, w_down):
    n_tok, d = x2d.shape
    n_assign = n_tok * TOP_K
    top_logit, top_e = lax.top_k(logits, TOP_K)
    weights = jax.nn.softmax(top_logit, axis=-1)
    flat_e = top_e.reshape(-1).astype(jnp.int32)
    onehot = (flat_e[:, None] == jnp.arange(N_EXPERTS, dtype=jnp.int32)[None, :]).astype(jnp.int32)
    running = jnp.cumsum(onehot, axis=0)
    counts = running[-1]
    padded = (counts + MOE_ROWS - 1) // MOE_ROWS * MOE_ROWS
    pad_end = jnp.cumsum(padded)
    pad_start = pad_end - padded
    grp_start = jnp.cumsum(counts) - counts
    pos = jnp.sum(onehot * (running - 1 + pad_start[None, :]), axis=1).reshape(n_tok, TOP_K)
    n_blocks = -(-n_assign // MOE_ROWS) + N_EXPERTS
    block_e = jnp.minimum(jnp.searchsorted(pad_end, jnp.arange(n_blocks) * MOE_ROWS, side='right'),
                          N_EXPERTS - 1).astype(jnp.int32)
    n_used = (pad_end[-1] // MOE_ROWS).astype(jnp.int32).reshape(1)
    order = jnp.argsort(flat_e)
    tok_sorted = (order // TOP_K).astype(jnp.int32)
    row_in_grp = (jnp.arange(n_blocks, dtype=jnp.int32) * MOE_ROWS - pad_start[block_e])[:, None] \
        + jnp.arange(MOE_ROWS, dtype=jnp.int32)[None, :]
    src = jnp.clip(grp_start[block_e][:, None] + row_in_grp, 0, n_assign - 1)
    row_tok = jnp.where(row_in_grp < counts[block_e][:, None], tok_sorted[src], 0).reshape(-1)
    xs = x2d[row_tok]
    ys = _moe_experts(xs, block_e, n_used, w_gu, w_down)
    return [ys[pos[:, s]] for s in range(TOP_K)], weights


def kernel(x, p, a_mu, a_w_rkv, a_w0, a_w1, a_w2, a_a0, a_a1, a_a2, a_g1, a_g2, a_k_k, a_k_a, a_r_k,
           a_gn_g, a_gn_b, a_w_o, b_w_kv, b_cmp_pos, b_phi_w1, b_phi_b1, b_phi_w2, b_w_in, b_w_o,
           f_w_gu, f_w_down, m_w_router, m_b_router, m_w_gu, m_w_down, ln_g, ln_b, ple_w, ple_gate_w,
           ple_gate_b):
    batch, seq, d = x.shape
    m = batch * seq
    x0 = x.reshape(m, d)
    p2d = p.reshape(DEPTH, m, PLE_DIM)
    bf = lambda w: w.astype(BF16)

    x1 = _rwkv_layer(x0, batch, seq, a_mu[0], bf(a_w_rkv[0]), a_w0[0], bf(a_w1[0]), bf(a_w2[0]), a_a0[0],
                     bf(a_a1[0]), bf(a_a2[0]), bf(a_g1[0]), bf(a_g2[0]), a_k_k[0], a_k_a[0], a_r_k[0],
                     a_gn_g[0], a_gn_b[0], bf(a_w_o[0]), ln_g[0, 0], ln_b[0, 0])
    x2 = _ffn_ple(x1, p2d, 0, bf(f_w_gu[0][:, :D_FF]), bf(f_w_gu[0][:, D_FF:]), bf(f_w_down[0]),
                  ln_g[0, 1], ln_b[0, 1], bf(ple_w[0]), bf(ple_gate_w[0]), ple_gate_b[0])

    slabs, ks, kw, vst, vwt = _nsa_kv(x2, b_w_kv, batch, seq)
    k_cmp, v_cmp = _nsa_compress(slabs, b_cmp_pos, b_phi_w1, b_phi_b1, b_phi_w2)
    q, qr, gates = _nsa_q(x2, b_w_in[0], batch, seq)
    attn = _nsa_attention(q, qr, gates, k_cmp, v_cmp, ks, vst, kw, vwt).reshape(m, d)
    x3, x3_bf, logits = _proj_ln_router(attn, x2, bf(b_w_o[0]), ln_g[1, 0], ln_b[1, 0], m_w_router[0],
                                        m_b_router[0])
    (y0, y1), route_w = _moe(x3_bf, logits[:, :N_EXPERTS], m_w_gu[0], m_w_down[0])
    out = _moe_combine_ln_ple(x3, y0, y1, route_w, p2d, 1, ln_g[1, 1], ln_b[1, 1], bf(ple_w[1]),
                              bf(ple_gate_w[1]), ple_gate_b[1])
    return out.reshape(batch, seq, d)
```

```python
import functools
import math

import jax
import jax.numpy as jnp
from jax import lax
from jax.experimental import pallas as pl
from jax.experimental.pallas import tpu as pltpu

BF16 = jnp.bfloat16
F32 = jnp.float32

LANES = 128
VMEM_LIMIT_BYTES = 56 * 1024 * 1024

D_MODEL = 1024
PLE_DIM = 256
RWKV_HEAD = 64
GN_EPS = 64e-5
N_HEADS = 16
HEAD_DIM = 64
N_KV_GROUPS = 4
Q_PER_GROUP = 4
N_BRANCH = 3
CMP_LEN = 32
CMP_STRIDE = 16
CMP_HIDDEN = 256
SEL_BLOCK = 64
N_SEL = 8
WINDOW = 512
ROPE_THETA = 10000.0
D_FF = 2816
N_EXPERTS = 8
TOP_K = 2
D_FF_EXPERT = 1408
MOE_ROWS = 256
LN_EPS = 1e-5
DEPTH = 2
DEEPNORM_ALPHA = (2.0 * DEPTH) ** 0.25
NEG_INF = -1e30
FORCE_SCORE = 1e4

SCAN_CHUNK = 64


def _cparams(*sem):
    return pltpu.CompilerParams(dimension_semantics=sem, vmem_limit_bytes=VMEM_LIMIT_BYTES)


def _resident(shape):
    nd = len(shape)
    return pl.BlockSpec(shape, lambda *_: (0,) * nd, pipeline_mode=pl.Buffered(1))


def _dot(a, b):
    return jnp.dot(a, b, preferred_element_type=F32)


def _dot_nt(a, b):
    return lax.dot_general(a, b, (((1,), (1,)), ((), ())), preferred_element_type=F32)


def _split2(x):
    hi = x.astype(BF16)
    lo = (x - hi.astype(F32)).astype(BF16)
    return hi, lo


def _split3(x):
    hi = x.astype(BF16)
    r1 = x - hi.astype(F32)
    mid = r1.astype(BF16)
    lo = (r1 - mid.astype(F32)).astype(BF16)
    return hi, mid, lo


def _layer_norm(y, g, b):
    mu = jnp.mean(y, axis=-1, keepdims=True)
    yc = y - mu
    var = jnp.mean(yc * yc, axis=-1, keepdims=True)
    return yc * lax.rsqrt(var + LN_EPS) * g + b


def _sigmoid(z):
    return 1.0 / (1.0 + jnp.exp(-z))


def _rwkv_project(first, x_ref, xh_ref, mu_ref, wrkv_ref, w0_ref, w1_ref, w2_ref, a0_ref, a1_ref,
                  a2_ref, g1_ref, g2_ref, r_ref, k_ref, v_ref, dl_ref, a_ref, g_ref):
    x = x_ref[...]
    prev_row = jnp.where(first, 0.0, xh_ref[0, 7:8, :])
    row = lax.broadcasted_iota(jnp.int32, x.shape, 0)
    x_shift = jnp.where(row == 0, prev_row, pltpu.roll(x, shift=1, axis=0))
    xx = x_shift - x

    def mix(i):
        return (x + xx * mu_ref[i:i + 1, :]).astype(BF16)

    r_ref[...] = _dot(mix(0), wrkv_ref[0])
    k_ref[...] = _dot(mix(1), wrkv_ref[1])
    v_ref[...] = _dot(mix(2), wrkv_ref[2])
    z = w0_ref[...] + _dot(jnp.tanh(_dot(mix(3), w1_ref[...])).astype(BF16), w2_ref[...])
    dl_ref[...] = -math.exp(-0.5) * _sigmoid(z)
    a_ref[...] = _sigmoid(a0_ref[...] + _dot(_dot(mix(4), a1_ref[...]).astype(BF16), a2_ref[...]))
    g_ref[...] = _dot(_sigmoid(_dot(mix(5), g1_ref[...])).astype(BF16), g2_ref[...])


def _rwkv_recurrence(first, r_ref, k_ref, v_ref, dl_ref, a_ref, g_ref, kk_ref, ka_ref, rk_ref, gng_ref,
                     gnb_ref, o_ref, state_ref, kkn_ref, cum_ref, *, n_pairs, n_chunks):
    C = SCAN_CHUNK
    N = RWKV_HEAD
    W = 2 * N
    tb = n_chunks * C

    @pl.when(first)
    def _():
        state_ref[...] = jnp.zeros_like(state_ref)

    tt = lax.broadcasted_iota(jnp.int32, (C, W), 0)
    ln = lax.broadcasted_iota(jnp.int32, (C, W), 1)
    ss = ln & (N - 1)
    strict_lower = ss < tt
    lower = ss <= tt
    eye = jnp.where(ss == tt, 1.0, 0.0)
    level_masks = []
    for lg in range(int(math.log2(C))):
        level_masks.append(((tt >> (lg + 1)) == (ss >> (lg + 1))) & ((tt >> lg) == (ss >> lg) + 1))
    rr = lax.broadcasted_iota(jnp.int32, (W, W), 0)
    cc = lax.broadcasted_iota(jnp.int32, (W, W), 1)
    same_head = (rr < N) == (cc < N)
    lane_head0 = ln < N
    WG = 2 * W
    lg_n = int(math.log2(N))
    ones_grp = jnp.where((lax.broadcasted_iota(jnp.int32, (WG, WG), 0) >> lg_n)
                         == (lax.broadcasted_iota(jnp.int32, (WG, WG), 1) >> lg_n), 1.0, 0.0).astype(BF16)
    col_groups = [slice(c, c + WG) for c in range(0, n_pairs * W, WG)]

    def bd(y):
        return jnp.concatenate([jnp.where(lane_head0, y, 0.0), jnp.where(lane_head0, 0.0, y)],
                               axis=0).astype(BF16)

    def pmm(xp, ybd):
        return _dot(xp.astype(BF16), ybd)

    def head_sum(xg):
        return _dot(xg.astype(BF16), ones_grp)

    tri_r = lax.broadcasted_iota(jnp.int32, (tb, tb), 0)
    tri_c = lax.broadcasted_iota(jnp.int32, (tb, tb), 1)
    lg_c = int(math.log2(C))
    tri_chunk = jnp.where((tri_c <= tri_r) & ((tri_c >> lg_c) == (tri_r >> lg_c)), 1.0, 0.0).astype(BF16)
    d_hi, d_mid, d_lo = _split3(dl_ref[...])
    cum_ref[...] = _dot(tri_chunk, d_hi) + _dot(tri_chunk, d_mid) + _dot(tri_chunk, d_lo)
    for cols in col_groups:
        kk_raw = k_ref[:, cols] * kk_ref[:, cols]
        kkn_ref[:, cols] = kk_raw / jnp.maximum(jnp.sqrt(head_sum(kk_raw * kk_raw)), 1e-12)

    def chunk_load(ci, pi):
        rows = pl.ds(pl.multiple_of(ci * C, C), C)
        cols = slice(pi * W, (pi + 1) * W)
        return (r_ref[rows, cols], k_ref[rows, cols], v_ref[rows, cols], dl_ref[rows, cols],
                a_ref[rows, cols], kkn_ref[rows, cols], cum_ref[rows, cols], state_ref[pi])

    def chunk_compute(pi, r, k, v, dl, a, kk, cum, m0):
        cols = slice(pi * W, (pi + 1) * W)
        k2 = k * (1.0 + (a - 1.0) * ka_ref[:, cols])
        b = a * kk
        p_incl = jnp.exp(cum)
        p_excl = jnp.exp(cum - dl)
        p_inv = jnp.exp(-cum)
        kq = kk * p_excl
        rq = r * p_incl
        bk = b * p_inv
        kkd = k2 * p_inv

        bkt = jnp.transpose(jnp.concatenate([b, k2], axis=0))
        cumt = jnp.transpose(jnp.concatenate([cum, cum], axis=0))
        last_col = cumt[:, C - 1:C]
        lhs_t = (bkt * jnp.exp(last_col - cumt)).astype(BF16)
        m_decayed = m0 * jnp.exp(last_col)

        lhs = jnp.concatenate([kq, rq], axis=0).astype(BF16)
        yield
        abg = _dot_nt(lhs, jnp.concatenate([bd(bk), bd(kkd)], axis=0))
        a_m = jnp.where(strict_lower, abg[:C, :W], 0.0)
        aq_m = jnp.where(lower, abg[C:, :W], 0.0)
        g_m = jnp.where(strict_lower, abg[:C, W:], 0.0)
        gq_m = jnp.where(lower, abg[C:, W:], 0.0)

        tinv = eye - jnp.where(level_masks[0], a_m, 0.0)
        yield
        gv = pmm(jnp.concatenate([g_m, gq_m], axis=0), bd(v))
        for lm in level_masks[1:]:
            w_ = pmm(jnp.where(lm, a_m, 0.0), bd(tinv))
            yield
            tinv = tinv - pmm(tinv, bd(w_))
            yield

        kq_y = pmm(tinv, jnp.concatenate([bd(kq), bd(gv[:C])], axis=1))
        kq1 = kq_y[:, :W]
        y = kq_y[:, W:]
        yield
        aq_ky = pmm(aq_m, jnp.concatenate([bd(kq1), bd(y)], axis=1))
        rq1 = rq - aq_ky[:, :W]
        o_loc = gv[C:] - aq_ky[:, W:]
        yield
        st = _dot(jnp.concatenate([rq1, kq1], axis=0).astype(BF16), m0.astype(BF16))
        o = st[:C] + o_loc
        z = -(st[C:] + y)
        yield
        upd = _dot(lhs_t, jnp.concatenate([z, v], axis=0).astype(BF16))
        return o, m_decayed + jnp.where(same_head, upd, 0.0)

    def interleave(gens):
        results = [None] * len(gens)
        live = list(range(len(gens)))
        while live:
            for i in list(live):
                try:
                    next(gens[i])
                except StopIteration as done:
                    results[i] = done.value
                    live.remove(i)
        return results

    def body(ci, carry):
        loaded = [chunk_load(ci, pi) for pi in range(n_pairs)]
        results = interleave([chunk_compute(pi, *loaded[pi]) for pi in range(n_pairs)])
        rows = pl.ds(pl.multiple_of(ci * C, C), C)
        for pi, (o, m_new) in enumerate(results):
            o_ref[rows, pi * W:(pi + 1) * W] = o
            state_ref[pi] = m_new
        return carry

    lax.fori_loop(0, n_chunks, body, 0)

    for cols in col_groups:
        o = o_ref[:, cols]
        r = r_ref[:, cols]
        v = v_ref[:, cols]
        k2 = k_ref[:, cols] * (1.0 + (a_ref[:, cols] - 1.0) * ka_ref[:, cols])
        oc = o - head_sum(o) * (1.0 / N)
        var_o = head_sum(oc * oc) * (1.0 / N)
        out = oc * lax.rsqrt(var_o + GN_EPS) * gng_ref[:, cols] + gnb_ref[:, cols]
        out = out + head_sum(r * k2 * rk_ref[:, cols]) * v
        o_ref[:, cols] = out * g_ref[:, cols]


def _rwkv_layer_kernel(x_ref, xh_ref, mu_ref, wrkv_ref, w0_ref, w1_ref, w2_ref, a0_ref, a1_ref, a2_ref,
                       g1_ref, g2_ref, kk_ref, ka_ref, rk_ref, gng_ref, gnb_ref, wo_ref, lng_ref, lnb_ref,
                       o_ref, state_ref, r_s, k_s, v_s, dl_s, a_s, g_s, kkn_s, cum_s, y_s, *, n_chunks):
    first = pl.program_id(1) == 0
    _rwkv_project(first, x_ref, xh_ref, mu_ref, wrkv_ref, w0_ref, w1_ref, w2_ref, a0_ref, a1_ref, a2_ref,
                  g1_ref, g2_ref, r_s, k_s, v_s, dl_s, a_s, g_s)
    _rwkv_recurrence(first, r_s, k_s, v_s, dl_s, a_s, g_s, kk_ref, ka_ref, rk_ref, gng_ref, gnb_ref, y_s,
                     state_ref, kkn_s, cum_s, n_pairs=x_ref.shape[1] // (2 * RWKV_HEAD), n_chunks=n_chunks)
    mix = _dot(y_s[...].astype(BF16), wo_ref[...])
    o_ref[...] = _layer_norm(DEEPNORM_ALPHA * x_ref[...] + mix, lng_ref[...], lnb_ref[...])


def _rwkv_layer(x2d, batch, seq, mu, w_rkv, w0, w1, w2, a0, a1, a2, g1, g2, k_k, k_a, r_k, gn_g, gn_b,
                w_o, ln_g, ln_b, tb=256):
    m, d = x2d.shape
    nt = seq // tb
    xh = x2d.reshape(m // 8, 8, d)
    vec = lambda z: z.reshape(1, d)
    blk = pl.BlockSpec((tb, d), lambda b, t: (b * nt + t, 0))
    weights = [mu, w_rkv, vec(w0), w1, w2, vec(a0), a1, a2, g1, g2, vec(k_k), vec(k_a), vec(r_k),
               vec(gn_g), vec(gn_b), w_o, vec(ln_g), vec(ln_b)]
    n_pairs = d // (2 * RWKV_HEAD)
    return pl.pallas_call(
        functools.partial(_rwkv_layer_kernel, n_chunks=tb // SCAN_CHUNK),
        grid=(batch, nt),
        in_specs=[blk, pl.BlockSpec((1, 8, d), lambda b, t: (jnp.maximum((b * nt + t) * (tb // 8) - 1, 0), 0, 0))]
        + [_resident(w.shape) for w in weights],
        out_specs=blk,
        out_shape=jax.ShapeDtypeStruct((m, d), F32),
        scratch_shapes=[pltpu.VMEM((n_pairs, 2 * RWKV_HEAD, 2 * RWKV_HEAD), F32)]
        + [pltpu.VMEM((tb, d), F32)] * 9,
        compiler_params=_cparams("parallel", "arbitrary"),
        name="rwkv_layer",
    )(x2d, xh, *weights)


def _proj_ln_router_kernel(y_ref, x_ref, w_ref, lng_ref, lnb_ref, wr_ref, br_ref, o_ref, ob_ref, lg_ref):
    mix = _dot(y_ref[...].astype(BF16), w_ref[...])
    xn = _layer_norm(DEEPNORM_ALPHA * x_ref[...] + mix, lng_ref[...], lnb_ref[...])
    o_ref[...] = xn
    xh, xl = _split2(xn)
    ob_ref[...] = xh
    lg_ref[...] = _dot(xh, wr_ref[0]) + _dot(xh, wr_ref[1]) + _dot(xl, wr_ref[0]) + br_ref[...]


def _proj_ln_router(y, x2d, w, ln_g, ln_b, w_router, b_router, tm=512):
    m, d = x2d.shape
    row = lambda i: (i, 0)
    wr = jnp.zeros((d, LANES), F32).at[:, :N_EXPERTS].set(w_router)
    wr3 = jnp.stack(_split2(wr))
    br =jnp.zeros((1, LANES), F32).at[0, :N_EXPERTS].set(b_router)
    return pl.pallas_call(
        _proj_ln_router_kernel,
        grid=(m // tm,),
        in_specs=[pl.BlockSpec((tm, y.shape[1]), row), pl.BlockSpec((tm, d), row), _resident(w.shape),
                  _resident((1, d)), _resident((1, d)), _resident(wr3.shape), _resident(br.shape)],
        out_specs=[pl.BlockSpec((tm, d), row), pl.BlockSpec((tm, d), row), pl.BlockSpec((tm, LANES), row)],
        out_shape=[jax.ShapeDtypeStruct((m, d), F32), jax.ShapeDtypeStruct((m, d), BF16),
                   jax.ShapeDtypeStruct((m, LANES), F32)],
        compiler_params=_cparams("parallel"),
        name="proj_ln_router",
    )(y, x2d, w, ln_g.reshape(1, d), ln_b.reshape(1, d), wr3, br)


def _ple(xn, p, plew_ref, gw_ref, gb_ref):
    gate = _sigmoid(_dot(xn.astype(BF16), gw_ref[...]) + gb_ref[...])
    return xn + _dot(p.astype(BF16), plew_ref[...]) * gate


def _ffn_ple_kernel(x_ref, p_ref, wg_ref, wu_ref, wd_ref, lng_ref, lnb_ref, plew_ref, gw_ref, gb_ref,
                    o_ref, *, ff_chunk):
    x = x_ref[...]
    xb = x.astype(BF16)
    acc = jnp.zeros_like(x)
    for c in range(wg_ref.shape[1] // ff_chunk):
        sl = slice(c * ff_chunk, (c + 1) * ff_chunk)
        gt = _dot(xb, wg_ref[:, sl])
        up = _dot(xb, wu_ref[:, sl])
        h = (gt * _sigmoid(gt) * up).astype(BF16)
        acc = acc + _dot(h, wd_ref[sl, :])
    xn = _layer_norm(DEEPNORM_ALPHA * x + acc, lng_ref[...], lnb_ref[...])
    o_ref[...] = _ple(xn, p_ref[0], plew_ref, gw_ref, gb_ref)


def _ple_block(p3d, layer, tm):
    return pl.BlockSpec((1, tm, p3d.shape[2]), lambda i: (layer, i, 0))


def _ffn_ple(x2d, p3d, layer, wg, wu, wd, ln_g, ln_b, ple_w, gate_w, gate_b, tm=512, ff_chunk=1408):
    m, d = x2d.shape
    row = lambda i: (i, 0)
    return pl.pallas_call(
        functools.partial(_ffn_ple_kernel, ff_chunk=ff_chunk),
        grid=(m // tm,),
        in_specs=[pl.BlockSpec((tm, d), row), _ple_block(p3d, layer, tm),
                  _resident(wg.shape), _resident(wu.shape), _resident(wd.shape),
                  _resident((1, d)), _resident((1, d)), _resident(ple_w.shape), _resident(gate_w.shape),
                  _resident((1, d))],
        out_specs=pl.BlockSpec((tm, d), row),
        out_shape=jax.ShapeDtypeStruct((m, d), F32),
        compiler_params=_cparams("parallel"),
        name="ffn_ple",
    )(x2d, p3d, wg, wu, wd, ln_g.reshape(1, d), ln_b.reshape(1, d), ple_w, gate_w, gate_b.reshape(1, d))


def _moe_combine_ln_ple_kernel(x_ref, y_ref, rw_ref, p_ref, lng_ref, lnb_ref, plew_ref, gw_ref, gb_ref,
                               o_ref):
    rw = rw_ref[...]
    d = x_ref.shape[1]
    ffn = y_ref[:, :d] * rw[:, 0:1]
    for s in range(1, TOP_K):
        ffn = ffn + y_ref[:, s * d:(s + 1) * d] * rw[:, s:s + 1]
    xn = _layer_norm(DEEPNORM_ALPHA * x_ref[...] + ffn, lng_ref[...], lnb_ref[...])
    o_ref[...] = _ple(xn, p_ref[0], plew_ref, gw_ref, gb_ref)


def _moe_combine_ln_ple(x2d, y_rows, route_w, p3d, layer, ln_g, ln_b, ple_w, gate_w, gate_b, tm=512):
    m, d = x2d.shape
    row = lambda i: (i, 0)
    tile = pl.BlockSpec((tm, d), row)
    return pl.pallas_call(
        _moe_combine_ln_ple_kernel,
        grid=(m // tm,),
        in_specs=[tile, pl.BlockSpec((tm, TOP_K * d), row), pl.BlockSpec((tm, TOP_K), row),
                  _ple_block(p3d, layer, tm),
                  _resident((1, d)), _resident((1, d)), _resident(ple_w.shape), _resident(gate_w.shape),
                  _resident((1, d))],
        out_specs=tile,
        out_shape=jax.ShapeDtypeStruct((m, d), F32),
        compiler_params=_cparams("parallel"),
        name="moe_combine_ln_ple",
    )(x2d, y_rows, route_w, p3d, ln_g.reshape(1, d), ln_b.reshape(1, d), ple_w, gate_w, gate_b.reshape(1, d))


def _swap_half_cols(w):
    k, n = w.shape
    return w.reshape(k, n // HEAD_DIM, 2, HEAD_DIM // 2)[:, :, ::-1, :].reshape(k, n)


def _rope_tables(seq):
    half = HEAD_DIM // 2
    inv = ROPE_THETA ** (-jnp.arange(half, dtype=F32) / half)
    ang = jnp.arange(seq, dtype=F32)[:, None] * inv[None, :]
    cos, sin = jnp.cos(ang), jnp.sin(ang)
    return jnp.concatenate([cos, cos], -1), jnp.concatenate([-sin, sin], -1)


def _nsa_kv_kernel(x_ref, wk_ref, wvt_ref, cos_ref, sin_ref, slab_ref, ks_ref, kw_ref, vst_ref, vwt_ref,
                   cz_s):
    G, dh = N_KV_GROUPS, HEAD_DIM
    gw = G * dh
    xb = x_ref[...].astype(BF16)
    res = _dot(xb, wk_ref[...])
    res_t = _dot_nt(wvt_ref[...], xb)
    tm = res.shape[0]
    n_slab = tm // CMP_STRIDE
    for c in range(cz_s.shape[0]):
        cz_s[c] = res[:, c * LANES:(c + 1) * LANES]
        rows = [cz_s[c, pl.ds(l, n_slab, stride=CMP_STRIDE), :] for l in range(CMP_STRIDE)]
        for half in range(LANES // dh):
            jg = c * (LANES // dh) + half
            slab_ref[jg // G, 0, jg % G] = jnp.concatenate(
                [r[:, half * dh:(half + 1) * dh] for r in rows], axis=1).astype(BF16)
    cos = cos_ref[...]
    sin = sin_ref[...]
    ks = res[:, 2 * gw:3 * gw] * cos + res[:, 3 * gw:4 * gw] * sin
    kw = res[:, 4 * gw:5 * gw] * cos + res[:, 5 * gw:6 * gw] * sin
    t_glob = pl.program_id(1) * tm + lax.broadcasted_iota(jnp.int32, (tm, dh), 0)
    onehot = jnp.where(lax.broadcasted_iota(jnp.int32, (tm, dh), 1) == t_glob // SEL_BLOCK, 1.0, 0.0)
    for g in range(G):
        sl = slice(g * dh, (g + 1) * dh)
        ks_ref[0, g] = jnp.concatenate([ks[:, sl], onehot], axis=1).astype(BF16)
        kw_ref[0, g] = kw[:, sl].astype(BF16)
    vst_ref[0] = res_t[:gw].astype(BF16)
    vwt_ref[0] = res_t[gw:].astype(BF16)


def _nsa_kv(x2d, w_kv, batch, seq, tm=256):
    m, d = x2d.shape
    G, dh = N_KV_GROUPS, HEAD_DIM
    gw = G * dh
    parts = [w_kv[:, j * gw:(j + 1) * gw] for j in range(6)]
    w_k = jnp.concatenate([parts[0], parts[1], parts[2], _swap_half_cols(parts[2]),
                           parts[4], _swap_half_cols(parts[4])], axis=1).astype(BF16)
    w_vt = jnp.concatenate([parts[3], parts[5]], axis=1).T.astype(BF16)
    cos, sin = _rope_tables(seq)
    cos = jnp.tile(cos, (1, G))
    sin = jnp.tile(sin, (1, G))
    nt = seq // tm
    o64 = jax.ShapeDtypeStruct((batch, G, seq, dh), BF16)
    o128 = jax.ShapeDtypeStruct((batch, G, seq, 2 * dh), BF16)
    ot = jax.ShapeDtypeStruct((batch, gw, seq), BF16)
    b64 = pl.BlockSpec((1, G, tm, dh), lambda b, t: (b, 0, t, 0))
    b128 = pl.BlockSpec((1, G, tm, 2 * dh), lambda b, t: (b, 0, t, 0))
    bt = pl.BlockSpec((1, gw, tm), lambda b, t: (b, 0, t))
    slab_w = CMP_STRIDE * dh
    oslab = jax.ShapeDtypeStruct((2, batch, G, seq // CMP_STRIDE, slab_w), BF16)
    bslab = pl.BlockSpec((2, 1, G, tm // CMP_STRIDE, slab_w), lambda b, t: (0, b, 0, t, 0))
    return pl.pallas_call(
        _nsa_kv_kernel,
        grid=(batch, nt),
        in_specs=[pl.BlockSpec((tm, d), lambda b, t: (b * nt + t, 0)), _resident(w_k.shape),
                  _resident(w_vt.shape),
                  pl.BlockSpec((tm, gw), lambda b, t: (t, 0)), pl.BlockSpec((tm, gw), lambda b, t: (t, 0))],
        out_specs=[bslab, b128, b64, bt, bt],
        out_shape=[oslab, o128, o64, ot, ot],
        scratch_shapes=[pltpu.VMEM((2 * gw // LANES, tm, LANES), F32)],
        compiler_params=_cparams("parallel", "parallel"),
        name="nsa_kv",
    )(x2d, w_k, w_vt, cos, sin)


def _nsa_cmp_kernel(z_ref, pos_ref, w1_ref, b1_ref, w2_ref, o_ref, *, slabs_per_seq):
    half = w1_ref.shape[1] // 2
    z = z_ref[0]
    tm = z.shape[0]
    first = _dot(z, w1_ref[0, :half, :])
    second = _dot(z, w1_ref[0, half:, :])
    const = _dot(pos_ref[0], w1_ref[0])[0:1, :] + b1_ref[0]
    hid = first + pltpu.roll(second, shift=tm - 1, axis=0) + const
    act = 0.5 * hid * (1.0 + jnp.tanh(math.sqrt(2.0 / math.pi) * (hid + 0.044715 * hid * hid * hid)))
    out = _dot(act.astype(BF16), w2_ref[0])
    row = lax.broadcasted_iota(jnp.int32, out.shape, 0)
    o_ref[0] = jnp.where(row % slabs_per_seq == slabs_per_seq - 1, 0.0, out).astype(BF16)


def _nsa_compress(slabs, cmp_pos, phi_w1, phi_b1, phi_w2, tm=512):
    _, batch, G, n_slab, slab = slabs.shape
    dh = slab // CMP_STRIDE
    seq = n_slab * CMP_STRIDE
    rows = batch * G * n_slab
    tm = min(tm, rows)
    z = slabs.reshape(2, rows, slab)
    pos = jnp.broadcast_to(cmp_pos.reshape(2, 1, CMP_LEN * dh), (2, 8, CMP_LEN * dh)).astype(BF16)
    out = pl.pallas_call(
        functools.partial(_nsa_cmp_kernel, slabs_per_seq=seq // CMP_STRIDE),
        grid=(2, rows // tm),
        in_specs=[pl.BlockSpec((1, tm, slab), lambda j, i: (j, i, 0)),
                  pl.BlockSpec((1, 8, CMP_LEN * dh), lambda j, i: (j, 0, 0)),
                  pl.BlockSpec((1, CMP_LEN * dh, CMP_HIDDEN), lambda j, i: (j, 0, 0)),
                  pl.BlockSpec((1, 1, CMP_HIDDEN), lambda j, i: (j, 0, 0)),
                  pl.BlockSpec((1, CMP_HIDDEN, dh), lambda j, i: (j, 0, 0))],
        out_specs=pl.BlockSpec((1, tm, dh), lambda j, i: (j, i, 0)),
        out_shape=jax.ShapeDtypeStruct((2, rows, dh), BF16),
        compiler_params=_cparams("parallel", "parallel"),
        name="nsa_compress",
    )(z, pos, phi_w1.astype(BF16), phi_b1.reshape(2, 1, CMP_HIDDEN), phi_w2.astype(BF16))
    return out[0].reshape(batch, G, n_slab, dh), out[1].reshape(batch, G, n_slab, dh)


GATE_ROWS = 16


def _nsa_q_kernel(x_ref, wt_ref, cos_ref, sin_ref, q_ref, qr_ref, gate_ref):
    dh = HEAD_DIM
    hw = N_HEADS * dh
    scale = HEAD_DIM ** -0.5 * math.log2(math.e)
    res_t = _dot_nt(wt_ref[...], x_ref[...].astype(BF16))
    tm = res_t.shape[1]
    q = res_t[:hw].reshape(N_HEADS, dh, tm)
    q_sw = jnp.concatenate([q[:, dh // 2:], q[:, :dh // 2]], axis=1)
    qr = q * cos_ref[...][None] + q_sw * sin_ref[...][None]
    q_ref[0] = (q * scale).reshape(hw, tm).astype(BF16)
    qr_ref[0] = (qr * scale).reshape(hw, tm).astype(BF16)
    gate_ref[0] = _sigmoid(res_t[hw:])


def _nsa_q(x2d, w_in, batch, seq, tm=512):
    m, d = x2d.shape
    G, R, dh = N_KV_GROUPS, Q_PER_GROUP, HEAD_DIM
    hw = N_HEADS * dh
    w_gate = w_in[:, hw:].reshape(d, G, R * N_BRANCH)
    w_gate = jnp.pad(w_gate, ((0, 0), (0, 0), (0, GATE_ROWS - R * N_BRANCH))).reshape(d, G * GATE_ROWS)
    wt = jnp.concatenate([w_in[:, :hw], w_gate], axis=1).T.astype(BF16)
    cos, sin = _rope_tables(seq)
    nt = seq // tm
    oq = jax.ShapeDtypeStruct((batch, hw, seq), BF16)
    bq = pl.BlockSpec((1, hw, tm), lambda b, t: (b, 0, t))
    tab = pl.BlockSpec((dh, tm), lambda b, t: (0, t))
    return pl.pallas_call(
        _nsa_q_kernel,
        grid=(batch, nt),
        in_specs=[pl.BlockSpec((tm, d), lambda b, t: (b * nt + t, 0)), _resident(wt.shape), tab, tab],
        out_specs=[bq, bq, pl.BlockSpec((1, G * GATE_ROWS, tm), lambda b, t: (b, 0, t))],
        out_shape=[oq, oq, jax.ShapeDtypeStruct((batch, G * GATE_ROWS, seq), F32)],
        compiler_params=_cparams("parallel", "parallel"),
        name="nsa_q",
    )(x2d, wt, cos.T, sin.T)


def _nsa_attn_kernel(q_ref, qr_ref, gate_ref, kc_ref, vct_ref, ks_ref, vst_ref, kw_ref, vwt_ref, ovlt_ref,
                     o_ref, *, tq, tk, n_blk):
    R, dh = Q_PER_GROUP, HEAD_DIM
    cols = R * tq
    qi = pl.program_id(2)
    t0 = qi * tq
    n_blk_pad = ovlt_ref.shape[0]

    def heads_to_lanes(x):
        return jnp.concatenate([x[r * dh:(r + 1) * dh] for r in range(R)], axis=1)

    q = heads_to_lanes(q_ref[0])
    qr = heads_to_lanes(qr_ref[0])

    def col_pos(shape):
        return t0 + (lax.broadcasted_iota(jnp.int32, shape, 1) & (tq - 1))

    def key_tile(j):
        return pl.ds(pl.multiple_of(j * tk, tk), tk)

    ct = tq
    col_tiles = [slice(c, c + ct) for c in range(0, cols, ct)]
    ones_rows = 16
    ones_blk = jnp.ones((ones_rows, tk), BF16)

    def tile_scores(k_ref, q_op, tiles):
        return [[_dot(k_ref[0, 0, key_tile(j), :], q_op[:, cs]) for j, _ in tiles] for cs in col_tiles]

    def attend(carry, sc, vt_ref, tiles):
        m_i, acc = carry
        outs = []
        for cs, sc_c in zip(col_tiles, sc):
            masked = [s_j if mask_fn is None else jnp.where(mask_fn(k_off, q_off), s_j, NEG_INF)
                      for s_j, (_, mask_fn) in zip(sc_c, tiles)]
            m_new = m_i[:, cs]
            for s_j in masked:
                m_new = jnp.maximum(m_new, jnp.max(s_j, axis=0, keepdims=True))
            acc_new = jnp.exp2(m_i[:, cs] - m_new) * acc[:, cs]
            for s_j, (j, _) in zip(masked, tiles):
                v_ext = jnp.concatenate([vt_ref[0, :, key_tile(j)], ones_blk], axis=0)
                acc_new = acc_new + _dot(v_ext, jnp.exp2(s_j - m_new).astype(BF16))
            outs.append((m_new, acc_new))
        return tuple(jnp.concatenate([o[i] for o in outs], axis=1) for i in range(2))

    def normalised(acc):
        return acc[:dh] / acc[dh:dh + 1]

    init = (jnp.full((1, cols), NEG_INF, F32), jnp.zeros((dh + ones_rows, cols), F32))

    k_off = lax.broadcasted_iota(jnp.int32, (tk, ct), 0)
    q_off = lax.broadcasted_iota(jnp.int32, (tk, ct), 1)
    causal = lambda k, q_: k <= q_
    present = lambda cond: (lambda k, q_: (k >= 0) & cond)

    win_tiles = [(jnp.maximum(qi - 2, 0), lambda k, q_: (q_ < k) & (qi >= 2)),
                 (jnp.maximum(qi - 1, 0), present(qi >= 1)),
                 (qi, causal)]
    sc_win = tile_scores(kw_ref, qr, win_tiles)

    s = _dot(kc_ref[0, 0], q)
    cmp_end = lax.broadcasted_iota(jnp.int32, s.shape, 0) * CMP_STRIDE + (CMP_LEN - 1)
    valid = cmp_end <= col_pos(s.shape)
    sm = jnp.where(valid, s, NEG_INF)
    e = jnp.where(valid, jnp.exp2(sm - jnp.max(sm, axis=0, keepdims=True)), 0.0)
    l = jnp.sum(e, axis=0, keepdims=True)
    p_cmp = e / jnp.where(l > 0.0, l, 1.0)
    o_cmp = _dot(vct_ref[0, 0], p_cmp.astype(BF16))

    p_sum = p_cmp[:, 0:tq]
    for r in range(1, R):
        p_sum = p_sum + p_cmp[:, r * tq:(r + 1) * tq]
    hi, lo = _split2(p_sum)
    imp = _dot(ovlt_ref[...], hi) + _dot(ovlt_ref[...], lo)
    blk = lax.broadcasted_iota(jnp.int32, imp.shape, 0)
    t_q = t0 + lax.broadcasted_iota(jnp.int32, imp.shape, 1)
    cur = t_q // SEL_BLOCK
    forced = (blk == 0) | (blk == cur) | (blk == cur - 1)
    score = jnp.where(forced, FORCE_SCORE, jnp.where(blk * SEL_BLOCK <= t_q, imp, NEG_INF))
    score = jnp.where(blk < n_blk, score, -jnp.inf)
    selected = blk >= n_blk
    for _ in range(N_SEL):
        best = jnp.max(score, axis=0, keepdims=True)
        first = jnp.min(jnp.where(score == best, blk, 2 * LANES), axis=0, keepdims=True)
        pick = blk == first
        selected = selected | pick
        score = jnp.where(pick, -jnp.inf, score)
    bias = jnp.where(selected, 0.0, NEG_INF)
    qa = jnp.concatenate([qr, jnp.concatenate([bias] * R, axis=1).astype(BF16),
                          jnp.zeros((dh - n_blk_pad, cols), BF16)], axis=0)

    o_win = normalised(attend(init, sc_win, vwt_ref, win_tiles)[1])

    n_pairs = qi // 2

    def pair_body(i, carry):
        tiles = [(2 * i, None), (2 * i + 1, None)]
        return attend(carry, tile_scores(ks_ref, qa, tiles), vst_ref, tiles)

    carry = lax.fori_loop(0, n_pairs, pair_body, init)
    j_odd = 2 * n_pairs
    tail_tiles = [(j_odd, present(j_odd < qi)), (qi, causal)]
    o_slc = normalised(attend(carry, tile_scores(ks_ref, qa, tail_tiles), vst_ref, tail_tiles)[1])

    gates = gate_ref[0]
    outs = []
    for r in range(R):
        cs = slice(r * tq, (r + 1) * tq)
        outs.append(gates[3 * r:3 * r + 1] * o_cmp[:, cs] + gates[3 * r + 1:3 * r + 2] * o_slc[:, cs]
                    + gates[3 * r + 2:3 * r + 3] * o_win[:, cs])
    halves = [jnp.transpose(jnp.concatenate(outs[i:i + 2], axis=0)) for i in range(0, R, 2)]
    o_ref[0] = jnp.concatenate(halves, axis=1)


SEL_ROWS = 32


def _nsa_attention(q, qr, gates, k_cmp, v_cmp, ks, vst, kw, vwt, tq=256, tk=256):
    batch, _, seq = q.shape
    G, R, dh = N_KV_GROUPS, Q_PER_GROUP, HEAD_DIM
    n_slab = k_cmp.shape[2]
    n_blk = seq // SEL_BLOCK
    assert n_blk <= SEL_ROWS
    assert tq == tk and WINDOW == 2 * tk, "the window branch is written as exactly three key tiles"
    cmp_start = jnp.arange(n_slab) * CMP_STRIDE
    blk_start = jnp.arange(SEL_ROWS) * SEL_BLOCK
    overlap_t = ((cmp_start[None, :] < blk_start[:, None] + SEL_BLOCK)
                 & (cmp_start[None, :] + CMP_LEN - 1 >= blk_start[:, None])
                 & (jnp.arange(SEL_ROWS)[:, None] < n_blk)
                 & (jnp.arange(n_slab)[None, :] < (seq - CMP_LEN) // CMP_STRIDE + 1)).astype(BF16)
    v_cmp_t = jnp.swapaxes(v_cmp, 2, 3)
    qspec = pl.BlockSpec((1, R * dh, tq), lambda b, g, i: (b, g, i))
    full = lambda a: pl.BlockSpec((1, 1) + a.shape[2:], lambda b, g, i: (b, g, 0, 0))
    vspec = pl.BlockSpec((1, dh, seq), lambda b, g, i: (b, g, 0))
    return pl.pallas_call(
        functools.partial(_nsa_attn_kernel, tq=tq, tk=tk, n_blk=n_blk),
        grid=(batch, G, seq // tq),
        in_specs=[qspec, qspec, pl.BlockSpec((1, GATE_ROWS, tq), lambda b, g, i: (b, g, i)),
                  full(k_cmp), full(v_cmp_t), full(ks), vspec, full(kw), vspec,
                  _resident(overlap_t.shape)],
        out_specs=pl.BlockSpec((1, tq, R * dh), lambda b, g, i: (b, i, g)),
        out_shape=jax.ShapeDtypeStruct((batch, seq, G * R * dh), F32),
        compiler_params=_cparams("parallel", "parallel", "parallel"),
        name="nsa_attn",
    )(q, qr, gates, k_cmp, v_cmp_t, ks, vst, kw, vwt, overlap_t)


def _moe_kernel(be_ref, nb_ref, x_ref, wgu_ref, wd_ref, o_ref, wgu_bf, wd_bf):
    i = pl.program_id(0)
    ff = wd_ref.shape[1]
    used = i < nb_ref[0]

    @pl.when(used & ((i == 0) | (be_ref[i] != be_ref[jnp.maximum(i - 1, 0)])))
    def _():
        wgu_bf[...] = wgu_ref[0].astype(BF16)
        wd_bf[...] = wd_ref[0].astype(BF16)

    @pl.when(used)
    def _():
        xb = x_ref[...]
        gt = _dot(xb, wgu_bf[:, :ff])
        up = _dot(xb, wgu_bf[:, ff:])
        h = (gt * _sigmoid(gt) * up).astype(BF16)
        o_ref[...] = _dot(h, wd_bf[...])

    @pl.when(jnp.logical_not(used))
    def _():
        o_ref[...] = jnp.zeros_like(o_ref)


def _moe_experts(xs, block_e, n_used, w_gu, w_down):
    n_rows, d = xs.shape
    n_blocks = n_rows // MOE_ROWS
    ff = w_down.shape[1]
    grid_spec = pltpu.PrefetchScalarGridSpec(
        num_scalar_prefetch=2,
        grid=(n_blocks,),
        in_specs=[pl.BlockSpec((MOE_ROWS, d), lambda i, be, nb: (i, 0)),
                  pl.BlockSpec((1, d, 2 * ff), lambda i, be, nb: (be[i], 0, 0)),
                  pl.BlockSpec((1, ff, d), lambda i, be, nb: (be[i], 0, 0))],
        out_specs=pl.BlockSpec((MOE_ROWS, d), lambda i, be, nb: (i, 0)),
        scratch_shapes=[pltpu.VMEM((d, 2 * ff), BF16), pltpu.VMEM((ff, d), BF16)],
    )
    return pl.pallas_call(
        _moe_kernel,
        grid_spec=grid_spec,
        out_shape=jax.ShapeDtypeStruct((n_rows, d), F32),
        compiler_params=_cparams("arbitrary"),
        name="moe_experts",
    )(block_e, n_used, xs, w_gu, w_down)


def _moe(x2d, logits, w_gu, w_down):
    n_tok, d = x2d.shape
    n_assign = n_tok * TOP_K
    top_logit, top_e = lax.top_k(logits, TOP_K)
    weights = jax.nn.softmax(top_logit, axis=-1)
    flat_e = top_e.reshape(-1).astype(jnp.int32)
    onehot = (flat_e[:, None] == jnp.arange(N_EXPERTS, dtype=jnp.int32)[None, :]).astype(jnp.int32)
    running = jnp.cumsum(onehot, axis=0)
    counts = running[-1]
    padded = (counts + MOE_ROWS - 1) // MOE_ROWS * MOE_ROWS
    pad_end = jnp.cumsum(padded)
    pad_start = pad_end - padded
    grp_start = jnp.cumsum(counts) - counts
    pos = jnp.sum(onehot * (running - 1 + pad_start[None, :]), axis=1).reshape(n_tok, TOP_K)
    n_blocks = -(-n_assign // MOE_ROWS) + N_EXPERTS
    block_e = jnp.minimum(jnp.searchsorted(pad_end, jnp.arange(n_blocks) * MOE_ROWS, side='right'),
                          N_EXPERTS - 1).astype(jnp.int32)
    n_used = (pad_end[-1] // MOE_ROWS).astype(jnp.int32).reshape(1)
    order = jnp.argsort(flat_e)
    tok_sorted = (order // TOP_K).astype(jnp.int32)
    row_in_grp = (jnp.arange(n_blocks, dtype=jnp.int32) * MOE_ROWS - pad_start[block_e])[:, None] \
        + jnp.arange(MOE_ROWS, dtype=jnp.int32)[None, :]
    src = jnp.clip(grp_start[block_e][:, None] + row_in_grp, 0, n_assign - 1)
    row_tok = jnp.where(row_in_grp < counts[block_e][:, None], tok_sorted[src], 0).reshape(-1)
    xs = x2d[row_tok]
    ys = _moe_experts(xs, block_e, n_used, w_gu, w_down)
    return ys[pos.reshape(-1)].reshape(n_tok, TOP_K * d), weights


def kernel(x, p, a_mu, a_w_rkv, a_w0, a_w1, a_w2, a_a0, a_a1, a_a2, a_g1, a_g2, a_k_k, a_k_a, a_r_k,
           a_gn_g, a_gn_b, a_w_o, b_w_kv, b_cmp_pos, b_phi_w1, b_phi_b1, b_phi_w2, b_w_in, b_w_o,
           f_w_gu, f_w_down, m_w_router, m_b_router, m_w_gu, m_w_down, ln_g, ln_b, ple_w, ple_gate_w,
           ple_gate_b):
    batch, seq, d = x.shape
    m = batch * seq
    x0 = x.reshape(m, d)
    p2d = p.reshape(DEPTH, m, PLE_DIM)
    bf = lambda w: w.astype(BF16)

    x1 = _rwkv_layer(x0, batch, seq, a_mu[0], bf(a_w_rkv[0]), a_w0[0], bf(a_w1[0]), bf(a_w2[0]), a_a0[0],
                     bf(a_a1[0]), bf(a_a2[0]), bf(a_g1[0]), bf(a_g2[0]), a_k_k[0], a_k_a[0], a_r_k[0],
                     a_gn_g[0], a_gn_b[0], bf(a_w_o[0]), ln_g[0, 0], ln_b[0, 0])
    x2 = _ffn_ple(x1, p2d, 0, bf(f_w_gu[0][:, :D_FF]), bf(f_w_gu[0][:, D_FF:]), bf(f_w_down[0]),
                  ln_g[0, 1], ln_b[0, 1], bf(ple_w[0]), bf(ple_gate_w[0]), ple_gate_b[0])

    slabs, ks, kw, vst, vwt = _nsa_kv(x2, b_w_kv, batch, seq)
    k_cmp, v_cmp = _nsa_compress(slabs, b_cmp_pos, b_phi_w1, b_phi_b1, b_phi_w2)
    q, qr, gates = _nsa_q(x2, b_w_in[0], batch, seq)
    attn = _nsa_attention(q, qr, gates, k_cmp, v_cmp, ks, vst, kw, vwt).reshape(m, d)
    x3, x3_bf, logits = _proj_ln_router(attn, x2, bf(b_w_o[0]), ln_g[1, 0], ln_b[1, 0], m_w_router[0],
                                        m_b_router[0])
    y_rows, route_w = _moe(x3_bf, logits[:, :N_EXPERTS], m_w_gu[0], m_w_down[0])
    out = _moe_combine_ln_ple(x3, y_rows, route_w, p2d, 1, ln_g[1, 1], ln_b[1, 1], bf(ple_w[1]),
                              bf(ple_gate_w[1]), ple_gate_b[1])
    return out.reshape(batch, seq, d)
```

```python
import functools
import math

import jax
import jax.numpy as jnp
from jax import lax
from jax.experimental import pallas as pl
from jax.experimental.pallas import tpu as pltpu

BF16 = jnp.bfloat16
F32 = jnp.float32

LANES = 128
VMEM_LIMIT_BYTES = 56 * 1024 * 1024

D_MODEL = 1024
PLE_DIM = 256
RWKV_HEAD = 64
GN_EPS = 64e-5
N_HEADS = 16
HEAD_DIM = 64
N_KV_GROUPS = 4
Q_PER_GROUP = 4
N_BRANCH = 3
CMP_LEN = 32
CMP_STRIDE = 16
CMP_HIDDEN = 256
SEL_BLOCK = 64
N_SEL = 8
WINDOW = 512
ROPE_THETA = 10000.0
D_FF = 2816
N_EXPERTS = 8
TOP_K = 2
D_FF_EXPERT = 1408
MOE_ROWS = 256
LN_EPS = 1e-5
DEPTH = 2
DEEPNORM_ALPHA = (2.0 * DEPTH) ** 0.25
NEG_INF = -1e30
FORCE_SCORE = 1e4

SCAN_CHUNK = 64


def _cparams(*sem):
    return pltpu.CompilerParams(dimension_semantics=sem, vmem_limit_bytes=VMEM_LIMIT_BYTES)


def _resident(shape):
    nd = len(shape)
    return pl.BlockSpec(shape, lambda *_: (0,) * nd, pipeline_mode=pl.Buffered(1))


def _dot(a, b):
    return jnp.dot(a, b, preferred_element_type=F32)


def _dot_nt(a, b):
    return lax.dot_general(a, b, (((1,), (1,)), ((), ())), preferred_element_type=F32)


def _split2(x):
    hi = x.astype(BF16)
    lo = (x - hi.astype(F32)).astype(BF16)
    return hi, lo


def _split3(x):
    hi = x.astype(BF16)
    r1 = x - hi.astype(F32)
    mid = r1.astype(BF16)
    lo = (r1 - mid.astype(F32)).astype(BF16)
    return hi, mid, lo


def _layer_norm(y, g, b):
    mu = jnp.mean(y, axis=-1, keepdims=True)
    yc = y - mu
    var = jnp.mean(yc * yc, axis=-1, keepdims=True)
    return yc * lax.rsqrt(var + LN_EPS) * g + b


def _sigmoid(z):
    return 1.0 / (1.0 + jnp.exp(-z))


def _rwkv_project(first, x_ref, xh_ref, mu_ref, wrkv_ref, w0_ref, w1_ref, w2_ref, a0_ref, a1_ref,
                  a2_ref, g1_ref, g2_ref, r_ref, k_ref, v_ref, dl_ref, a_ref, g_ref):
    x = x_ref[...]
    prev_row = jnp.where(first, 0.0, xh_ref[0, 7:8, :])
    row = lax.broadcasted_iota(jnp.int32, x.shape, 0)
    x_shift = jnp.where(row == 0, prev_row, pltpu.roll(x, shift=1, axis=0))
    xx = x_shift - x

    def mix(i):
        return (x + xx * mu_ref[i:i + 1, :]).astype(BF16)

    r_ref[...] = _dot(mix(0), wrkv_ref[0])
    k_ref[...] = _dot(mix(1), wrkv_ref[1])
    v_ref[...] = _dot(mix(2), wrkv_ref[2])
    z = w0_ref[...] + _dot(jnp.tanh(_dot(mix(3), w1_ref[...])).astype(BF16), w2_ref[...])
    dl_ref[...] = -math.exp(-0.5) * _sigmoid(z)
    a_ref[...] = _sigmoid(a0_ref[...] + _dot(_dot(mix(4), a1_ref[...]).astype(BF16), a2_ref[...]))
    g_ref[...] = _dot(_sigmoid(_dot(mix(5), g1_ref[...])).astype(BF16), g2_ref[...])


def _rwkv_recurrence(first, r_ref, k_ref, v_ref, dl_ref, a_ref, g_ref, kk_ref, ka_ref, rk_ref, gng_ref,
                     gnb_ref, o_ref, state_ref, kkn_ref, cum_ref, *, n_pairs, n_chunks):
    C = SCAN_CHUNK
    N = RWKV_HEAD
    W = 2 * N
    tb = n_chunks * C

    @pl.when(first)
    def _():
        state_ref[...] = jnp.zeros_like(state_ref)

    tt = lax.broadcasted_iota(jnp.int32, (C, W), 0)
    ln = lax.broadcasted_iota(jnp.int32, (C, W), 1)
    ss = ln & (N - 1)
    strict_lower = ss < tt
    lower = ss <= tt
    eye = jnp.where(ss == tt, 1.0, 0.0)
    level_masks = []
    for lg in range(int(math.log2(C))):
        level_masks.append(((tt >> (lg + 1)) == (ss >> (lg + 1))) & ((tt >> lg) == (ss >> lg) + 1))
    rr = lax.broadcasted_iota(jnp.int32, (W, W), 0)
    cc = lax.broadcasted_iota(jnp.int32, (W, W), 1)
    same_head = (rr < N) == (cc < N)
    lane_head0 = ln < N
    WG = 2 * W
    lg_n = int(math.log2(N))
    ones_grp = jnp.where((lax.broadcasted_iota(jnp.int32, (WG, WG), 0) >> lg_n)
                         == (lax.broadcasted_iota(jnp.int32, (WG, WG), 1) >> lg_n), 1.0, 0.0).astype(BF16)
    col_groups = [slice(c, c + WG) for c in range(0, n_pairs * W, WG)]

    def bd(y):
        return jnp.concatenate([jnp.where(lane_head0, y, 0.0), jnp.where(lane_head0, 0.0, y)],
                               axis=0).astype(BF16)

    def pmm(xp, ybd):
        return _dot(xp.astype(BF16), ybd)

    def head_sum(xg):
        return _dot(xg.astype(BF16), ones_grp)

    tri_r = lax.broadcasted_iota(jnp.int32, (tb, tb), 0)
    tri_c = lax.broadcasted_iota(jnp.int32, (tb, tb), 1)
    lg_c = int(math.log2(C))
    tri_chunk = jnp.where((tri_c <= tri_r) & ((tri_c >> lg_c) == (tri_r >> lg_c)), 1.0, 0.0).astype(BF16)
    d_hi, d_mid, d_lo = _split3(dl_ref[...])
    cum_ref[...] = _dot(tri_chunk, d_hi) + _dot(tri_chunk, d_mid) + _dot(tri_chunk, d_lo)
    for cols in col_groups:
        kk_raw = k_ref[:, cols] * kk_ref[:, cols]
        kkn_ref[:, cols] = kk_raw / jnp.maximum(jnp.sqrt(head_sum(kk_raw * kk_raw)), 1e-12)

    def chunk_load(ci, pi):
        rows = pl.ds(pl.multiple_of(ci * C, C), C)
        cols = slice(pi * W, (pi + 1) * W)
        return (r_ref[rows, cols], k_ref[rows, cols], v_ref[rows, cols], dl_ref[rows, cols],
                a_ref[rows, cols], kkn_ref[rows, cols], cum_ref[rows, cols], state_ref[pi])

    def chunk_compute(pi, r, k, v, dl, a, kk, cum, m0):
        cols = slice(pi * W, (pi + 1) * W)
        k2 = k * (1.0 + (a - 1.0) * ka_ref[:, cols])
        b = a * kk
        p_incl = jnp.exp(cum)
        p_excl = jnp.exp(cum - dl)
        p_inv = jnp.exp(-cum)
        kq = kk * p_excl
        rq = r * p_incl
        bk = b * p_inv
        kkd = k2 * p_inv

        bkt = jnp.transpose(jnp.concatenate([b, k2], axis=0))
        cumt = jnp.transpose(jnp.concatenate([cum, cum], axis=0))
        last_col = cumt[:, C - 1:C]
        lhs_t = (bkt * jnp.exp(last_col - cumt)).astype(BF16)
        m_decayed = m0 * jnp.exp(last_col)

        lhs = jnp.concatenate([kq, rq], axis=0).astype(BF16)
        yield
        abg = _dot_nt(lhs, jnp.concatenate([bd(bk), bd(kkd)], axis=0))
        a_m = jnp.where(strict_lower, abg[:C, :W], 0.0)
        aq_m = jnp.where(lower, abg[C:, :W], 0.0)
        g_m = jnp.where(strict_lower, abg[:C, W:], 0.0)
        gq_m = jnp.where(lower, abg[C:, W:], 0.0)

        tinv = eye - jnp.where(level_masks[0], a_m, 0.0)
        yield
        gv = pmm(jnp.concatenate([g_m, gq_m], axis=0), bd(v))
        for lm in level_masks[1:]:
            w_ = pmm(jnp.where(lm, a_m, 0.0), bd(tinv))
            yield
            tinv = tinv - pmm(tinv, bd(w_))
            yield

        kq_y = pmm(tinv, jnp.concatenate([bd(kq), bd(gv[:C])], axis=1))
        kq1 = kq_y[:, :W]
        y = kq_y[:, W:]
        yield
        aq_ky = pmm(aq_m, jnp.concatenate([bd(kq1), bd(y)], axis=1))
        rq1 = rq - aq_ky[:, :W]
        o_loc = gv[C:] - aq_ky[:, W:]
        yield
        st = _dot(jnp.concatenate([rq1, kq1], axis=0).astype(BF16), m0.astype(BF16))
        o = st[:C] + o_loc
        z = -(st[C:] + y)
        yield
        upd = _dot(lhs_t, jnp.concatenate([z, v], axis=0).astype(BF16))
        return o, m_decayed + jnp.where(same_head, upd, 0.0)

    def interleave(gens):
        results = [None] * len(gens)
        live = list(range(len(gens)))
        while live:
            for i in list(live):
                try:
                    next(gens[i])
                except StopIteration as done:
                    results[i] = done.value
                    live.remove(i)
        return results

    def body(ci, carry):
        loaded = [chunk_load(ci, pi) for pi in range(n_pairs)]
        results = interleave([chunk_compute(pi, *loaded[pi]) for pi in range(n_pairs)])
        rows = pl.ds(pl.multiple_of(ci * C, C), C)
        for pi, (o, m_new) in enumerate(results):
            o_ref[rows, pi * W:(pi + 1) * W] = o
            state_ref[pi] = m_new
        return carry

    lax.fori_loop(0, n_chunks, body, 0)

    for cols in col_groups:
        o = o_ref[:, cols]
        r = r_ref[:, cols]
        v = v_ref[:, cols]
        k2 = k_ref[:, cols] * (1.0 + (a_ref[:, cols] - 1.0) * ka_ref[:, cols])
        oc = o - head_sum(o) * (1.0 / N)
        var_o = head_sum(oc * oc) * (1.0 / N)
        out = oc * lax.rsqrt(var_o + GN_EPS) * gng_ref[:, cols] + gnb_ref[:, cols]
        out = out + head_sum(r * k2 * rk_ref[:, cols]) * v
        o_ref[:, cols] = out * g_ref[:, cols]


def _rwkv_layer_kernel(x_ref, xh_ref, mu_ref, wrkv_ref, w0_ref, w1_ref, w2_ref, a0_ref, a1_ref, a2_ref,
                       g1_ref, g2_ref, kk_ref, ka_ref, rk_ref, gng_ref, gnb_ref, wo_ref, lng_ref, lnb_ref,
                       o_ref, state_ref, r_s, k_s, v_s, dl_s, a_s, g_s, kkn_s, cum_s, y_s, *, n_chunks):
    first = pl.program_id(1) == 0
    _rwkv_project(first, x_ref, xh_ref, mu_ref, wrkv_ref, w0_ref, w1_ref, w2_ref, a0_ref, a1_ref, a2_ref,
                  g1_ref, g2_ref, r_s, k_s, v_s, dl_s, a_s, g_s)
    _rwkv_recurrence(first, r_s, k_s, v_s, dl_s, a_s, g_s, kk_ref, ka_ref, rk_ref, gng_ref, gnb_ref, y_s,
                     state_ref, kkn_s, cum_s, n_pairs=x_ref.shape[1] // (2 * RWKV_HEAD), n_chunks=n_chunks)
    mix = _dot(y_s[...].astype(BF16), wo_ref[...])
    o_ref[...] = _layer_norm(DEEPNORM_ALPHA * x_ref[...] + mix, lng_ref[...], lnb_ref[...])


def _rwkv_layer(x2d, batch, seq, mu, w_rkv, w0, w1, w2, a0, a1, a2, g1, g2, k_k, k_a, r_k, gn_g, gn_b,
                w_o, ln_g, ln_b, tb=256):
    m, d = x2d.shape
    nt = seq // tb
    xh = x2d.reshape(m // 8, 8, d)
    vec = lambda z: z.reshape(1, d)
    blk = pl.BlockSpec((tb, d), lambda b, t: (b * nt + t, 0))
    weights = [mu, w_rkv, vec(w0), w1, w2, vec(a0), a1, a2, g1, g2, vec(k_k), vec(k_a), vec(r_k),
               vec(gn_g), vec(gn_b), w_o, vec(ln_g), vec(ln_b)]
    n_pairs = d // (2 * RWKV_HEAD)
    return pl.pallas_call(
        functools.partial(_rwkv_layer_kernel, n_chunks=tb // SCAN_CHUNK),
        grid=(batch, nt),
        in_specs=[blk, pl.BlockSpec((1, 8, d), lambda b, t: (jnp.maximum((b * nt + t) * (tb // 8) - 1, 0), 0, 0))]
        + [_resident(w.shape) for w in weights],
        out_specs=blk,
        out_shape=jax.ShapeDtypeStruct((m, d), F32),
        scratch_shapes=[pltpu.VMEM((n_pairs, 2 * RWKV_HEAD, 2 * RWKV_HEAD), F32)]
        + [pltpu.VMEM((tb, d), F32)] * 9,
        compiler_params=_cparams("parallel", "arbitrary"),
        name="rwkv_layer",
    )(x2d, xh, *weights)


def _proj_ln_router_kernel(y_ref, x_ref, w_ref, lng_ref, lnb_ref, wr_ref, br_ref, o_ref, ob_ref, lg_ref):
    mix = _dot(y_ref[...].astype(BF16), w_ref[...])
    xn = _layer_norm(DEEPNORM_ALPHA * x_ref[...] + mix, lng_ref[...], lnb_ref[...])
    o_ref[...] = xn
    xh, xl = _split2(xn)
    ob_ref[...] = xh
    lg_ref[...] = _dot(xh, wr_ref[0]) + _dot(xh, wr_ref[1]) + _dot(xl, wr_ref[0]) + br_ref[...]


def _proj_ln_router(y, x2d, w, ln_g, ln_b, w_router, b_router, tm=512):
    m, d = x2d.shape
    row = lambda i: (i, 0)
    wr = jnp.zeros((d, LANES), F32).at[:, :N_EXPERTS].set(w_router)
    wr3 = jnp.stack(_split2(wr))
    br =jnp.zeros((1, LANES), F32).at[0, :N_EXPERTS].set(b_router)
    return pl.pallas_call(
        _proj_ln_router_kernel,
        grid=(m // tm,),
        in_specs=[pl.BlockSpec((tm, y.shape[1]), row), pl.BlockSpec((tm, d), row), _resident(w.shape),
                  _resident((1, d)), _resident((1, d)), _resident(wr3.shape), _resident(br.shape)],
        out_specs=[pl.BlockSpec((tm, d), row), pl.BlockSpec((tm, d), row), pl.BlockSpec((tm, LANES), row)],
        out_shape=[jax.ShapeDtypeStruct((m, d), F32), jax.ShapeDtypeStruct((m, d), BF16),
                   jax.ShapeDtypeStruct((m, LANES), F32)],
        compiler_params=_cparams("parallel"),
        name="proj_ln_router",
    )(y, x2d, w, ln_g.reshape(1, d), ln_b.reshape(1, d), wr3, br)


def _ple(xn, p, plew_ref, gw_ref, gb_ref):
    gate = _sigmoid(_dot(xn.astype(BF16), gw_ref[...]) + gb_ref[...])
    return xn + _dot(p.astype(BF16), plew_ref[...]) * gate


def _ffn_ple_kernel(x_ref, p_ref, wg_ref, wu_ref, wd_ref, lng_ref, lnb_ref, plew_ref, gw_ref, gb_ref,
                    o_ref, *, ff_chunk):
    x = x_ref[...]
    xb = x.astype(BF16)
    acc = jnp.zeros_like(x)
    for c in range(wg_ref.shape[1] // ff_chunk):
        sl = slice(c * ff_chunk, (c + 1) * ff_chunk)
        gt = _dot(xb, wg_ref[:, sl])
        up = _dot(xb, wu_ref[:, sl])
        h = (gt * _sigmoid(gt) * up).astype(BF16)
        acc = acc + _dot(h, wd_ref[sl, :])
    xn = _layer_norm(DEEPNORM_ALPHA * x + acc, lng_ref[...], lnb_ref[...])
    o_ref[...] = _ple(xn, p_ref[0], plew_ref, gw_ref, gb_ref)


def _ple_block(p3d, layer, tm):
    return pl.BlockSpec((1, tm, p3d.shape[2]), lambda i: (layer, i, 0))


def _ffn_ple(x2d, p3d, layer, wg, wu, wd, ln_g, ln_b, ple_w, gate_w, gate_b, tm=512, ff_chunk=1408):
    m, d = x2d.shape
    row = lambda i: (i, 0)
    return pl.pallas_call(
        functools.partial(_ffn_ple_kernel, ff_chunk=ff_chunk),
        grid=(m // tm,),
        in_specs=[pl.BlockSpec((tm, d), row), _ple_block(p3d, layer, tm),
                  _resident(wg.shape), _resident(wu.shape), _resident(wd.shape),
                  _resident((1, d)), _resident((1, d)), _resident(ple_w.shape), _resident(gate_w.shape),
                  _resident((1, d))],
        out_specs=pl.BlockSpec((tm, d), row),
        out_shape=jax.ShapeDtypeStruct((m, d), F32),
        compiler_params=_cparams("parallel"),
        name="ffn_ple",
    )(x2d, p3d, wg, wu, wd, ln_g.reshape(1, d), ln_b.reshape(1, d), ple_w, gate_w, gate_b.reshape(1, d))


def _moe_combine_ln_ple_kernel(x_ref, y0_ref, y1_ref, rw_ref, p_ref, lng_ref, lnb_ref, plew_ref, gw_ref,
                               gb_ref, o_ref):
    rw = rw_ref[...]
    ffn = y0_ref[...] * rw[:, 0:1] + y1_ref[...] * rw[:, 1:2]
    xn = _layer_norm(DEEPNORM_ALPHA * x_ref[...] + ffn, lng_ref[...], lnb_ref[...])
    o_ref[...] = _ple(xn, p_ref[0], plew_ref, gw_ref, gb_ref)


def _moe_combine_ln_ple(x2d, y0, y1, route_w, p3d, layer, ln_g, ln_b, ple_w, gate_w, gate_b, tm=512):
    m, d = x2d.shape
    row = lambda i: (i, 0)
    tile = pl.BlockSpec((tm, d), row)
    return pl.pallas_call(
        _moe_combine_ln_ple_kernel,
        grid=(m // tm,),
        in_specs=[tile, tile, tile, pl.BlockSpec((tm, TOP_K), row), _ple_block(p3d, layer, tm),
                  _resident((1, d)), _resident((1, d)), _resident(ple_w.shape), _resident(gate_w.shape),
                  _resident((1, d))],
        out_specs=tile,
        out_shape=jax.ShapeDtypeStruct((m, d), F32),
        compiler_params=_cparams("parallel"),
        name="moe_combine_ln_ple",
    )(x2d, y0, y1, route_w, p3d, ln_g.reshape(1, d), ln_b.reshape(1, d), ple_w, gate_w, gate_b.reshape(1, d))


def _swap_half_cols(w):
    k, n = w.shape
    return w.reshape(k, n // HEAD_DIM, 2, HEAD_DIM // 2)[:, :, ::-1, :].reshape(k, n)


def _rope_tables(seq):
    half = HEAD_DIM // 2
    inv = ROPE_THETA ** (-jnp.arange(half, dtype=F32) / half)
    ang = jnp.arange(seq, dtype=F32)[:, None] * inv[None, :]
    cos, sin = jnp.cos(ang), jnp.sin(ang)
    return jnp.concatenate([cos, cos], -1), jnp.concatenate([-sin, sin], -1)


def _nsa_kv_kernel(x_ref, wk_ref, wvt_ref, cos_ref, sin_ref, slab_ref, ks_ref, kw_ref, vst_ref, vwt_ref,
                   cz_s):
    G, dh = N_KV_GROUPS, HEAD_DIM
    gw = G * dh
    xb = x_ref[...].astype(BF16)
    res = _dot(xb, wk_ref[...])
    res_t = _dot_nt(wvt_ref[...], xb)
    tm = res.shape[0]
    n_slab = tm // CMP_STRIDE
    for c in range(cz_s.shape[0]):
        cz_s[c] = res[:, c * LANES:(c + 1) * LANES]
        rows = [cz_s[c, pl.ds(l, n_slab, stride=CMP_STRIDE), :] for l in range(CMP_STRIDE)]
        for half in range(LANES // dh):
            jg = c * (LANES // dh) + half
            slab_ref[jg // G, 0, jg % G] = jnp.concatenate(
                [r[:, half * dh:(half + 1) * dh] for r in rows], axis=1).astype(BF16)
    cos = cos_ref[...]
    sin = sin_ref[...]
    ks = res[:, 2 * gw:3 * gw] * cos + res[:, 3 * gw:4 * gw] * sin
    kw = res[:, 4 * gw:5 * gw] * cos + res[:, 5 * gw:6 * gw] * sin
    t_glob = pl.program_id(1) * tm + lax.broadcasted_iota(jnp.int32, (tm, dh), 0)
    onehot = jnp.where(lax.broadcasted_iota(jnp.int32, (tm, dh), 1) == t_glob // SEL_BLOCK, 1.0, 0.0)
    for g in range(G):
        sl = slice(g * dh, (g + 1) * dh)
        ks_ref[0, g] = jnp.concatenate([ks[:, sl], onehot], axis=1).astype(BF16)
        kw_ref[0, g] = kw[:, sl].astype(BF16)
    vst_ref[0] = res_t[:gw].astype(BF16)
    vwt_ref[0] = res_t[gw:].astype(BF16)


def _nsa_kv(x2d, w_kv, batch, seq, tm=256):
    m, d = x2d.shape
    G, dh = N_KV_GROUPS, HEAD_DIM
    gw = G * dh
    parts = [w_kv[:, j * gw:(j + 1) * gw] for j in range(6)]
    w_k = jnp.concatenate([parts[0], parts[1], parts[2], _swap_half_cols(parts[2]),
                           parts[4], _swap_half_cols(parts[4])], axis=1).astype(BF16)
    w_vt = jnp.concatenate([parts[3], parts[5]], axis=1).T.astype(BF16)
    cos, sin = _rope_tables(seq)
    cos = jnp.tile(cos, (1, G))
    sin = jnp.tile(sin, (1, G))
    nt = seq // tm
    o64 = jax.ShapeDtypeStruct((batch, G, seq, dh), BF16)
    o128 = jax.ShapeDtypeStruct((batch, G, seq, 2 * dh), BF16)
    ot = jax.ShapeDtypeStruct((batch, gw, seq), BF16)
    b64 = pl.BlockSpec((1, G, tm, dh), lambda b, t: (b, 0, t, 0))
    b128 = pl.BlockSpec((1, G, tm, 2 * dh), lambda b, t: (b, 0, t, 0))
    bt = pl.BlockSpec((1, gw, tm), lambda b, t: (b, 0, t))
    slab_w = CMP_STRIDE * dh
    oslab = jax.ShapeDtypeStruct((2, batch, G, seq // CMP_STRIDE, slab_w), BF16)
    bslab = pl.BlockSpec((2, 1, G, tm // CMP_STRIDE, slab_w), lambda b, t: (0, b, 0, t, 0))
    return pl.pallas_call(
        _nsa_kv_kernel,
        grid=(batch, nt),
        in_specs=[pl.BlockSpec((tm, d), lambda b, t: (b * nt + t, 0)), _resident(w_k.shape),
                  _resident(w_vt.shape),
                  pl.BlockSpec((tm, gw), lambda b, t: (t, 0)), pl.BlockSpec((tm, gw), lambda b, t: (t, 0))],
        out_specs=[bslab, b128, b64, bt, bt],
        out_shape=[oslab, o128, o64, ot, ot],
        scratch_shapes=[pltpu.VMEM((2 * gw // LANES, tm, LANES), F32)],
        compiler_params=_cparams("parallel", "parallel"),
        name="nsa_kv",
    )(x2d, w_k, w_vt, cos, sin)


def _nsa_cmp_kernel(z_ref, pos_ref, w1_ref, b1_ref, w2_ref, o_ref, *, slabs_per_seq):
    half = w1_ref.shape[1] // 2
    z = z_ref[0]
    tm = z.shape[0]
    first = _dot(z, w1_ref[0, :half, :])
    second = _dot(z, w1_ref[0, half:, :])
    const = _dot(pos_ref[0], w1_ref[0])[0:1, :] + b1_ref[0]
    hid = first + pltpu.roll(second, shift=tm - 1, axis=0) + const
    act = 0.5 * hid * (1.0 + jnp.tanh(math.sqrt(2.0 / math.pi) * (hid + 0.044715 * hid * hid * hid)))
    out = _dot(act.astype(BF16), w2_ref[0])
    row = lax.broadcasted_iota(jnp.int32, out.shape, 0)
    o_ref[0] = jnp.where(row % slabs_per_seq == slabs_per_seq - 1, 0.0, out).astype(BF16)


def _nsa_compress(slabs, cmp_pos, phi_w1, phi_b1, phi_w2, tm=512):
    _, batch, G, n_slab, slab = slabs.shape
    dh = slab // CMP_STRIDE
    seq = n_slab * CMP_STRIDE
    rows = batch * G * n_slab
    tm = min(tm, rows)
    z = slabs.reshape(2, rows, slab)
    pos = jnp.broadcast_to(cmp_pos.reshape(2, 1, CMP_LEN * dh), (2, 8, CMP_LEN * dh)).astype(BF16)
    out = pl.pallas_call(
        functools.partial(_nsa_cmp_kernel, slabs_per_seq=seq // CMP_STRIDE),
        grid=(2, rows // tm),
        in_specs=[pl.BlockSpec((1, tm, slab), lambda j, i: (j, i, 0)),
                  pl.BlockSpec((1, 8, CMP_LEN * dh), lambda j, i: (j, 0, 0)),
                  pl.BlockSpec((1, CMP_LEN * dh, CMP_HIDDEN), lambda j, i: (j, 0, 0)),
                  pl.BlockSpec((1, 1, CMP_HIDDEN), lambda j, i: (j, 0, 0)),
                  pl.BlockSpec((1, CMP_HIDDEN, dh), lambda j, i: (j, 0, 0))],
        out_specs=pl.BlockSpec((1, tm, dh), lambda j, i: (j, i, 0)),
        out_shape=jax.ShapeDtypeStruct((2, rows, dh), BF16),
        compiler_params=_cparams("parallel", "parallel"),
        name="nsa_compress",
    )(z, pos, phi_w1.astype(BF16), phi_b1.reshape(2, 1, CMP_HIDDEN), phi_w2.astype(BF16))
    return out[0].reshape(batch, G, n_slab, dh), out[1].reshape(batch, G, n_slab, dh)


GATE_ROWS = 16


def _nsa_q_kernel(x_ref, wt_ref, cos_ref, sin_ref, q_ref, qr_ref, gate_ref):
    dh = HEAD_DIM
    hw = N_HEADS * dh
    scale = HEAD_DIM ** -0.5 * math.log2(math.e)
    res_t = _dot_nt(wt_ref[...], x_ref[...].astype(BF16))
    tm = res_t.shape[1]
    q = res_t[:hw].reshape(N_HEADS, dh, tm)
    q_sw = jnp.concatenate([q[:, dh // 2:], q[:, :dh // 2]], axis=1)
    qr = q * cos_ref[...][None] + q_sw * sin_ref[...][None]
    q_ref[0] = (q * scale).reshape(hw, tm).astype(BF16)
    qr_ref[0] = (qr * scale).reshape(hw, tm).astype(BF16)
    gate_ref[0] = _sigmoid(res_t[hw:])


def _nsa_q(x2d, w_in, batch, seq, tm=512):
    m, d = x2d.shape
    G, R, dh = N_KV_GROUPS, Q_PER_GROUP, HEAD_DIM
    hw = N_HEADS * dh
    w_gate = w_in[:, hw:].reshape(d, G, R * N_BRANCH)
    w_gate = jnp.pad(w_gate, ((0, 0), (0, 0), (0, GATE_ROWS - R * N_BRANCH))).reshape(d, G * GATE_ROWS)
    wt = jnp.concatenate([w_in[:, :hw], w_gate], axis=1).T.astype(BF16)
    cos, sin = _rope_tables(seq)
    nt = seq // tm
    oq = jax.ShapeDtypeStruct((batch, hw, seq), BF16)
    bq = pl.BlockSpec((1, hw, tm), lambda b, t: (b, 0, t))
    tab = pl.BlockSpec((dh, tm), lambda b, t: (0, t))
    return pl.pallas_call(
        _nsa_q_kernel,
        grid=(batch, nt),
        in_specs=[pl.BlockSpec((tm, d), lambda b, t: (b * nt + t, 0)), _resident(wt.shape), tab, tab],
        out_specs=[bq, bq, pl.BlockSpec((1, G * GATE_ROWS, tm), lambda b, t: (b, 0, t))],
        out_shape=[oq, oq, jax.ShapeDtypeStruct((batch, G * GATE_ROWS, seq), F32)],
        compiler_params=_cparams("parallel", "parallel"),
        name="nsa_q",
    )(x2d, wt, cos.T, sin.T)


def _nsa_attn_kernel(q_ref, qr_ref, gate_ref, kc_ref, vct_ref, ks_ref, vst_ref, kw_ref, vwt_ref, ovlt_ref,
                     o_ref, *, tq, tk, n_blk):
    R, dh = Q_PER_GROUP, HEAD_DIM
    cols = R * tq
    qi = pl.program_id(2)
    t0 = qi * tq
    n_blk_pad = ovlt_ref.shape[0]

    def heads_to_lanes(x):
        return jnp.concatenate([x[r * dh:(r + 1) * dh] for r in range(R)], axis=1)

    q = heads_to_lanes(q_ref[0])
    qr = heads_to_lanes(qr_ref[0])

    def col_pos(shape):
        return t0 + (lax.broadcasted_iota(jnp.int32, shape, 1) & (tq - 1))

    def key_tile(j):
        return pl.ds(pl.multiple_of(j * tk, tk), tk)

    ct = tq
    col_tiles = [slice(c, c + ct) for c in range(0, cols, ct)]
    ones_rows = 16
    ones_blk = jnp.ones((ones_rows, tk), BF16)

    def tile_scores(k_ref, q_op, tiles):
        return [[_dot(k_ref[0, 0, key_tile(j), :], q_op[:, cs]) for j, _ in tiles] for cs in col_tiles]

    def attend(carry, sc, vt_ref, tiles):
        m_i, acc = carry
        outs = []
        for cs, sc_c in zip(col_tiles, sc):
            masked = [s_j if mask_fn is None else jnp.where(mask_fn(k_off, q_off), s_j, NEG_INF)
                      for s_j, (_, mask_fn) in zip(sc_c, tiles)]
            m_new = m_i[:, cs]
            for s_j in masked:
                m_new = jnp.maximum(m_new, jnp.max(s_j, axis=0, keepdims=True))
            acc_new = jnp.exp2(m_i[:, cs] - m_new) * acc[:, cs]
            for s_j, (j, _) in zip(masked, tiles):
                v_ext = jnp.concatenate([vt_ref[0, :, key_tile(j)], ones_blk], axis=0)
                acc_new = acc_new + _dot(v_ext, jnp.exp2(s_j - m_new).astype(BF16))
            outs.append((m_new, acc_new))
        return tuple(jnp.concatenate([o[i] for o in outs], axis=1) for i in range(2))

    def normalised(acc):
        return acc[:dh] / acc[dh:dh + 1]

    init = (jnp.full((1, cols), NEG_INF, F32), jnp.zeros((dh + ones_rows, cols), F32))

    k_off = lax.broadcasted_iota(jnp.int32, (tk, ct), 0)
    q_off = lax.broadcasted_iota(jnp.int32, (tk, ct), 1)
    causal = lambda k, q_: k <= q_
    present = lambda cond: (lambda k, q_: (k >= 0) & cond)

    win_tiles = [(jnp.maximum(qi - 2, 0), lambda k, q_: (q_ < k) & (qi >= 2)),
                 (jnp.maximum(qi - 1, 0), present(qi >= 1)),
                 (qi, causal)]
    sc_win = tile_scores(kw_ref, qr, win_tiles)

    s = _dot(kc_ref[0, 0], q)
    cmp_end = lax.broadcasted_iota(jnp.int32, s.shape, 0) * CMP_STRIDE + (CMP_LEN - 1)
    valid = cmp_end <= col_pos(s.shape)
    sm = jnp.where(valid, s, NEG_INF)
    e = jnp.where(valid, jnp.exp2(sm - jnp.max(sm, axis=0, keepdims=True)), 0.0)
    l = jnp.sum(e, axis=0, keepdims=True)
    p_cmp = e / jnp.where(l > 0.0, l, 1.0)
    o_cmp = _dot(vct_ref[0, 0], p_cmp.astype(BF16))

    p_sum = p_cmp[:, 0:tq]
    for r in range(1, R):
        p_sum = p_sum + p_cmp[:, r * tq:(r + 1) * tq]
    hi, lo = _split2(p_sum)
    imp = _dot(ovlt_ref[...], hi) + _dot(ovlt_ref[...], lo)
    blk = lax.broadcasted_iota(jnp.int32, imp.shape, 0)
    t_q = t0 + lax.broadcasted_iota(jnp.int32, imp.shape, 1)
    cur = t_q // SEL_BLOCK
    forced = (blk == 0) | (blk == cur) | (blk == cur - 1)
    score = jnp.where(forced, FORCE_SCORE, jnp.where(blk * SEL_BLOCK <= t_q, imp, NEG_INF))
    score = jnp.where(blk < n_blk, score, -jnp.inf)
    selected = blk >= n_blk
    for _ in range(N_SEL):
        best = jnp.max(score, axis=0, keepdims=True)
        first = jnp.min(jnp.where(score == best, blk, 2 * LANES), axis=0, keepdims=True)
        pick = blk == first
        selected = selected | pick
        score = jnp.where(pick, -jnp.inf, score)
    bias = jnp.where(selected, 0.0, NEG_INF)
    qa = jnp.concatenate([qr, jnp.concatenate([bias] * R, axis=1).astype(BF16),
                          jnp.zeros((dh - n_blk_pad, cols), BF16)], axis=0)

    o_win = normalised(attend(init, sc_win, vwt_ref, win_tiles)[1])

    n_pairs = qi // 2

    def pair_body(i, carry):
        tiles = [(2 * i, None), (2 * i + 1, None)]
        return attend(carry, tile_scores(ks_ref, qa, tiles), vst_ref, tiles)

    carry = lax.fori_loop(0, n_pairs, pair_body, init)
    j_odd = 2 * n_pairs
    tail_tiles = [(j_odd, present(j_odd < qi)), (qi, causal)]
    o_slc = normalised(attend(carry, tile_scores(ks_ref, qa, tail_tiles), vst_ref, tail_tiles)[1])

    gates = gate_ref[0]
    outs = []
    for r in range(R):
        cs = slice(r * tq, (r + 1) * tq)
        outs.append(gates[3 * r:3 * r + 1] * o_cmp[:, cs] + gates[3 * r + 1:3 * r + 2] * o_slc[:, cs]
                    + gates[3 * r + 2:3 * r + 3] * o_win[:, cs])
    halves = [jnp.transpose(jnp.concatenate(outs[i:i + 2], axis=0)) for i in range(0, R, 2)]
    o_ref[0] = jnp.concatenate(halves, axis=1)


SEL_ROWS = 32


def _nsa_attention(q, qr, gates, k_cmp, v_cmp, ks, vst, kw, vwt, tq=256, tk=256):
    batch, _, seq = q.shape
    G, R, dh = N_KV_GROUPS, Q_PER_GROUP, HEAD_DIM
    n_slab = k_cmp.shape[2]
    n_blk = seq // SEL_BLOCK
    assert n_blk <= SEL_ROWS
    assert tq == tk and WINDOW == 2 * tk, "the window branch is written as exactly three key tiles"
    cmp_start = jnp.arange(n_slab) * CMP_STRIDE
    blk_start = jnp.arange(SEL_ROWS) * SEL_BLOCK
    overlap_t = ((cmp_start[None, :] < blk_start[:, None] + SEL_BLOCK)
                 & (cmp_start[None, :] + CMP_LEN - 1 >= blk_start[:, None])
                 & (jnp.arange(SEL_ROWS)[:, None] < n_blk)
                 & (jnp.arange(n_slab)[None, :] < (seq - CMP_LEN) // CMP_STRIDE + 1)).astype(BF16)
    v_cmp_t = jnp.swapaxes(v_cmp, 2, 3)
    qspec = pl.BlockSpec((1, R * dh, tq), lambda b, g, i: (b, g, i))
    full = lambda a: pl.BlockSpec((1, 1) + a.shape[2:], lambda b, g, i: (b, g, 0, 0))
    vspec = pl.BlockSpec((1, dh, seq), lambda b, g, i: (b, g, 0))
    return pl.pallas_call(
        functools.partial(_nsa_attn_kernel, tq=tq, tk=tk, n_blk=n_blk),
        grid=(batch, G, seq // tq),
        in_specs=[qspec, qspec, pl.BlockSpec((1, GATE_ROWS, tq), lambda b, g, i: (b, g, i)),
                  full(k_cmp), full(v_cmp_t), full(ks), vspec, full(kw), vspec,
                  _resident(overlap_t.shape)],
        out_specs=pl.BlockSpec((1, tq, R * dh), lambda b, g, i: (b, i, g)),
        out_shape=jax.ShapeDtypeStruct((batch, seq, G * R * dh), F32),
        compiler_params=_cparams("parallel", "parallel", "parallel"),
        name="nsa_attn",
    )(q, qr, gates, k_cmp, v_cmp_t, ks, vst, kw, vwt, overlap_t)


def _moe_kernel(be_ref, nb_ref, x_ref, wgu_ref, wd_ref, *rest, first_block):
    o_ref, wgu_bf, wd_bf = rest[-3:]
    i = pl.program_id(0)
    blk = i + first_block
    ff = wd_ref.shape[1]
    used = blk < nb_ref[0]

    @pl.when(used & ((i == 0) | (be_ref[blk] != be_ref[jnp.maximum(blk - 1, 0)])))
    def _():
        wgu_bf[...] = wgu_ref[0].astype(BF16)
        wd_bf[...] = wd_ref[0].astype(BF16)

    @pl.when(used)
    def _():
        xb = x_ref[...]
        gt = _dot(xb, wgu_bf[:, :ff])
        up = _dot(xb, wgu_bf[:, ff:])
        h = (gt * _sigmoid(gt) * up).astype(BF16)
        o_ref[...] = _dot(h, wd_bf[...])

    @pl.when(jnp.logical_not(used))
    def _():
        o_ref[...] = jnp.zeros_like(o_ref)


def _moe_experts(xs_parts, block_e, n_used, w_gu, w_down):
    d = xs_parts[0].shape[1]
    n_rows = sum(xs.shape[0] for xs in xs_parts)
    ff = w_down.shape[1]
    ys, first = None, 0
    for xs in xs_parts:
        n_blocks = xs.shape[0] // MOE_ROWS
        in_specs = [pl.BlockSpec((MOE_ROWS, d), lambda i, be, nb: (i, 0)),
                    pl.BlockSpec((1, d, 2 * ff), lambda i, be, nb, first=first: (be[i + first], 0, 0)),
                    pl.BlockSpec((1, ff, d), lambda i, be, nb, first=first: (be[i + first], 0, 0))]
        args = [block_e, n_used, xs, w_gu, w_down]
        aliases = {}
        if ys is not None:
            in_specs.append(pl.BlockSpec(memory_space=pl.ANY))
            args.append(ys)
            aliases = {len(args) - 1: 0}
        ys = pl.pallas_call(
            functools.partial(_moe_kernel, first_block=first),
            grid_spec=pltpu.PrefetchScalarGridSpec(
                num_scalar_prefetch=2,
                grid=(n_blocks,),
                in_specs=in_specs,
                out_specs=pl.BlockSpec((MOE_ROWS, d), lambda i, be, nb, first=first: (i + first, 0)),
                scratch_shapes=[pltpu.VMEM((d, 2 * ff), BF16), pltpu.VMEM((ff, d), BF16)],
            ),
            out_shape=jax.ShapeDtypeStruct((n_rows, d), F32),
            input_output_aliases=aliases,
            compiler_params=_cparams("arbitrary"),
            name="moe_experts",
        )(*args)
        first += n_blocks
    return ys


def _moe(x2d, logits, w_gu, w_down):
    n_tok, d = x2d.shape
    n_assign = n_tok * TOP_K
    top_logit, top_e = lax.top_k(logits, TOP_K)
    weights = jax.nn.softmax(top_logit, axis=-1)
    flat_e = top_e.reshape(-1).astype(jnp.int32)
    onehot = (flat_e[:, None] == jnp.arange(N_EXPERTS, dtype=jnp.int32)[None, :]).astype(jnp.int32)
    running = jnp.cumsum(onehot, axis=0)
    counts = running[-1]
    padded = (counts + MOE_ROWS - 1) // MOE_ROWS * MOE_ROWS
    pad_end = jnp.cumsum(padded)
    pad_start = pad_end - padded
    grp_start = jnp.cumsum(counts) - counts
    pos = jnp.sum(onehot * (running - 1 + pad_start[None, :]), axis=1).reshape(n_tok, TOP_K)
    n_blocks = -(-n_assign // MOE_ROWS) + N_EXPERTS
    block_e = jnp.minimum(jnp.searchsorted(pad_end, jnp.arange(n_blocks) * MOE_ROWS, side='right'),
                          N_EXPERTS - 1).astype(jnp.int32)
    n_used = (pad_end[-1] // MOE_ROWS).astype(jnp.int32).reshape(1)
    order = jnp.argsort(flat_e)
    tok_sorted = (order // TOP_K).astype(jnp.int32)
    row_in_grp = (jnp.arange(n_blocks, dtype=jnp.int32) * MOE_ROWS - pad_start[block_e])[:, None] \
        + jnp.arange(MOE_ROWS, dtype=jnp.int32)[None, :]
    src = jnp.clip(grp_start[block_e][:, None] + row_in_grp, 0, n_assign - 1)
    row_tok = jnp.where(row_in_grp < counts[block_e][:, None], tok_sorted[src], 0).reshape(-1)
    half = n_blocks // 2 * MOE_ROWS
    ys = _moe_experts([x2d[row_tok[:half]], x2d[row_tok[half:]]], block_e, n_used, w_gu, w_down)
    return [ys[pos[:, s]] for s in range(TOP_K)], weights


def kernel(x, p, a_mu, a_w_rkv, a_w0, a_w1, a_w2, a_a0, a_a1, a_a2, a_g1, a_g2, a_k_k, a_k_a, a_r_k,
           a_gn_g, a_gn_b, a_w_o, b_w_kv, b_cmp_pos, b_phi_w1, b_phi_b1, b_phi_w2, b_w_in, b_w_o,
           f_w_gu, f_w_down, m_w_router, m_b_router, m_w_gu, m_w_down, ln_g, ln_b, ple_w, ple_gate_w,
           ple_gate_b):
    batch, seq, d = x.shape
    m = batch * seq
    x0 = x.reshape(m, d)
    p2d = p.reshape(DEPTH, m, PLE_DIM)
    bf = lambda w: w.astype(BF16)

    x1 = _rwkv_layer(x0, batch, seq, a_mu[0], bf(a_w_rkv[0]), a_w0[0], bf(a_w1[0]), bf(a_w2[0]), a_a0[0],
                     bf(a_a1[0]), bf(a_a2[0]), bf(a_g1[0]), bf(a_g2[0]), a_k_k[0], a_k_a[0], a_r_k[0],
                     a_gn_g[0], a_gn_b[0], bf(a_w_o[0]), ln_g[0, 0], ln_b[0, 0])
    x2 = _ffn_ple(x1, p2d, 0, bf(f_w_gu[0][:, :D_FF]), bf(f_w_gu[0][:, D_FF:]), bf(f_w_down[0]),
                  ln_g[0, 1], ln_b[0, 1], bf(ple_w[0]), bf(ple_gate_w[0]), ple_gate_b[0])

    slabs, ks, kw, vst, vwt = _nsa_kv(x2, b_w_kv, batch, seq)
    k_cmp, v_cmp = _nsa_compress(slabs, b_cmp_pos, b_phi_w1, b_phi_b1, b_phi_w2)
    q, qr, gates = _nsa_q(x2, b_w_in[0], batch, seq)
    attn = _nsa_attention(q, qr, gates, k_cmp, v_cmp, ks, vst, kw, vwt).reshape(m, d)
    x3, x3_bf, logits = _proj_ln_router(attn, x2, bf(b_w_o[0]), ln_g[1, 0], ln_b[1, 0], m_w_router[0],
                                        m_b_router[0])
    (y0, y1), route_w = _moe(x3_bf, logits[:, :N_EXPERTS], m_w_gu[0], m_w_down[0])
    out = _moe_combine_ln_ple(x3, y0, y1, route_w, p2d, 1, ln_g[1, 1], ln_b[1, 1], bf(ple_w[1]),
                              bf(ple_gate_w[1]), ple_gate_b[1])
    return out.reshape(batch, seq, d)
```

```python
import functools
import math

import jax
import jax.numpy as jnp
from jax import lax
from jax.experimental import pallas as pl
from jax.experimental.pallas import tpu as pltpu

BF16 = jnp.bfloat16
F32 = jnp.float32

LANES = 128
VMEM_LIMIT_BYTES = 56 * 1024 * 1024

D_MODEL = 1024
PLE_DIM = 256
RWKV_HEAD = 64
GN_EPS = 64e-5
N_HEADS = 16
HEAD_DIM = 64
N_KV_GROUPS = 4
Q_PER_GROUP = 4
N_BRANCH = 3
CMP_LEN = 32
CMP_STRIDE = 16
CMP_HIDDEN = 256
SEL_BLOCK = 64
N_SEL = 8
WINDOW = 512
ROPE_THETA = 10000.0
D_FF = 2816
N_EXPERTS = 8
TOP_K = 2
D_FF_EXPERT = 1408
MOE_ROWS = 256
LN_EPS = 1e-5
DEPTH = 2
DEEPNORM_ALPHA = (2.0 * DEPTH) ** 0.25
NEG_INF = -1e30
FORCE_SCORE = 1e4

SCAN_CHUNK = 64


def _cparams(*sem):
    return pltpu.CompilerParams(dimension_semantics=sem, vmem_limit_bytes=VMEM_LIMIT_BYTES)


def _resident(shape):
    nd = len(shape)
    return pl.BlockSpec(shape, lambda *_: (0,) * nd, pipeline_mode=pl.Buffered(1))


def _dot(a, b):
    return jnp.dot(a, b, preferred_element_type=F32)


def _dot_nt(a, b):
    return lax.dot_general(a, b, (((1,), (1,)), ((), ())), preferred_element_type=F32)


def _split2(x):
    hi = x.astype(BF16)
    lo = (x - hi.astype(F32)).astype(BF16)
    return hi, lo


def _split3(x):
    hi = x.astype(BF16)
    r1 = x - hi.astype(F32)
    mid = r1.astype(BF16)
    lo = (r1 - mid.astype(F32)).astype(BF16)
    return hi, mid, lo


def _layer_norm(y, g, b):
    mu = jnp.mean(y, axis=-1, keepdims=True)
    yc = y - mu
    var = jnp.mean(yc * yc, axis=-1, keepdims=True)
    return yc * lax.rsqrt(var + LN_EPS) * g + b


def _sigmoid(z):
    return 1.0 / (1.0 + jnp.exp(-z))


def _rwkv_project(first, x_ref, xh_ref, mu_ref, wrkv_ref, w0_ref, w1_ref, w2_ref, a0_ref, a1_ref,
                  a2_ref, g1_ref, g2_ref, r_ref, k_ref, v_ref, dl_ref, a_ref, g_ref):
    x = x_ref[...]
    prev_row = jnp.where(first, 0.0, xh_ref[0, 7:8, :])
    row = lax.broadcasted_iota(jnp.int32, x.shape, 0)
    x_shift = jnp.where(row == 0, prev_row, pltpu.roll(x, shift=1, axis=0))
    xx = x_shift - x

    def mix(i):
        return (x + xx * mu_ref[i:i + 1, :]).astype(BF16)

    r_ref[...] = _dot(mix(0), wrkv_ref[0])
    k_ref[...] = _dot(mix(1), wrkv_ref[1])
    v_ref[...] = _dot(mix(2), wrkv_ref[2])
    z = w0_ref[...] + _dot(jnp.tanh(_dot(mix(3), w1_ref[...])).astype(BF16), w2_ref[...])
    dl_ref[...] = -math.exp(-0.5) * _sigmoid(z)
    a_ref[...] = _sigmoid(a0_ref[...] + _dot(_dot(mix(4), a1_ref[...]).astype(BF16), a2_ref[...]))
    g_ref[...] = _dot(_sigmoid(_dot(mix(5), g1_ref[...])).astype(BF16), g2_ref[...])


def _rwkv_recurrence(first, r_ref, k_ref, v_ref, dl_ref, a_ref, g_ref, kk_ref, ka_ref, rk_ref, gng_ref,
                     gnb_ref, o_ref, state_ref, kkn_ref, cum_ref, *, n_pairs, n_chunks):
    C = SCAN_CHUNK
    N = RWKV_HEAD
    W = 2 * N
    tb = n_chunks * C

    @pl.when(first)
    def _():
        state_ref[...] = jnp.zeros_like(state_ref)

    tt = lax.broadcasted_iota(jnp.int32, (C, W), 0)
    ln = lax.broadcasted_iota(jnp.int32, (C, W), 1)
    ss = ln & (N - 1)
    strict_lower = ss < tt
    lower = ss <= tt
    eye = jnp.where(ss == tt, 1.0, 0.0)
    level_masks = []
    for lg in range(int(math.log2(C))):
        level_masks.append(((tt >> (lg + 1)) == (ss >> (lg + 1))) & ((tt >> lg) == (ss >> lg) + 1))
    rr = lax.broadcasted_iota(jnp.int32, (W, W), 0)
    cc = lax.broadcasted_iota(jnp.int32, (W, W), 1)
    same_head = (rr < N) == (cc < N)
    lane_head0 = ln < N
    WG = 2 * W
    lg_n = int(math.log2(N))
    ones_grp = jnp.where((lax.broadcasted_iota(jnp.int32, (WG, WG), 0) >> lg_n)
                         == (lax.broadcasted_iota(jnp.int32, (WG, WG), 1) >> lg_n), 1.0, 0.0).astype(BF16)
    col_groups = [slice(c, c + WG) for c in range(0, n_pairs * W, WG)]

    def bd(y):
        return jnp.concatenate([jnp.where(lane_head0, y, 0.0), jnp.where(lane_head0, 0.0, y)],
                               axis=0).astype(BF16)

    def pmm(xp, ybd):
        return _dot(xp.astype(BF16), ybd)

    def head_sum(xg):
        return _dot(xg.astype(BF16), ones_grp)

    tri_r = lax.broadcasted_iota(jnp.int32, (tb, tb), 0)
    tri_c = lax.broadcasted_iota(jnp.int32, (tb, tb), 1)
    lg_c = int(math.log2(C))
    tri_chunk = jnp.where((tri_c <= tri_r) & ((tri_c >> lg_c) == (tri_r >> lg_c)), 1.0, 0.0).astype(BF16)
    d_hi, d_mid, d_lo = _split3(dl_ref[...])
    cum_ref[...] = _dot(tri_chunk, d_hi) + _dot(tri_chunk, d_mid) + _dot(tri_chunk, d_lo)
    for cols in col_groups:
        kk_raw = k_ref[:, cols] * kk_ref[:, cols]
        kkn_ref[:, cols] = kk_raw / jnp.maximum(jnp.sqrt(head_sum(kk_raw * kk_raw)), 1e-12)

    def chunk_load(ci, pi):
        rows = pl.ds(pl.multiple_of(ci * C, C), C)
        cols = slice(pi * W, (pi + 1) * W)
        return (r_ref[rows, cols], k_ref[rows, cols], v_ref[rows, cols], dl_ref[rows, cols],
                a_ref[rows, cols], kkn_ref[rows, cols], cum_ref[rows, cols], state_ref[pi])

    def chunk_compute(pi, r, k, v, dl, a, kk, cum, m0):
        cols = slice(pi * W, (pi + 1) * W)
        k2 = k * (1.0 + (a - 1.0) * ka_ref[:, cols])
        b = a * kk
        p_incl = jnp.exp(cum)
        p_excl = jnp.exp(cum - dl)
        p_inv = jnp.exp(-cum)
        kq = kk * p_excl
        rq = r * p_incl
        bk = b * p_inv
        kkd = k2 * p_inv

        bkt = jnp.transpose(jnp.concatenate([b, k2], axis=0))
        cumt = jnp.transpose(jnp.concatenate([cum, cum], axis=0))
        last_col = cumt[:, C - 1:C]
        lhs_t = (bkt * jnp.exp(last_col - cumt)).astype(BF16)
        m_decayed = m0 * jnp.exp(last_col)

        lhs = jnp.concatenate([kq, rq], axis=0).astype(BF16)
        yield
        abg = _dot_nt(lhs, jnp.concatenate([bd(bk), bd(kkd)], axis=0))
        a_m = jnp.where(strict_lower, abg[:C, :W], 0.0)
        aq_m = jnp.where(lower, abg[C:, :W], 0.0)
        g_m = jnp.where(strict_lower, abg[:C, W:], 0.0)
        gq_m = jnp.where(lower, abg[C:, W:], 0.0)

        tinv = eye - jnp.where(level_masks[0], a_m, 0.0)
        yield
        gv = pmm(jnp.concatenate([g_m, gq_m], axis=0), bd(v))
        for lm in level_masks[1:]:
            w_ = pmm(jnp.where(lm, a_m, 0.0), bd(tinv))
            yield
            tinv = tinv - pmm(tinv, bd(w_))
            yield

        kq_y = pmm(tinv, jnp.concatenate([bd(kq), bd(gv[:C])], axis=1))
        kq1 = kq_y[:, :W]
        y = kq_y[:, W:]
        yield
        aq_ky = pmm(aq_m, jnp.concatenate([bd(kq1), bd(y)], axis=1))
        rq1 = rq - aq_ky[:, :W]
        o_loc = gv[C:] - aq_ky[:, W:]
        yield
        st = _dot(jnp.concatenate([rq1, kq1], axis=0).astype(BF16), m0.astype(BF16))
        o = st[:C] + o_loc
        z = -(st[C:] + y)
        yield
        upd = _dot(lhs_t, jnp.concatenate([z, v], axis=0).astype(BF16))
        return o, m_decayed + jnp.where(same_head, upd, 0.0)

    def interleave(gens):
        results = [None] * len(gens)
        live = list(range(len(gens)))
        while live:
            for i in list(live):
                try:
                    next(gens[i])
                except StopIteration as done:
                    results[i] = done.value
                    live.remove(i)
        return results

    def body(ci, carry):
        loaded = [chunk_load(ci, pi) for pi in range(n_pairs)]
        results = interleave([chunk_compute(pi, *loaded[pi]) for pi in range(n_pairs)])
        rows = pl.ds(pl.multiple_of(ci * C, C), C)
        for pi, (o, m_new) in enumerate(results):
            o_ref[rows, pi * W:(pi + 1) * W] = o
            state_ref[pi] = m_new
        return carry

    lax.fori_loop(0, n_chunks, body, 0)

    for cols in col_groups:
        o = o_ref[:, cols]
        r = r_ref[:, cols]
        v = v_ref[:, cols]
        k2 = k_ref[:, cols] * (1.0 + (a_ref[:, cols] - 1.0) * ka_ref[:, cols])
        oc = o - head_sum(o) * (1.0 / N)
        var_o = head_sum(oc * oc) * (1.0 / N)
        out = oc * lax.rsqrt(var_o + GN_EPS) * gng_ref[:, cols] + gnb_ref[:, cols]
        out = out + head_sum(r * k2 * rk_ref[:, cols]) * v
        o_ref[:, cols] = out * g_ref[:, cols]


def _rwkv_layer_kernel(x_ref, xh_ref, mu_ref, wrkv_ref, w0_ref, w1_ref, w2_ref, a0_ref, a1_ref, a2_ref,
                       g1_ref, g2_ref, kk_ref, ka_ref, rk_ref, gng_ref, gnb_ref, wo_ref, lng_ref, lnb_ref,
                       o_ref, state_ref, r_s, k_s, v_s, dl_s, a_s, g_s, kkn_s, cum_s, y_s, *, n_chunks):
    first = pl.program_id(1) == 0
    _rwkv_project(first, x_ref, xh_ref, mu_ref, wrkv_ref, w0_ref, w1_ref, w2_ref, a0_ref, a1_ref, a2_ref,
                  g1_ref, g2_ref, r_s, k_s, v_s, dl_s, a_s, g_s)
    _rwkv_recurrence(first, r_s, k_s, v_s, dl_s, a_s, g_s, kk_ref, ka_ref, rk_ref, gng_ref, gnb_ref, y_s,
                     state_ref, kkn_s, cum_s, n_pairs=x_ref.shape[1] // (2 * RWKV_HEAD), n_chunks=n_chunks)
    mix = _dot(y_s[...].astype(BF16), wo_ref[...])
    o_ref[...] = _layer_norm(DEEPNORM_ALPHA * x_ref[...] + mix, lng_ref[...], lnb_ref[...])


def _rwkv_layer(x2d, batch, seq, mu, w_rkv, w0, w1, w2, a0, a1, a2, g1, g2, k_k, k_a, r_k, gn_g, gn_b,
                w_o, ln_g, ln_b, tb=256):
    m, d = x2d.shape
    nt = seq // tb
    xh = x2d.reshape(m // 8, 8, d)
    vec = lambda z: z.reshape(1, d)
    blk = pl.BlockSpec((tb, d), lambda b, t: (b * nt + t, 0))
    weights = [mu, w_rkv, vec(w0), w1, w2, vec(a0), a1, a2, g1, g2, vec(k_k), vec(k_a), vec(r_k),
               vec(gn_g), vec(gn_b), w_o, vec(ln_g), vec(ln_b)]
    n_pairs = d // (2 * RWKV_HEAD)
    return pl.pallas_call(
        functools.partial(_rwkv_layer_kernel, n_chunks=tb // SCAN_CHUNK),
        grid=(batch, nt),
        in_specs=[blk, pl.BlockSpec((1, 8, d), lambda b, t: (jnp.maximum((b * nt + t) * (tb // 8) - 1, 0), 0, 0))]
        + [_resident(w.shape) for w in weights],
        out_specs=blk,
        out_shape=jax.ShapeDtypeStruct((m, d), F32),
        scratch_shapes=[pltpu.VMEM((n_pairs, 2 * RWKV_HEAD, 2 * RWKV_HEAD), F32)]
        + [pltpu.VMEM((tb, d), F32)] * 9,
        compiler_params=_cparams("parallel", "arbitrary"),
        name="rwkv_layer",
    )(x2d, xh, *weights)


def _proj_ln_router_kernel(y_ref, x_ref, w_ref, lng_ref, lnb_ref, wr_ref, br_ref, o_ref, ob_ref, lg_ref):
    mix = _dot(y_ref[...].astype(BF16), w_ref[...])
    xn = _layer_norm(DEEPNORM_ALPHA * x_ref[...] + mix, lng_ref[...], lnb_ref[...])
    o_ref[...] = xn
    xh, xl = _split2(xn)
    ob_ref[...] = xh
    lg_ref[...] = _dot(xh, wr_ref[0]) + _dot(xh, wr_ref[1]) + _dot(xl, wr_ref[0]) + br_ref[...]


def _proj_ln_router(y, x2d, w, ln_g, ln_b, w_router, b_router, tm=512):
    m, d = x2d.shape
    row = lambda i: (i, 0)
    wr = jnp.zeros((d, LANES), F32).at[:, :N_EXPERTS].set(w_router)
    wr3 = jnp.stack(_split2(wr))
    br =jnp.zeros((1, LANES), F32).at[0, :N_EXPERTS].set(b_router)
    return pl.pallas_call(
        _proj_ln_router_kernel,
        grid=(m // tm,),
        in_specs=[pl.BlockSpec((tm, y.shape[1]), row), pl.BlockSpec((tm, d), row), _resident(w.shape),
                  _resident((1, d)), _resident((1, d)), _resident(wr3.shape), _resident(br.shape)],
        out_specs=[pl.BlockSpec((tm, d), row), pl.BlockSpec((tm, d), row), pl.BlockSpec((tm, LANES), row)],
        out_shape=[jax.ShapeDtypeStruct((m, d), F32), jax.ShapeDtypeStruct((m, d), BF16),
                   jax.ShapeDtypeStruct((m, LANES), F32)],
        compiler_params=_cparams("parallel"),
        name="proj_ln_router",
    )(y, x2d, w, ln_g.reshape(1, d), ln_b.reshape(1, d), wr3, br)


def _ple(xn, p, plew_ref, gw_ref, gb_ref):
    gate = _sigmoid(_dot(xn.astype(BF16), gw_ref[...]) + gb_ref[...])
    return xn + _dot(p.astype(BF16), plew_ref[...]) * gate


def _ffn_ple_kernel(x_ref, p_ref, wg_ref, wu_ref, wd_ref, lng_ref, lnb_ref, plew_ref, gw_ref, gb_ref,
                    o_ref, *, ff_chunk):
    x = x_ref[...]
    xb = x.astype(BF16)
    acc = jnp.zeros_like(x)
    for c in range(wg_ref.shape[1] // ff_chunk):
        sl = slice(c * ff_chunk, (c + 1) * ff_chunk)
        gt = _dot(xb, wg_ref[:, sl])
        up = _dot(xb, wu_ref[:, sl])
        h = (gt * _sigmoid(gt) * up).astype(BF16)
        acc = acc + _dot(h, wd_ref[sl, :])
    xn = _layer_norm(DEEPNORM_ALPHA * x + acc, lng_ref[...], lnb_ref[...])
    o_ref[...] = _ple(xn, p_ref[0], plew_ref, gw_ref, gb_ref)


def _ple_block(p3d, layer, tm):
    return pl.BlockSpec((1, tm, p3d.shape[2]), lambda i: (layer, i, 0))


def _ffn_ple(x2d, p3d, layer, wg, wu, wd, ln_g, ln_b, ple_w, gate_w, gate_b, tm=512, ff_chunk=1408):
    m, d = x2d.shape
    row = lambda i: (i, 0)
    return pl.pallas_call(
        functools.partial(_ffn_ple_kernel, ff_chunk=ff_chunk),
        grid=(m // tm,),
        in_specs=[pl.BlockSpec((tm, d), row), _ple_block(p3d, layer, tm),
                  _resident(wg.shape), _resident(wu.shape), _resident(wd.shape),
                  _resident((1, d)), _resident((1, d)), _resident(ple_w.shape), _resident(gate_w.shape),
                  _resident((1, d))],
        out_specs=pl.BlockSpec((tm, d), row),
        out_shape=jax.ShapeDtypeStruct((m, d), F32),
        compiler_params=_cparams("parallel"),
        name="ffn_ple",
    )(x2d, p3d, wg, wu, wd, ln_g.reshape(1, d), ln_b.reshape(1, d), ple_w, gate_w, gate_b.reshape(1, d))


def _moe_combine_ln_ple_kernel(x_ref, y0_ref, y1_ref, rw_ref, p_ref, lng_ref, lnb_ref, plew_ref, gw_ref,
                               gb_ref, *rest):
    o_ref = rest[-1]
    rw = rw_ref[...]
    ffn = y0_ref[...] * rw[:, 0:1] + y1_ref[...] * rw[:, 1:2]
    xn = _layer_norm(DEEPNORM_ALPHA * x_ref[...] + ffn, lng_ref[...], lnb_ref[...])
    o_ref[...] = _ple(xn, p_ref[0], plew_ref, gw_ref, gb_ref)


def _moe_combine_ln_ple(x2d, y_parts, route_w, p3d, layer, ln_g, ln_b, ple_w, gate_w, gate_b, tm=512):
    m, d = x2d.shape
    tm = min(tm, min(y0.shape[0] for y0, _ in y_parts))
    out, first = None, 0
    for y0, y1 in y_parts:
        n_tiles = y0.shape[0] // tm
        glob = lambda i, first=first: (i + first, 0)
        loc = pl.BlockSpec((tm, d), lambda i: (i, 0))
        in_specs = [pl.BlockSpec((tm, d), glob), loc, loc, pl.BlockSpec((tm, TOP_K), glob),
                    pl.BlockSpec((1, tm, p3d.shape[2]), lambda i, first=first: (layer, i + first, 0)),
                    _resident((1, d)), _resident((1, d)), _resident(ple_w.shape), _resident(gate_w.shape),
                    _resident((1, d))]
        args = [x2d, y0, y1, route_w, p3d, ln_g.reshape(1, d), ln_b.reshape(1, d), ple_w, gate_w,
                gate_b.reshape(1, d)]
        aliases = {}
        if out is not None:
            in_specs.append(pl.BlockSpec(memory_space=pl.ANY))
            args.append(out)
            aliases = {len(args) - 1: 0}
        out = pl.pallas_call(
            _moe_combine_ln_ple_kernel,
            grid=(n_tiles,),
            in_specs=in_specs,
            out_specs=pl.BlockSpec((tm, d), glob),
            out_shape=jax.ShapeDtypeStruct((m, d), F32),
            input_output_aliases=aliases,
            compiler_params=_cparams("parallel"),
            name="moe_combine_ln_ple",
        )(*args)
        first += n_tiles
    return out


def _swap_half_cols(w):
    k, n = w.shape
    return w.reshape(k, n // HEAD_DIM, 2, HEAD_DIM // 2)[:, :, ::-1, :].reshape(k, n)


def _rope_tables(seq):
    half = HEAD_DIM // 2
    inv = ROPE_THETA ** (-jnp.arange(half, dtype=F32) / half)
    ang = jnp.arange(seq, dtype=F32)[:, None] * inv[None, :]
    cos, sin = jnp.cos(ang), jnp.sin(ang)
    return jnp.concatenate([cos, cos], -1), jnp.concatenate([-sin, sin], -1)


def _nsa_kv_kernel(x_ref, wk_ref, wvt_ref, cos_ref, sin_ref, slab_ref, ks_ref, kw_ref, vst_ref, vwt_ref,
                   cz_s):
    G, dh = N_KV_GROUPS, HEAD_DIM
    gw = G * dh
    xb = x_ref[...].astype(BF16)
    res = _dot(xb, wk_ref[...])
    res_t = _dot_nt(wvt_ref[...], xb)
    tm = res.shape[0]
    n_slab = tm // CMP_STRIDE
    for c in range(cz_s.shape[0]):
        cz_s[c] = res[:, c * LANES:(c + 1) * LANES]
        rows = [cz_s[c, pl.ds(l, n_slab, stride=CMP_STRIDE), :] for l in range(CMP_STRIDE)]
        for half in range(LANES // dh):
            jg = c * (LANES // dh) + half
            slab_ref[jg // G, 0, jg % G] = jnp.concatenate(
                [r[:, half * dh:(half + 1) * dh] for r in rows], axis=1).astype(BF16)
    cos = cos_ref[...]
    sin = sin_ref[...]
    ks = res[:, 2 * gw:3 * gw] * cos + res[:, 3 * gw:4 * gw] * sin
    kw = res[:, 4 * gw:5 * gw] * cos + res[:, 5 * gw:6 * gw] * sin
    t_glob = pl.program_id(1) * tm + lax.broadcasted_iota(jnp.int32, (tm, dh), 0)
    onehot = jnp.where(lax.broadcasted_iota(jnp.int32, (tm, dh), 1) == t_glob // SEL_BLOCK, 1.0, 0.0)
    for g in range(G):
        sl = slice(g * dh, (g + 1) * dh)
        ks_ref[0, g] = jnp.concatenate([ks[:, sl], onehot], axis=1).astype(BF16)
        kw_ref[0, g] = kw[:, sl].astype(BF16)
    vst_ref[0] = res_t[:gw].astype(BF16)
    vwt_ref[0] = res_t[gw:].astype(BF16)


def _nsa_kv(x2d, w_kv, batch, seq, tm=256):
    m, d = x2d.shape
    G, dh = N_KV_GROUPS, HEAD_DIM
    gw = G * dh
    parts = [w_kv[:, j * gw:(j + 1) * gw] for j in range(6)]
    w_k = jnp.concatenate([parts[0], parts[1], parts[2], _swap_half_cols(parts[2]),
                           parts[4], _swap_half_cols(parts[4])], axis=1).astype(BF16)
    w_vt = jnp.concatenate([parts[3], parts[5]], axis=1).T.astype(BF16)
    cos, sin = _rope_tables(seq)
    cos = jnp.tile(cos, (1, G))
    sin = jnp.tile(sin, (1, G))
    nt = seq // tm
    o64 = jax.ShapeDtypeStruct((batch, G, seq, dh), BF16)
    o128 = jax.ShapeDtypeStruct((batch, G, seq, 2 * dh), BF16)
    ot = jax.ShapeDtypeStruct((batch, gw, seq), BF16)
    b64 = pl.BlockSpec((1, G, tm, dh), lambda b, t: (b, 0, t, 0))
    b128 = pl.BlockSpec((1, G, tm, 2 * dh), lambda b, t: (b, 0, t, 0))
    bt = pl.BlockSpec((1, gw, tm), lambda b, t: (b, 0, t))
    slab_w = CMP_STRIDE * dh
    oslab = jax.ShapeDtypeStruct((2, batch, G, seq // CMP_STRIDE, slab_w), BF16)
    bslab = pl.BlockSpec((2, 1, G, tm // CMP_STRIDE, slab_w), lambda b, t: (0, b, 0, t, 0))
    return pl.pallas_call(
        _nsa_kv_kernel,
        grid=(batch, nt),
        in_specs=[pl.BlockSpec((tm, d), lambda b, t: (b * nt + t, 0)), _resident(w_k.shape),
                  _resident(w_vt.shape),
                  pl.BlockSpec((tm, gw), lambda b, t: (t, 0)), pl.BlockSpec((tm, gw), lambda b, t: (t, 0))],
        out_specs=[bslab, b128, b64, bt, bt],
        out_shape=[oslab, o128, o64, ot, ot],
        scratch_shapes=[pltpu.VMEM((2 * gw // LANES, tm, LANES), F32)],
        compiler_params=_cparams("parallel", "parallel"),
        name="nsa_kv",
    )(x2d, w_k, w_vt, cos, sin)


def _nsa_cmp_kernel(z_ref, pos_ref, w1_ref, b1_ref, w2_ref, o_ref, *, slabs_per_seq):
    half = w1_ref.shape[1] // 2
    z = z_ref[0]
    tm = z.shape[0]
    first = _dot(z, w1_ref[0, :half, :])
    second = _dot(z, w1_ref[0, half:, :])
    const = _dot(pos_ref[0], w1_ref[0])[0:1, :] + b1_ref[0]
    hid = first + pltpu.roll(second, shift=tm - 1, axis=0) + const
    act = 0.5 * hid * (1.0 + jnp.tanh(math.sqrt(2.0 / math.pi) * (hid + 0.044715 * hid * hid * hid)))
    out = _dot(act.astype(BF16), w2_ref[0])
    row = lax.broadcasted_iota(jnp.int32, out.shape, 0)
    o_ref[0] = jnp.where(row % slabs_per_seq == slabs_per_seq - 1, 0.0, out).astype(BF16)


def _nsa_compress(slabs, cmp_pos, phi_w1, phi_b1, phi_w2, tm=512):
    _, batch, G, n_slab, slab = slabs.shape
    dh = slab // CMP_STRIDE
    seq = n_slab * CMP_STRIDE
    rows = batch * G * n_slab
    tm = min(tm, rows)
    z = slabs.reshape(2, rows, slab)
    pos = jnp.broadcast_to(cmp_pos.reshape(2, 1, CMP_LEN * dh), (2, 8, CMP_LEN * dh)).astype(BF16)
    out = pl.pallas_call(
        functools.partial(_nsa_cmp_kernel, slabs_per_seq=seq // CMP_STRIDE),
        grid=(2, rows // tm),
        in_specs=[pl.BlockSpec((1, tm, slab), lambda j, i: (j, i, 0)),
                  pl.BlockSpec((1, 8, CMP_LEN * dh), lambda j, i: (j, 0, 0)),
                  pl.BlockSpec((1, CMP_LEN * dh, CMP_HIDDEN), lambda j, i: (j, 0, 0)),
                  pl.BlockSpec((1, 1, CMP_HIDDEN), lambda j, i: (j, 0, 0)),
                  pl.BlockSpec((1, CMP_HIDDEN, dh), lambda j, i: (j, 0, 0))],
        out_specs=pl.BlockSpec((1, tm, dh), lambda j, i: (j, i, 0)),
        out_shape=jax.ShapeDtypeStruct((2, rows, dh), BF16),
        compiler_params=_cparams("parallel", "parallel"),
        name="nsa_compress",
    )(z, pos, phi_w1.astype(BF16), phi_b1.reshape(2, 1, CMP_HIDDEN), phi_w2.astype(BF16))
    return out[0].reshape(batch, G, n_slab, dh), out[1].reshape(batch, G, n_slab, dh)


GATE_ROWS = 16


def _nsa_q_kernel(x_ref, wt_ref, cos_ref, sin_ref, q_ref, qr_ref, gate_ref):
    dh = HEAD_DIM
    hw = N_HEADS * dh
    scale = HEAD_DIM ** -0.5 * math.log2(math.e)
    res_t = _dot_nt(wt_ref[...], x_ref[...].astype(BF16))
    tm = res_t.shape[1]
    q = res_t[:hw].reshape(N_HEADS, dh, tm)
    q_sw = jnp.concatenate([q[:, dh // 2:], q[:, :dh // 2]], axis=1)
    qr = q * cos_ref[...][None] + q_sw * sin_ref[...][None]
    q_ref[0] = (q * scale).reshape(hw, tm).astype(BF16)
    qr_ref[0] = (qr * scale).reshape(hw, tm).astype(BF16)
    gate_ref[0] = _sigmoid(res_t[hw:])


def _nsa_q(x2d, w_in, batch, seq, tm=512):
    m, d = x2d.shape
    G, R, dh = N_KV_GROUPS, Q_PER_GROUP, HEAD_DIM
    hw = N_HEADS * dh
    w_gate = w_in[:, hw:].reshape(d, G, R * N_BRANCH)
    w_gate = jnp.pad(w_gate, ((0, 0), (0, 0), (0, GATE_ROWS - R * N_BRANCH))).reshape(d, G * GATE_ROWS)
    wt = jnp.concatenate([w_in[:, :hw], w_gate], axis=1).T.astype(BF16)
    cos, sin = _rope_tables(seq)
    nt = seq // tm
    oq = jax.ShapeDtypeStruct((batch, hw, seq), BF16)
    bq = pl.BlockSpec((1, hw, tm), lambda b, t: (b, 0, t))
    tab = pl.BlockSpec((dh, tm), lambda b, t: (0, t))
    return pl.pallas_call(
        _nsa_q_kernel,
        grid=(batch, nt),
        in_specs=[pl.BlockSpec((tm, d), lambda b, t: (b * nt + t, 0)), _resident(wt.shape), tab, tab],
        out_specs=[bq, bq, pl.BlockSpec((1, G * GATE_ROWS, tm), lambda b, t: (b, 0, t))],
        out_shape=[oq, oq, jax.ShapeDtypeStruct((batch, G * GATE_ROWS, seq), F32)],
        compiler_params=_cparams("parallel", "parallel"),
        name="nsa_q",
    )(x2d, wt, cos.T, sin.T)


def _nsa_attn_kernel(q_ref, qr_ref, gate_ref, kc_ref, vct_ref, ks_ref, vst_ref, kw_ref, vwt_ref, ovlt_ref,
                     o_ref, *, tq, tk, n_blk):
    R, dh = Q_PER_GROUP, HEAD_DIM
    cols = R * tq
    qi = pl.program_id(2)
    t0 = qi * tq
    n_blk_pad = ovlt_ref.shape[0]

    def heads_to_lanes(x):
        return jnp.concatenate([x[r * dh:(r + 1) * dh] for r in range(R)], axis=1)

    q = heads_to_lanes(q_ref[0])
    qr = heads_to_lanes(qr_ref[0])

    def col_pos(shape):
        return t0 + (lax.broadcasted_iota(jnp.int32, shape, 1) & (tq - 1))

    def key_tile(j):
        return pl.ds(pl.multiple_of(j * tk, tk), tk)

    ct = tq
    col_tiles = [slice(c, c + ct) for c in range(0, cols, ct)]
    ones_rows = 16
    ones_blk = jnp.ones((ones_rows, tk), BF16)

    def tile_scores(k_ref, q_op, tiles):
        return [[_dot(k_ref[0, 0, key_tile(j), :], q_op[:, cs]) for j, _ in tiles] for cs in col_tiles]

    def attend(carry, sc, vt_ref, tiles):
        m_i, acc = carry
        outs = []
        for cs, sc_c in zip(col_tiles, sc):
            masked = [s_j if mask_fn is None else jnp.where(mask_fn(k_off, q_off), s_j, NEG_INF)
                      for s_j, (_, mask_fn) in zip(sc_c, tiles)]
            m_new = m_i[:, cs]
            for s_j in masked:
                m_new = jnp.maximum(m_new, jnp.max(s_j, axis=0, keepdims=True))
            acc_new = jnp.exp2(m_i[:, cs] - m_new) * acc[:, cs]
            for s_j, (j, _) in zip(masked, tiles):
                v_ext = jnp.concatenate([vt_ref[0, :, key_tile(j)], ones_blk], axis=0)
                acc_new = acc_new + _dot(v_ext, jnp.exp2(s_j - m_new).astype(BF16))
            outs.append((m_new, acc_new))
        return tuple(jnp.concatenate([o[i] for o in outs], axis=1) for i in range(2))

    def normalised(acc):
        return acc[:dh] / acc[dh:dh + 1]

    init = (jnp.full((1, cols), NEG_INF, F32), jnp.zeros((dh + ones_rows, cols), F32))

    k_off = lax.broadcasted_iota(jnp.int32, (tk, ct), 0)
    q_off = lax.broadcasted_iota(jnp.int32, (tk, ct), 1)
    causal = lambda k, q_: k <= q_
    present = lambda cond: (lambda k, q_: (k >= 0) & cond)

    win_tiles = [(jnp.maximum(qi - 2, 0), lambda k, q_: (q_ < k) & (qi >= 2)),
                 (jnp.maximum(qi - 1, 0), present(qi >= 1)),
                 (qi, causal)]
    sc_win = tile_scores(kw_ref, qr, win_tiles)

    s = _dot(kc_ref[0, 0], q)
    cmp_end = lax.broadcasted_iota(jnp.int32, s.shape, 0) * CMP_STRIDE + (CMP_LEN - 1)
    valid = cmp_end <= col_pos(s.shape)
    sm = jnp.where(valid, s, NEG_INF)
    e = jnp.where(valid, jnp.exp2(sm - jnp.max(sm, axis=0, keepdims=True)), 0.0)
    l = jnp.sum(e, axis=0, keepdims=True)
    p_cmp = e / jnp.where(l > 0.0, l, 1.0)
    o_cmp = _dot(vct_ref[0, 0], p_cmp.astype(BF16))

    p_sum = p_cmp[:, 0:tq]
    for r in range(1, R):
        p_sum = p_sum + p_cmp[:, r * tq:(r + 1) * tq]
    hi, lo = _split2(p_sum)
    imp = _dot(ovlt_ref[...], hi) + _dot(ovlt_ref[...], lo)
    blk = lax.broadcasted_iota(jnp.int32, imp.shape, 0)
    t_q = t0 + lax.broadcasted_iota(jnp.int32, imp.shape, 1)
    cur = t_q // SEL_BLOCK
    forced = (blk == 0) | (blk == cur) | (blk == cur - 1)
    score = jnp.where(forced, FORCE_SCORE, jnp.where(blk * SEL_BLOCK <= t_q, imp, NEG_INF))
    score = jnp.where(blk < n_blk, score, -jnp.inf)
    selected = blk >= n_blk
    for _ in range(N_SEL):
        best = jnp.max(score, axis=0, keepdims=True)
        first = jnp.min(jnp.where(score == best, blk, 2 * LANES), axis=0, keepdims=True)
        pick = blk == first
        selected = selected | pick
        score = jnp.where(pick, -jnp.inf, score)
    bias = jnp.where(selected, 0.0, NEG_INF)
    qa = jnp.concatenate([qr, jnp.concatenate([bias] * R, axis=1).astype(BF16),
                          jnp.zeros((dh - n_blk_pad, cols), BF16)], axis=0)

    o_win = normalised(attend(init, sc_win, vwt_ref, win_tiles)[1])

    n_pairs = qi // 2

    def pair_body(i, carry):
        tiles = [(2 * i, None), (2 * i + 1, None)]
        return attend(carry, tile_scores(ks_ref, qa, tiles), vst_ref, tiles)

    carry = lax.fori_loop(0, n_pairs, pair_body, init)
    j_odd = 2 * n_pairs
    tail_tiles = [(j_odd, present(j_odd < qi)), (qi, causal)]
    o_slc = normalised(attend(carry, tile_scores(ks_ref, qa, tail_tiles), vst_ref, tail_tiles)[1])

    gates = gate_ref[0]
    outs = []
    for r in range(R):
        cs = slice(r * tq, (r + 1) * tq)
        outs.append(gates[3 * r:3 * r + 1] * o_cmp[:, cs] + gates[3 * r + 1:3 * r + 2] * o_slc[:, cs]
                    + gates[3 * r + 2:3 * r + 3] * o_win[:, cs])
    halves = [jnp.transpose(jnp.concatenate(outs[i:i + 2], axis=0)) for i in range(0, R, 2)]
    o_ref[0] = jnp.concatenate(halves, axis=1)


SEL_ROWS = 32


def _nsa_attention(q, qr, gates, k_cmp, v_cmp, ks, vst, kw, vwt, tq=256, tk=256):
    batch, _, seq = q.shape
    G, R, dh = N_KV_GROUPS, Q_PER_GROUP, HEAD_DIM
    n_slab = k_cmp.shape[2]
    n_blk = seq // SEL_BLOCK
    assert n_blk <= SEL_ROWS
    assert tq == tk and WINDOW == 2 * tk, "the window branch is written as exactly three key tiles"
    cmp_start = jnp.arange(n_slab) * CMP_STRIDE
    blk_start = jnp.arange(SEL_ROWS) * SEL_BLOCK
    overlap_t = ((cmp_start[None, :] < blk_start[:, None] + SEL_BLOCK)
                 & (cmp_start[None, :] + CMP_LEN - 1 >= blk_start[:, None])
                 & (jnp.arange(SEL_ROWS)[:, None] < n_blk)
                 & (jnp.arange(n_slab)[None, :] < (seq - CMP_LEN) // CMP_STRIDE + 1)).astype(BF16)
    v_cmp_t = jnp.swapaxes(v_cmp, 2, 3)
    qspec = pl.BlockSpec((1, R * dh, tq), lambda b, g, i: (b, g, i))
    full = lambda a: pl.BlockSpec((1, 1) + a.shape[2:], lambda b, g, i: (b, g, 0, 0))
    vspec = pl.BlockSpec((1, dh, seq), lambda b, g, i: (b, g, 0))
    return pl.pallas_call(
        functools.partial(_nsa_attn_kernel, tq=tq, tk=tk, n_blk=n_blk),
        grid=(batch, G, seq // tq),
        in_specs=[qspec, qspec, pl.BlockSpec((1, GATE_ROWS, tq), lambda b, g, i: (b, g, i)),
                  full(k_cmp), full(v_cmp_t), full(ks), vspec, full(kw), vspec,
                  _resident(overlap_t.shape)],
        out_specs=pl.BlockSpec((1, tq, R * dh), lambda b, g, i: (b, i, g)),
        out_shape=jax.ShapeDtypeStruct((batch, seq, G * R * dh), F32),
        compiler_params=_cparams("parallel", "parallel", "parallel"),
        name="nsa_attn",
    )(q, qr, gates, k_cmp, v_cmp_t, ks, vst, kw, vwt, overlap_t)


def _moe_kernel(be_ref, nb_ref, x_ref, wgu_ref, wd_ref, *rest, first_block):
    o_ref, wgu_bf, wd_bf = rest[-3:]
    i = pl.program_id(0)
    blk = i + first_block
    ff = wd_ref.shape[1]
    used = blk < nb_ref[0]

    @pl.when(used & ((i == 0) | (be_ref[blk] != be_ref[jnp.maximum(blk - 1, 0)])))
    def _():
        wgu_bf[...] = wgu_ref[0].astype(BF16)
        wd_bf[...] = wd_ref[0].astype(BF16)

    @pl.when(used)
    def _():
        xb = x_ref[...]
        gt = _dot(xb, wgu_bf[:, :ff])
        up = _dot(xb, wgu_bf[:, ff:])
        h = (gt * _sigmoid(gt) * up).astype(BF16)
        o_ref[...] = _dot(h, wd_bf[...])

    @pl.when(jnp.logical_not(used))
    def _():
        o_ref[...] = jnp.zeros_like(o_ref)


def _moe_experts(xs_parts, block_e, n_used, w_gu, w_down):
    d = xs_parts[0].shape[1]
    n_rows = sum(xs.shape[0] for xs in xs_parts)
    ff = w_down.shape[1]
    ys, first = None, 0
    for xs in xs_parts:
        n_blocks = xs.shape[0] // MOE_ROWS
        in_specs = [pl.BlockSpec((MOE_ROWS, d), lambda i, be, nb: (i, 0)),
                    pl.BlockSpec((1, d, 2 * ff), lambda i, be, nb, first=first: (be[i + first], 0, 0)),
                    pl.BlockSpec((1, ff, d), lambda i, be, nb, first=first: (be[i + first], 0, 0))]
        args = [block_e, n_used, xs, w_gu, w_down]
        aliases = {}
        if ys is not None:
            in_specs.append(pl.BlockSpec(memory_space=pl.ANY))
            args.append(ys)
            aliases = {len(args) - 1: 0}
        ys = pl.pallas_call(
            functools.partial(_moe_kernel, first_block=first),
            grid_spec=pltpu.PrefetchScalarGridSpec(
                num_scalar_prefetch=2,
                grid=(n_blocks,),
                in_specs=in_specs,
                out_specs=pl.BlockSpec((MOE_ROWS, d), lambda i, be, nb, first=first: (i + first, 0)),
                scratch_shapes=[pltpu.VMEM((d, 2 * ff), BF16), pltpu.VMEM((ff, d), BF16)],
            ),
            out_shape=jax.ShapeDtypeStruct((n_rows, d), F32),
            input_output_aliases=aliases,
            compiler_params=_cparams("arbitrary"),
            name="moe_experts",
        )(*args)
        first += n_blocks
    return ys


def _moe(x2d, logits, w_gu, w_down):
    n_tok, d = x2d.shape
    n_assign = n_tok * TOP_K
    top_logit, top_e = lax.top_k(logits, TOP_K)
    weights = jax.nn.softmax(top_logit, axis=-1)
    flat_e = top_e.reshape(-1).astype(jnp.int32)
    onehot = (flat_e[:, None] == jnp.arange(N_EXPERTS, dtype=jnp.int32)[None, :]).astype(jnp.int32)
    running = jnp.cumsum(onehot, axis=0)
    counts = running[-1]
    padded = (counts + MOE_ROWS - 1) // MOE_ROWS * MOE_ROWS
    pad_end = jnp.cumsum(padded)
    pad_start = pad_end - padded
    grp_start = jnp.cumsum(counts) - counts
    pos = jnp.sum(onehot * (running - 1 + pad_start[None, :]), axis=1).reshape(n_tok, TOP_K)
    n_blocks = -(-n_assign // MOE_ROWS) + N_EXPERTS
    block_e = jnp.minimum(jnp.searchsorted(pad_end, jnp.arange(n_blocks) * MOE_ROWS, side='right'),
                          N_EXPERTS - 1).astype(jnp.int32)
    n_used = (pad_end[-1] // MOE_ROWS).astype(jnp.int32).reshape(1)
    order = jnp.argsort(flat_e)
    tok_sorted = (order // TOP_K).astype(jnp.int32)
    row_in_grp = (jnp.arange(n_blocks, dtype=jnp.int32) * MOE_ROWS - pad_start[block_e])[:, None] \
        + jnp.arange(MOE_ROWS, dtype=jnp.int32)[None, :]
    src = jnp.clip(grp_start[block_e][:, None] + row_in_grp, 0, n_assign - 1)
    row_tok = jnp.where(row_in_grp < counts[block_e][:, None], tok_sorted[src], 0).reshape(-1)
    half = n_blocks // 2 * MOE_ROWS
    ys = _moe_experts([x2d[row_tok[:half]], x2d[row_tok[half:]]], block_e, n_used, w_gu, w_down)
    bounds = [0, n_tok // 2, n_tok]
    return [[ys[pos[a:b, s]] for s in range(TOP_K)] for a, b in zip(bounds[:-1], bounds[1:])], weights


def kernel(x, p, a_mu, a_w_rkv, a_w0, a_w1, a_w2, a_a0, a_a1, a_a2, a_g1, a_g2, a_k_k, a_k_a, a_r_k,
           a_gn_g, a_gn_b, a_w_o, b_w_kv, b_cmp_pos, b_phi_w1, b_phi_b1, b_phi_w2, b_w_in, b_w_o,
           f_w_gu, f_w_down, m_w_router, m_b_router, m_w_gu, m_w_down, ln_g, ln_b, ple_w, ple_gate_w,
           ple_gate_b):
    batch, seq, d = x.shape
    m = batch * seq
    x0 = x.reshape(m, d)
    p2d = p.reshape(DEPTH, m, PLE_DIM)
    bf = lambda w: w.astype(BF16)

    x1 = _rwkv_layer(x0, batch, seq, a_mu[0], bf(a_w_rkv[0]), a_w0[0], bf(a_w1[0]), bf(a_w2[0]), a_a0[0],
                     bf(a_a1[0]), bf(a_a2[0]), bf(a_g1[0]), bf(a_g2[0]), a_k_k[0], a_k_a[0], a_r_k[0],
                     a_gn_g[0], a_gn_b[0], bf(a_w_o[0]), ln_g[0, 0], ln_b[0, 0])
    x2 = _ffn_ple(x1, p2d, 0, bf(f_w_gu[0][:, :D_FF]), bf(f_w_gu[0][:, D_FF:]), bf(f_w_down[0]),
                  ln_g[0, 1], ln_b[0, 1], bf(ple_w[0]), bf(ple_gate_w[0]), ple_gate_b[0])

    slabs, ks, kw, vst, vwt = _nsa_kv(x2, b_w_kv, batch, seq)
    k_cmp, v_cmp = _nsa_compress(slabs, b_cmp_pos, b_phi_w1, b_phi_b1, b_phi_w2)
    q, qr, gates = _nsa_q(x2, b_w_in[0], batch, seq)
    attn = _nsa_attention(q, qr, gates, k_cmp, v_cmp, ks, vst, kw, vwt).reshape(m, d)
    x3, x3_bf, logits = _proj_ln_router(attn, x2, bf(b_w_o[0]), ln_g[1, 0], ln_b[1, 0], m_w_router[0],
                                        m_b_router[0])
    y_parts, route_w = _moe(x3_bf, logits[:, :N_EXPERTS], m_w_gu[0], m_w_down[0])
    out = _moe_combine_ln_ple(x3, y_parts, route_w, p2d, 1, ln_g[1, 1], ln_b[1, 1], bf(ple_w[1]),
                              bf(ple_gate_w[1]), ple_gate_b[1])
    return out.reshape(batch, seq, d)
```

```python
import functools
import math

import jax
import jax.numpy as jnp
from jax import lax
from jax.experimental import pallas as pl
from jax.experimental.pallas import tpu as pltpu

BF16 = jnp.bfloat16
F32 = jnp.float32

LANES = 128
VMEM_LIMIT_BYTES = 56 * 1024 * 1024

D_MODEL = 1024
PLE_DIM = 256
RWKV_HEAD = 64
GN_EPS = 64e-5
N_HEADS = 16
HEAD_DIM = 64
N_KV_GROUPS = 4
Q_PER_GROUP = 4
N_BRANCH = 3
CMP_LEN = 32
CMP_STRIDE = 16
CMP_HIDDEN = 256
SEL_BLOCK = 64
N_SEL = 8
WINDOW = 512
ROPE_THETA = 10000.0
D_FF = 2816
N_EXPERTS = 8
TOP_K = 2
D_FF_EXPERT = 1408
MOE_ROWS = 256
LN_EPS = 1e-5
DEPTH = 2
DEEPNORM_ALPHA = (2.0 * DEPTH) ** 0.25
NEG_INF = -1e30
FORCE_SCORE = 1e4

SCAN_CHUNK = 64


def _cparams(*sem):
    return pltpu.CompilerParams(dimension_semantics=sem, vmem_limit_bytes=VMEM_LIMIT_BYTES)


def _resident(shape):
    nd = len(shape)
    return pl.BlockSpec(shape, lambda *_: (0,) * nd, pipeline_mode=pl.Buffered(1))


def _dot(a, b):
    return jnp.dot(a, b, preferred_element_type=F32)


def _dot_nt(a, b):
    return lax.dot_general(a, b, (((1,), (1,)), ((), ())), preferred_element_type=F32)


def _split2(x):
    hi = x.astype(BF16)
    lo = (x - hi.astype(F32)).astype(BF16)
    return hi, lo


def _split3(x):
    hi = x.astype(BF16)
    r1 = x - hi.astype(F32)
    mid = r1.astype(BF16)
    lo = (r1 - mid.astype(F32)).astype(BF16)
    return hi, mid, lo


def _layer_norm(y, g, b):
    mu = jnp.mean(y, axis=-1, keepdims=True)
    yc = y - mu
    var = jnp.mean(yc * yc, axis=-1, keepdims=True)
    return yc * lax.rsqrt(var + LN_EPS) * g + b


def _sigmoid(z):
    return 1.0 / (1.0 + jnp.exp(-z))


def _rwkv_project(first, x_ref, xh_ref, mu_ref, wrkv_ref, w0_ref, w1_ref, w2_ref, a0_ref, a1_ref,
                  a2_ref, g1_ref, g2_ref, r_ref, k_ref, v_ref, dl_ref, a_ref, g_ref):
    x = x_ref[...]
    prev_row = jnp.where(first, 0.0, xh_ref[0, 7:8, :])
    row = lax.broadcasted_iota(jnp.int32, x.shape, 0)
    x_shift = jnp.where(row == 0, prev_row, pltpu.roll(x, shift=1, axis=0))
    xx = x_shift - x

    def mix(i):
        return (x + xx * mu_ref[i:i + 1, :]).astype(BF16)

    r_ref[...] = _dot(mix(0), wrkv_ref[0])
    k_ref[...] = _dot(mix(1), wrkv_ref[1])
    v_ref[...] = _dot(mix(2), wrkv_ref[2])
    z = w0_ref[...] + _dot(jnp.tanh(_dot(mix(3), w1_ref[...])).astype(BF16), w2_ref[...])
    dl_ref[...] = -math.exp(-0.5) * _sigmoid(z)
    a_ref[...] = _sigmoid(a0_ref[...] + _dot(_dot(mix(4), a1_ref[...]).astype(BF16), a2_ref[...]))
    g_ref[...] = _dot(_sigmoid(_dot(mix(5), g1_ref[...])).astype(BF16), g2_ref[...])


def _rwkv_recurrence(first, r_ref, k_ref, v_ref, dl_ref, a_ref, g_ref, kk_ref, ka_ref, rk_ref, gng_ref,
                     gnb_ref, o_ref, state_ref, kkn_ref, cum_ref, *, n_pairs, n_chunks):
    C = SCAN_CHUNK
    N = RWKV_HEAD
    W = 2 * N
    tb = n_chunks * C

    @pl.when(first)
    def _():
        state_ref[...] = jnp.zeros_like(state_ref)

    tt = lax.broadcasted_iota(jnp.int32, (C, W), 0)
    ln = lax.broadcasted_iota(jnp.int32, (C, W), 1)
    ss = ln & (N - 1)
    strict_lower = ss < tt
    lower = ss <= tt
    eye = jnp.where(ss == tt, 1.0, 0.0)
    level_masks = []
    for lg in range(int(math.log2(C))):
        level_masks.append(((tt >> (lg + 1)) == (ss >> (lg + 1))) & ((tt >> lg) == (ss >> lg) + 1))
    rr = lax.broadcasted_iota(jnp.int32, (W, W), 0)
    cc = lax.broadcasted_iota(jnp.int32, (W, W), 1)
    same_head = (rr < N) == (cc < N)
    lane_head0 = ln < N
    WG = 2 * W
    lg_n = int(math.log2(N))
    ones_grp = jnp.where((lax.broadcasted_iota(jnp.int32, (WG, WG), 0) >> lg_n)
                         == (lax.broadcasted_iota(jnp.int32, (WG, WG), 1) >> lg_n), 1.0, 0.0).astype(BF16)
    col_groups = [slice(c, c + WG) for c in range(0, n_pairs * W, WG)]

    def bd(y):
        return jnp.concatenate([jnp.where(lane_head0, y, 0.0), jnp.where(lane_head0, 0.0, y)],
                               axis=0).astype(BF16)

    def pmm(xp, ybd):
        return _dot(xp.astype(BF16), ybd)

    def head_sum(xg):
        return _dot(xg.astype(BF16), ones_grp)

    tri_r = lax.broadcasted_iota(jnp.int32, (tb, tb), 0)
    tri_c = lax.broadcasted_iota(jnp.int32, (tb, tb), 1)
    lg_c = int(math.log2(C))
    tri_chunk = jnp.where((tri_c <= tri_r) & ((tri_c >> lg_c) == (tri_r >> lg_c)), 1.0, 0.0).astype(BF16)
    d_hi, d_mid, d_lo = _split3(dl_ref[...])
    cum_ref[...] = _dot(tri_chunk, d_hi) + _dot(tri_chunk, d_mid) + _dot(tri_chunk, d_lo)
    for cols in col_groups:
        kk_raw = k_ref[:, cols] * kk_ref[:, cols]
        kkn_ref[:, cols] = kk_raw / jnp.maximum(jnp.sqrt(head_sum(kk_raw * kk_raw)), 1e-12)

    def chunk_load(ci, pi):
        rows = pl.ds(pl.multiple_of(ci * C, C), C)
        cols = slice(pi * W, (pi + 1) * W)
        return (r_ref[rows, cols], k_ref[rows, cols], v_ref[rows, cols], dl_ref[rows, cols],
                a_ref[rows, cols], kkn_ref[rows, cols], cum_ref[rows, cols], state_ref[pi])

    def chunk_compute(pi, r, k, v, dl, a, kk, cum, m0):
        cols = slice(pi * W, (pi + 1) * W)
        k2 = k * (1.0 + (a - 1.0) * ka_ref[:, cols])
        b = a * kk
        p_incl = jnp.exp(cum)
        p_excl = jnp.exp(cum - dl)
        p_inv = jnp.exp(-cum)
        kq = kk * p_excl
        rq = r * p_incl
        bk = b * p_inv
        kkd = k2 * p_inv

        bkt = jnp.transpose(jnp.concatenate([b, k2], axis=0))
        cumt = jnp.transpose(jnp.concatenate([cum, cum], axis=0))
        last_col = cumt[:, C - 1:C]
        lhs_t = (bkt * jnp.exp(last_col - cumt)).astype(BF16)
        m_decayed = m0 * jnp.exp(last_col)

        lhs = jnp.concatenate([kq, rq], axis=0).astype(BF16)
        yield
        abg = _dot_nt(lhs, jnp.concatenate([bd(bk), bd(kkd)], axis=0))
        a_m = jnp.where(strict_lower, abg[:C, :W], 0.0)
        aq_m = jnp.where(lower, abg[C:, :W], 0.0)
        g_m = jnp.where(strict_lower, abg[:C, W:], 0.0)
        gq_m = jnp.where(lower, abg[C:, W:], 0.0)

        tinv = eye - jnp.where(level_masks[0], a_m, 0.0)
        yield
        gv = pmm(jnp.concatenate([g_m, gq_m], axis=0), bd(v))
        for lm in level_masks[1:]:
            w_ = pmm(jnp.where(lm, a_m, 0.0), bd(tinv))
            yield
            tinv = tinv - pmm(tinv, bd(w_))
            yield

        kq_y = pmm(tinv, jnp.concatenate([bd(kq), bd(gv[:C])], axis=1))
        kq1 = kq_y[:, :W]
        y = kq_y[:, W:]
        yield
        aq_ky = pmm(aq_m, jnp.concatenate([bd(kq1), bd(y)], axis=1))
        rq1 = rq - aq_ky[:, :W]
        o_loc = gv[C:] - aq_ky[:, W:]
        yield
        st = _dot(jnp.concatenate([rq1, kq1], axis=0).astype(BF16), m0.astype(BF16))
        o = st[:C] + o_loc
        z = -(st[C:] + y)
        yield
        upd = _dot(lhs_t, jnp.concatenate([z, v], axis=0).astype(BF16))
        return o, m_decayed + jnp.where(same_head, upd, 0.0)

    def interleave(gens):
        results = [None] * len(gens)
        live = list(range(len(gens)))
        while live:
            for i in list(live):
                try:
                    next(gens[i])
                except StopIteration as done:
                    results[i] = done.value
                    live.remove(i)
        return results

    def body(ci, carry):
        loaded = [chunk_load(ci, pi) for pi in range(n_pairs)]
        results = interleave([chunk_compute(pi, *loaded[pi]) for pi in range(n_pairs)])
        rows = pl.ds(pl.multiple_of(ci * C, C), C)
        for pi, (o, m_new) in enumerate(results):
            o_ref[rows, pi * W:(pi + 1) * W] = o
            state_ref[pi] = m_new
        return carry

    lax.fori_loop(0, n_chunks, body, 0)

    for cols in col_groups:
        o = o_ref[:, cols]
        r = r_ref[:, cols]
        v = v_ref[:, cols]
        k2 = k_ref[:, cols] * (1.0 + (a_ref[:, cols] - 1.0) * ka_ref[:, cols])
        oc = o - head_sum(o) * (1.0 / N)
        var_o = head_sum(oc * oc) * (1.0 / N)
        out = oc * lax.rsqrt(var_o + GN_EPS) * gng_ref[:, cols] + gnb_ref[:, cols]
        out = out + head_sum(r * k2 * rk_ref[:, cols]) * v
        o_ref[:, cols] = out * g_ref[:, cols]


def _rwkv_layer_kernel(x_ref, xh_ref, mu_ref, wrkv_ref, w0_ref, w1_ref, w2_ref, a0_ref, a1_ref, a2_ref,
                       g1_ref, g2_ref, kk_ref, ka_ref, rk_ref, gng_ref, gnb_ref, wo_ref, lng_ref, lnb_ref,
                       o_ref, state_ref, r_s, k_s, v_s, dl_s, a_s, g_s, kkn_s, cum_s, y_s, *, n_chunks):
    first = pl.program_id(1) == 0
    _rwkv_project(first, x_ref, xh_ref, mu_ref, wrkv_ref, w0_ref, w1_ref, w2_ref, a0_ref, a1_ref, a2_ref,
                  g1_ref, g2_ref, r_s, k_s, v_s, dl_s, a_s, g_s)
    _rwkv_recurrence(first, r_s, k_s, v_s, dl_s, a_s, g_s, kk_ref, ka_ref, rk_ref, gng_ref, gnb_ref, y_s,
                     state_ref, kkn_s, cum_s, n_pairs=x_ref.shape[1] // (2 * RWKV_HEAD), n_chunks=n_chunks)
    mix = _dot(y_s[...].astype(BF16), wo_ref[...])
    o_ref[...] = _layer_norm(DEEPNORM_ALPHA * x_ref[...] + mix, lng_ref[...], lnb_ref[...])


def _rwkv_layer(x2d, batch, seq, mu, w_rkv, w0, w1, w2, a0, a1, a2, g1, g2, k_k, k_a, r_k, gn_g, gn_b,
                w_o, ln_g, ln_b, tb=256):
    m, d = x2d.shape
    nt = seq // tb
    xh = x2d.reshape(m // 8, 8, d)
    vec = lambda z: z.reshape(1, d)
    blk = pl.BlockSpec((tb, d), lambda b, t: (b * nt + t, 0))
    weights = [mu, w_rkv, vec(w0), w1, w2, vec(a0), a1, a2, g1, g2, vec(k_k), vec(k_a), vec(r_k),
               vec(gn_g), vec(gn_b), w_o, vec(ln_g), vec(ln_b)]
    n_pairs = d // (2 * RWKV_HEAD)
    return pl.pallas_call(
        functools.partial(_rwkv_layer_kernel, n_chunks=tb // SCAN_CHUNK),
        grid=(batch, nt),
        in_specs=[blk, pl.BlockSpec((1, 8, d), lambda b, t: (jnp.maximum((b * nt + t) * (tb // 8) - 1, 0), 0, 0))]
        + [_resident(w.shape) for w in weights],
        out_specs=blk,
        out_shape=jax.ShapeDtypeStruct((m, d), F32),
        scratch_shapes=[pltpu.VMEM((n_pairs, 2 * RWKV_HEAD, 2 * RWKV_HEAD), F32)]
        + [pltpu.VMEM((tb, d), F32)] * 9,
        compiler_params=_cparams("parallel", "arbitrary"),
        name="rwkv_layer",
    )(x2d, xh, *weights)


def _proj_ln_router_kernel(y_ref, x_ref, w_ref, lng_ref, lnb_ref, wr_ref, br_ref, o_ref, ob_ref, lg_ref):
    mix = _dot(y_ref[...].astype(BF16), w_ref[...])
    xn = _layer_norm(DEEPNORM_ALPHA * x_ref[...] + mix, lng_ref[...], lnb_ref[...])
    o_ref[...] = xn
    xh, xl = _split2(xn)
    ob_ref[...] = xh
    lg = _dot(xh, wr_ref[0]) + _dot(xh, wr_ref[1]) + _dot(xl, wr_ref[0]) + br_ref[...]
    lane = lax.broadcasted_iota(jnp.int32, lg.shape, 1)
    lg = jnp.where(lane < N_EXPERTS, lg, -jnp.inf)
    picks = []
    for _ in range(TOP_K):
        best = jnp.max(lg, axis=-1, keepdims=True)
        idx = jnp.min(jnp.where(lg == best, lane, LANES), axis=-1, keepdims=True)
        picks.append((best, idx))
        lg = jnp.where(lane == idx, -jnp.inf, lg)
    (l0, e0), (l1, e1) = picks
    ex = jnp.exp(l1 - l0)
    out = jnp.where(lane == 0, e0.astype(F32), 0.0)
    out = jnp.where(lane == 1, e1.astype(F32), out)
    out = jnp.where(lane == 2, 1.0 / (1.0 + ex), out)
    lg_ref[...] = jnp.where(lane == 3, ex / (1.0 + ex), out)


def _proj_ln_router(y, x2d, w, ln_g, ln_b, w_router, b_router, tm=512):
    m, d = x2d.shape
    row = lambda i: (i, 0)
    wr = jnp.zeros((d, LANES), F32).at[:, :N_EXPERTS].set(w_router)
    wr3 = jnp.stack(_split2(wr))
    br =jnp.zeros((1, LANES), F32).at[0, :N_EXPERTS].set(b_router)
    return pl.pallas_call(
        _proj_ln_router_kernel,
        grid=(m // tm,),
        in_specs=[pl.BlockSpec((tm, y.shape[1]), row), pl.BlockSpec((tm, d), row), _resident(w.shape),
                  _resident((1, d)), _resident((1, d)), _resident(wr3.shape), _resident(br.shape)],
        out_specs=[pl.BlockSpec((tm, d), row), pl.BlockSpec((tm, d), row), pl.BlockSpec((tm, LANES), row)],
        out_shape=[jax.ShapeDtypeStruct((m, d), F32), jax.ShapeDtypeStruct((m, d), BF16),
                   jax.ShapeDtypeStruct((m, LANES), F32)],
        compiler_params=_cparams("parallel"),
        name="proj_ln_router",
    )(y, x2d, w, ln_g.reshape(1, d), ln_b.reshape(1, d), wr3, br)


def _ple(xn, p, plew_ref, gw_ref, gb_ref):
    gate = _sigmoid(_dot(xn.astype(BF16), gw_ref[...]) + gb_ref[...])
    return xn + _dot(p.astype(BF16), plew_ref[...]) * gate


def _ffn_ple_kernel(x_ref, p_ref, wg_ref, wu_ref, wd_ref, lng_ref, lnb_ref, plew_ref, gw_ref, gb_ref,
                    o_ref, *, ff_chunk):
    x = x_ref[...]
    xb = x.astype(BF16)
    acc = jnp.zeros_like(x)
    for c in range(wg_ref.shape[1] // ff_chunk):
        sl = slice(c * ff_chunk, (c + 1) * ff_chunk)
        gt = _dot(xb, wg_ref[:, sl])
        up = _dot(xb, wu_ref[:, sl])
        h = (gt * _sigmoid(gt) * up).astype(BF16)
        acc = acc + _dot(h, wd_ref[sl, :])
    xn = _layer_norm(DEEPNORM_ALPHA * x + acc, lng_ref[...], lnb_ref[...])
    o_ref[...] = _ple(xn, p_ref[0], plew_ref, gw_ref, gb_ref)


def _ple_block(p3d, layer, tm):
    return pl.BlockSpec((1, tm, p3d.shape[2]), lambda i: (layer, i, 0))


def _ffn_ple(x2d, p3d, layer, wg, wu, wd, ln_g, ln_b, ple_w, gate_w, gate_b, tm=512, ff_chunk=1408):
    m, d = x2d.shape
    row = lambda i: (i, 0)
    return pl.pallas_call(
        functools.partial(_ffn_ple_kernel, ff_chunk=ff_chunk),
        grid=(m // tm,),
        in_specs=[pl.BlockSpec((tm, d), row), _ple_block(p3d, layer, tm),
                  _resident(wg.shape), _resident(wu.shape), _resident(wd.shape),
                  _resident((1, d)), _resident((1, d)), _resident(ple_w.shape), _resident(gate_w.shape),
                  _resident((1, d))],
        out_specs=pl.BlockSpec((tm, d), row),
        out_shape=jax.ShapeDtypeStruct((m, d), F32),
        compiler_params=_cparams("parallel"),
        name="ffn_ple",
    )(x2d, p3d, wg, wu, wd, ln_g.reshape(1, d), ln_b.reshape(1, d), ple_w, gate_w, gate_b.reshape(1, d))


def _moe_combine_ln_ple_kernel(x_ref, y0_ref, y1_ref, rw_ref, p_ref, lng_ref, lnb_ref, plew_ref, gw_ref,
                               gb_ref, *rest):
    o_ref = rest[-1]
    rw = rw_ref[...]
    ffn = y0_ref[...] * rw[:, 0:1] + y1_ref[...] * rw[:, 1:2]
    xn = _layer_norm(DEEPNORM_ALPHA * x_ref[...] + ffn, lng_ref[...], lnb_ref[...])
    o_ref[...] = _ple(xn, p_ref[0], plew_ref, gw_ref, gb_ref)


def _moe_combine_ln_ple(x2d, y_parts, route_w, p3d, layer, ln_g, ln_b, ple_w, gate_w, gate_b, tm=512):
    m, d = x2d.shape
    tm = min(tm, min(y0.shape[0] for y0, _ in y_parts))
    out, first = None, 0
    for y0, y1 in y_parts:
        n_tiles = y0.shape[0] // tm
        glob = lambda i, first=first: (i + first, 0)
        loc = pl.BlockSpec((tm, d), lambda i: (i, 0))
        in_specs = [pl.BlockSpec((tm, d), glob), loc, loc, pl.BlockSpec((tm, TOP_K), glob),
                    pl.BlockSpec((1, tm, p3d.shape[2]), lambda i, first=first: (layer, i + first, 0)),
                    _resident((1, d)), _resident((1, d)), _resident(ple_w.shape), _resident(gate_w.shape),
                    _resident((1, d))]
        args = [x2d, y0, y1, route_w, p3d, ln_g.reshape(1, d), ln_b.reshape(1, d), ple_w, gate_w,
                gate_b.reshape(1, d)]
        aliases = {}
        if out is not None:
            in_specs.append(pl.BlockSpec(memory_space=pl.ANY))
            args.append(out)
            aliases = {len(args) - 1: 0}
        out = pl.pallas_call(
            _moe_combine_ln_ple_kernel,
            grid=(n_tiles,),
            in_specs=in_specs,
            out_specs=pl.BlockSpec((tm, d), glob),
            out_shape=jax.ShapeDtypeStruct((m, d), F32),
            input_output_aliases=aliases,
            compiler_params=_cparams("parallel"),
            name="moe_combine_ln_ple",
        )(*args)
        first += n_tiles
    return out


def _swap_half_cols(w):
    k, n = w.shape
    return w.reshape(k, n // HEAD_DIM, 2, HEAD_DIM // 2)[:, :, ::-1, :].reshape(k, n)


def _rope_tables(seq):
    half = HEAD_DIM // 2
    inv = ROPE_THETA ** (-jnp.arange(half, dtype=F32) / half)
    ang = jnp.arange(seq, dtype=F32)[:, None] * inv[None, :]
    cos, sin = jnp.cos(ang), jnp.sin(ang)
    return jnp.concatenate([cos, cos], -1), jnp.concatenate([-sin, sin], -1)


def _nsa_kv_kernel(x_ref, wk_ref, wvt_ref, cos_ref, sin_ref, slab_ref, ks_ref, kw_ref, vst_ref, vwt_ref,
                   cz_s):
    G, dh = N_KV_GROUPS, HEAD_DIM
    gw = G * dh
    xb = x_ref[...].astype(BF16)
    res = _dot(xb, wk_ref[...])
    res_t = _dot_nt(wvt_ref[...], xb)
    tm = res.shape[0]
    n_slab = tm // CMP_STRIDE
    for c in range(cz_s.shape[0]):
        cz_s[c] = res[:, c * LANES:(c + 1) * LANES]
        rows = [cz_s[c, pl.ds(l, n_slab, stride=CMP_STRIDE), :] for l in range(CMP_STRIDE)]
        for half in range(LANES // dh):
            jg = c * (LANES // dh) + half
            slab_ref[jg // G, 0, jg % G] = jnp.concatenate(
                [r[:, half * dh:(half + 1) * dh] for r in rows], axis=1).astype(BF16)
    cos = cos_ref[...]
    sin = sin_ref[...]
    ks = res[:, 2 * gw:3 * gw] * cos + res[:, 3 * gw:4 * gw] * sin
    kw = res[:, 4 * gw:5 * gw] * cos + res[:, 5 * gw:6 * gw] * sin
    t_glob = pl.program_id(1) * tm + lax.broadcasted_iota(jnp.int32, (tm, dh), 0)
    onehot = jnp.where(lax.broadcasted_iota(jnp.int32, (tm, dh), 1) == t_glob // SEL_BLOCK, 1.0, 0.0)
    for g in range(G):
        sl = slice(g * dh, (g + 1) * dh)
        ks_ref[0, g] = jnp.concatenate([ks[:, sl], onehot], axis=1).astype(BF16)
        kw_ref[0, g] = kw[:, sl].astype(BF16)
    vst_ref[0] = res_t[:gw].astype(BF16)
    vwt_ref[0] = res_t[gw:].astype(BF16)


def _nsa_kv(x2d, w_kv, batch, seq, tm=256):
    m, d = x2d.shape
    G, dh = N_KV_GROUPS, HEAD_DIM
    gw = G * dh
    parts = [w_kv[:, j * gw:(j + 1) * gw] for j in range(6)]
    w_k = jnp.concatenate([parts[0], parts[1], parts[2], _swap_half_cols(parts[2]),
                           parts[4], _swap_half_cols(parts[4])], axis=1).astype(BF16)
    w_vt = jnp.concatenate([parts[3], parts[5]], axis=1).T.astype(BF16)
    cos, sin = _rope_tables(seq)
    cos = jnp.tile(cos, (1, G))
    sin = jnp.tile(sin, (1, G))
    nt = seq // tm
    o64 = jax.ShapeDtypeStruct((batch, G, seq, dh), BF16)
    o128 = jax.ShapeDtypeStruct((batch, G, seq, 2 * dh), BF16)
    ot = jax.ShapeDtypeStruct((batch, gw, seq), BF16)
    b64 = pl.BlockSpec((1, G, tm, dh), lambda b, t: (b, 0, t, 0))
    b128 = pl.BlockSpec((1, G, tm, 2 * dh), lambda b, t: (b, 0, t, 0))
    bt = pl.BlockSpec((1, gw, tm), lambda b, t: (b, 0, t))
    slab_w = CMP_STRIDE * dh
    oslab = jax.ShapeDtypeStruct((2, batch, G, seq // CMP_STRIDE, slab_w), BF16)
    bslab = pl.BlockSpec((2, 1, G, tm // CMP_STRIDE, slab_w), lambda b, t: (0, b, 0, t, 0))
    return pl.pallas_call(
        _nsa_kv_kernel,
        grid=(batch, nt),
        in_specs=[pl.BlockSpec((tm, d), lambda b, t: (b * nt + t, 0)), _resident(w_k.shape),
                  _resident(w_vt.shape),
                  pl.BlockSpec((tm, gw), lambda b, t: (t, 0)), pl.BlockSpec((tm, gw), lambda b, t: (t, 0))],
        out_specs=[bslab, b128, b64, bt, bt],
        out_shape=[oslab, o128, o64, ot, ot],
        scratch_shapes=[pltpu.VMEM((2 * gw // LANES, tm, LANES), F32)],
        compiler_params=_cparams("parallel", "parallel"),
        name="nsa_kv",
    )(x2d, w_k, w_vt, cos, sin)


def _nsa_cmp_kernel(z_ref, pos_ref, w1_ref, b1_ref, w2_ref, o_ref, *, slabs_per_seq):
    half = w1_ref.shape[1] // 2
    z = z_ref[0]
    tm = z.shape[0]
    first = _dot(z, w1_ref[0, :half, :])
    second = _dot(z, w1_ref[0, half:, :])
    const = _dot(pos_ref[0], w1_ref[0])[0:1, :] + b1_ref[0]
    hid = first + pltpu.roll(second, shift=tm - 1, axis=0) + const
    act = 0.5 * hid * (1.0 + jnp.tanh(math.sqrt(2.0 / math.pi) * (hid + 0.044715 * hid * hid * hid)))
    out = _dot(act.astype(BF16), w2_ref[0])
    row = lax.broadcasted_iota(jnp.int32, out.shape, 0)
    o_ref[0] = jnp.where(row % slabs_per_seq == slabs_per_seq - 1, 0.0, out).astype(BF16)


def _nsa_compress(slabs, cmp_pos, phi_w1, phi_b1, phi_w2, tm=512):
    _, batch, G, n_slab, slab = slabs.shape
    dh = slab // CMP_STRIDE
    seq = n_slab * CMP_STRIDE
    rows = batch * G * n_slab
    tm = min(tm, rows)
    z = slabs.reshape(2, rows, slab)
    pos = jnp.broadcast_to(cmp_pos.reshape(2, 1, CMP_LEN * dh), (2, 8, CMP_LEN * dh)).astype(BF16)
    out = pl.pallas_call(
        functools.partial(_nsa_cmp_kernel, slabs_per_seq=seq // CMP_STRIDE),
        grid=(2, rows // tm),
        in_specs=[pl.BlockSpec((1, tm, slab), lambda j, i: (j, i, 0)),
                  pl.BlockSpec((1, 8, CMP_LEN * dh), lambda j, i: (j, 0, 0)),
                  pl.BlockSpec((1, CMP_LEN * dh, CMP_HIDDEN), lambda j, i: (j, 0, 0)),
                  pl.BlockSpec((1, 1, CMP_HIDDEN), lambda j, i: (j, 0, 0)),
                  pl.BlockSpec((1, CMP_HIDDEN, dh), lambda j, i: (j, 0, 0))],
        out_specs=pl.BlockSpec((1, tm, dh), lambda j, i: (j, i, 0)),
        out_shape=jax.ShapeDtypeStruct((2, rows, dh), BF16),
        compiler_params=_cparams("parallel", "parallel"),
        name="nsa_compress",
    )(z, pos, phi_w1.astype(BF16), phi_b1.reshape(2, 1, CMP_HIDDEN), phi_w2.astype(BF16))
    return out[0].reshape(batch, G, n_slab, dh), out[1].reshape(batch, G, n_slab, dh)


GATE_ROWS = 16


def _nsa_q_kernel(x_ref, wt_ref, cos_ref, sin_ref, q_ref, qr_ref, gate_ref):
    dh = HEAD_DIM
    hw = N_HEADS * dh
    scale = HEAD_DIM ** -0.5 * math.log2(math.e)
    res_t = _dot_nt(wt_ref[...], x_ref[...].astype(BF16))
    tm = res_t.shape[1]
    q = res_t[:hw].reshape(N_HEADS, dh, tm)
    q_sw = jnp.concatenate([q[:, dh // 2:], q[:, :dh // 2]], axis=1)
    qr = q * cos_ref[...][None] + q_sw * sin_ref[...][None]
    q_ref[0] = (q * scale).reshape(hw, tm).astype(BF16)
    qr_ref[0] = (qr * scale).reshape(hw, tm).astype(BF16)
    gate_ref[0] = _sigmoid(res_t[hw:])


def _nsa_q(x2d, w_in, batch, seq, tm=512):
    m, d = x2d.shape
    G, R, dh = N_KV_GROUPS, Q_PER_GROUP, HEAD_DIM
    hw = N_HEADS * dh
    w_gate = w_in[:, hw:].reshape(d, G, R * N_BRANCH)
    w_gate = jnp.pad(w_gate, ((0, 0), (0, 0), (0, GATE_ROWS - R * N_BRANCH))).reshape(d, G * GATE_ROWS)
    wt = jnp.concatenate([w_in[:, :hw], w_gate], axis=1).T.astype(BF16)
    cos, sin = _rope_tables(seq)
    nt = seq // tm
    oq = jax.ShapeDtypeStruct((batch, hw, seq), BF16)
    bq = pl.BlockSpec((1, hw, tm), lambda b, t: (b, 0, t))
    tab = pl.BlockSpec((dh, tm), lambda b, t: (0, t))
    return pl.pallas_call(
        _nsa_q_kernel,
        grid=(batch, nt),
        in_specs=[pl.BlockSpec((tm, d), lambda b, t: (b * nt + t, 0)), _resident(wt.shape), tab, tab],
        out_specs=[bq, bq, pl.BlockSpec((1, G * GATE_ROWS, tm), lambda b, t: (b, 0, t))],
        out_shape=[oq, oq, jax.ShapeDtypeStruct((batch, G * GATE_ROWS, seq), F32)],
        compiler_params=_cparams("parallel", "parallel"),
        name="nsa_q",
    )(x2d, wt, cos.T, sin.T)


def _nsa_attn_kernel(q_ref, qr_ref, gate_ref, kc_ref, vct_ref, ks_ref, vst_ref, kw_ref, vwt_ref, ovlt_ref,
                     o_ref, *, tq, tk, n_blk):
    R, dh = Q_PER_GROUP, HEAD_DIM
    cols = R * tq
    qi = pl.program_id(2)
    t0 = qi * tq
    n_blk_pad = ovlt_ref.shape[0]

    def heads_to_lanes(x):
        return jnp.concatenate([x[r * dh:(r + 1) * dh] for r in range(R)], axis=1)

    q = heads_to_lanes(q_ref[0])
    qr = heads_to_lanes(qr_ref[0])

    def col_pos(shape):
        return t0 + (lax.broadcasted_iota(jnp.int32, shape, 1) & (tq - 1))

    def key_tile(j):
        return pl.ds(pl.multiple_of(j * tk, tk), tk)

    ct = tq
    col_tiles = [slice(c, c + ct) for c in range(0, cols, ct)]
    ones_rows = 16
    ones_blk = jnp.ones((ones_rows, tk), BF16)

    def tile_scores(k_ref, q_op, tiles):
        return [[_dot(k_ref[0, 0, key_tile(j), :], q_op[:, cs]) for j, _ in tiles] for cs in col_tiles]

    def attend(carry, sc, vt_ref, tiles):
        m_i, acc = carry
        outs = []
        for cs, sc_c in zip(col_tiles, sc):
            masked = [s_j if mask_fn is None else jnp.where(mask_fn(k_off, q_off), s_j, NEG_INF)
                      for s_j, (_, mask_fn) in zip(sc_c, tiles)]
            m_new = m_i[:, cs]
            for s_j in masked:
                m_new = jnp.maximum(m_new, jnp.max(s_j, axis=0, keepdims=True))
            acc_new = jnp.exp2(m_i[:, cs] - m_new) * acc[:, cs]
            for s_j, (j, _) in zip(masked, tiles):
                v_ext = jnp.concatenate([vt_ref[0, :, key_tile(j)], ones_blk], axis=0)
                acc_new = acc_new + _dot(v_ext, jnp.exp2(s_j - m_new).astype(BF16))
            outs.append((m_new, acc_new))
        return tuple(jnp.concatenate([o[i] for o in outs], axis=1) for i in range(2))

    def normalised(acc):
        return acc[:dh] / acc[dh:dh + 1]

    init = (jnp.full((1, cols), NEG_INF, F32), jnp.zeros((dh + ones_rows, cols), F32))

    k_off = lax.broadcasted_iota(jnp.int32, (tk, ct), 0)
    q_off = lax.broadcasted_iota(jnp.int32, (tk, ct), 1)
    causal = lambda k, q_: k <= q_
    present = lambda cond: (lambda k, q_: (k >= 0) & cond)

    win_tiles = [(jnp.maximum(qi - 2, 0), lambda k, q_: (q_ < k) & (qi >= 2)),
                 (jnp.maximum(qi - 1, 0), present(qi >= 1)),
                 (qi, causal)]
    sc_win = tile_scores(kw_ref, qr, win_tiles)

    s = _dot(kc_ref[0, 0], q)
    cmp_end = lax.broadcasted_iota(jnp.int32, s.shape, 0) * CMP_STRIDE + (CMP_LEN - 1)
    valid = cmp_end <= col_pos(s.shape)
    sm = jnp.where(valid, s, NEG_INF)
    e = jnp.where(valid, jnp.exp2(sm - jnp.max(sm, axis=0, keepdims=True)), 0.0)
    l = jnp.sum(e, axis=0, keepdims=True)
    p_cmp = e / jnp.where(l > 0.0, l, 1.0)
    o_cmp = _dot(vct_ref[0, 0], p_cmp.astype(BF16))

    p_sum = p_cmp[:, 0:tq]
    for r in range(1, R):
        p_sum = p_sum + p_cmp[:, r * tq:(r + 1) * tq]
    hi, lo = _split2(p_sum)
    imp = _dot(ovlt_ref[...], hi) + _dot(ovlt_ref[...], lo)
    blk = lax.broadcasted_iota(jnp.int32, imp.shape, 0)
    t_q = t0 + lax.broadcasted_iota(jnp.int32, imp.shape, 1)
    cur = t_q // SEL_BLOCK
    forced = (blk == 0) | (blk == cur) | (blk == cur - 1)
    score = jnp.where(forced, FORCE_SCORE, jnp.where(blk * SEL_BLOCK <= t_q, imp, NEG_INF))
    score = jnp.where(blk < n_blk, score, -jnp.inf)
    selected = blk >= n_blk
    for _ in range(N_SEL):
        best = jnp.max(score, axis=0, keepdims=True)
        first = jnp.min(jnp.where(score == best, blk, 2 * LANES), axis=0, keepdims=True)
        pick = blk == first
        selected = selected | pick
        score = jnp.where(pick, -jnp.inf, score)
    bias = jnp.where(selected, 0.0, NEG_INF)
    qa = jnp.concatenate([qr, jnp.concatenate([bias] * R, axis=1).astype(BF16),
                          jnp.zeros((dh - n_blk_pad, cols), BF16)], axis=0)

    o_win = normalised(attend(init, sc_win, vwt_ref, win_tiles)[1])

    n_pairs = qi // 2

    def pair_body(i, carry):
        tiles = [(2 * i, None), (2 * i + 1, None)]
        return attend(carry, tile_scores(ks_ref, qa, tiles), vst_ref, tiles)

    carry = lax.fori_loop(0, n_pairs, pair_body, init)
    j_odd = 2 * n_pairs
    tail_tiles = [(j_odd, present(j_odd < qi)), (qi, causal)]
    o_slc = normalised(attend(carry, tile_scores(ks_ref, qa, tail_tiles), vst_ref, tail_tiles)[1])

    gates = gate_ref[0]
    outs = []
    for r in range(R):
        cs = slice(r * tq, (r + 1) * tq)
        outs.append(gates[3 * r:3 * r + 1] * o_cmp[:, cs] + gates[3 * r + 1:3 * r + 2] * o_slc[:, cs]
                    + gates[3 * r + 2:3 * r + 3] * o_win[:, cs])
    halves = [jnp.transpose(jnp.concatenate(outs[i:i + 2], axis=0)) for i in range(0, R, 2)]
    o_ref[0] = jnp.concatenate(halves, axis=1)


SEL_ROWS = 32


def _nsa_attention(q, qr, gates, k_cmp, v_cmp, ks, vst, kw, vwt, tq=256, tk=256):
    batch, _, seq = q.shape
    G, R, dh = N_KV_GROUPS, Q_PER_GROUP, HEAD_DIM
    n_slab = k_cmp.shape[2]
    n_blk = seq // SEL_BLOCK
    assert n_blk <= SEL_ROWS
    assert tq == tk and WINDOW == 2 * tk, "the window branch is written as exactly three key tiles"
    cmp_start = jnp.arange(n_slab) * CMP_STRIDE
    blk_start = jnp.arange(SEL_ROWS) * SEL_BLOCK
    overlap_t = ((cmp_start[None, :] < blk_start[:, None] + SEL_BLOCK)
                 & (cmp_start[None, :] + CMP_LEN - 1 >= blk_start[:, None])
                 & (jnp.arange(SEL_ROWS)[:, None] < n_blk)
                 & (jnp.arange(n_slab)[None, :] < (seq - CMP_LEN) // CMP_STRIDE + 1)).astype(BF16)
    v_cmp_t = jnp.swapaxes(v_cmp, 2, 3)
    qspec = pl.BlockSpec((1, R * dh, tq), lambda b, g, i: (b, g, i))
    full = lambda a: pl.BlockSpec((1, 1) + a.shape[2:], lambda b, g, i: (b, g, 0, 0))
    vspec = pl.BlockSpec((1, dh, seq), lambda b, g, i: (b, g, 0))
    return pl.pallas_call(
        functools.partial(_nsa_attn_kernel, tq=tq, tk=tk, n_blk=n_blk),
        grid=(batch, G, seq // tq),
        in_specs=[qspec, qspec, pl.BlockSpec((1, GATE_ROWS, tq), lambda b, g, i: (b, g, i)),
                  full(k_cmp), full(v_cmp_t), full(ks), vspec, full(kw), vspec,
                  _resident(overlap_t.shape)],
        out_specs=pl.BlockSpec((1, tq, R * dh), lambda b, g, i: (b, i, g)),
        out_shape=jax.ShapeDtypeStruct((batch, seq, G * R * dh), F32),
        compiler_params=_cparams("parallel", "parallel", "parallel"),
        name="nsa_attn",
    )(q, qr, gates, k_cmp, v_cmp_t, ks, vst, kw, vwt, overlap_t)


def _moe_kernel(be_ref, nb_ref, x_ref, wgu_ref, wd_ref, *rest, first_block):
    o_ref, wgu_bf, wd_bf = rest[-3:]
    i = pl.program_id(0)
    blk = i + first_block
    ff = wd_ref.shape[1]
    used = blk < nb_ref[0]

    @pl.when(used & ((i == 0) | (be_ref[blk] != be_ref[jnp.maximum(blk - 1, 0)])))
    def _():
        wgu_bf[...] = wgu_ref[0].astype(BF16)
        wd_bf[...] = wd_ref[0].astype(BF16)

    @pl.when(used)
    def _():
        xb = x_ref[...]
        gt = _dot(xb, wgu_bf[:, :ff])
        up = _dot(xb, wgu_bf[:, ff:])
        h = (gt * _sigmoid(gt) * up).astype(BF16)
        o_ref[...] = _dot(h, wd_bf[...])

    @pl.when(jnp.logical_not(used))
    def _():
        o_ref[...] = jnp.zeros_like(o_ref)


def _moe_experts(xs_parts, block_e, n_used, w_gu, w_down):
    d = xs_parts[0].shape[1]
    n_rows = sum(xs.shape[0] for xs in xs_parts)
    ff = w_down.shape[1]
    ys, first = None, 0
    for xs in xs_parts:
        n_blocks = xs.shape[0] // MOE_ROWS
        in_specs = [pl.BlockSpec((MOE_ROWS, d), lambda i, be, nb: (i, 0)),
                    pl.BlockSpec((1, d, 2 * ff), lambda i, be, nb, first=first: (be[i + first], 0, 0)),
                    pl.BlockSpec((1, ff, d), lambda i, be, nb, first=first: (be[i + first], 0, 0))]
        args = [block_e, n_used, xs, w_gu, w_down]
        aliases = {}
        if ys is not None:
            in_specs.append(pl.BlockSpec(memory_space=pl.ANY))
            args.append(ys)
            aliases = {len(args) - 1: 0}
        ys = pl.pallas_call(
            functools.partial(_moe_kernel, first_block=first),
            grid_spec=pltpu.PrefetchScalarGridSpec(
                num_scalar_prefetch=2,
                grid=(n_blocks,),
                in_specs=in_specs,
                out_specs=pl.BlockSpec((MOE_ROWS, d), lambda i, be, nb, first=first: (i + first, 0)),
                scratch_shapes=[pltpu.VMEM((d, 2 * ff), BF16), pltpu.VMEM((ff, d), BF16)],
            ),
            out_shape=jax.ShapeDtypeStruct((n_rows, d), F32),
            input_output_aliases=aliases,
            compiler_params=_cparams("arbitrary"),
            name="moe_experts",
        )(*args)
        first += n_blocks
    return ys


def _moe(x2d, route, w_gu, w_down):
    n_tok, d = x2d.shape
    n_assign = n_tok * TOP_K
    weights = route[:, TOP_K:2 * TOP_K]
    flat_e = route[:, :TOP_K].astype(jnp.int32).reshape(-1)
    onehot = (flat_e[:, None] == jnp.arange(N_EXPERTS, dtype=jnp.int32)[None, :]).astype(jnp.int32)
    running = jnp.cumsum(onehot, axis=0)
    counts = running[-1]
    padded = (counts + MOE_ROWS - 1) // MOE_ROWS * MOE_ROWS
    pad_end = jnp.cumsum(padded)
    pad_start = pad_end - padded
    grp_start = jnp.cumsum(counts) - counts
    pos = jnp.sum(onehot * (running - 1 + pad_start[None, :]), axis=1).reshape(n_tok, TOP_K)
    n_blocks = -(-n_assign // MOE_ROWS) + N_EXPERTS
    block_e = jnp.minimum(jnp.searchsorted(pad_end, jnp.arange(n_blocks) * MOE_ROWS, side='right'),
                          N_EXPERTS - 1).astype(jnp.int32)
    n_used = (pad_end[-1] // MOE_ROWS).astype(jnp.int32).reshape(1)
    order = jnp.argsort(flat_e)
    tok_sorted = (order // TOP_K).astype(jnp.int32)
    row_in_grp = (jnp.arange(n_blocks, dtype=jnp.int32) * MOE_ROWS - pad_start[block_e])[:, None] \
        + jnp.arange(MOE_ROWS, dtype=jnp.int32)[None, :]
    src = jnp.clip(grp_start[block_e][:, None] + row_in_grp, 0, n_assign - 1)
    row_tok = jnp.where(row_in_grp < counts[block_e][:, None], tok_sorted[src], 0).reshape(-1)
    half = n_blocks // 2 * MOE_ROWS
    ys = _moe_experts([x2d[row_tok[:half]], x2d[row_tok[half:]]], block_e, n_used, w_gu, w_down)
    bounds = [0, n_tok // 2, n_tok]
    return [[ys[pos[a:b, s]] for s in range(TOP_K)] for a, b in zip(bounds[:-1], bounds[1:])], weights


def kernel(x, p, a_mu, a_w_rkv, a_w0, a_w1, a_w2, a_a0, a_a1, a_a2, a_g1, a_g2, a_k_k, a_k_a, a_r_k,
           a_gn_g, a_gn_b, a_w_o, b_w_kv, b_cmp_pos, b_phi_w1, b_phi_b1, b_phi_w2, b_w_in, b_w_o,
           f_w_gu, f_w_down, m_w_router, m_b_router, m_w_gu, m_w_down, ln_g, ln_b, ple_w, ple_gate_w,
           ple_gate_b):
    batch, seq, d = x.shape
    m = batch * seq
    x0 = x.reshape(m, d)
    p2d = p.reshape(DEPTH, m, PLE_DIM)
    bf = lambda w: w.astype(BF16)

    x1 = _rwkv_layer(x0, batch, seq, a_mu[0], bf(a_w_rkv[0]), a_w0[0], bf(a_w1[0]), bf(a_w2[0]), a_a0[0],
                     bf(a_a1[0]), bf(a_a2[0]), bf(a_g1[0]), bf(a_g2[0]), a_k_k[0], a_k_a[0], a_r_k[0],
                     a_gn_g[0], a_gn_b[0], bf(a_w_o[0]), ln_g[0, 0], ln_b[0, 0])
    x2 = _ffn_ple(x1, p2d, 0, bf(f_w_gu[0][:, :D_FF]), bf(f_w_gu[0][:, D_FF:]), bf(f_w_down[0]),
                  ln_g[0, 1], ln_b[0, 1], bf(ple_w[0]), bf(ple_gate_w[0]), ple_gate_b[0])

    slabs, ks, kw, vst, vwt = _nsa_kv(x2, b_w_kv, batch, seq)
    k_cmp, v_cmp = _nsa_compress(slabs, b_cmp_pos, b_phi_w1, b_phi_b1, b_phi_w2)
    q, qr, gates = _nsa_q(x2, b_w_in[0], batch, seq)
    attn = _nsa_attention(q, qr, gates, k_cmp, v_cmp, ks, vst, kw, vwt).reshape(m, d)
    x3, x3_bf, route = _proj_ln_router(attn, x2, bf(b_w_o[0]), ln_g[1, 0], ln_b[1, 0], m_w_router[0],
                                       m_b_router[0])
    y_parts, route_w = _moe(x3_bf, route, m_w_gu[0], m_w_down[0])
    out = _moe_combine_ln_ple(x3, y_parts, route_w, p2d, 1, ln_g[1, 1], ln_b[1, 1], bf(ple_w[1]),
                              bf(ple_gate_w[1]), ple_gate_b[1])
    return out.reshape(batch, seq, d)
```

```python
import functools
import math

import jax
import jax.numpy as jnp
from jax import lax
from jax.experimental import pallas as pl
from jax.experimental.pallas import tpu as pltpu

BF16 = jnp.bfloat16
F32 = jnp.float32

LANES = 128
VMEM_LIMIT_BYTES = 56 * 1024 * 1024

D_MODEL = 1024
PLE_DIM = 256
RWKV_HEAD = 64
GN_EPS = 64e-5
N_HEADS = 16
HEAD_DIM = 64
N_KV_GROUPS = 4
Q_PER_GROUP = 4
N_BRANCH = 3
CMP_LEN = 32
CMP_STRIDE = 16
CMP_HIDDEN = 256
SEL_BLOCK = 64
N_SEL = 8
WINDOW = 512
ROPE_THETA = 10000.0
D_FF = 2816
N_EXPERTS = 8
TOP_K = 2
D_FF_EXPERT = 1408
MOE_ROWS = 256
LN_EPS = 1e-5
DEPTH = 2
DEEPNORM_ALPHA = (2.0 * DEPTH) ** 0.25
NEG_INF = -1e30
FORCE_SCORE = 1e4

SCAN_CHUNK = 64


def _cparams(*sem):
    return pltpu.CompilerParams(dimension_semantics=sem, vmem_limit_bytes=VMEM_LIMIT_BYTES)


def _resident(shape):
    nd = len(shape)
    return pl.BlockSpec(shape, lambda *_: (0,) * nd, pipeline_mode=pl.Buffered(1))


def _dot(a, b):
    return jnp.dot(a, b, preferred_element_type=F32)


def _dot_nt(a, b):
    return lax.dot_general(a, b, (((1,), (1,)), ((), ())), preferred_element_type=F32)


def _split2(x):
    hi = x.astype(BF16)
    lo = (x - hi.astype(F32)).astype(BF16)
    return hi, lo


def _layer_norm(y, g, b):
    mu = jnp.mean(y, axis=-1, keepdims=True)
    yc = y - mu
    var = jnp.mean(yc * yc, axis=-1, keepdims=True)
    return yc * lax.rsqrt(var + LN_EPS) * g + b


def _sigmoid(z):
    return 1.0 / (1.0 + jnp.exp(-z))


def _rwkv_project(first, x_ref, xh_ref, mu_ref, wrkv_ref, w0_ref, w1_ref, w2_ref, a0_ref, a1_ref,
                  a2_ref, g1_ref, g2_ref, r_ref, k_ref, v_ref, dl_ref, a_ref, g_ref):
    x = x_ref[...]
    prev_row = jnp.where(first, 0.0, xh_ref[0, 7:8, :])
    row = lax.broadcasted_iota(jnp.int32, x.shape, 0)
    x_shift = jnp.where(row == 0, prev_row, pltpu.roll(x, shift=1, axis=0))
    xx = x_shift - x

    def mix(i):
        return (x + xx * mu_ref[i:i + 1, :]).astype(BF16)

    r_ref[...] = _dot(mix(0), wrkv_ref[0])
    k_ref[...] = _dot(mix(1), wrkv_ref[1])
    v_ref[...] = _dot(mix(2), wrkv_ref[2])
    z = w0_ref[...] + _dot(jnp.tanh(_dot(mix(3), w1_ref[...])).astype(BF16), w2_ref[...])
    dl_ref[...] = -math.exp(-0.5) * _sigmoid(z)
    a_ref[...] = _sigmoid(a0_ref[...] + _dot(_dot(mix(4), a1_ref[...]).astype(BF16), a2_ref[...]))
    g_ref[...] = _dot(_sigmoid(_dot(mix(5), g1_ref[...])).astype(BF16), g2_ref[...])


def _rwkv_recurrence(first, r_ref, k_ref, v_ref, dl_ref, a_ref, g_ref, kk_ref, ka_ref, rk_ref, gng_ref,
                     gnb_ref, o_ref, state_ref, kkn_ref, cum_ref, emit, *, n_pairs, n_chunks):
    C = SCAN_CHUNK
    N = RWKV_HEAD
    W = 2 * N
    tb = n_chunks * C

    @pl.when(first)
    def _():
        state_ref[...] = jnp.zeros_like(state_ref)

    tt = lax.broadcasted_iota(jnp.int32, (C, W), 0)
    ln = lax.broadcasted_iota(jnp.int32, (C, W), 1)
    ss = ln & (N - 1)
    strict_lower = ss < tt
    lower = ss <= tt
    eye = jnp.where(ss == tt, 1.0, 0.0)
    level_masks = []
    for lg in range(int(math.log2(C))):
        level_masks.append(((tt >> (lg + 1)) == (ss >> (lg + 1))) & ((tt >> lg) == (ss >> lg) + 1))
    rr = lax.broadcasted_iota(jnp.int32, (W, W), 0)
    cc = lax.broadcasted_iota(jnp.int32, (W, W), 1)
    same_head = (rr < N) == (cc < N)
    lane_head0 = ln < N
    WG = 2 * W
    lg_n = int(math.log2(N))
    ones_grp = jnp.where((lax.broadcasted_iota(jnp.int32, (WG, WG), 0) >> lg_n)
                         == (lax.broadcasted_iota(jnp.int32, (WG, WG), 1) >> lg_n), 1.0, 0.0).astype(BF16)
    col_groups = [slice(c, c + WG) for c in range(0, n_pairs * W, WG)]

    def bd(y):
        return jnp.concatenate([jnp.where(lane_head0, y, 0.0), jnp.where(lane_head0, 0.0, y)],
                               axis=0).astype(BF16)

    def pmm(xp, ybd):
        return _dot(xp.astype(BF16), ybd)

    def head_sum(xg):
        return _dot(xg.astype(BF16), ones_grp)

    tri_r = lax.broadcasted_iota(jnp.int32, (tb, tb), 0)
    tri_c = lax.broadcasted_iota(jnp.int32, (tb, tb), 1)
    lg_c = int(math.log2(C))
    tri_chunk = jnp.where((tri_c <= tri_r) & ((tri_c >> lg_c) == (tri_r >> lg_c)), 1.0, 0.0).astype(BF16)
    d_hi, d_lo = _split2(dl_ref[...])
    cum_ref[...] = _dot(tri_chunk, d_hi) + _dot(tri_chunk, d_lo)
    for cols in col_groups:
        kk_raw = k_ref[:, cols] * kk_ref[:, cols]
        kkn_ref[:, cols] = kk_raw / jnp.maximum(jnp.sqrt(head_sum(kk_raw * kk_raw)), 1e-12)

    def chunk_load(ci, pi):
        rows = pl.ds(pl.multiple_of(ci * C, C), C)
        cols = slice(pi * W, (pi + 1) * W)
        return (r_ref[rows, cols], k_ref[rows, cols], v_ref[rows, cols], dl_ref[rows, cols],
                a_ref[rows, cols], kkn_ref[rows, cols], cum_ref[rows, cols], state_ref[pi])

    def chunk_compute(pi, r, k, v, dl, a, kk, cum, m0):
        cols = slice(pi * W, (pi + 1) * W)
        k2 = k * (1.0 + (a - 1.0) * ka_ref[:, cols])
        b = a * kk
        p_incl = jnp.exp(cum)
        p_excl = jnp.exp(cum - dl)
        p_inv = jnp.exp(-cum)
        kq = kk * p_excl
        rq = r * p_incl
        bk = b * p_inv
        kkd = k2 * p_inv

        bkt = jnp.transpose(jnp.concatenate([b, k2], axis=0))
        cumt = jnp.transpose(jnp.concatenate([cum, cum], axis=0))
        last_col = cumt[:, C - 1:C]
        lhs_t = (bkt * jnp.exp(last_col - cumt)).astype(BF16)
        m_decayed = m0 * jnp.exp(last_col)

        lhs = jnp.concatenate([kq, rq], axis=0).astype(BF16)
        yield
        abg = _dot_nt(lhs, jnp.concatenate([bd(bk), bd(kkd)], axis=0))
        a_m = jnp.where(strict_lower, abg[:C, :W], 0.0)
        aq_m = jnp.where(lower, abg[C:, :W], 0.0)
        g_m = jnp.where(strict_lower, abg[:C, W:], 0.0)
        gq_m = jnp.where(lower, abg[C:, W:], 0.0)

        tinv = eye - jnp.where(level_masks[0], a_m, 0.0)
        yield
        gv = pmm(jnp.concatenate([g_m, gq_m], axis=0), bd(v))
        for lm in level_masks[1:]:
            w_ = pmm(jnp.where(lm, a_m, 0.0), bd(tinv))
            yield
            tinv = tinv - pmm(tinv, bd(w_))
            yield

        kq_y = pmm(tinv, jnp.concatenate([bd(kq), bd(gv[:C])], axis=1))
        kq1 = kq_y[:, :W]
        y = kq_y[:, W:]
        yield
        aq_ky = pmm(aq_m, jnp.concatenate([bd(kq1), bd(y)], axis=1))
        rq1 = rq - aq_ky[:, :W]
        o_loc = gv[C:] - aq_ky[:, W:]
        yield
        st = _dot(jnp.concatenate([rq1, kq1], axis=0).astype(BF16), m0.astype(BF16))
        o = st[:C] + o_loc
        z = -(st[C:] + y)
        yield
        upd = _dot(lhs_t, jnp.concatenate([z, v], axis=0).astype(BF16))
        return o, m_decayed + jnp.where(same_head, upd, 0.0)

    def interleave(gens):
        results = [None] * len(gens)
        live = list(range(len(gens)))
        while live:
            for i in list(live):
                try:
                    next(gens[i])
                except StopIteration as done:
                    results[i] = done.value
                    live.remove(i)
        return results

    def body(ci, carry):
        loaded = [chunk_load(ci, pi) for pi in range(n_pairs)]
        results = interleave([chunk_compute(pi, *loaded[pi]) for pi in range(n_pairs)])
        rows = pl.ds(pl.multiple_of(ci * C, C), C)
        for pi, (o, m_new) in enumerate(results):
            o_ref[rows, pi * W:(pi + 1) * W] = o
            state_ref[pi] = m_new
        return carry

    lax.fori_loop(0, n_chunks, body, 0)

    for cols in col_groups:
        o = o_ref[:, cols]
        r = r_ref[:, cols]
        v = v_ref[:, cols]
        k2 = k_ref[:, cols] * (1.0 + (a_ref[:, cols] - 1.0) * ka_ref[:, cols])
        oc = o - head_sum(o) * (1.0 / N)
        var_o = head_sum(oc * oc) * (1.0 / N)
        out = oc * lax.rsqrt(var_o + GN_EPS) * gng_ref[:, cols] + gnb_ref[:, cols]
        out = out + head_sum(r * k2 * rk_ref[:, cols]) * v
        emit(cols, out * g_ref[:, cols])


def _rwkv_layer_kernel(x_ref, xh_ref, mu_ref, wrkv_ref, w0_ref, w1_ref, w2_ref, a0_ref, a1_ref, a2_ref,
                       g1_ref, g2_ref, kk_ref, ka_ref, rk_ref, gng_ref, gnb_ref, wo_ref, lng_ref, lnb_ref,
                       o_ref, state_ref, r_s, k_s, v_s, dl_s, a_s, g_s, kkn_s, cum_s, y_s, *, n_chunks):
    first = pl.program_id(1) == 0
    _rwkv_project(first, x_ref, xh_ref, mu_ref, wrkv_ref, w0_ref, w1_ref, w2_ref, a0_ref, a1_ref, a2_ref,
                  g1_ref, g2_ref, r_s, k_s, v_s, dl_s, a_s, g_s)
    partial = []
    _rwkv_recurrence(first, r_s, k_s, v_s, dl_s, a_s, g_s, kk_ref, ka_ref, rk_ref, gng_ref, gnb_ref, y_s,
                     state_ref, kkn_s, cum_s, lambda cols, y: partial.append(_dot(y.astype(BF16), wo_ref[cols, :])),
                     n_pairs=x_ref.shape[1] // (2 * RWKV_HEAD), n_chunks=n_chunks)
    mix = functools.reduce(lambda a, b: a + b, partial)
    o_ref[...] = _layer_norm(DEEPNORM_ALPHA * x_ref[...] + mix, lng_ref[...], lnb_ref[...])


def _rwkv_layer(x2d, batch, seq, mu, w_rkv, w0, w1, w2, a0, a1, a2, g1, g2, k_k, k_a, r_k, gn_g, gn_b,
                w_o, ln_g, ln_b, tb=256):
    m, d = x2d.shape
    nt = seq // tb
    xh = x2d.reshape(m // 8, 8, d)
    vec = lambda z: z.reshape(1, d)
    blk = pl.BlockSpec((tb, d), lambda b, t: (b * nt + t, 0))
    weights = [mu, w_rkv, vec(w0), w1, w2, vec(a0), a1, a2, g1, g2, vec(k_k), vec(k_a), vec(r_k),
               vec(gn_g), vec(gn_b), w_o, vec(ln_g), vec(ln_b)]
    n_pairs = d // (2 * RWKV_HEAD)
    return pl.pallas_call(
        functools.partial(_rwkv_layer_kernel, n_chunks=tb // SCAN_CHUNK),
        grid=(batch, nt),
        in_specs=[blk, pl.BlockSpec((1, 8, d), lambda b, t: (jnp.maximum((b * nt + t) * (tb // 8) - 1, 0), 0, 0))]
        + [_resident(w.shape) for w in weights],
        out_specs=blk,
        out_shape=jax.ShapeDtypeStruct((m, d), F32),
        scratch_shapes=[pltpu.VMEM((n_pairs, 2 * RWKV_HEAD, 2 * RWKV_HEAD), F32)]
        + [pltpu.VMEM((tb, d), F32)] * 9,
        compiler_params=_cparams("parallel", "arbitrary"),
        name="rwkv_layer",
    )(x2d, xh, *weights)


def _proj_ln_router_kernel(y_ref, x_ref, w_ref, lng_ref, lnb_ref, wr_ref, br_ref, o_ref, ob_ref, lg_ref):
    mix = _dot(y_ref[...].astype(BF16), w_ref[...])
    xn = _layer_norm(DEEPNORM_ALPHA * x_ref[...] + mix, lng_ref[...], lnb_ref[...])
    o_ref[...] = xn
    xh, xl = _split2(xn)
    ob_ref[...] = xh
    lg = _dot(xh, wr_ref[0]) + _dot(xh, wr_ref[1]) + _dot(xl, wr_ref[0]) + br_ref[...]
    lane = lax.broadcasted_iota(jnp.int32, lg.shape, 1)
    lg = jnp.where(lane < N_EXPERTS, lg, -jnp.inf)
    picks = []
    for _ in range(TOP_K):
        best = jnp.max(lg, axis=-1, keepdims=True)
        idx = jnp.min(jnp.where(lg == best, lane, LANES), axis=-1, keepdims=True)
        picks.append((best, idx))
        lg = jnp.where(lane == idx, -jnp.inf, lg)
    (l0, e0), (l1, e1) = picks
    ex = jnp.exp(l1 - l0)
    out = jnp.where(lane == 0, e0.astype(F32), 0.0)
    out = jnp.where(lane == 1, e1.astype(F32), out)
    out = jnp.where(lane == 2, 1.0 / (1.0 + ex), out)
    lg_ref[...] = jnp.where(lane == 3, ex / (1.0 + ex), out)


def _proj_ln_router(y, x2d, w, ln_g, ln_b, w_router, b_router, tm=512):
    m, d = x2d.shape
    row = lambda i: (i, 0)
    wr = jnp.zeros((d, LANES), F32).at[:, :N_EXPERTS].set(w_router)
    wr3 = jnp.stack(_split2(wr))
    br =jnp.zeros((1, LANES), F32).at[0, :N_EXPERTS].set(b_router)
    return pl.pallas_call(
        _proj_ln_router_kernel,
        grid=(m // tm,),
        in_specs=[pl.BlockSpec((tm, y.shape[1]), row), pl.BlockSpec((tm, d), row), _resident(w.shape),
                  _resident((1, d)), _resident((1, d)), _resident(wr3.shape), _resident(br.shape)],
        out_specs=[pl.BlockSpec((tm, d), row), pl.BlockSpec((tm, d), row), pl.BlockSpec((tm, LANES), row)],
        out_shape=[jax.ShapeDtypeStruct((m, d), F32), jax.ShapeDtypeStruct((m, d), BF16),
                   jax.ShapeDtypeStruct((m, LANES), F32)],
        compiler_params=_cparams("parallel"),
        name="proj_ln_router",
    )(y, x2d, w, ln_g.reshape(1, d), ln_b.reshape(1, d), wr3, br)


def _ple(xn, p, plew_ref, gw_ref, gb_ref):
    gate = _sigmoid(_dot(xn.astype(BF16), gw_ref[...]) + gb_ref[...])
    return xn + _dot(p.astype(BF16), plew_ref[...]) * gate


def _ffn_ple_kernel(x_ref, p_ref, wg_ref, wu_ref, wd_ref, lng_ref, lnb_ref, plew_ref, gw_ref, gb_ref,
                    o_ref, *, ff_chunk):
    x = x_ref[...]
    xb = x.astype(BF16)
    acc = jnp.zeros_like(x)
    for c in range(wg_ref.shape[1] // ff_chunk):
        sl = slice(c * ff_chunk, (c + 1) * ff_chunk)
        gt = _dot(xb, wg_ref[:, sl])
        up = _dot(xb, wu_ref[:, sl])
        h = (gt * _sigmoid(gt) * up).astype(BF16)
        acc = acc + _dot(h, wd_ref[sl, :])
    xn = _layer_norm(DEEPNORM_ALPHA * x + acc, lng_ref[...], lnb_ref[...])
    o_ref[...] = _ple(xn, p_ref[0], plew_ref, gw_ref, gb_ref)


def _ple_block(p3d, layer, tm):
    return pl.BlockSpec((1, tm, p3d.shape[2]), lambda i: (layer, i, 0))


def _ffn_ple(x2d, p3d, layer, wg, wu, wd, ln_g, ln_b, ple_w, gate_w, gate_b, tm=512, ff_chunk=1408):
    m, d = x2d.shape
    row = lambda i: (i, 0)
    return pl.pallas_call(
        functools.partial(_ffn_ple_kernel, ff_chunk=ff_chunk),
        grid=(m // tm,),
        in_specs=[pl.BlockSpec((tm, d), row), _ple_block(p3d, layer, tm),
                  _resident(wg.shape), _resident(wu.shape), _resident(wd.shape),
                  _resident((1, d)), _resident((1, d)), _resident(ple_w.shape), _resident(gate_w.shape),
                  _resident((1, d))],
        out_specs=pl.BlockSpec((tm, d), row),
        out_shape=jax.ShapeDtypeStruct((m, d), F32),
        compiler_params=_cparams("parallel"),
        name="ffn_ple",
    )(x2d, p3d, wg, wu, wd, ln_g.reshape(1, d), ln_b.reshape(1, d), ple_w, gate_w, gate_b.reshape(1, d))


def _moe_combine_ln_ple_kernel(x_ref, y0_ref, y1_ref, rw_ref, p_ref, lng_ref, lnb_ref, plew_ref, gw_ref,
                               gb_ref, *rest):
    o_ref = rest[-1]
    rw = rw_ref[...]
    ffn = y0_ref[...] * rw[:, 0:1] + y1_ref[...] * rw[:, 1:2]
    xn = _layer_norm(DEEPNORM_ALPHA * x_ref[...] + ffn, lng_ref[...], lnb_ref[...])
    o_ref[...] = _ple(xn, p_ref[0], plew_ref, gw_ref, gb_ref)


def _moe_combine_ln_ple(x2d, y_parts, route_w, p3d, layer, ln_g, ln_b, ple_w, gate_w, gate_b, tm=512):
    m, d = x2d.shape
    tm = min(tm, min(y0.shape[0] for y0, _ in y_parts))
    out, first = None, 0
    for y0, y1 in y_parts:
        n_tiles = y0.shape[0] // tm
        glob = lambda i, first=first: (i + first, 0)
        loc = pl.BlockSpec((tm, d), lambda i: (i, 0))
        in_specs = [pl.BlockSpec((tm, d), glob), loc, loc, pl.BlockSpec((tm, TOP_K), glob),
                    pl.BlockSpec((1, tm, p3d.shape[2]), lambda i, first=first: (layer, i + first, 0)),
                    _resident((1, d)), _resident((1, d)), _resident(ple_w.shape), _resident(gate_w.shape),
                    _resident((1, d))]
        args = [x2d, y0, y1, route_w, p3d, ln_g.reshape(1, d), ln_b.reshape(1, d), ple_w, gate_w,
                gate_b.reshape(1, d)]
        aliases = {}
        if out is not None:
            in_specs.append(pl.BlockSpec(memory_space=pl.ANY))
            args.append(out)
            aliases = {len(args) - 1: 0}
        out = pl.pallas_call(
            _moe_combine_ln_ple_kernel,
            grid=(n_tiles,),
            in_specs=in_specs,
            out_specs=pl.BlockSpec((tm, d), glob),
            out_shape=jax.ShapeDtypeStruct((m, d), F32),
            input_output_aliases=aliases,
            compiler_params=_cparams("parallel"),
            name="moe_combine_ln_ple",
        )(*args)
        first += n_tiles
    return out


def _swap_half_cols(w):
    k, n = w.shape
    return w.reshape(k, n // HEAD_DIM, 2, HEAD_DIM // 2)[:, :, ::-1, :].reshape(k, n)


def _rope_tables(seq):
    half = HEAD_DIM // 2
    inv = ROPE_THETA ** (-jnp.arange(half, dtype=F32) / half)
    ang = jnp.arange(seq, dtype=F32)[:, None] * inv[None, :]
    cos, sin = jnp.cos(ang), jnp.sin(ang)
    return jnp.concatenate([cos, cos], -1), jnp.concatenate([-sin, sin], -1)


def _nsa_kv_kernel(x_ref, wk_ref, wvt_ref, cos_ref, sin_ref, slab_ref, ks_ref, kw_ref, vst_ref, vwt_ref,
                   cz_s):
    G, dh = N_KV_GROUPS, HEAD_DIM
    gw = G * dh
    xb = x_ref[...].astype(BF16)
    res = _dot(xb, wk_ref[...])
    res_t = _dot_nt(wvt_ref[...], xb)
    tm = res.shape[0]
    n_slab = tm // CMP_STRIDE
    for c in range(cz_s.shape[0]):
        cz_s[c] = res[:, c * LANES:(c + 1) * LANES]
        rows = [cz_s[c, pl.ds(l, n_slab, stride=CMP_STRIDE), :] for l in range(CMP_STRIDE)]
        for half in range(LANES // dh):
            jg = c * (LANES // dh) + half
            slab_ref[jg // G, 0, jg % G] = jnp.concatenate(
                [r[:, half * dh:(half + 1) * dh] for r in rows], axis=1).astype(BF16)
    cos = cos_ref[...]
    sin = sin_ref[...]
    ks = res[:, 2 * gw:3 * gw] * cos + res[:, 3 * gw:4 * gw] * sin
    kw = res[:, 4 * gw:5 * gw] * cos + res[:, 5 * gw:6 * gw] * sin
    t_glob = pl.program_id(1) * tm + lax.broadcasted_iota(jnp.int32, (tm, dh), 0)
    onehot = jnp.where(lax.broadcasted_iota(jnp.int32, (tm, dh), 1) == t_glob // SEL_BLOCK, 1.0, 0.0)
    for g in range(G):
        sl = slice(g * dh, (g + 1) * dh)
        ks_ref[0, g] = jnp.concatenate([ks[:, sl], onehot], axis=1).astype(BF16)
        kw_ref[0, g] = kw[:, sl].astype(BF16)
    vst_ref[0] = res_t[:gw].astype(BF16)
    vwt_ref[0] = res_t[gw:].astype(BF16)


def _nsa_kv(x2d, w_kv, batch, seq, tm=256):
    m, d = x2d.shape
    G, dh = N_KV_GROUPS, HEAD_DIM
    gw = G * dh
    parts = [w_kv[:, j * gw:(j + 1) * gw] for j in range(6)]
    w_k = jnp.concatenate([parts[0], parts[1], parts[2], _swap_half_cols(parts[2]),
                           parts[4], _swap_half_cols(parts[4])], axis=1).astype(BF16)
    w_vt = jnp.concatenate([parts[3], parts[5]], axis=1).T.astype(BF16)
    cos, sin = _rope_tables(seq)
    cos = jnp.tile(cos, (1, G))
    sin = jnp.tile(sin, (1, G))
    nt = seq // tm
    o64 = jax.ShapeDtypeStruct((batch, G, seq, dh), BF16)
    o128 = jax.ShapeDtypeStruct((batch, G, seq, 2 * dh), BF16)
    ot = jax.ShapeDtypeStruct((batch, gw, seq), BF16)
    b64 = pl.BlockSpec((1, G, tm, dh), lambda b, t: (b, 0, t, 0))
    b128 = pl.BlockSpec((1, G, tm, 2 * dh), lambda b, t: (b, 0, t, 0))
    bt = pl.BlockSpec((1, gw, tm), lambda b, t: (b, 0, t))
    slab_w = CMP_STRIDE * dh
    oslab = jax.ShapeDtypeStruct((2, batch, G, seq // CMP_STRIDE, slab_w), BF16)
    bslab = pl.BlockSpec((2, 1, G, tm // CMP_STRIDE, slab_w), lambda b, t: (0, b, 0, t, 0))
    return pl.pallas_call(
        _nsa_kv_kernel,
        grid=(batch, nt),
        in_specs=[pl.BlockSpec((tm, d), lambda b, t: (b * nt + t, 0)), _resident(w_k.shape),
                  _resident(w_vt.shape),
                  pl.BlockSpec((tm, gw), lambda b, t: (t, 0)), pl.BlockSpec((tm, gw), lambda b, t: (t, 0))],
        out_specs=[bslab, b128, b64, bt, bt],
        out_shape=[oslab, o128, o64, ot, ot],
        scratch_shapes=[pltpu.VMEM((2 * gw // LANES, tm, LANES), F32)],
        compiler_params=_cparams("parallel", "parallel"),
        name="nsa_kv",
    )(x2d, w_k, w_vt, cos, sin)


def _nsa_cmp_kernel(z_ref, pos_ref, w1_ref, b1_ref, w2_ref, o_ref, *, slabs_per_seq):
    half = w1_ref.shape[1] // 2
    z = z_ref[0]
    tm = z.shape[0]
    first = _dot(z, w1_ref[0, :half, :])
    second = _dot(z, w1_ref[0, half:, :])
    const = _dot(pos_ref[0], w1_ref[0])[0:1, :] + b1_ref[0]
    hid = first + pltpu.roll(second, shift=tm - 1, axis=0) + const
    act = 0.5 * hid * (1.0 + jnp.tanh(math.sqrt(2.0 / math.pi) * (hid + 0.044715 * hid * hid * hid)))
    out = _dot(act.astype(BF16), w2_ref[0])
    row = lax.broadcasted_iota(jnp.int32, out.shape, 0)
    o_ref[0] = jnp.where(row % slabs_per_seq == slabs_per_seq - 1, 0.0, out).astype(BF16)


def _nsa_compress(slabs, cmp_pos, phi_w1, phi_b1, phi_w2, tm=512):
    _, batch, G, n_slab, slab = slabs.shape
    dh = slab // CMP_STRIDE
    seq = n_slab * CMP_STRIDE
    rows = batch * G * n_slab
    tm = min(tm, rows)
    z = slabs.reshape(2, rows, slab)
    pos = jnp.broadcast_to(cmp_pos.reshape(2, 1, CMP_LEN * dh), (2, 8, CMP_LEN * dh)).astype(BF16)
    out = pl.pallas_call(
        functools.partial(_nsa_cmp_kernel, slabs_per_seq=seq // CMP_STRIDE),
        grid=(2, rows // tm),
        in_specs=[pl.BlockSpec((1, tm, slab), lambda j, i: (j, i, 0)),
                  pl.BlockSpec((1, 8, CMP_LEN * dh), lambda j, i: (j, 0, 0)),
                  pl.BlockSpec((1, CMP_LEN * dh, CMP_HIDDEN), lambda j, i: (j, 0, 0)),
                  pl.BlockSpec((1, 1, CMP_HIDDEN), lambda j, i: (j, 0, 0)),
                  pl.BlockSpec((1, CMP_HIDDEN, dh), lambda j, i: (j, 0, 0))],
        out_specs=pl.BlockSpec((1, tm, dh), lambda j, i: (j, i, 0)),
        out_shape=jax.ShapeDtypeStruct((2, rows, dh), BF16),
        compiler_params=_cparams("parallel", "parallel"),
        name="nsa_compress",
    )(z, pos, phi_w1.astype(BF16), phi_b1.reshape(2, 1, CMP_HIDDEN), phi_w2.astype(BF16))
    return out[0].reshape(batch, G, n_slab, dh), out[1].reshape(batch, G, n_slab, dh)


GATE_ROWS = 16


def _nsa_q_kernel(x_ref, wt_ref, cos_ref, sin_ref, q_ref, qr_ref, gate_ref):
    dh = HEAD_DIM
    hw = N_HEADS * dh
    scale = HEAD_DIM ** -0.5 * math.log2(math.e)
    res_t = _dot_nt(wt_ref[...], x_ref[...].astype(BF16))
    tm = res_t.shape[1]
    q = res_t[:hw].reshape(N_HEADS, dh, tm)
    q_sw = jnp.concatenate([q[:, dh // 2:], q[:, :dh // 2]], axis=1)
    qr = q * cos_ref[...][None] + q_sw * sin_ref[...][None]
    q_ref[0] = (q * scale).reshape(hw, tm).astype(BF16)
    qr_ref[0] = (qr * scale).reshape(hw, tm).astype(BF16)
    gate_ref[0] = _sigmoid(res_t[hw:])


def _nsa_q(x2d, w_in, batch, seq, tm=512):
    m, d = x2d.shape
    G, R, dh = N_KV_GROUPS, Q_PER_GROUP, HEAD_DIM
    hw = N_HEADS * dh
    w_gate = w_in[:, hw:].reshape(d, G, R * N_BRANCH)
    w_gate = jnp.pad(w_gate, ((0, 0), (0, 0), (0, GATE_ROWS - R * N_BRANCH))).reshape(d, G * GATE_ROWS)
    wt = jnp.concatenate([w_in[:, :hw], w_gate], axis=1).T.astype(BF16)
    cos, sin = _rope_tables(seq)
    nt = seq // tm
    oq = jax.ShapeDtypeStruct((batch, hw, seq), BF16)
    bq = pl.BlockSpec((1, hw, tm), lambda b, t: (b, 0, t))
    tab = pl.BlockSpec((dh, tm), lambda b, t: (0, t))
    return pl.pallas_call(
        _nsa_q_kernel,
        grid=(batch, nt),
        in_specs=[pl.BlockSpec((tm, d), lambda b, t: (b * nt + t, 0)), _resident(wt.shape), tab, tab],
        out_specs=[bq, bq, pl.BlockSpec((1, G * GATE_ROWS, tm), lambda b, t: (b, 0, t))],
        out_shape=[oq, oq, jax.ShapeDtypeStruct((batch, G * GATE_ROWS, seq), F32)],
        compiler_params=_cparams("parallel", "parallel"),
        name="nsa_q",
    )(x2d, wt, cos.T, sin.T)


def _nsa_attn_kernel(q_ref, qr_ref, gate_ref, kc_ref, vct_ref, ks_ref, vst_ref, kw_ref, vwt_ref, ovlt_ref,
                     o_ref, *, tq, tk, n_blk):
    R, dh = Q_PER_GROUP, HEAD_DIM
    cols = R * tq
    qi = pl.program_id(2)
    t0 = qi * tq
    n_blk_pad = ovlt_ref.shape[0]

    def heads_to_lanes(x):
        return jnp.concatenate([x[r * dh:(r + 1) * dh] for r in range(R)], axis=1)

    q = heads_to_lanes(q_ref[0])
    qr = heads_to_lanes(qr_ref[0])

    def col_pos(shape):
        return t0 + (lax.broadcasted_iota(jnp.int32, shape, 1) & (tq - 1))

    def key_tile(j):
        return pl.ds(pl.multiple_of(j * tk, tk), tk)

    ct = tq
    col_tiles = [slice(c, c + ct) for c in range(0, cols, ct)]
    ones_rows = 16
    ones_blk = jnp.ones((ones_rows, tk), BF16)

    def tile_scores(k_ref, q_op, tiles):
        return [[_dot(k_ref[0, 0, key_tile(j), :], q_op[:, cs]) for j, _ in tiles] for cs in col_tiles]

    def attend(carry, sc, vt_ref, tiles):
        m_i, acc = carry
        outs = []
        for cs, sc_c in zip(col_tiles, sc):
            masked = [s_j if mask_fn is None else jnp.where(mask_fn(k_off, q_off), s_j, NEG_INF)
                      for s_j, (_, mask_fn) in zip(sc_c, tiles)]
            m_new = m_i[:, cs]
            for s_j in masked:
                m_new = jnp.maximum(m_new, jnp.max(s_j, axis=0, keepdims=True))
            acc_new = jnp.exp2(m_i[:, cs] - m_new) * acc[:, cs]
            for s_j, (j, _) in zip(masked, tiles):
                v_ext = jnp.concatenate([vt_ref[0, :, key_tile(j)], ones_blk], axis=0)
                acc_new = acc_new + _dot(v_ext, jnp.exp2(s_j - m_new).astype(BF16))
            outs.append((m_new, acc_new))
        return tuple(jnp.concatenate([o[i] for o in outs], axis=1) for i in range(2))

    def normalised(acc):
        return acc[:dh] / acc[dh:dh + 1]

    init = (jnp.full((1, cols), NEG_INF, F32), jnp.zeros((dh + ones_rows, cols), F32))

    k_off = lax.broadcasted_iota(jnp.int32, (tk, ct), 0)
    q_off = lax.broadcasted_iota(jnp.int32, (tk, ct), 1)
    causal = lambda k, q_: k <= q_
    present = lambda cond: (lambda k, q_: (k >= 0) & cond)

    win_tiles = [(jnp.maximum(qi - 2, 0), lambda k, q_: (q_ < k) & (qi >= 2)),
                 (jnp.maximum(qi - 1, 0), present(qi >= 1)),
                 (qi, causal)]
    sc_win = tile_scores(kw_ref, qr, win_tiles)

    s = _dot(kc_ref[0, 0], q)
    cmp_end = lax.broadcasted_iota(jnp.int32, s.shape, 0) * CMP_STRIDE + (CMP_LEN - 1)
    valid = cmp_end <= col_pos(s.shape)
    sm = jnp.where(valid, s, NEG_INF)
    e = jnp.where(valid, jnp.exp2(sm - jnp.max(sm, axis=0, keepdims=True)), 0.0)
    l = jnp.sum(e, axis=0, keepdims=True)
    p_cmp = e / jnp.where(l > 0.0, l, 1.0)
    o_cmp = _dot(vct_ref[0, 0], p_cmp.astype(BF16))

    p_sum = p_cmp[:, 0:tq]
    for r in range(1, R):
        p_sum = p_sum + p_cmp[:, r * tq:(r + 1) * tq]
    hi, lo = _split2(p_sum)
    imp = _dot(ovlt_ref[...], hi) + _dot(ovlt_ref[...], lo)
    blk = lax.broadcasted_iota(jnp.int32, imp.shape, 0)
    t_q = t0 + lax.broadcasted_iota(jnp.int32, imp.shape, 1)
    cur = t_q // SEL_BLOCK
    forced = (blk == 0) | (blk == cur) | (blk == cur - 1)
    score = jnp.where(forced, FORCE_SCORE, jnp.where(blk * SEL_BLOCK <= t_q, imp, NEG_INF))
    score = jnp.where(blk < n_blk, score, -jnp.inf)
    selected = blk >= n_blk
    for _ in range(N_SEL):
        best = jnp.max(score, axis=0, keepdims=True)
        first = jnp.min(jnp.where(score == best, blk, 2 * LANES), axis=0, keepdims=True)
        pick = blk == first
        selected = selected | pick
        score = jnp.where(pick, -jnp.inf, score)
    bias = jnp.where(selected, 0.0, NEG_INF)
    qa = jnp.concatenate([qr, jnp.concatenate([bias] * R, axis=1).astype(BF16),
                          jnp.zeros((dh - n_blk_pad, cols), BF16)], axis=0)

    o_win = normalised(attend(init, sc_win, vwt_ref, win_tiles)[1])

    n_pairs = qi // 2

    def pair_body(i, carry):
        tiles = [(2 * i, None), (2 * i + 1, None)]
        return attend(carry, tile_scores(ks_ref, qa, tiles), vst_ref, tiles)

    carry = lax.fori_loop(0, n_pairs, pair_body, init)
    j_odd = 2 * n_pairs
    tail_tiles = [(j_odd, present(j_odd < qi)), (qi, causal)]
    o_slc = normalised(attend(carry, tile_scores(ks_ref, qa, tail_tiles), vst_ref, tail_tiles)[1])

    gates = gate_ref[0]
    outs = []
    for r in range(R):
        cs = slice(r * tq, (r + 1) * tq)
        outs.append(gates[3 * r:3 * r + 1] * o_cmp[:, cs] + gates[3 * r + 1:3 * r + 2] * o_slc[:, cs]
                    + gates[3 * r + 2:3 * r + 3] * o_win[:, cs])
    halves = [jnp.transpose(jnp.concatenate(outs[i:i + 2], axis=0)) for i in range(0, R, 2)]
    o_ref[0] = jnp.concatenate(halves, axis=1)


SEL_ROWS = 32


def _nsa_attention(q, qr, gates, k_cmp, v_cmp, ks, vst, kw, vwt, tq=256, tk=256):
    batch, _, seq = q.shape
    G, R, dh = N_KV_GROUPS, Q_PER_GROUP, HEAD_DIM
    n_slab = k_cmp.shape[2]
    n_blk = seq // SEL_BLOCK
    assert n_blk <= SEL_ROWS
    assert tq == tk and WINDOW == 2 * tk, "the window branch is written as exactly three key tiles"
    cmp_start = jnp.arange(n_slab) * CMP_STRIDE
    blk_start = jnp.arange(SEL_ROWS) * SEL_BLOCK
    overlap_t = ((cmp_start[None, :] < blk_start[:, None] + SEL_BLOCK)
                 & (cmp_start[None, :] + CMP_LEN - 1 >= blk_start[:, None])
                 & (jnp.arange(SEL_ROWS)[:, None] < n_blk)
                 & (jnp.arange(n_slab)[None, :] < (seq - CMP_LEN) // CMP_STRIDE + 1)).astype(BF16)
    v_cmp_t = jnp.swapaxes(v_cmp, 2, 3)
    qspec = pl.BlockSpec((1, R * dh, tq), lambda b, g, i: (b, g, i))
    full = lambda a: pl.BlockSpec((1, 1) + a.shape[2:], lambda b, g, i: (b, g, 0, 0))
    vspec = pl.BlockSpec((1, dh, seq), lambda b, g, i: (b, g, 0))
    return pl.pallas_call(
        functools.partial(_nsa_attn_kernel, tq=tq, tk=tk, n_blk=n_blk),
        grid=(batch, G, seq // tq),
        in_specs=[qspec, qspec, pl.BlockSpec((1, GATE_ROWS, tq), lambda b, g, i: (b, g, i)),
                  full(k_cmp), full(v_cmp_t), full(ks), vspec, full(kw), vspec,
                  _resident(overlap_t.shape)],
        out_specs=pl.BlockSpec((1, tq, R * dh), lambda b, g, i: (b, i, g)),
        out_shape=jax.ShapeDtypeStruct((batch, seq, G * R * dh), F32),
        compiler_params=_cparams("parallel", "parallel", "parallel"),
        name="nsa_attn",
    )(q, qr, gates, k_cmp, v_cmp_t, ks, vst, kw, vwt, overlap_t)


def _moe_kernel(be_ref, nb_ref, x_ref, wgu_ref, wd_ref, *rest, first_block):
    o_ref, wgu_bf, wd_bf = rest[-3:]
    i = pl.program_id(0)
    blk = i + first_block
    ff = wd_ref.shape[1]
    used = blk < nb_ref[0]

    @pl.when(used & ((i == 0) | (be_ref[blk] != be_ref[jnp.maximum(blk - 1, 0)])))
    def _():
        wgu_bf[...] = wgu_ref[0].astype(BF16)
        wd_bf[...] = wd_ref[0].astype(BF16)

    @pl.when(used)
    def _():
        xb = x_ref[...]
        gt = _dot(xb, wgu_bf[:, :ff])
        up = _dot(xb, wgu_bf[:, ff:])
        h = (gt * _sigmoid(gt) * up).astype(BF16)
        o_ref[...] = _dot(h, wd_bf[...])

    @pl.when(jnp.logical_not(used))
    def _():
        o_ref[...] = jnp.zeros_like(o_ref)


def _moe_experts(xs_parts, block_e, n_used, w_gu, w_down):
    d = xs_parts[0].shape[1]
    n_rows = sum(xs.shape[0] for xs in xs_parts)
    ff = w_down.shape[1]
    ys, first = None, 0
    for xs in xs_parts:
        n_blocks = xs.shape[0] // MOE_ROWS
        in_specs = [pl.BlockSpec((MOE_ROWS, d), lambda i, be, nb: (i, 0)),
                    pl.BlockSpec((1, d, 2 * ff), lambda i, be, nb, first=first: (be[i + first], 0, 0)),
                    pl.BlockSpec((1, ff, d), lambda i, be, nb, first=first: (be[i + first], 0, 0))]
        args = [block_e, n_used, xs, w_gu, w_down]
        aliases = {}
        if ys is not None:
            in_specs.append(pl.BlockSpec(memory_space=pl.ANY))
            args.append(ys)
            aliases = {len(args) - 1: 0}
        ys = pl.pallas_call(
            functools.partial(_moe_kernel, first_block=first),
            grid_spec=pltpu.PrefetchScalarGridSpec(
                num_scalar_prefetch=2,
                grid=(n_blocks,),
                in_specs=in_specs,
                out_specs=pl.BlockSpec((MOE_ROWS, d), lambda i, be, nb, first=first: (i + first, 0)),
                scratch_shapes=[pltpu.VMEM((d, 2 * ff), BF16), pltpu.VMEM((ff, d), BF16)],
            ),
            out_shape=jax.ShapeDtypeStruct((n_rows, d), F32),
            input_output_aliases=aliases,
            compiler_params=_cparams("arbitrary"),
            name="moe_experts",
        )(*args)
        first += n_blocks
    return ys


def _moe(x2d, route, w_gu, w_down):
    n_tok, d = x2d.shape
    n_assign = n_tok * TOP_K
    weights = route[:, TOP_K:2 * TOP_K]
    flat_e = route[:, :TOP_K].astype(jnp.int32).reshape(-1)
    onehot = (flat_e[:, None] == jnp.arange(N_EXPERTS, dtype=jnp.int32)[None, :]).astype(jnp.int32)
    running = jnp.cumsum(onehot, axis=0)
    counts = running[-1]
    padded = (counts + MOE_ROWS - 1) // MOE_ROWS * MOE_ROWS
    pad_end = jnp.cumsum(padded)
    pad_start = pad_end - padded
    grp_start = jnp.cumsum(counts) - counts
    pos = jnp.sum(onehot * (running - 1 + pad_start[None, :]), axis=1).reshape(n_tok, TOP_K)
    n_blocks = -(-n_assign // MOE_ROWS) + N_EXPERTS
    block_e = jnp.minimum(jnp.searchsorted(pad_end, jnp.arange(n_blocks) * MOE_ROWS, side='right'),
                          N_EXPERTS - 1).astype(jnp.int32)
    n_used = (pad_end[-1] // MOE_ROWS).astype(jnp.int32).reshape(1)
    order = jnp.argsort(flat_e)
    tok_sorted = (order // TOP_K).astype(jnp.int32)
    row_in_grp = (jnp.arange(n_blocks, dtype=jnp.int32) * MOE_ROWS - pad_start[block_e])[:, None] \
        + jnp.arange(MOE_ROWS, dtype=jnp.int32)[None, :]
    src = jnp.clip(grp_start[block_e][:, None] + row_in_grp, 0, n_assign - 1)
    row_tok = jnp.where(row_in_grp < counts[block_e][:, None], tok_sorted[src], 0).reshape(-1)
    half = n_blocks // 2 * MOE_ROWS
    ys = _moe_experts([x2d[row_tok[:half]], x2d[row_tok[half:]]], block_e, n_used, w_gu, w_down)
    bounds = [0, n_tok // 2, n_tok]
    return [[ys[pos[a:b, s]] for s in range(TOP_K)] for a, b in zip(bounds[:-1], bounds[1:])], weights


def kernel(x, p, a_mu, a_w_rkv, a_w0, a_w1, a_w2, a_a0, a_a1, a_a2, a_g1, a_g2, a_k_k, a_k_a, a_r_k,
           a_gn_g, a_gn_b, a_w_o, b_w_kv, b_cmp_pos, b_phi_w1, b_phi_b1, b_phi_w2, b_w_in, b_w_o,
           f_w_gu, f_w_down, m_w_router, m_b_router, m_w_gu, m_w_down, ln_g, ln_b, ple_w, ple_gate_w,
           ple_gate_b):
    batch, seq, d = x.shape
    m = batch * seq
    x0 = x.reshape(m, d)
    p2d = p.reshape(DEPTH, m, PLE_DIM)
    bf = lambda w: w.astype(BF16)

    x1 = _rwkv_layer(x0, batch, seq, a_mu[0], bf(a_w_rkv[0]), a_w0[0], bf(a_w1[0]), bf(a_w2[0]), a_a0[0],
                     bf(a_a1[0]), bf(a_a2[0]), bf(a_g1[0]), bf(a_g2[0]), a_k_k[0], a_k_a[0], a_r_k[0],
                     a_gn_g[0], a_gn_b[0], bf(a_w_o[0]), ln_g[0, 0], ln_b[0, 0])
    x2 = _ffn_ple(x1, p2d, 0, bf(f_w_gu[0][:, :D_FF]), bf(f_w_gu[0][:, D_FF:]), bf(f_w_down[0]),
                  ln_g[0, 1], ln_b[0, 1], bf(ple_w[0]), bf(ple_gate_w[0]), ple_gate_b[0])

    slabs, ks, kw, vst, vwt = _nsa_kv(x2, b_w_kv, batch, seq)
    k_cmp, v_cmp = _nsa_compress(slabs, b_cmp_pos, b_phi_w1, b_phi_b1, b_phi_w2)
    q, qr, gates = _nsa_q(x2, b_w_in[0], batch, seq)
    attn = _nsa_attention(q, qr, gates, k_cmp, v_cmp, ks, vst, kw, vwt).reshape(m, d)
    x3, x3_bf, route = _proj_ln_router(attn, x2, bf(b_w_o[0]), ln_g[1, 0], ln_b[1, 0], m_w_router[0],
                                       m_b_router[0])
    y_parts, route_w = _moe(x3_bf, route, m_w_gu[0], m_w_down[0])
    out = _moe_combine_ln_ple(x3, y_parts, route_w, p2d, 1, ln_g[1, 1], ln_b[1, 1], bf(ple_w[1]),
                              bf(ple_gate_w[1]), ple_gate_b[1])
    return out.reshape(batch, seq, d)
```

```python
import functools
import math

import jax
import jax.numpy as jnp
from jax import lax
from jax.experimental import pallas as pl
from jax.experimental.pallas import tpu as pltpu

BF16 = jnp.bfloat16
F32 = jnp.float32

LANES = 128
VMEM_LIMIT_BYTES = 56 * 1024 * 1024

D_MODEL = 1024
PLE_DIM = 256
RWKV_HEAD = 64
GN_EPS = 64e-5
N_HEADS = 16
HEAD_DIM = 64
N_KV_GROUPS = 4
Q_PER_GROUP = 4
N_BRANCH = 3
CMP_LEN = 32
CMP_STRIDE = 16
CMP_HIDDEN = 256
SEL_BLOCK = 64
N_SEL = 8
WINDOW = 512
ROPE_THETA = 10000.0
D_FF = 2816
N_EXPERTS = 8
TOP_K = 2
D_FF_EXPERT = 1408
MOE_ROWS = 256
LN_EPS = 1e-5
DEPTH = 2
DEEPNORM_ALPHA = (2.0 * DEPTH) ** 0.25
NEG_INF = -1e30
FORCE_SCORE = 1e4

SCAN_CHUNK = 64


def _cparams(*sem):
    return pltpu.CompilerParams(dimension_semantics=sem, vmem_limit_bytes=VMEM_LIMIT_BYTES)


def _resident(shape):
    nd = len(shape)
    return pl.BlockSpec(shape, lambda *_: (0,) * nd, pipeline_mode=pl.Buffered(1))


def _dot(a, b):
    return jnp.dot(a, b, preferred_element_type=F32)


def _dot_nt(a, b):
    return lax.dot_general(a, b, (((1,), (1,)), ((), ())), preferred_element_type=F32)


def _split2(x):
    hi = x.astype(BF16)
    lo = (x - hi.astype(F32)).astype(BF16)
    return hi, lo


def _split3(x):
    hi = x.astype(BF16)
    r1 = x - hi.astype(F32)
    mid = r1.astype(BF16)
    lo = (r1 - mid.astype(F32)).astype(BF16)
    return hi, mid, lo


def _layer_norm(y, g, b):
    mu = jnp.mean(y, axis=-1, keepdims=True)
    yc = y - mu
    var = jnp.mean(yc * yc, axis=-1, keepdims=True)
    return yc * lax.rsqrt(var + LN_EPS) * g + b


def _sigmoid(z):
    return 1.0 / (1.0 + jnp.exp(-z))


def _rwkv_project(first, x_ref, xh_ref, mu_ref, wrkv_ref, w0_ref, w1_ref, w2_ref, a0_ref, a1_ref,
                  a2_ref, g1_ref, g2_ref, r_ref, k_ref, v_ref, dl_ref, a_ref, g_ref):
    x = x_ref[...]
    prev_row = jnp.where(first, 0.0, xh_ref[0, 7:8, :])
    row = lax.broadcasted_iota(jnp.int32, x.shape, 0)
    x_shift = jnp.where(row == 0, prev_row, pltpu.roll(x, shift=1, axis=0))
    xx = x_shift - x

    def mix(i):
        return (x + xx * mu_ref[i:i + 1, :]).astype(BF16)

    r_ref[...] = _dot(mix(0), wrkv_ref[0])
    k_ref[...] = _dot(mix(1), wrkv_ref[1])
    v_ref[...] = _dot(mix(2), wrkv_ref[2])
    z = w0_ref[...] + _dot(jnp.tanh(_dot(mix(3), w1_ref[...])).astype(BF16), w2_ref[...])
    dl_ref[...] = -math.exp(-0.5) * _sigmoid(z)
    a_ref[...] = _sigmoid(a0_ref[...] + _dot(_dot(mix(4), a1_ref[...]).astype(BF16), a2_ref[...]))
    g_ref[...] = _dot(_sigmoid(_dot(mix(5), g1_ref[...])).astype(BF16), g2_ref[...])


def _rwkv_recurrence(first, r_ref, k_ref, v_ref, dl_ref, a_ref, g_ref, kk_ref, ka_ref, rk_ref, gng_ref,
                     gnb_ref, o_ref, state_ref, kkn_ref, cum_ref, *, n_pairs, n_chunks):
    C = SCAN_CHUNK
    N = RWKV_HEAD
    W = 2 * N
    tb = n_chunks * C

    @pl.when(first)
    def _():
        state_ref[...] = jnp.zeros_like(state_ref)

    tt = lax.broadcasted_iota(jnp.int32, (C, W), 0)
    ln = lax.broadcasted_iota(jnp.int32, (C, W), 1)
    ss = ln & (N - 1)
    strict_lower = ss < tt
    lower = ss <= tt
    eye = jnp.where(ss == tt, 1.0, 0.0)
    level_masks = []
    for lg in range(int(math.log2(C))):
        level_masks.append(((tt >> (lg + 1)) == (ss >> (lg + 1))) & ((tt >> lg) == (ss >> lg) + 1))
    rr = lax.broadcasted_iota(jnp.int32, (W, W), 0)
    cc = lax.broadcasted_iota(jnp.int32, (W, W), 1)
    same_head = (rr < N) == (cc < N)
    lane_head0 = ln < N
    WG = 2 * W
    lg_n = int(math.log2(N))
    ones_grp = jnp.where((lax.broadcasted_iota(jnp.int32, (WG, WG), 0) >> lg_n)
                         == (lax.broadcasted_iota(jnp.int32, (WG, WG), 1) >> lg_n), 1.0, 0.0).astype(BF16)
    col_groups = [slice(c, c + WG) for c in range(0, n_pairs * W, WG)]

    def bd(y):
        return jnp.concatenate([jnp.where(lane_head0, y, 0.0), jnp.where(lane_head0, 0.0, y)],
                               axis=0).astype(BF16)

    def pmm(xp, ybd):
        return _dot(xp.astype(BF16), ybd)

    def head_sum(xg):
        return _dot(xg.astype(BF16), ones_grp)

    tri_r = lax.broadcasted_iota(jnp.int32, (tb, tb), 0)
    tri_c = lax.broadcasted_iota(jnp.int32, (tb, tb), 1)
    lg_c = int(math.log2(C))
    tri_chunk = jnp.where((tri_c <= tri_r) & ((tri_c >> lg_c) == (tri_r >> lg_c)), 1.0, 0.0).astype(BF16)
    d_hi, d_mid, d_lo = _split3(dl_ref[...])
    cum_ref[...] = _dot(tri_chunk, d_hi) + _dot(tri_chunk, d_mid) + _dot(tri_chunk, d_lo)
    for cols in col_groups:
        kk_raw = k_ref[:, cols] * kk_ref[:, cols]
        kkn_ref[:, cols] = kk_raw / jnp.maximum(jnp.sqrt(head_sum(kk_raw * kk_raw)), 1e-12)

    def chunk_load(ci, pi):
        rows = pl.ds(pl.multiple_of(ci * C, C), C)
        cols = slice(pi * W, (pi + 1) * W)
        return (r_ref[rows, cols], k_ref[rows, cols], v_ref[rows, cols], dl_ref[rows, cols],
                a_ref[rows, cols], kkn_ref[rows, cols], cum_ref[rows, cols], state_ref[pi])

    def chunk_compute(pi, r, k, v, dl, a, kk, cum, m0):
        cols = slice(pi * W, (pi + 1) * W)
        k2 = k * (1.0 + (a - 1.0) * ka_ref[:, cols])
        b = a * kk
        p_incl = jnp.exp(cum)
        p_excl = jnp.exp(cum - dl)
        p_inv = jnp.exp(-cum)
        kq = kk * p_excl
        rq = r * p_incl
        bk = b * p_inv
        kkd = k2 * p_inv

        bkt = jnp.transpose(jnp.concatenate([b, k2], axis=0))
        cumt = jnp.transpose(jnp.concatenate([cum, cum], axis=0))
        last_col = cumt[:, C - 1:C]
        lhs_t = (bkt * jnp.exp(last_col - cumt)).astype(BF16)
        m_decayed = m0 * jnp.exp(last_col)

        lhs = jnp.concatenate([kq, rq], axis=0).astype(BF16)
        yield
        abg = _dot_nt(lhs, jnp.concatenate([bd(bk), bd(kkd)], axis=0))
        a_m = jnp.where(strict_lower, abg[:C, :W], 0.0)
        aq_m = jnp.where(lower, abg[C:, :W], 0.0)
        g_m = jnp.where(strict_lower, abg[:C, W:], 0.0)
        gq_m = jnp.where(lower, abg[C:, W:], 0.0)

        tinv = eye - jnp.where(level_masks[0], a_m, 0.0)
        yield
        gv = pmm(jnp.concatenate([g_m, gq_m], axis=0), bd(v))
        for lm in level_masks[1:]:
            w_ = pmm(jnp.where(lm, a_m, 0.0), bd(tinv))
            yield
            tinv = tinv - pmm(tinv, bd(w_))
            yield

        kq_y = pmm(tinv, jnp.concatenate([bd(kq), bd(gv[:C])], axis=1))
        kq1 = kq_y[:, :W]
        y = kq_y[:, W:]
        yield
        aq_ky = pmm(aq_m, jnp.concatenate([bd(kq1), bd(y)], axis=1))
        rq1 = rq - aq_ky[:, :W]
        o_loc = gv[C:] - aq_ky[:, W:]
        yield
        st = _dot(jnp.concatenate([rq1, kq1], axis=0).astype(BF16), m0.astype(BF16))
        o = st[:C] + o_loc
        z = -(st[C:] + y)
        yield
        upd = _dot(lhs_t, jnp.concatenate([z, v], axis=0).astype(BF16))
        return o, m_decayed + jnp.where(same_head, upd, 0.0)

    def interleave(gens):
        results = [None] * len(gens)
        live = list(range(len(gens)))
        while live:
            for i in list(live):
                try:
                    next(gens[i])
                except StopIteration as done:
                    results[i] = done.value
                    live.remove(i)
        return results

    def body(ci, carry):
        loaded = [chunk_load(ci, pi) for pi in range(n_pairs)]
        results = interleave([chunk_compute(pi, *loaded[pi]) for pi in range(n_pairs)])
        rows = pl.ds(pl.multiple_of(ci * C, C), C)
        for pi, (o, m_new) in enumerate(results):
            o_ref[rows, pi * W:(pi + 1) * W] = o
            state_ref[pi] = m_new
        return carry

    lax.fori_loop(0, n_chunks, body, 0)

    for cols in col_groups:
        o = o_ref[:, cols]
        r = r_ref[:, cols]
        v = v_ref[:, cols]
        k2 = k_ref[:, cols] * (1.0 + (a_ref[:, cols] - 1.0) * ka_ref[:, cols])
        oc = o - head_sum(o) * (1.0 / N)
        var_o = head_sum(oc * oc) * (1.0 / N)
        out = oc * lax.rsqrt(var_o + GN_EPS) * gng_ref[:, cols] + gnb_ref[:, cols]
        out = out + head_sum(r * k2 * rk_ref[:, cols]) * v
        o_ref[:, cols] = out * g_ref[:, cols]


def _rwkv_layer_kernel(x_ref, xh_ref, mu_ref, wrkv_ref, w0_ref, w1_ref, w2_ref, a0_ref, a1_ref, a2_ref,
                       g1_ref, g2_ref, kk_ref, ka_ref, rk_ref, gng_ref, gnb_ref, wo_ref, lng_ref, lnb_ref,
                       o_ref, state_ref, r_s, k_s, v_s, dl_s, a_s, g_s, kkn_s, cum_s, y_s, *, n_chunks):
    first = pl.program_id(1) == 0
    _rwkv_project(first, x_ref, xh_ref, mu_ref, wrkv_ref, w0_ref, w1_ref, w2_ref, a0_ref, a1_ref, a2_ref,
                  g1_ref, g2_ref, r_s, k_s, v_s, dl_s, a_s, g_s)
    _rwkv_recurrence(first, r_s, k_s, v_s, dl_s, a_s, g_s, kk_ref, ka_ref, rk_ref, gng_ref, gnb_ref, y_s,
                     state_ref, kkn_s, cum_s, n_pairs=x_ref.shape[1] // (2 * RWKV_HEAD), n_chunks=n_chunks)
    mix = _dot(y_s[...].astype(BF16), wo_ref[...])
    o_ref[...] = _layer_norm(DEEPNORM_ALPHA * x_ref[...] + mix, lng_ref[...], lnb_ref[...])


def _rwkv_layer(x2d, batch, seq, mu, w_rkv, w0, w1, w2, a0, a1, a2, g1, g2, k_k, k_a, r_k, gn_g, gn_b,
                w_o, ln_g, ln_b, tb=256):
    m, d = x2d.shape
    nt = seq // tb
    xh = x2d.reshape(m // 8, 8, d)
    vec = lambda z: z.reshape(1, d)
    blk = pl.BlockSpec((tb, d), lambda b, t: (b * nt + t, 0))
    weights = [mu, w_rkv, vec(w0), w1, w2, vec(a0), a1, a2, g1, g2, vec(k_k), vec(k_a), vec(r_k),
               vec(gn_g), vec(gn_b), w_o, vec(ln_g), vec(ln_b)]
    n_pairs = d // (2 * RWKV_HEAD)
    return pl.pallas_call(
        functools.partial(_rwkv_layer_kernel, n_chunks=tb // SCAN_CHUNK),
        grid=(batch, nt),
        in_specs=[blk, pl.BlockSpec((1, 8, d), lambda b, t: (jnp.maximum((b * nt + t) * (tb // 8) - 1, 0), 0, 0))]
        + [_resident(w.shape) for w in weights],
        out_specs=blk,
        out_shape=jax.ShapeDtypeStruct((m, d), F32),
        scratch_shapes=[pltpu.VMEM((n_pairs, 2 * RWKV_HEAD, 2 * RWKV_HEAD), F32)]
        + [pltpu.VMEM((tb, d), F32)] * 9,
        compiler_params=_cparams("parallel", "arbitrary"),
        name="rwkv_layer",
    )(x2d, xh, *weights)


def _proj_ln_router_kernel(y_ref, x_ref, w_ref, lng_ref, lnb_ref, wr_ref, br_ref, o_ref, ob_ref, lg_ref):
    mix = _dot(y_ref[...].astype(BF16), w_ref[...])
    xn = _layer_norm(DEEPNORM_ALPHA * x_ref[...] + mix, lng_ref[...], lnb_ref[...])
    o_ref[...] = xn
    xh, xl = _split2(xn)
    ob_ref[...] = xh
    lg_ref[...] = _dot(xh, wr_ref[0]) + _dot(xh, wr_ref[1]) + _dot(xl, wr_ref[0]) + br_ref[...]


def _proj_ln_router(y, x2d, w, ln_g, ln_b, w_router, b_router, tm=512):
    m, d = x2d.shape
    row = lambda i: (i, 0)
    wr = jnp.zeros((d, LANES), F32).at[:, :N_EXPERTS].set(w_router)
    wr3 = jnp.stack(_split2(wr))
    br =jnp.zeros((1, LANES), F32).at[0, :N_EXPERTS].set(b_router)
    return pl.pallas_call(
        _proj_ln_router_kernel,
        grid=(m // tm,),
        in_specs=[pl.BlockSpec((tm, y.shape[1]), row), pl.BlockSpec((tm, d), row), _resident(w.shape),
                  _resident((1, d)), _resident((1, d)), _resident(wr3.shape), _resident(br.shape)],
        out_specs=[pl.BlockSpec((tm, d), row), pl.BlockSpec((tm, d), row), pl.BlockSpec((tm, LANES), row)],
        out_shape=[jax.ShapeDtypeStruct((m, d), F32), jax.ShapeDtypeStruct((m, d), BF16),
                   jax.ShapeDtypeStruct((m, LANES), F32)],
        compiler_params=_cparams("parallel"),
        name="proj_ln_router",
    )(y, x2d, w, ln_g.reshape(1, d), ln_b.reshape(1, d), wr3, br)


def _ple(xn, p, plew_ref, gw_ref, gb_ref):
    gate = _sigmoid(_dot(xn.astype(BF16), gw_ref[...]) + gb_ref[...])
    return xn + _dot(p.astype(BF16), plew_ref[...]) * gate


def _ffn_ple_kernel(x_ref, p_ref, wg_ref, wu_ref, wd_ref, lng_ref, lnb_ref, plew_ref, gw_ref, gb_ref,
                    o_ref, *, ff_chunk):
    x = x_ref[...]
    xb = x.astype(BF16)
    acc = jnp.zeros_like(x)
    for c in range(wg_ref.shape[1] // ff_chunk):
        sl = slice(c * ff_chunk, (c + 1) * ff_chunk)
        gt = _dot(xb, wg_ref[:, sl])
        up = _dot(xb, wu_ref[:, sl])
        h = (gt * _sigmoid(gt) * up).astype(BF16)
        acc = acc + _dot(h, wd_ref[sl, :])
    xn = _layer_norm(DEEPNORM_ALPHA * x + acc, lng_ref[...], lnb_ref[...])
    o_ref[...] = _ple(xn, p_ref[0], plew_ref, gw_ref, gb_ref)


def _ple_block(p3d, layer, tm):
    return pl.BlockSpec((1, tm, p3d.shape[2]), lambda i: (layer, i, 0))


def _ffn_ple(x2d, p3d, layer, wg, wu, wd, ln_g, ln_b, ple_w, gate_w, gate_b, tm=512, ff_chunk=1408):
    m, d = x2d.shape
    row = lambda i: (i, 0)
    return pl.pallas_call(
        functools.partial(_ffn_ple_kernel, ff_chunk=ff_chunk),
        grid=(m // tm,),
        in_specs=[pl.BlockSpec((tm, d), row), _ple_block(p3d, layer, tm),
                  _resident(wg.shape), _resident(wu.shape), _resident(wd.shape),
                  _resident((1, d)), _resident((1, d)), _resident(ple_w.shape), _resident(gate_w.shape),
                  _resident((1, d))],
        out_specs=pl.BlockSpec((tm, d), row),
        out_shape=jax.ShapeDtypeStruct((m, d), F32),
        compiler_params=_cparams("parallel"),
        name="ffn_ple",
    )(x2d, p3d, wg, wu, wd, ln_g.reshape(1, d), ln_b.reshape(1, d), ple_w, gate_w, gate_b.reshape(1, d))


def _moe_combine_ln_ple_kernel(x_ref, y0_ref, y1_ref, rw_ref, p_ref, lng_ref, lnb_ref, plew_ref, gw_ref,
                               gb_ref, o_ref):
    rw = rw_ref[...]
    ffn = y0_ref[...] * rw[:, 0:1] + y1_ref[...] * rw[:, 1:2]
    xn = _layer_norm(DEEPNORM_ALPHA * x_ref[...] + ffn, lng_ref[...], lnb_ref[...])
    o_ref[...] = _ple(xn, p_ref[0], plew_ref, gw_ref, gb_ref)


def _moe_combine_ln_ple(x2d, y0, y1, route_w, p3d, layer, ln_g, ln_b, ple_w, gate_w, gate_b, tm=512):
    m, d = x2d.shape
    row = lambda i: (i, 0)
    tile = pl.BlockSpec((tm, d), row)
    return pl.pallas_call(
        _moe_combine_ln_ple_kernel,
        grid=(m // tm,),
        in_specs=[tile, tile, tile, pl.BlockSpec((tm, TOP_K), row), _ple_block(p3d, layer, tm),
                  _resident((1, d)), _resident((1, d)), _resident(ple_w.shape), _resident(gate_w.shape),
                  _resident((1, d))],
        out_specs=tile,
        out_shape=jax.ShapeDtypeStruct((m, d), F32),
        compiler_params=_cparams("parallel"),
        name="moe_combine_ln_ple",
    )(x2d, y0, y1, route_w, p3d, ln_g.reshape(1, d), ln_b.reshape(1, d), ple_w, gate_w, gate_b.reshape(1, d))


def _swap_half_cols(w):
    k, n = w.shape
    return w.reshape(k, n // HEAD_DIM, 2, HEAD_DIM // 2)[:, :, ::-1, :].reshape(k, n)


def _rope_tables(seq):
    half = HEAD_DIM // 2
    inv = ROPE_THETA ** (-jnp.arange(half, dtype=F32) / half)
    ang = jnp.arange(seq, dtype=F32)[:, None] * inv[None, :]
    cos, sin = jnp.cos(ang), jnp.sin(ang)
    return jnp.concatenate([cos, cos], -1), jnp.concatenate([-sin, sin], -1)


def _nsa_kv_kernel(x_ref, wk_ref, wvt_ref, cos_ref, sin_ref, slab_ref, ks_ref, kw_ref, vst_ref, vwt_ref,
                   cz_s):
    G, dh = N_KV_GROUPS, HEAD_DIM
    gw = G * dh
    xb = x_ref[...].astype(BF16)
    res = _dot(xb, wk_ref[...])
    res_t = _dot_nt(wvt_ref[...], xb)
    tm = res.shape[0]
    n_slab = tm // CMP_STRIDE
    for c in range(cz_s.shape[0]):
        cz_s[c] = res[:, c * LANES:(c + 1) * LANES]
        rows = [cz_s[c, pl.ds(l, n_slab, stride=CMP_STRIDE), :] for l in range(CMP_STRIDE)]
        for half in range(LANES // dh):
            jg = c * (LANES // dh) + half
            slab_ref[jg // G, 0, jg % G] = jnp.concatenate(
                [r[:, half * dh:(half + 1) * dh] for r in rows], axis=1).astype(BF16)
    cos = cos_ref[...]
    sin = sin_ref[...]
    ks = res[:, 2 * gw:3 * gw] * cos + res[:, 3 * gw:4 * gw] * sin
    kw = res[:, 4 * gw:5 * gw] * cos + res[:, 5 * gw:6 * gw] * sin
    t_glob = pl.program_id(1) * tm + lax.broadcasted_iota(jnp.int32, (tm, dh), 0)
    onehot = jnp.where(lax.broadcasted_iota(jnp.int32, (tm, dh), 1) == t_glob // SEL_BLOCK, 1.0, 0.0)
    for g in range(G):
        sl = slice(g * dh, (g + 1) * dh)
        ks_ref[0, g] = jnp.concatenate([ks[:, sl], onehot], axis=1).astype(BF16)
        kw_ref[0, g] = kw[:, sl].astype(BF16)
    vst_ref[0] = res_t[:gw].astype(BF16)
    vwt_ref[0] = res_t[gw:].astype(BF16)


def _nsa_kv(x2d, w_kv, batch, seq, tm=256):
    m, d = x2d.shape
    G, dh = N_KV_GROUPS, HEAD_DIM
    gw = G * dh
    parts = [w_kv[:, j * gw:(j + 1) * gw] for j in range(6)]
    w_k = jnp.concatenate([parts[0], parts[1], parts[2], _swap_half_cols(parts[2]),
                           parts[4], _swap_half_cols(parts[4])], axis=1).astype(BF16)
    w_vt = jnp.concatenate([parts[3], parts[5]], axis=1).T.astype(BF16)
    cos, sin = _rope_tables(seq)
    cos = jnp.tile(cos, (1, G))
    sin = jnp.tile(sin, (1, G))
    nt = seq // tm
    o64 = jax.ShapeDtypeStruct((batch, G, seq, dh), BF16)
    o128 = jax.ShapeDtypeStruct((batch, G, seq, 2 * dh), BF16)
    ot = jax.ShapeDtypeStruct((batch, gw, seq), BF16)
    b64 = pl.BlockSpec((1, G, tm, dh), lambda b, t: (b, 0, t, 0))
    b128 = pl.BlockSpec((1, G, tm, 2 * dh), lambda b, t: (b, 0, t, 0))
    bt = pl.BlockSpec((1, gw, tm), lambda b, t: (b, 0, t))
    slab_w = CMP_STRIDE * dh
    oslab = jax.ShapeDtypeStruct((2, batch, G, seq // CMP_STRIDE, slab_w), BF16)
    bslab = pl.BlockSpec((2, 1, G, tm // CMP_STRIDE, slab_w), lambda b, t: (0, b, 0, t, 0))
    return pl.pallas_call(
        _nsa_kv_kernel,
        grid=(batch, nt),
        in_specs=[pl.BlockSpec((tm, d), lambda b, t: (b * nt + t, 0)), _resident(w_k.shape),
                  _resident(w_vt.shape),
                  pl.BlockSpec((tm, gw), lambda b, t: (t, 0)), pl.BlockSpec((tm, gw), lambda b, t: (t, 0))],
        out_specs=[bslab, b128, b64, bt, bt],
        out_shape=[oslab, o128, o64, ot, ot],
        scratch_shapes=[pltpu.VMEM((2 * gw // LANES, tm, LANES), F32)],
        compiler_params=_cparams("parallel", "parallel"),
        name="nsa_kv",
    )(x2d, w_k, w_vt, cos, sin)


def _nsa_cmp_kernel(z_ref, pos_ref, w1_ref, b1_ref, w2_ref, o_ref, *, slabs_per_seq):
    half = w1_ref.shape[1] // 2
    z = z_ref[0]
    tm = z.shape[0]
    first = _dot(z, w1_ref[0, :half, :])
    second = _dot(z, w1_ref[0, half:, :])
    const = _dot(pos_ref[0], w1_ref[0])[0:1, :] + b1_ref[0]
    hid = first + pltpu.roll(second, shift=tm - 1, axis=0) + const
    act = 0.5 * hid * (1.0 + jnp.tanh(math.sqrt(2.0 / math.pi) * (hid + 0.044715 * hid * hid * hid)))
    out = _dot(act.astype(BF16), w2_ref[0])
    row = lax.broadcasted_iota(jnp.int32, out.shape, 0)
    o_ref[0] = jnp.where(row % slabs_per_seq == slabs_per_seq - 1, 0.0, out).astype(BF16)


def _nsa_compress(slabs, cmp_pos, phi_w1, phi_b1, phi_w2, tm=512):
    _, batch, G, n_slab, slab = slabs.shape
    dh = slab // CMP_STRIDE
    seq = n_slab * CMP_STRIDE
    rows = batch * G * n_slab
    tm = min(tm, rows)
    z = slabs.reshape(2, rows, slab)
    pos = jnp.broadcast_to(cmp_pos.reshape(2, 1, CMP_LEN * dh), (2, 8, CMP_LEN * dh)).astype(BF16)
    out = pl.pallas_call(
        functools.partial(_nsa_cmp_kernel, slabs_per_seq=seq // CMP_STRIDE),
        grid=(2, rows // tm),
        in_specs=[pl.BlockSpec((1, tm, slab), lambda j, i: (j, i, 0)),
                  pl.BlockSpec((1, 8, CMP_LEN * dh), lambda j, i: (j, 0, 0)),
                  pl.BlockSpec((1, CMP_LEN * dh, CMP_HIDDEN), lambda j, i: (j, 0, 0)),
                  pl.BlockSpec((1, 1, CMP_HIDDEN), lambda j, i: (j, 0, 0)),
                  pl.BlockSpec((1, CMP_HIDDEN, dh), lambda j, i: (j, 0, 0))],
        out_specs=pl.BlockSpec((1, tm, dh), lambda j, i: (j, i, 0)),
        out_shape=jax.ShapeDtypeStruct((2, rows, dh), BF16),
        compiler_params=_cparams("parallel", "parallel"),
        name="nsa_compress",
    )(z, pos, phi_w1.astype(BF16), phi_b1.reshape(2, 1, CMP_HIDDEN), phi_w2.astype(BF16))
    return out[0].reshape(batch, G, n_slab, dh), out[1].reshape(batch, G, n_slab, dh)


GATE_ROWS = 16


def _nsa_q_kernel(x_ref, wt_ref, cos_ref, sin_ref, q_ref, qr_ref, gate_ref):
    dh = HEAD_DIM
    hw = N_HEADS * dh
    scale = HEAD_DIM ** -0.5 * math.log2(math.e)
    res_t = _dot_nt(wt_ref[...], x_ref[...].astype(BF16))
    tm = res_t.shape[1]
    q = res_t[:hw].reshape(N_HEADS, dh, tm)
    q_sw = jnp.concatenate([q[:, dh // 2:], q[:, :dh // 2]], axis=1)
    qr = q * cos_ref[...][None] + q_sw * sin_ref[...][None]
    q_ref[0] = (q * scale).reshape(hw, tm).astype(BF16)
    qr_ref[0] = (qr * scale).reshape(hw, tm).astype(BF16)
    gate_ref[0] = _sigmoid(res_t[hw:])


def _nsa_q(x2d, w_in, batch, seq, tm=512):
    m, d = x2d.shape
    G, R, dh = N_KV_GROUPS, Q_PER_GROUP, HEAD_DIM
    hw = N_HEADS * dh
    w_gate = w_in[:, hw:].reshape(d, G, R * N_BRANCH)
    w_gate = jnp.pad(w_gate, ((0, 0), (0, 0), (0, GATE_ROWS - R * N_BRANCH))).reshape(d, G * GATE_ROWS)
    wt = jnp.concatenate([w_in[:, :hw], w_gate], axis=1).T.astype(BF16)
    cos, sin = _rope_tables(seq)
    nt = seq // tm
    oq = jax.ShapeDtypeStruct((batch, hw, seq), BF16)
    bq = pl.BlockSpec((1, hw, tm), lambda b, t: (b, 0, t))
    tab = pl.BlockSpec((dh, tm), lambda b, t: (0, t))
    return pl.pallas_call(
        _nsa_q_kernel,
        grid=(batch, nt),
        in_specs=[pl.BlockSpec((tm, d), lambda b, t: (b * nt + t, 0)), _resident(wt.shape), tab, tab],
        out_specs=[bq, bq, pl.BlockSpec((1, G * GATE_ROWS, tm), lambda b, t: (b, 0, t))],
        out_shape=[oq, oq, jax.ShapeDtypeStruct((batch, G * GATE_ROWS, seq), F32)],
        compiler_params=_cparams("parallel", "parallel"),
        name="nsa_q",
    )(x2d, wt, cos.T, sin.T)


def _nsa_attn_kernel(q_ref, qr_ref, gate_ref, kc_ref, vct_ref, ks_ref, vst_ref, kw_ref, vwt_ref, ovlt_ref,
                     o_ref, *, tq, tk, n_blk):
    R, dh = Q_PER_GROUP, HEAD_DIM
    cols = R * tq
    qi = pl.program_id(2)
    t0 = qi * tq
    n_blk_pad = ovlt_ref.shape[0]

    def heads_to_lanes(x):
        return jnp.concatenate([x[r * dh:(r + 1) * dh] for r in range(R)], axis=1)

    q = heads_to_lanes(q_ref[0])
    qr = heads_to_lanes(qr_ref[0])

    def col_pos(shape):
        return t0 + (lax.broadcasted_iota(jnp.int32, shape, 1) & (tq - 1))

    def key_tile(j):
        return pl.ds(pl.multiple_of(j * tk, tk), tk)

    ct = tq
    col_tiles = [slice(c, c + ct) for c in range(0, cols, ct)]
    ones_rows = 16
    ones_blk = jnp.ones((ones_rows, tk), BF16)

    def tile_scores(k_ref, q_op, tiles):
        return [[_dot(k_ref[0, 0, key_tile(j), :], q_op[:, cs]) for j, _ in tiles] for cs in col_tiles]

    def attend(carry, sc, vt_ref, tiles):
        m_i, acc = carry
        outs = []
        for cs, sc_c in zip(col_tiles, sc):
            masked = [s_j if mask_fn is None else jnp.where(mask_fn(k_off, q_off), s_j, NEG_INF)
                      for s_j, (_, mask_fn) in zip(sc_c, tiles)]
            m_new = m_i[:, cs]
            for s_j in masked:
                m_new = jnp.maximum(m_new, jnp.max(s_j, axis=0, keepdims=True))
            acc_new = jnp.exp2(m_i[:, cs] - m_new) * acc[:, cs]
            for s_j, (j, _) in zip(masked, tiles):
                v_ext = jnp.concatenate([vt_ref[0, :, key_tile(j)], ones_blk], axis=0)
                acc_new = acc_new + _dot(v_ext, jnp.exp2(s_j - m_new).astype(BF16))
            outs.append((m_new, acc_new))
        return tuple(jnp.concatenate([o[i] for o in outs], axis=1) for i in range(2))

    def normalised(acc):
        return acc[:dh] / acc[dh:dh + 1]

    init = (jnp.full((1, cols), NEG_INF, F32), jnp.zeros((dh + ones_rows, cols), F32))

    k_off = lax.broadcasted_iota(jnp.int32, (tk, ct), 0)
    q_off = lax.broadcasted_iota(jnp.int32, (tk, ct), 1)
    causal = lambda k, q_: k <= q_
    present = lambda cond: (lambda k, q_: (k >= 0) & cond)

    win_tiles = [(jnp.maximum(qi - 2, 0), lambda k, q_: (q_ < k) & (qi >= 2)),
                 (jnp.maximum(qi - 1, 0), present(qi >= 1)),
                 (qi, causal)]
    sc_win = tile_scores(kw_ref, qr, win_tiles)

    s = _dot(kc_ref[0, 0], q)
    cmp_end = lax.broadcasted_iota(jnp.int32, s.shape, 0) * CMP_STRIDE + (CMP_LEN - 1)
    valid = cmp_end <= col_pos(s.shape)
    sm = jnp.where(valid, s, NEG_INF)
    e = jnp.where(valid, jnp.exp2(sm - jnp.max(sm, axis=0, keepdims=True)), 0.0)
    l = jnp.sum(e, axis=0, keepdims=True)
    p_cmp = e / jnp.where(l > 0.0, l, 1.0)
    o_cmp = _dot(vct_ref[0, 0], p_cmp.astype(BF16))

    p_sum = p_cmp[:, 0:tq]
    for r in range(1, R):
        p_sum = p_sum + p_cmp[:, r * tq:(r + 1) * tq]
    hi, lo = _split2(p_sum)
    imp = _dot(ovlt_ref[...], hi) + _dot(ovlt_ref[...], lo)
    blk = lax.broadcasted_iota(jnp.int32, imp.shape, 0)
    t_q = t0 + lax.broadcasted_iota(jnp.int32, imp.shape, 1)
    cur = t_q // SEL_BLOCK
    forced = (blk == 0) | (blk == cur) | (blk == cur - 1)
    score = jnp.where(forced, FORCE_SCORE, jnp.where(blk * SEL_BLOCK <= t_q, imp, NEG_INF))
    score = jnp.where(blk < n_blk, score, -jnp.inf)
    selected = blk >= n_blk
    for _ in range(N_SEL):
        best = jnp.max(score, axis=0, keepdims=True)
        first = jnp.min(jnp.where(score == best, blk, 2 * LANES), axis=0, keepdims=True)
        pick = blk == first
        selected = selected | pick
        score = jnp.where(pick, -jnp.inf, score)
    bias = jnp.where(selected, 0.0, NEG_INF)
    qa = jnp.concatenate([qr, jnp.concatenate([bias] * R, axis=1).astype(BF16),
                          jnp.zeros((dh - n_blk_pad, cols), BF16)], axis=0)

    o_win = normalised(attend(init, sc_win, vwt_ref, win_tiles)[1])

    n_pairs = qi // 2

    def pair_body(i, carry):
        tiles = [(2 * i, None), (2 * i + 1, None)]
        return attend(carry, tile_scores(ks_ref, qa, tiles), vst_ref, tiles)

    carry = lax.fori_loop(0, n_pairs, pair_body, init)
    j_odd = 2 * n_pairs
    tail_tiles = [(j_odd, present(j_odd < qi)), (qi, causal)]
    o_slc = normalised(attend(carry, tile_scores(ks_ref, qa, tail_tiles), vst_ref, tail_tiles)[1])

    gates = gate_ref[0]
    outs = []
    for r in range(R):
        cs = slice(r * tq, (r + 1) * tq)
        outs.append(gates[3 * r:3 * r + 1] * o_cmp[:, cs] + gates[3 * r + 1:3 * r + 2] * o_slc[:, cs]
                    + gates[3 * r + 2:3 * r + 3] * o_win[:, cs])
    halves = [jnp.transpose(jnp.concatenate(outs[i:i + 2], axis=0)) for i in range(0, R, 2)]
    o_ref[0] = jnp.concatenate(halves, axis=1)


SEL_ROWS = 32


def _nsa_attention(q, qr, gates, k_cmp, v_cmp, ks, vst, kw, vwt, tq=256, tk=256):
    batch, _, seq = q.shape
    G, R, dh = N_KV_GROUPS, Q_PER_GROUP, HEAD_DIM
    n_slab = k_cmp.shape[2]
    n_blk = seq // SEL_BLOCK
    assert n_blk <= SEL_ROWS
    assert tq == tk and WINDOW == 2 * tk, "the window branch is written as exactly three key tiles"
    cmp_start = jnp.arange(n_slab) * CMP_STRIDE
    blk_start = jnp.arange(SEL_ROWS) * SEL_BLOCK
    overlap_t = ((cmp_start[None, :] < blk_start[:, None] + SEL_BLOCK)
                 & (cmp_start[None, :] + CMP_LEN - 1 >= blk_start[:, None])
                 & (jnp.arange(SEL_ROWS)[:, None] < n_blk)
                 & (jnp.arange(n_slab)[None, :] < (seq - CMP_LEN) // CMP_STRIDE + 1)).astype(BF16)
    v_cmp_t = jnp.swapaxes(v_cmp, 2, 3)
    qspec = pl.BlockSpec((1, R * dh, tq), lambda b, g, i: (b, g, i))
    full = lambda a: pl.BlockSpec((1, 1) + a.shape[2:], lambda b, g, i: (b, g, 0, 0))
    vspec = pl.BlockSpec((1, dh, seq), lambda b, g, i: (b, g, 0))
    return pl.pallas_call(
        functools.partial(_nsa_attn_kernel, tq=tq, tk=tk, n_blk=n_blk),
        grid=(batch, G, seq // tq),
        in_specs=[qspec, qspec, pl.BlockSpec((1, GATE_ROWS, tq), lambda b, g, i: (b, g, i)),
                  full(k_cmp), full(v_cmp_t), full(ks), vspec, full(kw), vspec,
                  _resident(overlap_t.shape)],
        out_specs=pl.BlockSpec((1, tq, R * dh), lambda b, g, i: (b, i, g)),
        out_shape=jax.ShapeDtypeStruct((batch, seq, G * R * dh), F32),
        compiler_params=_cparams("parallel", "parallel", "parallel"),
        name="nsa_attn",
    )(q, qr, gates, k_cmp, v_cmp_t, ks, vst, kw, vwt, overlap_t)


def _moe_kernel(be_ref, nb_ref, x_ref, wgu_ref, wd_ref, *rest, first_block, n_own):
    o_ref, wgu_bf, wd_bf = rest[-3:]
    i = pl.program_id(0)
    blk = i + first_block
    ff = wd_ref.shape[1]
    used = (blk < nb_ref[0]) & (i < n_own)

    @pl.when(used & ((i == 0) | (be_ref[blk] != be_ref[jnp.maximum(blk - 1, 0)])))
    def _():
        wgu_bf[...] = wgu_ref[0].astype(BF16)
        wd_bf[...] = wd_ref[0].astype(BF16)

    @pl.when(used)
    def _():
        xb = x_ref[...]
        gt = _dot(xb, wgu_bf[:, :ff])
        up = _dot(xb, wgu_bf[:, ff:])
        h = (gt * _sigmoid(gt) * up).astype(BF16)
        o_ref[...] = _dot(h, wd_bf[...])

    @pl.when(jnp.logical_not(used))
    def _():
        o_ref[...] = jnp.zeros_like(o_ref)


def _moe_experts(xs_parts, block_e, n_used, w_gu, w_down):
    d = xs_parts[0].shape[1]
    n_rows = sum(xs.shape[0] for xs in xs_parts)
    ff = w_down.shape[1]
    ys, first = None, 0
    for xs in xs_parts:
        n_blocks = xs.shape[0] // MOE_ROWS
        own = lambda i, n=n_blocks: jnp.minimum(i, n - 1)
        in_specs = [pl.BlockSpec((MOE_ROWS, d), lambda i, be, nb, own=own: (own(i), 0)),
                    pl.BlockSpec((1, d, 2 * ff), lambda i, be, nb, own=own, first=first: (be[own(i) + first], 0, 0)),
                    pl.BlockSpec((1, ff, d), lambda i, be, nb, own=own, first=first: (be[own(i) + first], 0, 0))]
        args = [block_e, n_used, xs, w_gu, w_down]
        aliases = {}
        if ys is not None:
            in_specs.append(pl.BlockSpec(memory_space=pl.ANY))
            args.append(ys)
            aliases = {len(args) - 1: 0}
        ys = pl.pallas_call(
            functools.partial(_moe_kernel, first_block=first, n_own=n_blocks),
            grid_spec=pltpu.PrefetchScalarGridSpec(
                num_scalar_prefetch=2,
                grid=(n_rows // MOE_ROWS - first,),
                in_specs=in_specs,
                out_specs=pl.BlockSpec((MOE_ROWS, d), lambda i, be, nb, first=first: (i + first, 0)),
                scratch_shapes=[pltpu.VMEM((d, 2 * ff), BF16), pltpu.VMEM((ff, d), BF16)],
            ),
            out_shape=jax.ShapeDtypeStruct((n_rows, d), F32),
            input_output_aliases=aliases,
            compiler_params=_cparams("arbitrary"),
            name="moe_experts",
        )(*args)
        first += n_blocks
    return ys


def _moe(x2d, logits, w_gu, w_down):
    n_tok, d = x2d.shape
    n_assign = n_tok * TOP_K
    top_logit, top_e = lax.top_k(logits, TOP_K)
    weights = jax.nn.softmax(top_logit, axis=-1)
    flat_e = top_e.reshape(-1).astype(jnp.int32)
    onehot = (flat_e[:, None] == jnp.arange(N_EXPERTS, dtype=jnp.int32)[None, :]).astype(jnp.int32)
    running = jnp.cumsum(onehot, axis=0)
    counts = running[-1]
    padded = (counts + MOE_ROWS - 1) // MOE_ROWS * MOE_ROWS
    pad_end = jnp.cumsum(padded)
    pad_start = pad_end - padded
    grp_start = jnp.cumsum(counts) - counts
    pos = jnp.sum(onehot * (running - 1 + pad_start[None, :]), axis=1).reshape(n_tok, TOP_K)
    n_blocks = -(-n_assign // MOE_ROWS) + N_EXPERTS
    block_e = jnp.minimum(jnp.searchsorted(pad_end, jnp.arange(n_blocks) * MOE_ROWS, side='right'),
                          N_EXPERTS - 1).astype(jnp.int32)
    n_used = (pad_end[-1] // MOE_ROWS).astype(jnp.int32).reshape(1)
    order = jnp.argsort(flat_e)
    tok_sorted = (order // TOP_K).astype(jnp.int32)
    row_in_grp = (jnp.arange(n_blocks, dtype=jnp.int32) * MOE_ROWS - pad_start[block_e])[:, None] \
        + jnp.arange(MOE_ROWS, dtype=jnp.int32)[None, :]
    src = jnp.clip(grp_start[block_e][:, None] + row_in_grp, 0, n_assign - 1)
    row_tok = jnp.where(row_in_grp < counts[block_e][:, None], tok_sorted[src], 0).reshape(-1)
    half = n_blocks // 2 * MOE_ROWS
    ys = _moe_experts([x2d[row_tok[:half]], x2d[row_tok[half:]]], block_e, n_used, w_gu, w_down)
    return [ys[pos[:, s]] for s in range(TOP_K)], weights


def kernel(x, p, a_mu, a_w_rkv, a_w0, a_w1, a_w2, a_a0, a_a1, a_a2, a_g1, a_g2, a_k_k, a_k_a, a_r_k,
           a_gn_g, a_gn_b, a_w_o, b_w_kv, b_cmp_pos, b_phi_w1, b_phi_b1, b_phi_w2, b_w_in, b_w_o,
           f_w_gu, f_w_down, m_w_router, m_b_router, m_w_gu, m_w_down, ln_g, ln_b, ple_w, ple_gate_w,
           ple_gate_b):
    batch, seq, d = x.shape
    m = batch * seq
    x0 = x.reshape(m, d)
    p2d = p.reshape(DEPTH, m, PLE_DIM)
    bf = lambda w: w.astype(BF16)

    x1 = _rwkv_layer(x0, batch, seq, a_mu[0], bf(a_w_rkv[0]), a_w0[0], bf(a_w1[0]), bf(a_w2[0]), a_a0[0],
                     bf(a_a1[0]), bf(a_a2[0]), bf(a_g1[0]), bf(a_g2[0]), a_k_k[0], a_k_a[0], a_r_k[0],
                     a_gn_g[0], a_gn_b[0], bf(a_w_o[0]), ln_g[0, 0], ln_b[0, 0])
    x2 = _ffn_ple(x1, p2d, 0, bf(f_w_gu[0][:, :D_FF]), bf(f_w_gu[0][:, D_FF:]), bf(f_w_down[0]),
                  ln_g[0, 1], ln_b[0, 1], bf(ple_w[0]), bf(ple_gate_w[0]), ple_gate_b[0])

    slabs, ks, kw, vst, vwt = _nsa_kv(x2, b_w_kv, batch, seq)
    k_cmp, v_cmp = _nsa_compress(slabs, b_cmp_pos, b_phi_w1, b_phi_b1, b_phi_w2)
    q, qr, gates = _nsa_q(x2, b_w_in[0], batch, seq)
    attn = _nsa_attention(q, qr, gates, k_cmp, v_cmp, ks, vst, kw, vwt).reshape(m, d)
    x3, x3_bf, logits = _proj_ln_router(attn, x2, bf(b_w_o[0]), ln_g[1, 0], ln_b[1, 0], m_w_router[0],
                                        m_b_router[0])
    (y0, y1), route_w = _moe(x3_bf, logits[:, :N_EXPERTS], m_w_gu[0], m_w_down[0])
    out = _moe_combine_ln_ple(x3, y0, y1, route_w, p2d, 1, ln_g[1, 1], ln_b[1, 1], bf(ple_w[1]),
                              bf(ple_gate_w[1]), ple_gate_b[1])
    return out.reshape(batch, seq, d)
```

```python
import functools
import math

import jax
import jax.numpy as jnp
from jax import lax
from jax.experimental import pallas as pl
from jax.experimental.pallas import tpu as pltpu

BF16 = jnp.bfloat16
F32 = jnp.float32

LANES = 128
VMEM_LIMIT_BYTES = 56 * 1024 * 1024

D_MODEL = 1024
PLE_DIM = 256
RWKV_HEAD = 64
GN_EPS = 64e-5
N_HEADS = 16
HEAD_DIM = 64
N_KV_GROUPS = 4
Q_PER_GROUP = 4
N_BRANCH = 3
CMP_LEN = 32
CMP_STRIDE = 16
CMP_HIDDEN = 256
SEL_BLOCK = 64
N_SEL = 8
WINDOW = 512
ROPE_THETA = 10000.0
D_FF = 2816
N_EXPERTS = 8
TOP_K = 2
D_FF_EXPERT = 1408
MOE_ROWS = 256
LN_EPS = 1e-5
DEPTH = 2
DEEPNORM_ALPHA = (2.0 * DEPTH) ** 0.25
NEG_INF = -1e30
FORCE_SCORE = 1e4

SCAN_CHUNK = 64


def _cparams(*sem):
    return pltpu.CompilerParams(dimension_semantics=sem, vmem_limit_bytes=VMEM_LIMIT_BYTES)


def _resident(shape):
    nd = len(shape)
    return pl.BlockSpec(shape, lambda *_: (0,) * nd, pipeline_mode=pl.Buffered(1))


def _dot(a, b):
    return jnp.dot(a, b, preferred_element_type=F32)


def _dot_nt(a, b):
    return lax.dot_general(a, b, (((1,), (1,)), ((), ())), preferred_element_type=F32)


def _split2(x):
    hi = x.astype(BF16)
    lo = (x - hi.astype(F32)).astype(BF16)
    return hi, lo


def _split3(x):
    hi = x.astype(BF16)
    r1 = x - hi.astype(F32)
    mid = r1.astype(BF16)
    lo = (r1 - mid.astype(F32)).astype(BF16)
    return hi, mid, lo


def _layer_norm(y, g, b):
    mu = jnp.mean(y, axis=-1, keepdims=True)
    yc = y - mu
    var = jnp.mean(yc * yc, axis=-1, keepdims=True)
    return yc * lax.rsqrt(var + LN_EPS) * g + b


def _sigmoid(z):
    return 1.0 / (1.0 + jnp.exp(-z))


def _rwkv_project(first, x_ref, xh_ref, mu_ref, wrkv_ref, w0_ref, w1_ref, w2_ref, a0_ref, a1_ref,
                  a2_ref, g1_ref, g2_ref, r_ref, k_ref, v_ref, dl_ref, a_ref, g_ref):
    x = x_ref[...]
    prev_row = jnp.where(first, 0.0, xh_ref[0, 7:8, :])
    row = lax.broadcasted_iota(jnp.int32, x.shape, 0)
    x_shift = jnp.where(row == 0, prev_row, pltpu.roll(x, shift=1, axis=0))
    xx = x_shift - x

    def mix(i):
        return (x + xx * mu_ref[i:i + 1, :]).astype(BF16)

    r_ref[...] = _dot(mix(0), wrkv_ref[0])
    k_ref[...] = _dot(mix(1), wrkv_ref[1])
    v_ref[...] = _dot(mix(2), wrkv_ref[2])
    z = w0_ref[...] + _dot(jnp.tanh(_dot(mix(3), w1_ref[...])).astype(BF16), w2_ref[...])
    dl_ref[...] = -math.exp(-0.5) * _sigmoid(z)
    a_ref[...] = _sigmoid(a0_ref[...] + _dot(_dot(mix(4), a1_ref[...]).astype(BF16), a2_ref[...]))
    g_ref[...] = _dot(_sigmoid(_dot(mix(5), g1_ref[...])).astype(BF16), g2_ref[...])


def _rwkv_recurrence(first, r_ref, k_ref, v_ref, dl_ref, a_ref, g_ref, kk_ref, ka_ref, rk_ref, gng_ref,
                     gnb_ref, o_ref, state_ref, kkn_ref, cum_ref, *, n_pairs, n_chunks):
    C = SCAN_CHUNK
    N = RWKV_HEAD
    W = 2 * N
    tb = n_chunks * C

    @pl.when(first)
    def _():
        state_ref[...] = jnp.zeros_like(state_ref)

    tt = lax.broadcasted_iota(jnp.int32, (C, W), 0)
    ln = lax.broadcasted_iota(jnp.int32, (C, W), 1)
    ss = ln & (N - 1)
    strict_lower = ss < tt
    lower = ss <= tt
    eye = jnp.where(ss == tt, 1.0, 0.0)
    level_masks = []
    for lg in range(int(math.log2(C))):
        level_masks.append(((tt >> (lg + 1)) == (ss >> (lg + 1))) & ((tt >> lg) == (ss >> lg) + 1))
    rr = lax.broadcasted_iota(jnp.int32, (W, W), 0)
    cc = lax.broadcasted_iota(jnp.int32, (W, W), 1)
    same_head = (rr < N) == (cc < N)
    lane_head0 = ln < N
    WG = 2 * W
    lg_n = int(math.log2(N))
    ones_grp = jnp.where((lax.broadcasted_iota(jnp.int32, (WG, WG), 0) >> lg_n)
                         == (lax.broadcasted_iota(jnp.int32, (WG, WG), 1) >> lg_n), 1.0, 0.0).astype(BF16)
    col_groups = [slice(c, c + WG) for c in range(0, n_pairs * W, WG)]

    def bd(y):
        return jnp.concatenate([jnp.where(lane_head0, y, 0.0), jnp.where(lane_head0, 0.0, y)],
                               axis=0).astype(BF16)

    def pmm(xp, ybd):
        return _dot(xp.astype(BF16), ybd)

    def head_sum(xg):
        return _dot(xg.astype(BF16), ones_grp)

    tri_r = lax.broadcasted_iota(jnp.int32, (tb, tb), 0)
    tri_c = lax.broadcasted_iota(jnp.int32, (tb, tb), 1)
    lg_c = int(math.log2(C))
    tri_chunk = jnp.where((tri_c <= tri_r) & ((tri_c >> lg_c) == (tri_r >> lg_c)), 1.0, 0.0).astype(BF16)
    d_hi, d_mid, d_lo = _split3(dl_ref[...])
    cum_ref[...] = _dot(tri_chunk, d_hi) + _dot(tri_chunk, d_mid) + _dot(tri_chunk, d_lo)
    for cols in col_groups:
        kk_raw = k_ref[:, cols] * kk_ref[:, cols]
        kkn_ref[:, cols] = kk_raw / jnp.maximum(jnp.sqrt(head_sum(kk_raw * kk_raw)), 1e-12)

    def chunk_load(ci, pi):
        rows = pl.ds(pl.multiple_of(ci * C, C), C)
        cols = slice(pi * W, (pi + 1) * W)
        return (r_ref[rows, cols], k_ref[rows, cols], v_ref[rows, cols], dl_ref[rows, cols],
                a_ref[rows, cols], kkn_ref[rows, cols], cum_ref[rows, cols], state_ref[pi])

    def chunk_compute(pi, r, k, v, dl, a, kk, cum, m0):
        cols = slice(pi * W, (pi + 1) * W)
        k2 = k * (1.0 + (a - 1.0) * ka_ref[:, cols])
        b = a * kk
        p_incl = jnp.exp(cum)
        p_excl = jnp.exp(cum - dl)
        p_inv = jnp.exp(-cum)
        kq = kk * p_excl
        rq = r * p_incl
        bk = b * p_inv
        kkd = k2 * p_inv

        bkt = jnp.transpose(jnp.concatenate([b, k2], axis=0))
        cumt = jnp.transpose(jnp.concatenate([cum, cum], axis=0))
        last_col = cumt[:, C - 1:C]
        lhs_t = (bkt * jnp.exp(last_col - cumt)).astype(BF16)
        m_decayed = m0 * jnp.exp(last_col)

        lhs = jnp.concatenate([kq, rq], axis=0).astype(BF16)
        yield
        abg = _dot_nt(lhs, jnp.concatenate([bd(bk), bd(kkd)], axis=0))
        a_m = jnp.where(strict_lower, abg[:C, :W], 0.0)
        aq_m = jnp.where(lower, abg[C:, :W], 0.0)
        g_m = jnp.where(strict_lower, abg[:C, W:], 0.0)
        gq_m = jnp.where(lower, abg[C:, W:], 0.0)

        tinv = eye - jnp.where(level_masks[0], a_m, 0.0)
        yield
        gv = pmm(jnp.concatenate([g_m, gq_m], axis=0), bd(v))
        for lm in level_masks[1:]:
            w_ = pmm(jnp.where(lm, a_m, 0.0), bd(tinv))
            yield
            tinv = tinv - pmm(tinv, bd(w_))
            yield

        kq_y = pmm(tinv, jnp.concatenate([bd(kq), bd(gv[:C])], axis=1))
        kq1 = kq_y[:, :W]
        y = kq_y[:, W:]
        yield
        aq_ky = pmm(aq_m, jnp.concatenate([bd(kq1), bd(y)], axis=1))
        rq1 = rq - aq_ky[:, :W]
        o_loc = gv[C:] - aq_ky[:, W:]
        yield
        st = _dot(jnp.concatenate([rq1, kq1], axis=0).astype(BF16), m0.astype(BF16))
        o = st[:C] + o_loc
        z = -(st[C:] + y)
        yield
        upd = _dot(lhs_t, jnp.concatenate([z, v], axis=0).astype(BF16))
        return o, m_decayed + jnp.where(same_head, upd, 0.0)

    def interleave(gens):
        results = [None] * len(gens)
        live = list(range(len(gens)))
        while live:
            for i in list(live):
                try:
                    next(gens[i])
                except StopIteration as done:
                    results[i] = done.value
                    live.remove(i)
        return results

    def body(ci, carry):
        loaded = [chunk_load(ci, pi) for pi in range(n_pairs)]
        results = interleave([chunk_compute(pi, *loaded[pi]) for pi in range(n_pairs)])
        rows = pl.ds(pl.multiple_of(ci * C, C), C)
        for pi, (o, m_new) in enumerate(results):
            o_ref[rows, pi * W:(pi + 1) * W] = o
            state_ref[pi] = m_new
        return carry

    lax.fori_loop(0, n_chunks, body, 0)

    for cols in col_groups:
        o = o_ref[:, cols]
        r = r_ref[:, cols]
        v = v_ref[:, cols]
        k2 = k_ref[:, cols] * (1.0 + (a_ref[:, cols] - 1.0) * ka_ref[:, cols])
        oc = o - head_sum(o) * (1.0 / N)
        var_o = head_sum(oc * oc) * (1.0 / N)
        out = oc * lax.rsqrt(var_o + GN_EPS) * gng_ref[:, cols] + gnb_ref[:, cols]
        out = out + head_sum(r * k2 * rk_ref[:, cols]) * v
        o_ref[:, cols] = out * g_ref[:, cols]


def _rwkv_layer_kernel(x_ref, xh_ref, mu_ref, wrkv_ref, w0_ref, w1_ref, w2_ref, a0_ref, a1_ref, a2_ref,
                       g1_ref, g2_ref, kk_ref, ka_ref, rk_ref, gng_ref, gnb_ref, wo_ref, lng_ref, lnb_ref,
                       o_ref, state_ref, r_s, k_s, v_s, dl_s, a_s, g_s, kkn_s, cum_s, y_s, *, n_chunks):
    first = pl.program_id(1) == 0
    _rwkv_project(first, x_ref, xh_ref, mu_ref, wrkv_ref, w0_ref, w1_ref, w2_ref, a0_ref, a1_ref, a2_ref,
                  g1_ref, g2_ref, r_s, k_s, v_s, dl_s, a_s, g_s)
    _rwkv_recurrence(first, r_s, k_s, v_s, dl_s, a_s, g_s, kk_ref, ka_ref, rk_ref, gng_ref, gnb_ref, y_s,
                     state_ref, kkn_s, cum_s, n_pairs=x_ref.shape[1] // (2 * RWKV_HEAD), n_chunks=n_chunks)
    mix = _dot(y_s[...].astype(BF16), wo_ref[...])
    o_ref[...] = _layer_norm(DEEPNORM_ALPHA * x_ref[...] + mix, lng_ref[...], lnb_ref[...])


def _rwkv_layer(x2d, batch, seq, mu, w_rkv, w0, w1, w2, a0, a1, a2, g1, g2, k_k, k_a, r_k, gn_g, gn_b,
                w_o, ln_g, ln_b, tb=256):
    m, d = x2d.shape
    nt = seq // tb
    xh = x2d.reshape(m // 8, 8, d)
    vec = lambda z: z.reshape(1, d)
    blk = pl.BlockSpec((tb, d), lambda b, t: (b * nt + t, 0))
    weights = [mu, w_rkv, vec(w0), w1, w2, vec(a0), a1, a2, g1, g2, vec(k_k), vec(k_a), vec(r_k),
               vec(gn_g), vec(gn_b), w_o, vec(ln_g), vec(ln_b)]
    n_pairs = d // (2 * RWKV_HEAD)
    return pl.pallas_call(
        functools.partial(_rwkv_layer_kernel, n_chunks=tb // SCAN_CHUNK),
        grid=(batch, nt),
        in_specs=[blk, pl.BlockSpec((1, 8, d), lambda b, t: (jnp.maximum((b * nt + t) * (tb // 8) - 1, 0), 0, 0))]
        + [_resident(w.shape) for w in weights],
        out_specs=blk,
        out_shape=jax.ShapeDtypeStruct((m, d), F32),
        scratch_shapes=[pltpu.VMEM((n_pairs, 2 * RWKV_HEAD, 2 * RWKV_HEAD), F32)]
        + [pltpu.VMEM((tb, d), F32)] * 9,
        compiler_params=_cparams("parallel", "arbitrary"),
        name="rwkv_layer",
    )(x2d, xh, *weights)


def _proj_ln_router_kernel(y_ref, x_ref, w_ref, lng_ref, lnb_ref, wr_ref, br_ref, o_ref, ob_ref, lg_ref):
    mix = _dot(y_ref[...], w_ref[...])
    xn = _layer_norm(DEEPNORM_ALPHA * x_ref[...] + mix, lng_ref[...], lnb_ref[...])
    o_ref[...] = xn
    xh, xl = _split2(xn)
    ob_ref[...] = xh
    lg = _dot(xh, wr_ref[0]) + _dot(xh, wr_ref[1]) + _dot(xl, wr_ref[0]) + br_ref[...]
    lane = lax.broadcasted_iota(jnp.int32, lg.shape, 1)
    lg = jnp.where(lane < N_EXPERTS, lg, -jnp.inf)
    picks = []
    for _ in range(TOP_K):
        best = jnp.max(lg, axis=-1, keepdims=True)
        idx = jnp.min(jnp.where(lg == best, lane, LANES), axis=-1, keepdims=True)
        picks.append((best, idx))
        lg = jnp.where(lane == idx, -jnp.inf, lg)
    (l0, e0), (l1, e1) = picks
    ex = jnp.exp(l1 - l0)
    out = jnp.where(lane == 0, e0.astype(F32), 0.0)
    out = jnp.where(lane == 1, e1.astype(F32), out)
    out = jnp.where(lane == 2, 1.0 / (1.0 + ex), out)
    lg_ref[...] = jnp.where(lane == 3, ex / (1.0 + ex), out)


def _proj_ln_router(y, x2d, w, ln_g, ln_b, w_router, b_router, tm=512):
    m, d = x2d.shape
    row = lambda i: (i, 0)
    wr = jnp.zeros((d, LANES), F32).at[:, :N_EXPERTS].set(w_router)
    wr3 = jnp.stack(_split2(wr))
    br =jnp.zeros((1, LANES), F32).at[0, :N_EXPERTS].set(b_router)
    return pl.pallas_call(
        _proj_ln_router_kernel,
        grid=(m // tm,),
        in_specs=[pl.BlockSpec((tm, y.shape[1]), row), pl.BlockSpec((tm, d), row), _resident(w.shape),
                  _resident((1, d)), _resident((1, d)), _resident(wr3.shape), _resident(br.shape)],
        out_specs=[pl.BlockSpec((tm, d), row), pl.BlockSpec((tm, d), row), pl.BlockSpec((tm, LANES), row)],
        out_shape=[jax.ShapeDtypeStruct((m, d), F32), jax.ShapeDtypeStruct((m, d), BF16),
                   jax.ShapeDtypeStruct((m, LANES), F32)],
        compiler_params=_cparams("parallel"),
        name="proj_ln_router",
    )(y, x2d, w, ln_g.reshape(1, d), ln_b.reshape(1, d), wr3, br)


def _ple(xn, p, plew_ref, gw_ref, gb_ref):
    gate = _sigmoid(_dot(xn.astype(BF16), gw_ref[...]) + gb_ref[...])
    return xn + _dot(p.astype(BF16), plew_ref[...]) * gate


def _ffn_ple_kernel(x_ref, p_ref, wg_ref, wu_ref, wd_ref, lng_ref, lnb_ref, plew_ref, gw_ref, gb_ref,
                    o_ref, *, ff_chunk):
    x = x_ref[...]
    xb = x.astype(BF16)
    acc = jnp.zeros_like(x)
    for c in range(wg_ref.shape[1] // ff_chunk):
        sl = slice(c * ff_chunk, (c + 1) * ff_chunk)
        gt = _dot(xb, wg_ref[:, sl])
        up = _dot(xb, wu_ref[:, sl])
        h = (gt * _sigmoid(gt) * up).astype(BF16)
        acc = acc + _dot(h, wd_ref[sl, :])
    xn = _layer_norm(DEEPNORM_ALPHA * x + acc, lng_ref[...], lnb_ref[...])
    o_ref[...] = _ple(xn, p_ref[0], plew_ref, gw_ref, gb_ref)


def _ple_block(p3d, layer, tm):
    return pl.BlockSpec((1, tm, p3d.shape[2]), lambda i: (layer, i, 0))


def _ffn_ple(x2d, p3d, layer, wg, wu, wd, ln_g, ln_b, ple_w, gate_w, gate_b, tm=512, ff_chunk=1408):
    m, d = x2d.shape
    row = lambda i: (i, 0)
    return pl.pallas_call(
        functools.partial(_ffn_ple_kernel, ff_chunk=ff_chunk),
        grid=(m // tm,),
        in_specs=[pl.BlockSpec((tm, d), row), _ple_block(p3d, layer, tm),
                  _resident(wg.shape), _resident(wu.shape), _resident(wd.shape),
                  _resident((1, d)), _resident((1, d)), _resident(ple_w.shape), _resident(gate_w.shape),
                  _resident((1, d))],
        out_specs=pl.BlockSpec((tm, d), row),
        out_shape=jax.ShapeDtypeStruct((m, d), F32),
        compiler_params=_cparams("parallel"),
        name="ffn_ple",
    )(x2d, p3d, wg, wu, wd, ln_g.reshape(1, d), ln_b.reshape(1, d), ple_w, gate_w, gate_b.reshape(1, d))


def _moe_combine_ln_ple_kernel(x_ref, y0_ref, y1_ref, rw_ref, p_ref, lng_ref, lnb_ref, plew_ref, gw_ref,
                               gb_ref, o_ref):
    rw = rw_ref[...]
    ffn = y0_ref[...] * rw[:, 0:1] + y1_ref[...] * rw[:, 1:2]
    xn = _layer_norm(DEEPNORM_ALPHA * x_ref[...] + ffn, lng_ref[...], lnb_ref[...])
    o_ref[...] = _ple(xn, p_ref[0], plew_ref, gw_ref, gb_ref)


def _moe_combine_ln_ple(x2d, y0, y1, route_w, p3d, layer, ln_g, ln_b, ple_w, gate_w, gate_b, tm=512):
    m, d = x2d.shape
    row = lambda i: (i, 0)
    tile = pl.BlockSpec((tm, d), row)
    return pl.pallas_call(
        _moe_combine_ln_ple_kernel,
        grid=(m // tm,),
        in_specs=[tile, tile, tile, pl.BlockSpec((tm, TOP_K), row), _ple_block(p3d, layer, tm),
                  _resident((1, d)), _resident((1, d)), _resident(ple_w.shape), _resident(gate_w.shape),
                  _resident((1, d))],
        out_specs=tile,
        out_shape=jax.ShapeDtypeStruct((m, d), F32),
        compiler_params=_cparams("parallel"),
        name="moe_combine_ln_ple",
    )(x2d, y0, y1, route_w, p3d, ln_g.reshape(1, d), ln_b.reshape(1, d), ple_w, gate_w, gate_b.reshape(1, d))


def _swap_half_cols(w):
    k, n = w.shape
    return w.reshape(k, n // HEAD_DIM, 2, HEAD_DIM // 2)[:, :, ::-1, :].reshape(k, n)


def _rope_tables(seq):
    half = HEAD_DIM // 2
    inv = ROPE_THETA ** (-jnp.arange(half, dtype=F32) / half)
    ang = jnp.arange(seq, dtype=F32)[:, None] * inv[None, :]
    cos, sin = jnp.cos(ang), jnp.sin(ang)
    return jnp.concatenate([cos, cos], -1), jnp.concatenate([-sin, sin], -1)


def _nsa_kv_kernel(x_ref, wk_ref, wvt_ref, cos_ref, sin_ref, slab_ref, ks_ref, kw_ref, vst_ref, vwt_ref,
                   cz_s):
    G, dh = N_KV_GROUPS, HEAD_DIM
    gw = G * dh
    xb = x_ref[...].astype(BF16)
    res = _dot(xb, wk_ref[...])
    res_t = _dot_nt(wvt_ref[...], xb)
    tm = res.shape[0]
    n_slab = tm // CMP_STRIDE
    for c in range(cz_s.shape[0]):
        cz_s[c] = res[:, c * LANES:(c + 1) * LANES]
        rows = [cz_s[c, pl.ds(l, n_slab, stride=CMP_STRIDE), :] for l in range(CMP_STRIDE)]
        for half in range(LANES // dh):
            jg = c * (LANES // dh) + half
            slab_ref[jg // G, 0, jg % G] = jnp.concatenate(
                [r[:, half * dh:(half + 1) * dh] for r in rows], axis=1).astype(BF16)
    cos = cos_ref[...]
    sin = sin_ref[...]
    ks = res[:, 2 * gw:3 * gw] * cos + res[:, 3 * gw:4 * gw] * sin
    kw = res[:, 4 * gw:5 * gw] * cos + res[:, 5 * gw:6 * gw] * sin
    t_glob = pl.program_id(1) * tm + lax.broadcasted_iota(jnp.int32, (tm, dh), 0)
    onehot = jnp.where(lax.broadcasted_iota(jnp.int32, (tm, dh), 1) == t_glob // SEL_BLOCK, 1.0, 0.0)
    for g in range(G):
        sl = slice(g * dh, (g + 1) * dh)
        ks_ref[0, g] = jnp.concatenate([ks[:, sl], onehot], axis=1).astype(BF16)
        kw_ref[0, g] = kw[:, sl].astype(BF16)
    vst_ref[0] = res_t[:gw].astype(BF16)
    vwt_ref[0] = res_t[gw:].astype(BF16)


def _nsa_kv(x2d, w_kv, batch, seq, tm=256):
    m, d = x2d.shape
    G, dh = N_KV_GROUPS, HEAD_DIM
    gw = G * dh
    parts = [w_kv[:, j * gw:(j + 1) * gw] for j in range(6)]
    w_k = jnp.concatenate([parts[0], parts[1], parts[2], _swap_half_cols(parts[2]),
                           parts[4], _swap_half_cols(parts[4])], axis=1).astype(BF16)
    w_vt = jnp.concatenate([parts[3], parts[5]], axis=1).T.astype(BF16)
    cos, sin = _rope_tables(seq)
    cos = jnp.tile(cos, (1, G))
    sin = jnp.tile(sin, (1, G))
    nt = seq // tm
    o64 = jax.ShapeDtypeStruct((batch, G, seq, dh), BF16)
    o128 = jax.ShapeDtypeStruct((batch, G, seq, 2 * dh), BF16)
    ot = jax.ShapeDtypeStruct((batch, gw, seq), BF16)
    b64 = pl.BlockSpec((1, G, tm, dh), lambda b, t: (b, 0, t, 0))
    b128 = pl.BlockSpec((1, G, tm, 2 * dh), lambda b, t: (b, 0, t, 0))
    bt = pl.BlockSpec((1, gw, tm), lambda b, t: (b, 0, t))
    slab_w = CMP_STRIDE * dh
    oslab = jax.ShapeDtypeStruct((2, batch, G, seq // CMP_STRIDE, slab_w), BF16)
    bslab = pl.BlockSpec((2, 1, G, tm // CMP_STRIDE, slab_w), lambda b, t: (0, b, 0, t, 0))
    return pl.pallas_call(
        _nsa_kv_kernel,
        grid=(batch, nt),
        in_specs=[pl.BlockSpec((tm, d), lambda b, t: (b * nt + t, 0)), _resident(w_k.shape),
                  _resident(w_vt.shape),
                  pl.BlockSpec((tm, gw), lambda b, t: (t, 0)), pl.BlockSpec((tm, gw), lambda b, t: (t, 0))],
        out_specs=[bslab, b128, b64, bt, bt],
        out_shape=[oslab, o128, o64, ot, ot],
        scratch_shapes=[pltpu.VMEM((2 * gw // LANES, tm, LANES), F32)],
        compiler_params=_cparams("parallel", "parallel"),
        name="nsa_kv",
    )(x2d, w_k, w_vt, cos, sin)


def _nsa_cmp_kernel(z_ref, pos_ref, w1_ref, b1_ref, w2_ref, o_ref, *, slabs_per_seq):
    half = w1_ref.shape[1] // 2
    z = z_ref[0]
    tm = z.shape[0]
    first = _dot(z, w1_ref[0, :half, :])
    second = _dot(z, w1_ref[0, half:, :])
    const = _dot(pos_ref[0], w1_ref[0])[0:1, :] + b1_ref[0]
    hid = first + pltpu.roll(second, shift=tm - 1, axis=0) + const
    act = 0.5 * hid * (1.0 + jnp.tanh(math.sqrt(2.0 / math.pi) * (hid + 0.044715 * hid * hid * hid)))
    out = _dot(act.astype(BF16), w2_ref[0])
    row = lax.broadcasted_iota(jnp.int32, out.shape, 0)
    o_ref[0] = jnp.where(row % slabs_per_seq == slabs_per_seq - 1, 0.0, out).astype(BF16)


def _nsa_compress(slabs, cmp_pos, phi_w1, phi_b1, phi_w2, tm=512):
    _, batch, G, n_slab, slab = slabs.shape
    dh = slab // CMP_STRIDE
    seq = n_slab * CMP_STRIDE
    rows = batch * G * n_slab
    tm = min(tm, rows)
    z = slabs.reshape(2, rows, slab)
    pos = jnp.broadcast_to(cmp_pos.reshape(2, 1, CMP_LEN * dh), (2, 8, CMP_LEN * dh)).astype(BF16)
    out = pl.pallas_call(
        functools.partial(_nsa_cmp_kernel, slabs_per_seq=seq // CMP_STRIDE),
        grid=(2, rows // tm),
        in_specs=[pl.BlockSpec((1, tm, slab), lambda j, i: (j, i, 0)),
                  pl.BlockSpec((1, 8, CMP_LEN * dh), lambda j, i: (j, 0, 0)),
                  pl.BlockSpec((1, CMP_LEN * dh, CMP_HIDDEN), lambda j, i: (j, 0, 0)),
                  pl.BlockSpec((1, 1, CMP_HIDDEN), lambda j, i: (j, 0, 0)),
                  pl.BlockSpec((1, CMP_HIDDEN, dh), lambda j, i: (j, 0, 0))],
        out_specs=pl.BlockSpec((1, tm, dh), lambda j, i: (j, i, 0)),
        out_shape=jax.ShapeDtypeStruct((2, rows, dh), BF16),
        compiler_params=_cparams("parallel", "parallel"),
        name="nsa_compress",
    )(z, pos, phi_w1.astype(BF16), phi_b1.reshape(2, 1, CMP_HIDDEN), phi_w2.astype(BF16))
    return out[0].reshape(batch, G, n_slab, dh), out[1].reshape(batch, G, n_slab, dh)


GATE_ROWS = 16


def _nsa_q_kernel(x_ref, wt_ref, cos_ref, sin_ref, q_ref, qr_ref, gate_ref):
    dh = HEAD_DIM
    hw = N_HEADS * dh
    scale = HEAD_DIM ** -0.5 * math.log2(math.e)
    res_t = _dot_nt(wt_ref[...], x_ref[...].astype(BF16))
    tm = res_t.shape[1]
    q = res_t[:hw].reshape(N_HEADS, dh, tm)
    q_sw = jnp.concatenate([q[:, dh // 2:], q[:, :dh // 2]], axis=1)
    qr = q * cos_ref[...][None] + q_sw * sin_ref[...][None]
    q_ref[0] = (q * scale).reshape(hw, tm).astype(BF16)
    qr_ref[0] = (qr * scale).reshape(hw, tm).astype(BF16)
    gate_ref[0] = _sigmoid(res_t[hw:])


def _nsa_q(x2d, w_in, batch, seq, tm=512):
    m, d = x2d.shape
    G, R, dh = N_KV_GROUPS, Q_PER_GROUP, HEAD_DIM
    hw = N_HEADS * dh
    w_gate = w_in[:, hw:].reshape(d, G, R * N_BRANCH)
    w_gate = jnp.pad(w_gate, ((0, 0), (0, 0), (0, GATE_ROWS - R * N_BRANCH))).reshape(d, G * GATE_ROWS)
    wt = jnp.concatenate([w_in[:, :hw], w_gate], axis=1).T.astype(BF16)
    cos, sin = _rope_tables(seq)
    nt = seq // tm
    oq = jax.ShapeDtypeStruct((batch, hw, seq), BF16)
    bq = pl.BlockSpec((1, hw, tm), lambda b, t: (b, 0, t))
    tab = pl.BlockSpec((dh, tm), lambda b, t: (0, t))
    return pl.pallas_call(
        _nsa_q_kernel,
        grid=(batch, nt),
        in_specs=[pl.BlockSpec((tm, d), lambda b, t: (b * nt + t, 0)), _resident(wt.shape), tab, tab],
        out_specs=[bq, bq, pl.BlockSpec((1, G * GATE_ROWS, tm), lambda b, t: (b, 0, t))],
        out_shape=[oq, oq, jax.ShapeDtypeStruct((batch, G * GATE_ROWS, seq), F32)],
        compiler_params=_cparams("parallel", "parallel"),
        name="nsa_q",
    )(x2d, wt, cos.T, sin.T)


def _nsa_attn_kernel(q_ref, qr_ref, gate_ref, kc_ref, vct_ref, ks_ref, vst_ref, kw_ref, vwt_ref, ovlt_ref,
                     o_ref, *, tq, tk, n_blk):
    R, dh = Q_PER_GROUP, HEAD_DIM
    cols = R * tq
    qi = pl.program_id(2)
    t0 = qi * tq
    n_blk_pad = ovlt_ref.shape[0]

    def heads_to_lanes(x):
        return jnp.concatenate([x[r * dh:(r + 1) * dh] for r in range(R)], axis=1)

    q = heads_to_lanes(q_ref[0])
    qr = heads_to_lanes(qr_ref[0])

    def col_pos(shape):
        return t0 + (lax.broadcasted_iota(jnp.int32, shape, 1) & (tq - 1))

    def key_tile(j):
        return pl.ds(pl.multiple_of(j * tk, tk), tk)

    ct = tq
    col_tiles = [slice(c, c + ct) for c in range(0, cols, ct)]
    ones_rows = 16
    ones_blk = jnp.ones((ones_rows, tk), BF16)

    def tile_scores(k_ref, q_op, tiles):
        return [[_dot(k_ref[0, 0, key_tile(j), :], q_op[:, cs]) for j, _ in tiles] for cs in col_tiles]

    def attend(carry, sc, vt_ref, tiles):
        m_i, acc = carry
        outs = []
        for cs, sc_c in zip(col_tiles, sc):
            masked = [s_j if mask_fn is None else jnp.where(mask_fn(k_off, q_off), s_j, NEG_INF)
                      for s_j, (_, mask_fn) in zip(sc_c, tiles)]
            m_new = m_i[:, cs]
            for s_j in masked:
                m_new = jnp.maximum(m_new, jnp.max(s_j, axis=0, keepdims=True))
            acc_new = jnp.exp2(m_i[:, cs] - m_new) * acc[:, cs]
            for s_j, (j, _) in zip(masked, tiles):
                v_ext = jnp.concatenate([vt_ref[0, :, key_tile(j)], ones_blk], axis=0)
                acc_new = acc_new + _dot(v_ext, jnp.exp2(s_j - m_new).astype(BF16))
            outs.append((m_new, acc_new))
        return tuple(jnp.concatenate([o[i] for o in outs], axis=1) for i in range(2))

    def normalised(acc):
        return acc[:dh] / acc[dh:dh + 1]

    init = (jnp.full((1, cols), NEG_INF, F32), jnp.zeros((dh + ones_rows, cols), F32))

    k_off = lax.broadcasted_iota(jnp.int32, (tk, ct), 0)
    q_off = lax.broadcasted_iota(jnp.int32, (tk, ct), 1)
    causal = lambda k, q_: k <= q_
    present = lambda cond: (lambda k, q_: (k >= 0) & cond)

    win_tiles = [(jnp.maximum(qi - 2, 0), lambda k, q_: (q_ < k) & (qi >= 2)),
                 (jnp.maximum(qi - 1, 0), present(qi >= 1)),
                 (qi, causal)]
    sc_win = tile_scores(kw_ref, qr, win_tiles)

    s = _dot(kc_ref[0, 0], q)
    cmp_end = lax.broadcasted_iota(jnp.int32, s.shape, 0) * CMP_STRIDE + (CMP_LEN - 1)
    valid = cmp_end <= col_pos(s.shape)
    sm = jnp.where(valid, s, NEG_INF)
    e = jnp.where(valid, jnp.exp2(sm - jnp.max(sm, axis=0, keepdims=True)), 0.0)
    l = jnp.sum(e, axis=0, keepdims=True)
    p_cmp = e / jnp.where(l > 0.0, l, 1.0)
    o_cmp = _dot(vct_ref[0, 0], p_cmp.astype(BF16))

    p_sum = p_cmp[:, 0:tq]
    for r in range(1, R):
        p_sum = p_sum + p_cmp[:, r * tq:(r + 1) * tq]
    hi, lo = _split2(p_sum)
    imp = _dot(ovlt_ref[...], hi) + _dot(ovlt_ref[...], lo)
    blk = lax.broadcasted_iota(jnp.int32, imp.shape, 0)
    t_q = t0 + lax.broadcasted_iota(jnp.int32, imp.shape, 1)
    cur = t_q // SEL_BLOCK
    forced = (blk == 0) | (blk == cur) | (blk == cur - 1)
    score = jnp.where(forced, FORCE_SCORE, jnp.where(blk * SEL_BLOCK <= t_q, imp, NEG_INF))
    score = jnp.where(blk < n_blk, score, -jnp.inf)
    selected = blk >= n_blk
    for _ in range(N_SEL):
        best = jnp.max(score, axis=0, keepdims=True)
        first = jnp.min(jnp.where(score == best, blk, 2 * LANES), axis=0, keepdims=True)
        pick = blk == first
        selected = selected | pick
        score = jnp.where(pick, -jnp.inf, score)
    bias = jnp.where(selected, 0.0, NEG_INF)
    qa = jnp.concatenate([qr, jnp.concatenate([bias] * R, axis=1).astype(BF16),
                          jnp.zeros((dh - n_blk_pad, cols), BF16)], axis=0)

    o_win = normalised(attend(init, sc_win, vwt_ref, win_tiles)[1])

    n_pairs = qi // 2

    def pair_body(i, carry):
        tiles = [(2 * i, None), (2 * i + 1, None)]
        return attend(carry, tile_scores(ks_ref, qa, tiles), vst_ref, tiles)

    carry = lax.fori_loop(0, n_pairs, pair_body, init)
    j_odd = 2 * n_pairs
    tail_tiles = [(j_odd, present(j_odd < qi)), (qi, causal)]
    o_slc = normalised(attend(carry, tile_scores(ks_ref, qa, tail_tiles), vst_ref, tail_tiles)[1])

    gates = gate_ref[0]
    outs = []
    for r in range(R):
        cs = slice(r * tq, (r + 1) * tq)
        outs.append(gates[3 * r:3 * r + 1] * o_cmp[:, cs] + gates[3 * r + 1:3 * r + 2] * o_slc[:, cs]
                    + gates[3 * r + 2:3 * r + 3] * o_win[:, cs])
    halves = [jnp.transpose(jnp.concatenate(outs[i:i + 2], axis=0)) for i in range(0, R, 2)]
    o_ref[0] = jnp.concatenate(halves, axis=1).astype(o_ref.dtype)


SEL_ROWS = 32


def _nsa_attention(q, qr, gates, k_cmp, v_cmp, ks, vst, kw, vwt, tq=256, tk=256):
    batch, _, seq = q.shape
    G, R, dh = N_KV_GROUPS, Q_PER_GROUP, HEAD_DIM
    n_slab = k_cmp.shape[2]
    n_blk = seq // SEL_BLOCK
    assert n_blk <= SEL_ROWS
    assert tq == tk and WINDOW == 2 * tk, "the window branch is written as exactly three key tiles"
    cmp_start = jnp.arange(n_slab) * CMP_STRIDE
    blk_start = jnp.arange(SEL_ROWS) * SEL_BLOCK
    overlap_t = ((cmp_start[None, :] < blk_start[:, None] + SEL_BLOCK)
                 & (cmp_start[None, :] + CMP_LEN - 1 >= blk_start[:, None])
                 & (jnp.arange(SEL_ROWS)[:, None] < n_blk)
                 & (jnp.arange(n_slab)[None, :] < (seq - CMP_LEN) // CMP_STRIDE + 1)).astype(BF16)
    v_cmp_t = jnp.swapaxes(v_cmp, 2, 3)
    qspec = pl.BlockSpec((1, R * dh, tq), lambda b, g, i: (b, g, i))
    full = lambda a: pl.BlockSpec((1, 1) + a.shape[2:], lambda b, g, i: (b, g, 0, 0))
    vspec = pl.BlockSpec((1, dh, seq), lambda b, g, i: (b, g, 0))
    return pl.pallas_call(
        functools.partial(_nsa_attn_kernel, tq=tq, tk=tk, n_blk=n_blk),
        grid=(batch, G, seq // tq),
        in_specs=[qspec, qspec, pl.BlockSpec((1, GATE_ROWS, tq), lambda b, g, i: (b, g, i)),
                  full(k_cmp), full(v_cmp_t), full(ks), vspec, full(kw), vspec,
                  _resident(overlap_t.shape)],
        out_specs=pl.BlockSpec((1, tq, R * dh), lambda b, g, i: (b, i, g)),
        out_shape=jax.ShapeDtypeStruct((batch, seq, G * R * dh), BF16),
        compiler_params=_cparams("parallel", "parallel", "parallel"),
        name="nsa_attn",
    )(q, qr, gates, k_cmp, v_cmp_t, ks, vst, kw, vwt, overlap_t)


def _moe_kernel(be_ref, nb_ref, x_ref, wgu_ref, wd_ref, *rest, first_block, n_own):
    o_ref, wgu_bf, wd_bf = rest[-3:]
    i = pl.program_id(0)
    blk = i + first_block
    ff = wd_ref.shape[1]
    used = (blk < nb_ref[0]) & (i < n_own)

    @pl.when(used & ((i == 0) | (be_ref[blk] != be_ref[jnp.maximum(blk - 1, 0)])))
    def _():
        wgu_bf[...] = wgu_ref[0].astype(BF16)
        wd_bf[...] = wd_ref[0].astype(BF16)

    @pl.when(used)
    def _():
        xb = x_ref[...]
        gt = _dot(xb, wgu_bf[:, :ff])
        up = _dot(xb, wgu_bf[:, ff:])
        h = (gt * _sigmoid(gt) * up).astype(BF16)
        o_ref[...] = _dot(h, wd_bf[...])

    @pl.when(jnp.logical_not(used))
    def _():
        o_ref[...] = jnp.zeros_like(o_ref)


def _moe_experts(xs_parts, block_e, n_used, w_gu, w_down):
    d = xs_parts[0].shape[1]
    n_rows = sum(xs.shape[0] for xs in xs_parts)
    ff = w_down.shape[1]
    ys, first = None, 0
    for xs in xs_parts:
        n_blocks = xs.shape[0] // MOE_ROWS
        own = lambda i, n=n_blocks: jnp.minimum(i, n - 1)
        in_specs = [pl.BlockSpec((MOE_ROWS, d), lambda i, be, nb, own=own: (own(i), 0)),
                    pl.BlockSpec((1, d, 2 * ff), lambda i, be, nb, own=own, first=first: (be[own(i) + first], 0, 0)),
                    pl.BlockSpec((1, ff, d), lambda i, be, nb, own=own, first=first: (be[own(i) + first], 0, 0))]
        args = [block_e, n_used, xs, w_gu, w_down]
        aliases = {}
        if ys is not None:
            in_specs.append(pl.BlockSpec(memory_space=pl.ANY))
            args.append(ys)
            aliases = {len(args) - 1: 0}
        ys = pl.pallas_call(
            functools.partial(_moe_kernel, first_block=first, n_own=n_blocks),
            grid_spec=pltpu.PrefetchScalarGridSpec(
                num_scalar_prefetch=2,
                grid=(n_rows // MOE_ROWS - first,),
                in_specs=in_specs,
                out_specs=pl.BlockSpec((MOE_ROWS, d), lambda i, be, nb, first=first: (i + first, 0)),
                scratch_shapes=[pltpu.VMEM((d, 2 * ff), BF16), pltpu.VMEM((ff, d), BF16)],
            ),
            out_shape=jax.ShapeDtypeStruct((n_rows, d), F32),
            input_output_aliases=aliases,
            compiler_params=_cparams("arbitrary"),
            name="moe_experts",
        )(*args)
        first += n_blocks
    return ys


def _moe(x2d, route, w_gu, w_down):
    n_tok, d = x2d.shape
    n_assign = n_tok * TOP_K
    weights = route[:, TOP_K:2 * TOP_K]
    flat_e = route[:, :TOP_K].astype(jnp.int32).reshape(-1)
    onehot = (flat_e[:, None] == jnp.arange(N_EXPERTS, dtype=jnp.int32)[None, :]).astype(jnp.int32)
    running = jnp.cumsum(onehot, axis=0)
    counts = running[-1]
    padded = (counts + MOE_ROWS - 1) // MOE_ROWS * MOE_ROWS
    pad_end = jnp.cumsum(padded)
    pad_start = pad_end - padded
    grp_start = jnp.cumsum(counts) - counts
    pos = jnp.sum(onehot * (running - 1 + pad_start[None, :]), axis=1).reshape(n_tok, TOP_K)
    n_blocks = -(-n_assign // MOE_ROWS) + N_EXPERTS
    block_e = jnp.minimum(jnp.searchsorted(pad_end, jnp.arange(n_blocks) * MOE_ROWS, side='right'),
                          N_EXPERTS - 1).astype(jnp.int32)
    n_used = (pad_end[-1] // MOE_ROWS).astype(jnp.int32).reshape(1)
    order = jnp.argsort(flat_e)
    tok_sorted = (order // TOP_K).astype(jnp.int32)
    row_in_grp = (jnp.arange(n_blocks, dtype=jnp.int32) * MOE_ROWS - pad_start[block_e])[:, None] \
        + jnp.arange(MOE_ROWS, dtype=jnp.int32)[None, :]
    src = jnp.clip(grp_start[block_e][:, None] + row_in_grp, 0, n_assign - 1)
    row_tok = jnp.where(row_in_grp < counts[block_e][:, None], tok_sorted[src], 0).reshape(-1)
    half = n_blocks // 2 * MOE_ROWS
    ys = _moe_experts([x2d[row_tok[:half]], x2d[row_tok[half:]]], block_e, n_used, w_gu, w_down)
    return [ys[pos[:, s]] for s in range(TOP_K)], weights


def kernel(x, p, a_mu, a_w_rkv, a_w0, a_w1, a_w2, a_a0, a_a1, a_a2, a_g1, a_g2, a_k_k, a_k_a, a_r_k,
           a_gn_g, a_gn_b, a_w_o, b_w_kv, b_cmp_pos, b_phi_w1, b_phi_b1, b_phi_w2, b_w_in, b_w_o,
           f_w_gu, f_w_down, m_w_router, m_b_router, m_w_gu, m_w_down, ln_g, ln_b, ple_w, ple_gate_w,
           ple_gate_b):
    batch, seq, d = x.shape
    m = batch * seq
    x0 = x.reshape(m, d)
    p2d = p.reshape(DEPTH, m, PLE_DIM)
    bf = lambda w: w.astype(BF16)

    x1 = _rwkv_layer(x0, batch, seq, a_mu[0], bf(a_w_rkv[0]), a_w0[0], bf(a_w1[0]), bf(a_w2[0]), a_a0[0],
                     bf(a_a1[0]), bf(a_a2[0]), bf(a_g1[0]), bf(a_g2[0]), a_k_k[0], a_k_a[0], a_r_k[0],
                     a_gn_g[0], a_gn_b[0], bf(a_w_o[0]), ln_g[0, 0], ln_b[0, 0])
    x2 = _ffn_ple(x1, p2d, 0, bf(f_w_gu[0][:, :D_FF]), bf(f_w_gu[0][:, D_FF:]), bf(f_w_down[0]),
                  ln_g[0, 1], ln_b[0, 1], bf(ple_w[0]), bf(ple_gate_w[0]), ple_gate_b[0])

    slabs, ks, kw, vst, vwt = _nsa_kv(x2, b_w_kv, batch, seq)
    k_cmp, v_cmp = _nsa_compress(slabs, b_cmp_pos, b_phi_w1, b_phi_b1, b_phi_w2)
    q, qr, gates = _nsa_q(x2, b_w_in[0], batch, seq)
    attn = _nsa_attention(q, qr, gates, k_cmp, v_cmp, ks, vst, kw, vwt).reshape(m, d)
    x3, x3_bf, route = _proj_ln_router(attn, x2, bf(b_w_o[0]), ln_g[1, 0], ln_b[1, 0], m_w_router[0],
                                       m_b_router[0])
    (y0, y1), route_w = _moe(x3_bf, route, m_w_gu[0], m_w_down[0])
    out = _moe_combine_ln_ple(x3, y0, y1, route_w, p2d, 1, ln_g[1, 1], ln_b[1, 1], bf(ple_w[1]),
                              bf(ple_gate_w[1]), ple_gate_b[1])
    return out.reshape(batch, seq, d)
```
